```python
import math
import jax, jax.numpy as jnp
from jax import lax
import numpy as np

D_MODEL = 1024
BATCH = 8
SEQ = 2048
DEPTH = 1

MEM_LEN = 256
CHUNK = 128
A_GROUPS = 8
A_WIDTH = D_MODEL
A_GROUP_DIM = A_WIDTH // A_GROUPS
SB_HEADS = 8
SB_HEAD_DIM = D_MODEL // SB_HEADS
SB_WIDTH = SB_HEADS * SB_HEAD_DIM
Q_BLOCK = 128
MEM_HEADS = 4
MEM_HEAD_DIM = 128
MEM_WIDTH = MEM_HEADS * MEM_HEAD_DIM
N_EXPERTS = 64
TOP_K = 8
N_GROUPS = 8
TOPK_GROUPS = 4
EXPERT_HIDDEN = 256
SHARED_HIDDEN = 256
ROUTED_SCALE = 2.5
MOE_BLOCK = 256
LN_EPS = 1e-5
ALPHA = (2 * DEPTH) ** 0.25
BETA = (8 * DEPTH) ** -0.25
IN_WIDTH = 2 * A_WIDTH + 3 * SB_WIDTH + 2 * D_MODEL

kernel_name = "hybrid_gmlp_stickbreak_mem_moe_deepnorm"


def layer_norm(x, g, b):
    xf = x.astype(jnp.float32)
    mu = jnp.mean(xf, axis=-1, keepdims=True)
    var = jnp.mean(jnp.square(xf - mu), axis=-1, keepdims=True)
    y = (xf - mu) * lax.rsqrt(var + LN_EPS) * g.astype(jnp.float32) + b.astype(jnp.float32)
    return y.astype(x.dtype)


def chunk_gmlp(u, v, ln_g, ln_b, w_s, b_s):
    bsz, seq, _ = u.shape
    n_chunks = seq // CHUNK
    v = layer_norm(v, ln_g, ln_b).reshape(bsz, n_chunks, CHUNK, A_GROUPS, A_GROUP_DIM)
    w = jnp.tril(w_s)
    mixed = jnp.einsum('gts,bcsgd->bctgd', w, v) + b_s.T[None, None, :, :, None]
    return u * mixed.reshape(bsz, seq, A_WIDTH)


def stick_breaking_attention(q, k, v):
    seq = q.shape[1]
    scale = SB_HEAD_DIM ** -0.5
    outs = []
    for i in range(seq // Q_BLOCK):
        q0 = i * Q_BLOCK
        kend = q0 + Q_BLOCK
        qb = q[:, q0:kend]
        kb = k[:, :kend]
        vb = v[:, :kend]
        z = jnp.einsum('bqhd,bkhd->bhqk', qb, kb,
                       preferred_element_type=jnp.float32) * scale
        t_pos = q0 + jnp.arange(Q_BLOCK)
        s_pos = jnp.arange(kend)
        mask = s_pos[None, :] < t_pos[:, None]
        log_stay = jnp.where(mask, jax.nn.log_sigmoid(-z), 0.0)
        log_w = jax.nn.log_sigmoid(z) + lax.cumsum(log_stay, axis=3, reverse=True) - log_stay
        a = jnp.where(mask, jnp.exp(log_w), 0.0)
        outs.append(jnp.einsum('bhqk,bkhd->bqhd', a.astype(v.dtype), vb))
    return jnp.concatenate(outs, axis=1)


def memory_attention(x, mem, w_q, w_kv, w_o):
    bsz, seq, _ = x.shape
    q = (x @ w_q).reshape(bsz, seq, MEM_HEADS, MEM_HEAD_DIM)
    k, v = jnp.split(mem @ w_kv, 2, axis=-1)
    k = k.reshape(bsz, MEM_LEN, MEM_HEADS, MEM_HEAD_DIM)
    v = v.reshape(bsz, MEM_LEN, MEM_HEADS, MEM_HEAD_DIM)
    logits = jnp.einsum('bqhd,bmhd->bhqm', q, k,
                        preferred_element_type=jnp.float32) * (MEM_HEAD_DIM ** -0.5)
    p = jax.nn.softmax(logits, axis=-1).astype(v.dtype)
    o = jnp.einsum('bhqm,bmhd->bqhd', p, v).reshape(bsz, seq, MEM_WIDTH)
    return o @ w_o


def swiglu(x, w_gate, w_up, w_down):
    return (jax.nn.silu(x @ w_gate) * (x @ w_up)) @ w_down


def route(xf, w_router, e_bias):
    n_tok = xf.shape[0]
    scores = jax.nn.sigmoid(xf.astype(jnp.float32) @ w_router.astype(jnp.float32))
    sel = scores + e_bias.astype(jnp.float32)[None, :]
    grouped = sel.reshape(n_tok, N_GROUPS, N_EXPERTS // N_GROUPS)
    group_score = jnp.sum(lax.top_k(grouped, 2)[0], axis=-1)
    _, gidx = lax.top_k(group_score, TOPK_GROUPS)
    gmask = jnp.any(gidx[:, :, None] == jnp.arange(N_GROUPS)[None, None, :], axis=1)
    emask = jnp.repeat(gmask, N_EXPERTS // N_GROUPS, axis=1)
    sel = jnp.where(emask, sel, -jnp.inf)
    _, idx = lax.top_k(sel, TOP_K)
    w = jnp.take_along_axis(scores, idx, axis=1)
    w = w / jnp.sum(w, axis=-1, keepdims=True) * ROUTED_SCALE
    return idx, w


def routed_experts(xf, idx, gate_w, w_gate, w_up, w_down):
    n_tok, d = xf.shape
    n_assign = n_tok * TOP_K
    flat_e = idx.reshape(-1).astype(jnp.int32)
    flat_tok = jnp.repeat(jnp.arange(n_tok, dtype=jnp.int32), TOP_K)
    flat_w = gate_w.reshape(-1)
    order = jnp.argsort(flat_e)
    e_sorted = flat_e[order]
    tok_sorted = flat_tok[order]
    w_sorted = flat_w[order]
    counts = jnp.bincount(flat_e, length=N_EXPERTS).astype(jnp.int32)
    start = jnp.cumsum(counts) - counts
    padded = (counts + MOE_BLOCK - 1) // MOE_BLOCK * MOE_BLOCK
    pad_end = jnp.cumsum(padded)
    pad_start = pad_end - padded
    dest = pad_start[e_sorted] + (jnp.arange(n_assign, dtype=jnp.int32) - start[e_sorted])
    n_blocks = -(-n_assign // MOE_BLOCK) + N_EXPERTS
    n_rows = n_blocks * MOE_BLOCK
    row_tok = jnp.full((n_rows,), n_tok, jnp.int32).at[dest].set(tok_sorted)
    row_w = jnp.zeros((n_rows,), xf.dtype).at[dest].set(w_sorted.astype(xf.dtype))
    block_start = jnp.arange(n_blocks, dtype=jnp.int32) * MOE_BLOCK
    block_e = jnp.minimum(jnp.searchsorted(pad_end, block_start, side='right'),
                          N_EXPERTS - 1).astype(jnp.int32)
    x_pad = jnp.concatenate([xf, jnp.zeros((1, d), xf.dtype)], axis=0)

    def block_fn(args):
        toks, e = args
        xb = x_pad[toks]
        return swiglu(xb, w_gate[e], w_up[e], w_down[e])

    yb = lax.map(block_fn, (row_tok.reshape(n_blocks, MOE_BLOCK), block_e))
    y = yb.reshape(n_rows, d) * row_w[:, None]
    return jax.ops.segment_sum(y, row_tok, num_segments=n_tok + 1)[:n_tok]


def setup_inputs(seed: int = 0) -> dict:
    key = jax.random.key(seed)
    ks = jax.random.split(key, 32)
    f32 = jnp.float32

    def nrm(k, shape, scale):
        return jax.random.normal(k, shape, f32) * scale

    return {
        "x": nrm(ks[0], (BATCH, SEQ, D_MODEL), 1.0),
        "mem": nrm(ks[1], (BATCH, MEM_LEN, D_MODEL), 1.0),
        "ln_in_g": 1.0 + nrm(ks[2], (D_MODEL,), 0.02),
        "ln_in_b": nrm(ks[3], (D_MODEL,), 0.02),
        "w_in": nrm(ks[4], (DEPTH, D_MODEL, IN_WIDTH), D_MODEL ** -0.5),
        "b_in": nrm(ks[5], (DEPTH, IN_WIDTH), 0.02),
        "ln_v_g": 1.0 + nrm(ks[6], (DEPTH, A_WIDTH), 0.02),
        "ln_v_b": nrm(ks[7], (DEPTH, A_WIDTH), 0.02),
        "w_spatial": nrm(ks[8], (DEPTH, A_GROUPS, CHUNK, CHUNK), CHUNK ** -0.5),
        "b_spatial": 1.0 + nrm(ks[9], (DEPTH, A_GROUPS, CHUNK), 0.02),
        "w_out": nrm(ks[10], (DEPTH, D_MODEL, D_MODEL), BETA * D_MODEL ** -0.5),
        "ln1_g": 1.0 + nrm(ks[11], (DEPTH, D_MODEL), 0.02),
        "ln1_b": nrm(ks[12], (DEPTH, D_MODEL), 0.02),
        "w_mem_q": nrm(ks[13], (DEPTH, D_MODEL, MEM_WIDTH), D_MODEL ** -0.5),
        "w_mem_kv": nrm(ks[14], (DEPTH, D_MODEL, 2 * MEM_WIDTH), D_MODEL ** -0.5),
        "w_mem_o": nrm(ks[15], (DEPTH, MEM_WIDTH, D_MODEL), BETA * MEM_WIDTH ** -0.5),
        "ln2_g": 1.0 + nrm(ks[16], (DEPTH, D_MODEL), 0.02),
        "ln2_b": nrm(ks[17], (DEPTH, D_MODEL), 0.02),
        "w_router": nrm(ks[18], (DEPTH, D_MODEL, N_EXPERTS), D_MODEL ** -0.5),
        "router_bias": nrm(ks[19], (DEPTH, N_EXPERTS), 0.01),
        "w_exp_gate": nrm(ks[20], (DEPTH, N_EXPERTS, D_MODEL, EXPERT_HIDDEN), D_MODEL ** -0.5),
        "w_exp_up": nrm(ks[21], (DEPTH, N_EXPERTS, D_MODEL, EXPERT_HIDDEN), D_MODEL ** -0.5),
        "w_exp_down": nrm(ks[22], (DEPTH, N_EXPERTS, EXPERT_HIDDEN, D_MODEL), BETA * EXPERT_HIDDEN ** -0.5),
        "w_sh_gate": nrm(ks[23], (DEPTH, D_MODEL, SHARED_HIDDEN), D_MODEL ** -0.5),
        "w_sh_up": nrm(ks[24], (DEPTH, D_MODEL, SHARED_HIDDEN), D_MODEL ** -0.5),
        "w_sh_down": nrm(ks[25], (DEPTH, SHARED_HIDDEN, D_MODEL), BETA * SHARED_HIDDEN ** -0.5),
        "ln3_g": 1.0 + nrm(ks[26], (DEPTH, D_MODEL), 0.02),
        "ln3_b": nrm(ks[27], (DEPTH, D_MODEL), 0.02),
    }


def reference(x, mem, ln_in_g, ln_in_b, w_in, b_in, ln_v_g, ln_v_b, w_spatial, b_spatial,
              w_out, ln1_g, ln1_b, w_mem_q, w_mem_kv, w_mem_o, ln2_g, ln2_b,
              w_router, router_bias, w_exp_gate, w_exp_up, w_exp_down,
              w_sh_gate, w_sh_up, w_sh_down, ln3_g, ln3_b):
    bsz, seq, d = x.shape
    split_at = [A_WIDTH, 2 * A_WIDTH,
                2 * A_WIDTH + SB_WIDTH,
                2 * A_WIDTH + 2 * SB_WIDTH,
                2 * A_WIDTH + 3 * SB_WIDTH,
                2 * A_WIDTH + 3 * SB_WIDTH + D_MODEL]
    x = layer_norm(x, ln_in_g, ln_in_b)
    for l in range(DEPTH):
        proj = x @ w_in[l] + b_in[l]
        u_a, v_a, q_b, k_b, v_b, g_a, g_b = jnp.split(proj, split_at, axis=-1)
        y_a = chunk_gmlp(jax.nn.gelu(u_a, approximate=False), jax.nn.gelu(v_a, approximate=False),
                         ln_v_g[l], ln_v_b[l], w_spatial[l], b_spatial[l])
        heads = (bsz, seq, SB_HEADS, SB_HEAD_DIM)
        y_b = stick_breaking_attention(q_b.reshape(heads), k_b.reshape(heads),
                                       v_b.reshape(heads)).reshape(bsz, seq, SB_WIDTH)
        merged = jax.nn.sigmoid(g_a) * y_a + jax.nn.sigmoid(g_b) * y_b
        x = layer_norm(ALPHA * x + merged @ w_out[l], ln1_g[l], ln1_b[l])
        m = memory_attention(x, mem, w_mem_q[l], w_mem_kv[l], w_mem_o[l])
        x = layer_norm(ALPHA * x + m, ln2_g[l], ln2_b[l])
        xf = x.reshape(bsz * seq, d)
        idx, gate_w = route(xf, w_router[l], router_bias[l])
        moe = swiglu(xf, w_sh_gate[l], w_sh_up[l], w_sh_down[l]) + \
            routed_experts(xf, idx, gate_w, w_exp_gate[l], w_exp_up[l], w_exp_down[l])
        x = layer_norm(ALPHA * x + moe.reshape(bsz, seq, d), ln3_g[l], ln3_b[l])
    return x
```

```python
import functools

import jax
import jax.numpy as jnp
from jax import lax
from jax.experimental import pallas as pl
from jax.experimental.pallas import tpu as pltpu

F32 = jnp.float32
BF16 = jnp.bfloat16
I32 = jnp.int32

LANES = 128
SUBLANES = 8
GRAN = SUBLANES

CHUNK = 128
A_GROUPS = 8
SB_HEADS = 8
SB_HEAD_DIM = 128
MEM_HEADS = 4
MEM_HEAD_DIM = 128
N_EXPERTS = 64
TOP_K = 8
N_GROUPS = 8
TOPK_GROUPS = 4
ROUTED_SCALE = 2.5
LN_EPS = 1e-5
DEPTH = 1
ALPHA = (2 * DEPTH) ** 0.25

TM_PROJ = 256
TQ = 256
TK = 128
TM_MOE = 256
FFN_TILE = 256
VMEM_LIMIT = 56 * 1024 * 1024


def _ln(x, g, b):
    mu = jnp.mean(x, axis=-1, keepdims=True)
    xc = x - mu
    var = jnp.mean(xc * xc, axis=-1, keepdims=True)
    return xc * lax.rsqrt(var + LN_EPS) * g + b


def _gelu(x):
    return 0.5 * x * (1.0 + lax.erf(x * (2.0 ** -0.5)))


def _dot(a, b):
    return jnp.dot(a, b, preferred_element_type=F32)


def _dot_nt(a, b):
    return lax.dot_general(a, b, (((1,), (1,)), ((), ())), preferred_element_type=F32)


def _dot_tn(a, b):
    return lax.dot_general(a, b, (((0,), (0,)), ((), ())), preferred_element_type=F32)


def _full(shape):
    n = len(shape)
    return pl.BlockSpec(shape, lambda *_: (0,) * n)


def _in_proj_kernel(x_ref, lg_ref, lb_ref, w_ref, b_ref, vg_ref, vb_ref, ws_ref, bst_ref,
                    oa_ref, q_ref, k_ref, v_ref, gb_ref):
    tm, d = x_ref.shape
    xb = _ln(x_ref[...], lg_ref[...], lb_ref[...]).astype(BF16)

    def seg(i):
        return _dot(xb, w_ref[:, i * d:(i + 1) * d]) + b_ref[:, i * d:(i + 1) * d]

    q_ref[...] = (seg(2) * (SB_HEAD_DIM ** -0.5)).astype(BF16)
    k_ref[...] = seg(3).astype(BF16)
    v_ref[...] = seg(4).astype(BF16)
    gb_ref[...] = jax.nn.sigmoid(seg(6)).astype(BF16)

    gu = jax.nn.sigmoid(seg(5)) * _gelu(seg(0))
    vln = _ln(_gelu(seg(1)), vg_ref[...], vb_ref[...]).astype(BF16)
    gd = d // A_GROUPS
    row = lax.broadcasted_iota(I32, (CHUNK, CHUNK), 0)
    col = lax.broadcasted_iota(I32, (CHUNK, CHUNK), 1)
    for g in range(A_GROUPS):
        w = jnp.where(col <= row, ws_ref[g], 0.0).astype(BF16)
        bias = bst_ref[:, g:g + 1]
        for c in range(tm // CHUNK):
            rs = slice(c * CHUNK, (c + 1) * CHUNK)
            cs = slice(g * gd, (g + 1) * gd)
            mixed = _dot(w, vln[rs, cs]) + bias
            oa_ref[rs, cs] = (gu[rs, cs] * mixed).astype(BF16)


def _in_proj(x2d, ln_g, ln_b, w_in, b_in, vg, vb, w_s, b_st):
    t, d = x2d.shape
    n_in = w_in.shape[1]
    tm = TM_PROJ
    row_spec = pl.BlockSpec((tm, d), lambda i: (i, 0))
    out = jax.ShapeDtypeStruct((t, d), BF16)
    return pl.pallas_call(
        _in_proj_kernel,
        grid=(t // tm,),
        in_specs=[row_spec, _full((1, d)), _full((1, d)), _full((d, n_in)), _full((1, n_in)),
                  _full((1, d)), _full((1, d)), _full(w_s.shape), _full(b_st.shape)],
        out_specs=[row_spec] * 5,
        out_shape=[out] * 5,
        compiler_params=pltpu.CompilerParams(dimension_semantics=("parallel",),
                                             vmem_limit_bytes=VMEM_LIMIT),
        name="in_proj",
    )(x2d, ln_g, ln_b, w_in, b_in, vg, vb, w_s, b_st)


def _sb_kernel(q_ref, k_ref, v_ref, m_ref, o_ref):
    i = pl.program_id(2)
    tq = q_ref.shape[1]
    nsub = tq // TK
    q = q_ref[0]
    mcat = m_ref[...]
    row = lax.broadcasted_iota(I32, (tq, TK), 0)
    col = lax.broadcasted_iota(I32, (tq, TK), 1)

    def tile(j, carry, acc, mask):
        off = pl.multiple_of(j * TK, TK)
        kt = k_ref[0, pl.ds(off, TK), :]
        vt = v_ref[0, pl.ds(off, TK), :]
        z = _dot_nt(q, kt)
        sp = jnp.maximum(z, 0.0) + jnp.log1p(jnp.exp(-jnp.abs(z)))
        ls = -sp
        if mask is not None:
            ls = jnp.where(mask, ls, 0.0)
        hi = ls.astype(BF16)
        lo = (ls - hi.astype(F32)).astype(BF16)
        ct = _dot(hi, mcat) + _dot(lo, mcat)
        a = jnp.exp((z - sp) + ct[:, :TK] + carry)
        if mask is not None:
            a = jnp.where(mask, a, 0.0)
        acc = acc + _dot(a.astype(BF16), vt)
        return carry + ct[:, TK:], acc

    carry = jnp.zeros((tq, TK), F32)
    acc = jnp.zeros((tq, TK), F32)
    for dsub in reversed(range(nsub)):
        carry, acc = tile(i * nsub + dsub, carry, acc, (dsub * TK + col) < row)

    def body(n, ca):
        return tile(i * nsub - 1 - n, ca[0], ca[1], None)

    carry, acc = lax.fori_loop(0, i * nsub, body, (carry, acc))
    o_ref[0] = acc.astype(o_ref.dtype)


def _sb_attention(q, k, v):
    b, s, w = q.shape
    h = w // SB_HEAD_DIM
    jj = lax.broadcasted_iota(I32, (TK, 2 * TK), 0)
    ss = lax.broadcasted_iota(I32, (TK, 2 * TK), 1)
    mcat = jnp.where((ss >= TK) | (jj > ss), 1.0, 0.0).astype(BF16)
    q_spec = pl.BlockSpec((1, TQ, SB_HEAD_DIM), lambda bi, hi, i: (bi, i, hi))
    kv_spec = pl.BlockSpec((1, s, SB_HEAD_DIM), lambda bi, hi, i: (bi, 0, hi))
    return pl.pallas_call(
        _sb_kernel,
        grid=(b, h, s // TQ),
        in_specs=[q_spec, kv_spec, kv_spec, _full(mcat.shape)],
        out_specs=q_spec,
        out_shape=jax.ShapeDtypeStruct((b, s, w), BF16),
        compiler_params=pltpu.CompilerParams(
            dimension_semantics=("parallel", "parallel", "arbitrary"),
            vmem_limit_bytes=VMEM_LIMIT),
        name="sb_attn",
    )(q, k, v, mcat)


def _mem_kv_kernel(m_ref, w_ref, k_ref, v_ref):
    kv = _dot(m_ref[0].astype(BF16), w_ref[...])
    half = kv.shape[1] // 2
    k_ref[0] = kv[:, :half].astype(BF16)
    v_ref[0] = kv[:, half:].astype(BF16)


def _mem_kv(mem, w_kv):
    b, m, d = mem.shape
    half = w_kv.shape[1] // 2
    out = jax.ShapeDtypeStruct((b, m, half), BF16)
    o_spec = pl.BlockSpec((1, m, half), lambda i: (i, 0, 0))
    return pl.pallas_call(
        _mem_kv_kernel,
        grid=(b,),
        in_specs=[pl.BlockSpec((1, m, d), lambda i: (i, 0, 0)), _full(w_kv.shape)],
        out_specs=[o_spec, o_spec],
        out_shape=[out, out],
        compiler_params=pltpu.CompilerParams(dimension_semantics=("parallel",),
                                             vmem_limit_bytes=VMEM_LIMIT),
        name="mem_kv",
    )(mem, w_kv)


def _mid_kernel(x_ref, lg_ref, lb_ref, oa_ref, yb_ref, gb_ref, wo_ref, l1g_ref, l1b_ref,
                wq_ref, km_ref, vm_ref, wmo_ref, l2g_ref, l2b_ref, wrh_ref, wrl_ref,
                wsg_ref, wsu_ref, wsd_ref,
                base_ref, x2_ref, lgt_ref):
    xln = _ln(x_ref[...], lg_ref[...], lb_ref[...])
    merged = oa_ref[...].astype(F32) + gb_ref[...].astype(F32) * yb_ref[...].astype(F32)
    x1 = _ln(ALPHA * xln + _dot(merged.astype(BF16), wo_ref[...]), l1g_ref[...], l1b_ref[...])

    q = (_dot(x1.astype(BF16), wq_ref[...]) * (MEM_HEAD_DIM ** -0.5)).astype(BF16)
    heads = []
    for h in range(MEM_HEADS):
        hs = slice(h * MEM_HEAD_DIM, (h + 1) * MEM_HEAD_DIM)
        logits = _dot_nt(q[:, hs], km_ref[0, :, hs])
        p = jnp.exp(logits - jnp.max(logits, axis=-1, keepdims=True))
        p = p / jnp.sum(p, axis=-1, keepdims=True)
        heads.append(_dot(p.astype(BF16), vm_ref[0, :, hs]))
    o = jnp.concatenate(heads, axis=1).astype(BF16)
    x2 = _ln(ALPHA * x1 + _dot(o, wmo_ref[...]), l2g_ref[...], l2b_ref[...])

    x2h = x2.astype(BF16)
    x2l = (x2 - x2h.astype(F32)).astype(BF16)
    lgt_ref[...] = (_dot_nt(wrh_ref[...], x2h) + _dot_nt(wrh_ref[...], x2l)
                    + _dot_nt(wrl_ref[...], x2h))
    x2_ref[...] = x2h

    hid = jax.nn.silu(_dot(x2h, wsg_ref[...])) * _dot(x2h, wsu_ref[...])
    base_ref[...] = ALPHA * x2 + _dot(hid.astype(BF16), wsd_ref[...])


def _mid(x2d, ln_g, ln_b, oa, yb, gb, w_out, l1g, l1b, w_q, k_mem, v_mem, w_mo, l2g, l2b,
         wr_hi, wr_lo, w_sg, w_su, w_sd, seq):
    t, d = x2d.shape
    tm = TM_PROJ
    per_batch = seq // tm
    row_spec = pl.BlockSpec((tm, d), lambda i: (i, 0))
    mem_spec = pl.BlockSpec((1,) + k_mem.shape[1:], lambda i: (i // per_batch, 0, 0))
    vec = _full((1, d))
    return pl.pallas_call(
        _mid_kernel,
        grid=(t // tm,),
        in_specs=[row_spec, vec, vec, row_spec, row_spec, row_spec, _full(w_out.shape), vec, vec,
                  _full(w_q.shape), mem_spec, mem_spec, _full(w_mo.shape), vec, vec,
                  _full(wr_hi.shape), _full(wr_lo.shape),
                  _full(w_sg.shape), _full(w_su.shape), _full(w_sd.shape)],
        out_specs=[row_spec, row_spec, pl.BlockSpec((N_EXPERTS, tm), lambda i: (0, i))],
        out_shape=[jax.ShapeDtypeStruct((t, d), F32), jax.ShapeDtypeStruct((t, d), BF16),
                   jax.ShapeDtypeStruct((N_EXPERTS, t), F32)],
        compiler_params=pltpu.CompilerParams(dimension_semantics=("parallel",),
                                             vmem_limit_bytes=VMEM_LIMIT),
        name="mid",
    )(x2d, ln_g, ln_b, oa, yb, gb, w_out, l1g, l1b, w_q, k_mem, v_mem, w_mo, l2g, l2b,
      wr_hi, wr_lo, w_sg, w_su, w_sd)


def _route_kernel(lgt_ref, bias_ref, sel_ref, w_ref, cnt_ref):
    e, tm = lgt_ref.shape
    per_group = e // N_GROUPS
    scores = jax.nn.sigmoid(lgt_ref[...])
    sel = scores + bias_ref[...]

    g3 = sel.reshape(N_GROUPS, per_group, tm)
    j3 = lax.broadcasted_iota(I32, g3.shape, 1)
    m1 = jnp.max(g3, axis=1, keepdims=True)
    first = jnp.min(jnp.where(g3 == m1, j3, per_group), axis=1, keepdims=True)
    m2 = jnp.max(jnp.where(j3 == first, -jnp.inf, g3), axis=1, keepdims=True)
    gs = (m1 + m2).reshape(N_GROUPS, tm)

    gi = lax.broadcasted_iota(I32, (N_GROUPS, tm), 0)
    grank = jnp.zeros((N_GROUPS, tm), I32)
    for o in range(N_GROUPS):
        other = gs[o:o + 1, :]
        grank += ((other > gs) | ((other == gs) & (o < gi))).astype(I32)
    gmask = (grank < TOPK_GROUPS).astype(F32)
    emask = jnp.broadcast_to(gmask.reshape(N_GROUPS, 1, tm), (N_GROUPS, per_group, tm)).reshape(e, tm)
    cand = jnp.where(emask > 0.5, sel, -jnp.inf)

    ei = lax.broadcasted_iota(I32, (e, tm), 0)
    erank = jnp.zeros((e, tm), I32)
    for o in range(e):
        other = cand[o:o + 1, :]
        erank += ((other > cand) | ((other == cand) & (o < ei))).astype(I32)
    chosen = erank < TOP_K

    w = jnp.where(chosen, scores, 0.0)
    w = w / jnp.sum(w, axis=0, keepdims=True) * ROUTED_SCALE
    chosen_f = chosen.astype(F32)
    sel_ref[...] = chosen_f
    w_ref[...] = w
    cnt_ref[0] = jnp.broadcast_to(jnp.sum(chosen_f, axis=1, keepdims=True), (e, LANES))


def _route(logits_t, bias_col):
    e, t = logits_t.shape
    tm = TM_MOE
    nb = t // tm
    blk = pl.BlockSpec((e, tm), lambda i: (0, i))
    return pl.pallas_call(
        _route_kernel,
        grid=(nb,),
        in_specs=[blk, _full((e, 1))],
        out_specs=[blk, blk, pl.BlockSpec((1, e, LANES), lambda i: (i, 0, 0))],
        out_shape=[jax.ShapeDtypeStruct((e, t), F32), jax.ShapeDtypeStruct((e, t), F32),
                   jax.ShapeDtypeStruct((nb, e, LANES), F32)],
        compiler_params=pltpu.CompilerParams(dimension_semantics=("parallel",),
                                             vmem_limit_bytes=VMEM_LIMIT),
        name="route",
    )(logits_t, bias_col)


def _slot_tables(sel, w, loc_col):
    e, tm = sel.shape
    selb = sel.astype(BF16)
    tj = lax.broadcasted_iota(I32, (tm, tm), 0)
    tt = lax.broadcasted_iota(I32, (tm, tm), 1)
    before_t = jnp.where(tj < tt, 1.0, 0.0).astype(BF16)
    rank = _dot(selb, before_t)
    ej = lax.broadcasted_iota(I32, (e, e), 0)
    ee = lax.broadcasted_iota(I32, (e, e), 1)
    before_e = jnp.where(ee < ej, 1.0, 0.0).astype(BF16)
    slot = _dot(before_e, selb)
    pos = loc_col + rank
    chosen = sel > 0.5
    rows, wts = [], []
    for k in range(TOP_K):
        hit = chosen & (slot == float(k))
        rows.append(jnp.sum(jnp.where(hit, pos, 0.0), axis=0, keepdims=True))
        wts.append(jnp.sum(jnp.where(hit, w, 0.0), axis=0, keepdims=True))
    return jnp.concatenate(rows, axis=0), jnp.concatenate(wts, axis=0)


def _granule_copy(src, src_row, dst, dst_row, sem):
    return pltpu.make_async_copy(src.at[pl.ds(pl.multiple_of(src_row, GRAN), GRAN)],
                                 dst.at[pl.ds(pl.multiple_of(dst_row, GRAN), GRAN)], sem)


def _dispatch_kernel(loc_ref, glob_ref, ngr_ref, tail_lo_ref, tail_hi_ref,
                     x_ref, sel_ref, locc_ref, xs_hbm, xs_loc, zero_gran, sem):
    b = pl.program_id(0)
    nb = pl.num_programs(0)
    e, tm = sel_ref.shape
    r_loc = xs_loc.shape[0]
    rows_k, _ = _slot_tables(sel_ref[...], sel_ref[...], locc_ref[0])
    x = x_ref[...]
    rchunk = tm
    for c in range(r_loc // rchunk):
        ri = (lax.broadcasted_iota(I32, (rchunk, tm), 0) + c * rchunk).astype(F32)
        hit = rows_k[0:1, :] == ri
        for k in range(1, TOP_K):
            hit = hit | (rows_k[k:k + 1, :] == ri)
        onehot = jnp.where(hit, 1.0, 0.0).astype(BF16)
        xs_loc[c * rchunk:(c + 1) * rchunk, :] = _dot(onehot, x)

    def per_expert(ex, total):
        n = ngr_ref[b * e + ex]
        lo = loc_ref[b * e + ex]
        go = glob_ref[b * e + ex]

        def per_gran(j, _):
            _granule_copy(xs_loc, lo + j * GRAN, xs_hbm, go + j * GRAN, sem).start()
            return 0

        lax.fori_loop(0, n, per_gran, 0)
        return total + n

    total = lax.fori_loop(0, e, per_expert, 0)

    @pl.when(b == nb - 1)
    def _():
        zero_gran[...] = jnp.zeros_like(zero_gran)

    def per_tail(ex, total):
        lo = tail_lo_ref[ex]
        n = jnp.where(b == nb - 1, (tail_hi_ref[ex] - lo) // GRAN, 0)

        def per_gran(j, _):
            _granule_copy(zero_gran, 0, xs_hbm, lo + j * GRAN, sem).start()
            return 0

        lax.fori_loop(0, n, per_gran, 0)
        return total + n

    total = lax.fori_loop(0, tail_lo_ref.shape[0], per_tail, total)

    def drain(j, _):
        _granule_copy(xs_loc, 0, xs_hbm, 0, sem).wait()
        return 0

    lax.fori_loop(0, total, drain, 0)


def _dispatch(x2b, sel_t, loc_off, glob_off, ngran, tail_lo, tail_hi, loc_col, r_glob):
    t, d = x2b.shape
    e = sel_t.shape[0]
    tm = TM_MOE
    nb = t // tm
    r_loc = tm * TOP_K + e * GRAN
    grid_spec = pltpu.PrefetchScalarGridSpec(
        num_scalar_prefetch=5,
        grid=(nb,),
        in_specs=[pl.BlockSpec((tm, d), lambda i, *_: (i, 0)),
                  pl.BlockSpec((e, tm), lambda i, *_: (0, i)),
                  pl.BlockSpec((1, e, 1), lambda i, *_: (i, 0, 0))],
        out_specs=pl.BlockSpec(memory_space=pl.ANY),
        scratch_shapes=[pltpu.VMEM((r_loc, d), F32), pltpu.VMEM((GRAN, d), F32),
                        pltpu.SemaphoreType.DMA],
    )
    return pl.pallas_call(
        _dispatch_kernel,
        grid_spec=grid_spec,
        out_shape=jax.ShapeDtypeStruct((r_glob, d), F32),
        compiler_params=pltpu.CompilerParams(dimension_semantics=("arbitrary",),
                                             vmem_limit_bytes=VMEM_LIMIT),
        name="dispatch",
    )(loc_off, glob_off, ngran, tail_lo, tail_hi, x2b, sel_t, loc_col)


def _experts_kernel(tile_e_ref, n_used_ref, xs_ref, wg_ref, wu_ref, wd_ref, ys_ref,
                    wg_b, wu_b, wd_b):
    i = pl.program_id(0)
    prev = tile_e_ref[jnp.maximum(i - 1, 0)]

    @pl.when((i == 0) | (tile_e_ref[i] != prev))
    def _():
        wg_b[...] = wg_ref[0].astype(BF16)
        wu_b[...] = wu_ref[0].astype(BF16)
        wd_b[...] = wd_ref[0].astype(BF16)

    @pl.when(i < n_used_ref[0])
    def _():
        x = xs_ref[...].astype(BF16)
        hid = jax.nn.silu(_dot(x, wg_b[...])) * _dot(x, wu_b[...])
        ys_ref[...] = _dot(hid.astype(BF16), wd_b[...])

    @pl.when(i >= n_used_ref[0])
    def _():
        ys_ref[...] = jnp.zeros_like(ys_ref)


def _experts(xs, tile_e, n_used, w_gate, w_up, w_down):
    r_glob, d = xs.shape
    hdim = w_gate.shape[2]
    n_tiles = r_glob // FFN_TILE

    def row_map(i, tile_e_ref, n_used_ref):
        return (jnp.minimum(i, n_used_ref[0] - 1), 0)

    def w_map(i, tile_e_ref, n_used_ref):
        return (tile_e_ref[i], 0, 0)

    grid_spec = pltpu.PrefetchScalarGridSpec(
        num_scalar_prefetch=2,
        grid=(n_tiles,),
        in_specs=[pl.BlockSpec((FFN_TILE, d), row_map),
                  pl.BlockSpec((1, d, hdim), w_map),
                  pl.BlockSpec((1, d, hdim), w_map),
                  pl.BlockSpec((1, hdim, d), w_map)],
        out_specs=pl.BlockSpec((FFN_TILE, d), lambda i, *_: (i, 0)),
        scratch_shapes=[pltpu.VMEM((d, hdim), BF16), pltpu.VMEM((d, hdim), BF16),
                        pltpu.VMEM((hdim, d), BF16)],
    )
    return pl.pallas_call(
        _experts_kernel,
        grid_spec=grid_spec,
        out_shape=jax.ShapeDtypeStruct((r_glob, d), F32),
        compiler_params=pltpu.CompilerParams(dimension_semantics=("arbitrary",),
                                             vmem_limit_bytes=VMEM_LIMIT),
        name="experts",
    )(tile_e, n_used, xs, w_gate, w_up, w_down)


def _combine_kernel(loc_ref, glob_ref, ngr_ref, used_ref,
                    base_ref, sel_ref, w_ref, locc_ref, g_ref, bta_ref, ys_hbm,
                    o_ref, ys_loc, sem):
    b = pl.program_id(0)
    e, tm = sel_ref.shape
    r_loc = ys_loc.shape[0]

    def per_expert(ex, total):
        n = ngr_ref[b * e + ex]
        lo = loc_ref[b * e + ex]
        go = glob_ref[b * e + ex]

        def per_gran(j, _):
            _granule_copy(ys_hbm, go + j * GRAN, ys_loc, lo + j * GRAN, sem).start()
            return 0

        lax.fori_loop(0, n, per_gran, 0)
        return total + n

    total = lax.fori_loop(0, e, per_expert, 0)
    rows_k, wts_k = _slot_tables(sel_ref[...], w_ref[...], locc_ref[0])

    def drain(j, _):
        _granule_copy(ys_hbm, 0, ys_loc, 0, sem).wait()
        return 0

    lax.fori_loop(0, total, drain, 0)

    used = used_ref[b]
    acc = base_ref[...]
    rchunk = tm
    for c in range(r_loc // rchunk):
        ri_i = lax.broadcasted_iota(I32, (rchunk, tm), 0) + c * rchunk
        ri = ri_i.astype(F32)
        wmat = jnp.where(rows_k[0:1, :] == ri, wts_k[0:1, :], 0.0)
        for k in range(1, TOP_K):
            wmat = jnp.where(rows_k[k:k + 1, :] == ri, wts_k[k:k + 1, :], wmat)
        rvalid = (lax.broadcasted_iota(I32, (rchunk, 1), 0) + c * rchunk) < used
        y = jnp.where(rvalid, ys_loc[c * rchunk:(c + 1) * rchunk, :], 0.0)
        acc = acc + _dot_tn(wmat.astype(BF16), y.astype(BF16))
    o_ref[...] = _ln(acc, g_ref[...], bta_ref[...])


def _combine(base, sel_t, w_t, loc_off, glob_off, ngran, used, loc_col, ys, ln_g, ln_b):
    t, d = base.shape
    e = sel_t.shape[0]
    tm = TM_MOE
    nb = t // tm
    r_loc = tm * TOP_K + e * GRAN
    blk = pl.BlockSpec((e, tm), lambda i, *_: (0, i))
    row_spec = pl.BlockSpec((tm, d), lambda i, *_: (i, 0))
    vec = pl.BlockSpec((1, d), lambda i, *_: (0, 0))
    grid_spec = pltpu.PrefetchScalarGridSpec(
        num_scalar_prefetch=4,
        grid=(nb,),
        in_specs=[row_spec, blk, blk, pl.BlockSpec((1, e, 1), lambda i, *_: (i, 0, 0)), vec, vec,
                  pl.BlockSpec(memory_space=pl.ANY)],
        out_specs=row_spec,
        scratch_shapes=[pltpu.VMEM((r_loc, d), F32), pltpu.SemaphoreType.DMA],
    )
    return pl.pallas_call(
        _combine_kernel,
        grid_spec=grid_spec,
        out_shape=jax.ShapeDtypeStruct((t, d), F32),
        compiler_params=pltpu.CompilerParams(dimension_semantics=("arbitrary",),
                                             vmem_limit_bytes=VMEM_LIMIT),
        name="combine",
    )(loc_off, glob_off, ngran, used, base, sel_t, w_t, loc_col, ln_g, ln_b, ys)


def _round_up(x, m):
    return (x + m - 1) // m * m


def _moe_layout(cnt, t):
    nb, e = cnt.shape
    cnt_g = _round_up(cnt, GRAN)
    loc_off = jnp.cumsum(cnt_g, axis=1) - cnt_g
    used = jnp.sum(cnt_g, axis=1)
    gcnt = jnp.sum(cnt_g, axis=0)
    gpad = _round_up(gcnt, FFN_TILE)
    gend = jnp.cumsum(gpad)
    gstart = gend - gpad
    glob_off = gstart[None, :] + jnp.cumsum(cnt_g, axis=0) - cnt_g
    r_glob = _round_up(t * TOP_K + nb * e * (GRAN - 1) + e * (FFN_TILE - 1), FFN_TILE)
    n_tiles = r_glob // FFN_TILE
    n_used = (gend[-1] // FFN_TILE).astype(I32)
    tile_start = jnp.minimum(jnp.arange(n_tiles, dtype=I32), n_used - 1) * FFN_TILE
    tile_e = jnp.minimum(jnp.searchsorted(gend, tile_start, side="right"), e - 1).astype(I32)
    return dict(loc_off=loc_off.astype(I32), glob_off=glob_off.astype(I32).reshape(-1),
                ngran=(cnt_g // GRAN).astype(I32).reshape(-1), used=used.astype(I32),
                tail_lo=jnp.append(gstart + gcnt, gend[-1]).astype(I32),
                tail_hi=jnp.append(gend, r_glob).astype(I32),
                tile_e=tile_e, n_used=n_used.reshape(1), r_glob=r_glob)


def kernel(x, mem, ln_in_g, ln_in_b, w_in, b_in, ln_v_g, ln_v_b, w_spatial, b_spatial, w_out,
           ln1_g, ln1_b, w_mem_q, w_mem_kv, w_mem_o, ln2_g, ln2_b, w_router, router_bias,
           w_exp_gate, w_exp_up, w_exp_down, w_sh_gate, w_sh_up, w_sh_down, ln3_g, ln3_b):
    bsz, seq, d = x.shape
    t = bsz * seq
    assert w_in.shape[0] == DEPTH
    x2d = x.reshape(t, d)
    row = lambda a: a.reshape(1, -1)

    oa, q, k, v, gb = _in_proj(x2d, row(ln_in_g), row(ln_in_b), w_in[0].astype(BF16), row(b_in[0]),
                               row(ln_v_g[0]), row(ln_v_b[0]), w_spatial[0], b_spatial[0].T)
    yb = _sb_attention(q.reshape(bsz, seq, d), k.reshape(bsz, seq, d), v.reshape(bsz, seq, d))
    k_mem, v_mem = _mem_kv(mem, w_mem_kv[0].astype(BF16))

    wr_t = w_router[0].T
    wr_hi = wr_t.astype(BF16)
    wr_lo = (wr_t - wr_hi.astype(F32)).astype(BF16)
    base, x2b, logits_t = _mid(
        x2d, row(ln_in_g), row(ln_in_b), oa, yb.reshape(t, d), gb, w_out[0].astype(BF16),
        row(ln1_g[0]), row(ln1_b[0]), w_mem_q[0].astype(BF16), k_mem, v_mem,
        w_mem_o[0].astype(BF16), row(ln2_g[0]), row(ln2_b[0]), wr_hi, wr_lo,
        w_sh_gate[0].astype(BF16), w_sh_up[0].astype(BF16), w_sh_down[0].astype(BF16), seq)

    sel_t, w_t, cnt = _route(logits_t, router_bias[0].reshape(-1, 1))
    lay = _moe_layout(cnt[:, :, 0].astype(I32), t)
    loc_col = lay["loc_off"].astype(F32)[:, :, None]
    loc_flat = lay["loc_off"].reshape(-1)
    xs = _dispatch(x2b, sel_t, loc_flat, lay["glob_off"], lay["ngran"], lay["tail_lo"],
                   lay["tail_hi"], loc_col, lay["r_glob"])
    ys = _experts(xs, lay["tile_e"], lay["n_used"], w_exp_gate[0], w_exp_up[0], w_exp_down[0])
    out = _combine(base, sel_t, w_t, loc_flat, lay["glob_off"], lay["ngran"], lay["used"],
                   loc_col, ys, row(ln3_g[0]), row(ln3_b[0]))
    return out.reshape(bsz, seq, d)
```

```python
import functools

import jax
import jax.numpy as jnp
from jax import lax
from jax.experimental import pallas as pl
from jax.experimental.pallas import tpu as pltpu

F32 = jnp.float32
BF16 = jnp.bfloat16
I32 = jnp.int32

LANES = 128
SUBLANES = 8
GRAN = SUBLANES

CHUNK = 128
A_GROUPS = 8
SB_HEADS = 8
SB_HEAD_DIM = 128
MEM_HEADS = 4
MEM_HEAD_DIM = 128
N_EXPERTS = 64
TOP_K = 8
N_GROUPS = 8
TOPK_GROUPS = 4
ROUTED_SCALE = 2.5
LN_EPS = 1e-5
DEPTH = 1
ALPHA = (2 * DEPTH) ** 0.25
LOG2E = 1.4426950408889634
SP_CLAMP = 64.0

TM_PROJ = 256
TQ = 1024
SLAB = 256
TM_MOE = 256
FFN_TILE = 256
VMEM_LIMIT = 56 * 1024 * 1024


def _ln(x, g, b):
    mu = jnp.mean(x, axis=-1, keepdims=True)
    xc = x - mu
    var = jnp.mean(xc * xc, axis=-1, keepdims=True)
    return xc * lax.rsqrt(var + LN_EPS) * g + b


def _gelu(x):
    return 0.5 * x * (1.0 + lax.erf(x * (2.0 ** -0.5)))


def _dot(a, b):
    return jnp.dot(a, b, preferred_element_type=F32)


def _dot_nt(a, b):
    return lax.dot_general(a, b, (((1,), (1,)), ((), ())), preferred_element_type=F32)


def _dot_tn(a, b):
    return lax.dot_general(a, b, (((0,), (0,)), ((), ())), preferred_element_type=F32)


def _full(shape):
    n = len(shape)
    return pl.BlockSpec(shape, lambda *_: (0,) * n)


def _in_proj_kernel(x_ref, lg_ref, lb_ref, w_ref, b_ref, vg_ref, vb_ref, ws_ref, bst_ref,
                    oa_ref, q_ref, k_ref, v_ref, gb_ref):
    tm, d = x_ref.shape
    xb = _ln(x_ref[...], lg_ref[...], lb_ref[...]).astype(BF16)

    def seg(i):
        return _dot(xb, w_ref[:, i * d:(i + 1) * d]) + b_ref[:, i * d:(i + 1) * d]

    q_ref[...] = (seg(2) * (SB_HEAD_DIM ** -0.5 * LOG2E)).astype(BF16)
    k_ref[...] = seg(3).astype(BF16)
    v_ref[...] = seg(4).astype(BF16)
    gb_ref[...] = jax.nn.sigmoid(seg(6)).astype(BF16)

    gu = jax.nn.sigmoid(seg(5)) * _gelu(seg(0))
    vln = _ln(_gelu(seg(1)), vg_ref[...], vb_ref[...]).astype(BF16)
    gd = d // A_GROUPS
    row = lax.broadcasted_iota(I32, (CHUNK, CHUNK), 0)
    col = lax.broadcasted_iota(I32, (CHUNK, CHUNK), 1)
    for g in range(A_GROUPS):
        w = jnp.where(col <= row, ws_ref[g], 0.0).astype(BF16)
        bias = bst_ref[:, g:g + 1]
        for c in range(tm // CHUNK):
            rs = slice(c * CHUNK, (c + 1) * CHUNK)
            cs = slice(g * gd, (g + 1) * gd)
            mixed = _dot(w, vln[rs, cs]) + bias
            oa_ref[rs, cs] = (gu[rs, cs] * mixed).astype(BF16)


def _in_proj(x2d, ln_g, ln_b, w_in, b_in, vg, vb, w_s, b_st):
    t, d = x2d.shape
    n_in = w_in.shape[1]
    tm = TM_PROJ
    row_spec = pl.BlockSpec((tm, d), lambda i: (i, 0))
    out = jax.ShapeDtypeStruct((t, d), BF16)
    return pl.pallas_call(
        _in_proj_kernel,
        grid=(t // tm,),
        in_specs=[row_spec, _full((1, d)), _full((1, d)), _full((d, n_in)), _full((1, n_in)),
                  _full((1, d)), _full((1, d)), _full(w_s.shape), _full(b_st.shape)],
        out_specs=[row_spec] * 5,
        out_shape=[out] * 5,
        compiler_params=pltpu.CompilerParams(dimension_semantics=("parallel",),
                                             vmem_limit_bytes=VMEM_LIMIT),
        name="in_proj",
    )(x2d, ln_g, ln_b, w_in, b_in, vg, vb, w_s, b_st)


def _sb_kernel(q_ref, k_ref, v_ref, m_ref, o_ref):
    i = pl.program_id(2)
    tq = q_ref.shape[1]
    nslab = tq // SLAB
    q = q_ref[0]
    later = m_ref[...]

    def slab(qs, j, carry, mask):
        off = pl.multiple_of(j * SLAB, SLAB)
        z = _dot_nt(qs, k_ref[0, pl.ds(off, SLAB), :])
        sp = jnp.maximum(jnp.log(1.0 + jnp.exp2(jnp.minimum(z, SP_CLAMP))) * LOG2E, z)
        if mask is not None:
            sp = jnp.where(mask, sp, 0.0)
        cs = _dot(sp.astype(BF16), later)
        a = jnp.exp2(z - sp - cs - carry)
        if mask is not None:
            a = jnp.where(mask, a, 0.0)
        total = cs[:, 0:1] + sp[:, 0:1]
        return carry + total, _dot(a.astype(BF16), v_ref[0, pl.ds(off, SLAB), :])

    carry = jnp.zeros((tq, 1), F32)
    acc = jnp.zeros((tq, SB_HEAD_DIM), F32)
    for d in reversed(range(nslab)):
        r0 = d * SLAB
        rr = lax.broadcasted_iota(I32, (tq - r0, SLAB), 0)
        cc = lax.broadcasted_iota(I32, (tq - r0, SLAB), 1)
        c_new, contrib = slab(q[r0:], i * nslab + d, carry[r0:], cc < rr)
        a_new = acc[r0:] + contrib
        carry = jnp.concatenate([carry[:r0], c_new], axis=0) if r0 else c_new
        acc = jnp.concatenate([acc[:r0], a_new], axis=0) if r0 else a_new

    def body(n, ca):
        carry, acc = ca
        for u in range(nslab):
            carry, contrib = slab(q, (i - n) * nslab - 1 - u, carry, None)
            acc = acc + contrib
        return carry, acc

    carry, acc = lax.fori_loop(0, i, body, (carry, acc))
    o_ref[0] = acc.astype(o_ref.dtype)


def _sb_attention(q, k, v):
    b, s, w = q.shape
    h = w // SB_HEAD_DIM
    jj = lax.broadcasted_iota(I32, (SLAB, SLAB), 0)
    ss = lax.broadcasted_iota(I32, (SLAB, SLAB), 1)
    mcat = jnp.where(jj > ss, 1.0, 0.0).astype(BF16)
    q_spec = pl.BlockSpec((1, TQ, SB_HEAD_DIM), lambda bi, hi, i: (bi, i, hi))
    kv_spec = pl.BlockSpec((1, s, SB_HEAD_DIM), lambda bi, hi, i: (bi, 0, hi))
    return pl.pallas_call(
        _sb_kernel,
        grid=(b, h, s // TQ),
        in_specs=[q_spec, kv_spec, kv_spec, _full(mcat.shape)],
        out_specs=q_spec,
        out_shape=jax.ShapeDtypeStruct((b, s, w), BF16),
        compiler_params=pltpu.CompilerParams(
            dimension_semantics=("parallel", "parallel", "arbitrary"),
            vmem_limit_bytes=VMEM_LIMIT),
        name="sb_attn",
    )(q, k, v, mcat)


def _mem_kv_kernel(m_ref, w_ref, k_ref, v_ref):
    kv = _dot(m_ref[0].astype(BF16), w_ref[...])
    half = kv.shape[1] // 2
    k_ref[0] = kv[:, :half].astype(BF16)
    v_ref[0] = kv[:, half:].astype(BF16)


def _mem_kv(mem, w_kv):
    b, m, d = mem.shape
    half = w_kv.shape[1] // 2
    out = jax.ShapeDtypeStruct((b, m, half), BF16)
    o_spec = pl.BlockSpec((1, m, half), lambda i: (i, 0, 0))
    return pl.pallas_call(
        _mem_kv_kernel,
        grid=(b,),
        in_specs=[pl.BlockSpec((1, m, d), lambda i: (i, 0, 0)), _full(w_kv.shape)],
        out_specs=[o_spec, o_spec],
        out_shape=[out, out],
        compiler_params=pltpu.CompilerParams(dimension_semantics=("parallel",),
                                             vmem_limit_bytes=VMEM_LIMIT),
        name="mem_kv",
    )(mem, w_kv)


def _mid_kernel(x_ref, lg_ref, lb_ref, oa_ref, yb_ref, gb_ref, wo_ref, l1g_ref, l1b_ref,
                wq_ref, km_ref, vm_ref, wmo_ref, l2g_ref, l2b_ref, wrh_ref, wrl_ref,
                wsg_ref, wsu_ref, wsd_ref,
                base_ref, x2_ref, lgt_ref):
    xln = _ln(x_ref[...], lg_ref[...], lb_ref[...])
    merged = oa_ref[...].astype(F32) + gb_ref[...].astype(F32) * yb_ref[...].astype(F32)
    x1 = _ln(ALPHA * xln + _dot(merged.astype(BF16), wo_ref[...]), l1g_ref[...], l1b_ref[...])

    q = (_dot(x1.astype(BF16), wq_ref[...]) * (MEM_HEAD_DIM ** -0.5)).astype(BF16)
    heads = []
    for h in range(MEM_HEADS):
        hs = slice(h * MEM_HEAD_DIM, (h + 1) * MEM_HEAD_DIM)
        logits = _dot_nt(q[:, hs], km_ref[0, :, hs])
        p = jnp.exp(logits - jnp.max(logits, axis=-1, keepdims=True))
        p = p / jnp.sum(p, axis=-1, keepdims=True)
        heads.append(_dot(p.astype(BF16), vm_ref[0, :, hs]))
    o = jnp.concatenate(heads, axis=1).astype(BF16)
    x2 = _ln(ALPHA * x1 + _dot(o, wmo_ref[...]), l2g_ref[...], l2b_ref[...])

    x2h = x2.astype(BF16)
    x2l = (x2 - x2h.astype(F32)).astype(BF16)
    lgt_ref[...] = (_dot_nt(wrh_ref[...], x2h) + _dot_nt(wrh_ref[...], x2l)
                    + _dot_nt(wrl_ref[...], x2h))
    x2_ref[...] = x2h

    hid = jax.nn.silu(_dot(x2h, wsg_ref[...])) * _dot(x2h, wsu_ref[...])
    base_ref[...] = ALPHA * x2 + _dot(hid.astype(BF16), wsd_ref[...])


def _mid(x2d, ln_g, ln_b, oa, yb, gb, w_out, l1g, l1b, w_q, k_mem, v_mem, w_mo, l2g, l2b,
         wr_hi, wr_lo, w_sg, w_su, w_sd, seq):
    t, d = x2d.shape
    tm = TM_PROJ
    per_batch = seq // tm
    row_spec = pl.BlockSpec((tm, d), lambda i: (i, 0))
    mem_spec = pl.BlockSpec((1,) + k_mem.shape[1:], lambda i: (i // per_batch, 0, 0))
    vec = _full((1, d))
    return pl.pallas_call(
        _mid_kernel,
        grid=(t // tm,),
        in_specs=[row_spec, vec, vec, row_spec, row_spec, row_spec, _full(w_out.shape), vec, vec,
                  _full(w_q.shape), mem_spec, mem_spec, _full(w_mo.shape), vec, vec,
                  _full(wr_hi.shape), _full(wr_lo.shape),
                  _full(w_sg.shape), _full(w_su.shape), _full(w_sd.shape)],
        out_specs=[row_spec, row_spec, pl.BlockSpec((N_EXPERTS, tm), lambda i: (0, i))],
        out_shape=[jax.ShapeDtypeStruct((t, d), F32), jax.ShapeDtypeStruct((t, d), BF16),
                   jax.ShapeDtypeStruct((N_EXPERTS, t), F32)],
        compiler_params=pltpu.CompilerParams(dimension_semantics=("parallel",),
                                             vmem_limit_bytes=VMEM_LIMIT),
        name="mid",
    )(x2d, ln_g, ln_b, oa, yb, gb, w_out, l1g, l1b, w_q, k_mem, v_mem, w_mo, l2g, l2b,
      wr_hi, wr_lo, w_sg, w_su, w_sd)


def _route_kernel(lgt_ref, bias_ref, sel_ref, w_ref, cnt_ref):
    e, tm = lgt_ref.shape
    per_group = e // N_GROUPS
    scores = jax.nn.sigmoid(lgt_ref[...])
    sel = scores + bias_ref[...]

    g3 = sel.reshape(N_GROUPS, per_group, tm)
    j3 = lax.broadcasted_iota(I32, g3.shape, 1)
    m1 = jnp.max(g3, axis=1, keepdims=True)
    first = jnp.min(jnp.where(g3 == m1, j3, per_group), axis=1, keepdims=True)
    m2 = jnp.max(jnp.where(j3 == first, -jnp.inf, g3), axis=1, keepdims=True)
    gs = (m1 + m2).reshape(N_GROUPS, tm)

    gi = lax.broadcasted_iota(I32, (N_GROUPS, tm), 0)
    grank = jnp.zeros((N_GROUPS, tm), I32)
    for o in range(N_GROUPS):
        other = gs[o:o + 1, :]
        grank += ((other > gs) | ((other == gs) & (o < gi))).astype(I32)
    gmask = (grank < TOPK_GROUPS).astype(F32)
    emask = jnp.broadcast_to(gmask.reshape(N_GROUPS, 1, tm), (N_GROUPS, per_group, tm)).reshape(e, tm)
    cand = jnp.where(emask > 0.5, sel, -jnp.inf)

    ei = lax.broadcasted_iota(I32, (e, tm), 0)
    erank = jnp.zeros((e, tm), I32)
    for o in range(e):
        other = cand[o:o + 1, :]
        erank += ((other > cand) | ((other == cand) & (o < ei))).astype(I32)
    chosen = erank < TOP_K

    w = jnp.where(chosen, scores, 0.0)
    w = w / jnp.sum(w, axis=0, keepdims=True) * ROUTED_SCALE
    chosen_f = chosen.astype(F32)
    sel_ref[...] = chosen_f
    w_ref[...] = w
    cnt_ref[0] = jnp.broadcast_to(jnp.sum(chosen_f, axis=1, keepdims=True), (e, LANES))


def _route(logits_t, bias_col):
    e, t = logits_t.shape
    tm = TM_MOE
    nb = t // tm
    blk = pl.BlockSpec((e, tm), lambda i: (0, i))
    return pl.pallas_call(
        _route_kernel,
        grid=(nb,),
        in_specs=[blk, _full((e, 1))],
        out_specs=[blk, blk, pl.BlockSpec((1, e, LANES), lambda i: (i, 0, 0))],
        out_shape=[jax.ShapeDtypeStruct((e, t), F32), jax.ShapeDtypeStruct((e, t), F32),
                   jax.ShapeDtypeStruct((nb, e, LANES), F32)],
        compiler_params=pltpu.CompilerParams(dimension_semantics=("parallel",),
                                             vmem_limit_bytes=VMEM_LIMIT),
        name="route",
    )(logits_t, bias_col)


def _slot_tables(sel, w, loc_col):
    e, tm = sel.shape
    selb = sel.astype(BF16)
    tj = lax.broadcasted_iota(I32, (tm, tm), 0)
    tt = lax.broadcasted_iota(I32, (tm, tm), 1)
    before_t = jnp.where(tj < tt, 1.0, 0.0).astype(BF16)
    rank = _dot(selb, before_t)
    ej = lax.broadcasted_iota(I32, (e, e), 0)
    ee = lax.broadcasted_iota(I32, (e, e), 1)
    before_e = jnp.where(ee < ej, 1.0, 0.0).astype(BF16)
    slot = _dot(before_e, selb)
    pos = loc_col + rank
    chosen = sel > 0.5
    rows, wts = [], []
    for k in range(TOP_K):
        hit = chosen & (slot == float(k))
        rows.append(jnp.sum(jnp.where(hit, pos, 0.0), axis=0, keepdims=True))
        wts.append(jnp.sum(jnp.where(hit, w, 0.0), axis=0, keepdims=True))
    return jnp.concatenate(rows, axis=0), jnp.concatenate(wts, axis=0)


def _granule_copy(src, src_row, dst, dst_row, sem):
    return pltpu.make_async_copy(src.at[pl.ds(pl.multiple_of(src_row, GRAN), GRAN)],
                                 dst.at[pl.ds(pl.multiple_of(dst_row, GRAN), GRAN)], sem)


def _dispatch_kernel(loc_ref, glob_ref, ngr_ref, tail_lo_ref, tail_hi_ref,
                     x_ref, sel_ref, locc_ref, xs_hbm, xs_loc, zero_gran, sem):
    b = pl.program_id(0)
    nb = pl.num_programs(0)
    e, tm = sel_ref.shape
    r_loc = xs_loc.shape[0]
    rows_k, _ = _slot_tables(sel_ref[...], sel_ref[...], locc_ref[0])
    x = x_ref[...]
    rchunk = tm
    for c in range(r_loc // rchunk):
        ri = (lax.broadcasted_iota(I32, (rchunk, tm), 0) + c * rchunk).astype(F32)
        hit = rows_k[0:1, :] == ri
        for k in range(1, TOP_K):
            hit = hit | (rows_k[k:k + 1, :] == ri)
        onehot = jnp.where(hit, 1.0, 0.0).astype(BF16)
        xs_loc[c * rchunk:(c + 1) * rchunk, :] = _dot(onehot, x)

    def per_expert(ex, total):
        n = ngr_ref[b * e + ex]
        lo = loc_ref[b * e + ex]
        go = glob_ref[b * e + ex]

        def per_gran(j, _):
            _granule_copy(xs_loc, lo + j * GRAN, xs_hbm, go + j * GRAN, sem).start()
            return 0

        lax.fori_loop(0, n, per_gran, 0)
        return total + n

    total = lax.fori_loop(0, e, per_expert, 0)

    @pl.when(b == nb - 1)
    def _():
        zero_gran[...] = jnp.zeros_like(zero_gran)

    def per_tail(ex, total):
        lo = tail_lo_ref[ex]
        n = jnp.where(b == nb - 1, (tail_hi_ref[ex] - lo) // GRAN, 0)

        def per_gran(j, _):
            _granule_copy(zero_gran, 0, xs_hbm, lo + j * GRAN, sem).start()
            return 0

        lax.fori_loop(0, n, per_gran, 0)
        return total + n

    total = lax.fori_loop(0, tail_lo_ref.shape[0], per_tail, total)

    def drain(j, _):
        _granule_copy(xs_loc, 0, xs_hbm, 0, sem).wait()
        return 0

    lax.fori_loop(0, total, drain, 0)


def _dispatch(x2b, sel_t, loc_off, glob_off, ngran, tail_lo, tail_hi, loc_col, r_glob):
    t, d = x2b.shape
    e = sel_t.shape[0]
    tm = TM_MOE
    nb = t // tm
    r_loc = tm * TOP_K + e * GRAN
    grid_spec = pltpu.PrefetchScalarGridSpec(
        num_scalar_prefetch=5,
        grid=(nb,),
        in_specs=[pl.BlockSpec((tm, d), lambda i, *_: (i, 0)),
                  pl.BlockSpec((e, tm), lambda i, *_: (0, i)),
                  pl.BlockSpec((1, e, 1), lambda i, *_: (i, 0, 0))],
        out_specs=pl.BlockSpec(memory_space=pl.ANY),
        scratch_shapes=[pltpu.VMEM((r_loc, d), F32), pltpu.VMEM((GRAN, d), F32),
                        pltpu.SemaphoreType.DMA],
    )
    return pl.pallas_call(
        _dispatch_kernel,
        grid_spec=grid_spec,
        out_shape=jax.ShapeDtypeStruct((r_glob, d), F32),
        compiler_params=pltpu.CompilerParams(dimension_semantics=("arbitrary",),
                                             vmem_limit_bytes=VMEM_LIMIT),
        name="dispatch",
    )(loc_off, glob_off, ngran, tail_lo, tail_hi, x2b, sel_t, loc_col)


def _experts_kernel(tile_e_ref, n_used_ref, xs_ref, wg_ref, wu_ref, wd_ref, ys_ref,
                    wg_b, wu_b, wd_b):
    i = pl.program_id(0)
    prev = tile_e_ref[jnp.maximum(i - 1, 0)]

    @pl.when((i == 0) | (tile_e_ref[i] != prev))
    def _():
        wg_b[...] = wg_ref[0].astype(BF16)
        wu_b[...] = wu_ref[0].astype(BF16)
        wd_b[...] = wd_ref[0].astype(BF16)

    @pl.when(i < n_used_ref[0])
    def _():
        x = xs_ref[...].astype(BF16)
        hid = jax.nn.silu(_dot(x, wg_b[...])) * _dot(x, wu_b[...])
        ys_ref[...] = _dot(hid.astype(BF16), wd_b[...])

    @pl.when(i >= n_used_ref[0])
    def _():
        ys_ref[...] = jnp.zeros_like(ys_ref)


def _experts(xs, tile_e, n_used, w_gate, w_up, w_down):
    r_glob, d = xs.shape
    hdim = w_gate.shape[2]
    n_tiles = r_glob // FFN_TILE

    def row_map(i, tile_e_ref, n_used_ref):
        return (jnp.minimum(i, n_used_ref[0] - 1), 0)

    def w_map(i, tile_e_ref, n_used_ref):
        return (tile_e_ref[i], 0, 0)

    grid_spec = pltpu.PrefetchScalarGridSpec(
        num_scalar_prefetch=2,
        grid=(n_tiles,),
        in_specs=[pl.BlockSpec((FFN_TILE, d), row_map),
                  pl.BlockSpec((1, d, hdim), w_map),
                  pl.BlockSpec((1, d, hdim), w_map),
                  pl.BlockSpec((1, hdim, d), w_map)],
        out_specs=pl.BlockSpec((FFN_TILE, d), lambda i, *_: (i, 0)),
        scratch_shapes=[pltpu.VMEM((d, hdim), BF16), pltpu.VMEM((d, hdim), BF16),
                        pltpu.VMEM((hdim, d), BF16)],
    )
    return pl.pallas_call(
        _experts_kernel,
        grid_spec=grid_spec,
        out_shape=jax.ShapeDtypeStruct((r_glob, d), F32),
        compiler_params=pltpu.CompilerParams(dimension_semantics=("arbitrary",),
                                             vmem_limit_bytes=VMEM_LIMIT),
        name="experts",
    )(tile_e, n_used, xs, w_gate, w_up, w_down)


def _combine_kernel(loc_ref, glob_ref, ngr_ref, used_ref,
                    base_ref, sel_ref, w_ref, locc_ref, g_ref, bta_ref, ys_hbm,
                    o_ref, ys_loc, sem):
    b = pl.program_id(0)
    e, tm = sel_ref.shape
    r_loc = ys_loc.shape[0]

    def per_expert(ex, total):
        n = ngr_ref[b * e + ex]
        lo = loc_ref[b * e + ex]
        go = glob_ref[b * e + ex]

        def per_gran(j, _):
            _granule_copy(ys_hbm, go + j * GRAN, ys_loc, lo + j * GRAN, sem).start()
            return 0

        lax.fori_loop(0, n, per_gran, 0)
        return total + n

    total = lax.fori_loop(0, e, per_expert, 0)
    rows_k, wts_k = _slot_tables(sel_ref[...], w_ref[...], locc_ref[0])

    def drain(j, _):
        _granule_copy(ys_hbm, 0, ys_loc, 0, sem).wait()
        return 0

    lax.fori_loop(0, total, drain, 0)

    used = used_ref[b]
    acc = base_ref[...]
    rchunk = tm
    for c in range(r_loc // rchunk):
        ri_i = lax.broadcasted_iota(I32, (rchunk, tm), 0) + c * rchunk
        ri = ri_i.astype(F32)
        wmat = jnp.where(rows_k[0:1, :] == ri, wts_k[0:1, :], 0.0)
        for k in range(1, TOP_K):
            wmat = jnp.where(rows_k[k:k + 1, :] == ri, wts_k[k:k + 1, :], wmat)
        rvalid = (lax.broadcasted_iota(I32, (rchunk, 1), 0) + c * rchunk) < used
        y = jnp.where(rvalid, ys_loc[c * rchunk:(c + 1) * rchunk, :], 0.0)
        acc = acc + _dot_tn(wmat.astype(BF16), y.astype(BF16))
    o_ref[...] = _ln(acc, g_ref[...], bta_ref[...])


def _combine(base, sel_t, w_t, loc_off, glob_off, ngran, used, loc_col, ys, ln_g, ln_b):
    t, d = base.shape
    e = sel_t.shape[0]
    tm = TM_MOE
    nb = t // tm
    r_loc = tm * TOP_K + e * GRAN
    blk = pl.BlockSpec((e, tm), lambda i, *_: (0, i))
    row_spec = pl.BlockSpec((tm, d), lambda i, *_: (i, 0))
    vec = pl.BlockSpec((1, d), lambda i, *_: (0, 0))
    grid_spec = pltpu.PrefetchScalarGridSpec(
        num_scalar_prefetch=4,
        grid=(nb,),
        in_specs=[row_spec, blk, blk, pl.BlockSpec((1, e, 1), lambda i, *_: (i, 0, 0)), vec, vec,
                  pl.BlockSpec(memory_space=pl.ANY)],
        out_specs=row_spec,
        scratch_shapes=[pltpu.VMEM((r_loc, d), F32), pltpu.SemaphoreType.DMA],
    )
    return pl.pallas_call(
        _combine_kernel,
        grid_spec=grid_spec,
        out_shape=jax.ShapeDtypeStruct((t, d), F32),
        compiler_params=pltpu.CompilerParams(dimension_semantics=("arbitrary",),
                                             vmem_limit_bytes=VMEM_LIMIT),
        name="combine",
    )(loc_off, glob_off, ngran, used, base, sel_t, w_t, loc_col, ln_g, ln_b, ys)


def _round_up(x, m):
    return (x + m - 1) // m * m


def _moe_layout(cnt, t):
    nb, e = cnt.shape
    cnt_g = _round_up(cnt, GRAN)
    loc_off = jnp.cumsum(cnt_g, axis=1) - cnt_g
    used = jnp.sum(cnt_g, axis=1)
    gcnt = jnp.sum(cnt_g, axis=0)
    gpad = _round_up(gcnt, FFN_TILE)
    gend = jnp.cumsum(gpad)
    gstart = gend - gpad
    glob_off = gstart[None, :] + jnp.cumsum(cnt_g, axis=0) - cnt_g
    r_glob = _round_up(t * TOP_K + nb * e * (GRAN - 1) + e * (FFN_TILE - 1), FFN_TILE)
    n_tiles = r_glob // FFN_TILE
    n_used = (gend[-1] // FFN_TILE).astype(I32)
    tile_start = jnp.minimum(jnp.arange(n_tiles, dtype=I32), n_used - 1) * FFN_TILE
    tile_e = jnp.minimum(jnp.sum(gend[None, :] <= tile_start[:, None], axis=1), e - 1).astype(I32)
    return dict(loc_off=loc_off.astype(I32), glob_off=glob_off.astype(I32).reshape(-1),
                ngran=(cnt_g // GRAN).astype(I32).reshape(-1), used=used.astype(I32),
                tail_lo=jnp.append(gstart + gcnt, gend[-1]).astype(I32),
                tail_hi=jnp.append(gend, r_glob).astype(I32),
                tile_e=tile_e, n_used=n_used.reshape(1), r_glob=r_glob)


def kernel(x, mem, ln_in_g, ln_in_b, w_in, b_in, ln_v_g, ln_v_b, w_spatial, b_spatial, w_out,
           ln1_g, ln1_b, w_mem_q, w_mem_kv, w_mem_o, ln2_g, ln2_b, w_router, router_bias,
           w_exp_gate, w_exp_up, w_exp_down, w_sh_gate, w_sh_up, w_sh_down, ln3_g, ln3_b):
    bsz, seq, d = x.shape
    t = bsz * seq
    assert w_in.shape[0] == DEPTH
    x2d = x.reshape(t, d)
    row = lambda a: a.reshape(1, -1)

    oa, q, k, v, gb = _in_proj(x2d, row(ln_in_g), row(ln_in_b), w_in[0].astype(BF16), row(b_in[0]),
                               row(ln_v_g[0]), row(ln_v_b[0]), w_spatial[0], b_spatial[0].T)
    yb = _sb_attention(q.reshape(bsz, seq, d), k.reshape(bsz, seq, d), v.reshape(bsz, seq, d))
    k_mem, v_mem = _mem_kv(mem, w_mem_kv[0].astype(BF16))

    wr_t = w_router[0].T
    wr_hi = wr_t.astype(BF16)
    wr_lo = (wr_t - wr_hi.astype(F32)).astype(BF16)
    base, x2b, logits_t = _mid(
        x2d, row(ln_in_g), row(ln_in_b), oa, yb.reshape(t, d), gb, w_out[0].astype(BF16),
        row(ln1_g[0]), row(ln1_b[0]), w_mem_q[0].astype(BF16), k_mem, v_mem,
        w_mem_o[0].astype(BF16), row(ln2_g[0]), row(ln2_b[0]), wr_hi, wr_lo,
        w_sh_gate[0].astype(BF16), w_sh_up[0].astype(BF16), w_sh_down[0].astype(BF16), seq)

    sel_t, w_t, cnt = _route(logits_t, router_bias[0].reshape(-1, 1))
    lay = _moe_layout(cnt[:, :, 0].astype(I32), t)
    loc_col = lay["loc_off"].astype(F32)[:, :, None]
    loc_flat = lay["loc_off"].reshape(-1)
    xs = _dispatch(x2b, sel_t, loc_flat, lay["glob_off"], lay["ngran"], lay["tail_lo"],
                   lay["tail_hi"], loc_col, lay["r_glob"])
    ys = _experts(xs, lay["tile_e"], lay["n_used"], w_exp_gate[0], w_exp_up[0], w_exp_down[0])
    out = _combine(base, sel_t, w_t, loc_flat, lay["glob_off"], lay["ngran"], lay["used"],
                   loc_col, ys, row(ln3_g[0]), row(ln3_b[0]))
    return out.reshape(bsz, seq, d)
```

```python
import functools

import jax
import jax.numpy as jnp
from jax import lax
from jax.experimental import pallas as pl
from jax.experimental.pallas import tpu as pltpu

F32 = jnp.float32
BF16 = jnp.bfloat16
I32 = jnp.int32
U32 = jnp.uint32

LANES = 128
SUBLANES = 8
GRAN = SUBLANES

CHUNK = 128
A_GROUPS = 8
SB_HEADS = 8
SB_HEAD_DIM = 128
MEM_HEADS = 4
MEM_HEAD_DIM = 128
N_EXPERTS = 64
TOP_K = 8
N_GROUPS = 8
TOPK_GROUPS = 4
ROUTED_SCALE = 2.5
LN_EPS = 1e-5
DEPTH = 1
ALPHA = (2 * DEPTH) ** 0.25
LOG2E = 1.4426950408889634
SP_CLAMP = 64.0

TM_PROJ = 256
TQ = 1024
SLAB = 256
TM_MOE = 256
FFN_TILE = 256
ROW_SPLIT = 256.0
VMEM_LIMIT = 56 * 1024 * 1024


def _ln(x, g, b):
    mu = jnp.mean(x, axis=-1, keepdims=True)
    xc = x - mu
    var = jnp.mean(xc * xc, axis=-1, keepdims=True)
    return xc * lax.rsqrt(var + LN_EPS) * g + b


def _gelu(x):
    return 0.5 * x * (1.0 + lax.erf(x * (2.0 ** -0.5)))


def _dot(a, b):
    return jnp.dot(a, b, preferred_element_type=F32)


def _dot_nt(a, b):
    return lax.dot_general(a, b, (((1,), (1,)), ((), ())), preferred_element_type=F32)


def _dot_tn(a, b):
    return lax.dot_general(a, b, (((0,), (0,)), ((), ())), preferred_element_type=F32)


def _full(shape):
    n = len(shape)
    return pl.BlockSpec(shape, lambda *_: (0,) * n)


def _in_proj_kernel(x_ref, lg_ref, lb_ref, w_ref, b_ref, vg_ref, vb_ref, ws_ref, bst_ref,
                    oa_ref, q_ref, k_ref, v_ref, gb_ref):
    tm, d = x_ref.shape
    xb = _ln(x_ref[...], lg_ref[...], lb_ref[...]).astype(BF16)

    def seg(i):
        return _dot(xb, w_ref[:, i * d:(i + 1) * d]) + b_ref[:, i * d:(i + 1) * d]

    q_ref[...] = (seg(2) * (SB_HEAD_DIM ** -0.5 * LOG2E)).astype(BF16)
    k_ref[...] = seg(3).astype(BF16)
    v_ref[...] = seg(4).astype(BF16)
    gb_ref[...] = jax.nn.sigmoid(seg(6)).astype(BF16)

    gu = jax.nn.sigmoid(seg(5)) * _gelu(seg(0))
    vln = _ln(_gelu(seg(1)), vg_ref[...], vb_ref[...]).astype(BF16)
    gd = d // A_GROUPS
    row = lax.broadcasted_iota(I32, (CHUNK, CHUNK), 0)
    col = lax.broadcasted_iota(I32, (CHUNK, CHUNK), 1)
    for g in range(A_GROUPS):
        w = jnp.where(col <= row, ws_ref[g], 0.0).astype(BF16)
        bias = bst_ref[:, g:g + 1]
        for c in range(tm // CHUNK):
            rs = slice(c * CHUNK, (c + 1) * CHUNK)
            cs = slice(g * gd, (g + 1) * gd)
            mixed = _dot(w, vln[rs, cs]) + bias
            oa_ref[rs, cs] = (gu[rs, cs] * mixed).astype(BF16)


def _in_proj(x2d, ln_g, ln_b, w_in, b_in, vg, vb, w_s, b_st):
    t, d = x2d.shape
    n_in = w_in.shape[1]
    tm = TM_PROJ
    row_spec = pl.BlockSpec((tm, d), lambda i: (i, 0))
    out = jax.ShapeDtypeStruct((t, d), BF16)
    return pl.pallas_call(
        _in_proj_kernel,
        grid=(t // tm,),
        in_specs=[row_spec, _full((1, d)), _full((1, d)), _full((d, n_in)), _full((1, n_in)),
                  _full((1, d)), _full((1, d)), _full(w_s.shape), _full(b_st.shape)],
        out_specs=[row_spec] * 5,
        out_shape=[out] * 5,
        compiler_params=pltpu.CompilerParams(dimension_semantics=("parallel",),
                                             vmem_limit_bytes=VMEM_LIMIT),
        name="in_proj",
    )(x2d, ln_g, ln_b, w_in, b_in, vg, vb, w_s, b_st)


def _sb_kernel(q_ref, k_ref, v_ref, m_ref, o_ref):
    i = pl.program_id(2)
    tq = q_ref.shape[1]
    nslab = tq // SLAB
    q = q_ref[0]
    later = m_ref[...]

    def slab(qs, j, carry, mask):
        off = pl.multiple_of(j * SLAB, SLAB)
        z = _dot_nt(qs, k_ref[0, pl.ds(off, SLAB), :])
        sp = jnp.maximum(jnp.log(1.0 + jnp.exp2(jnp.minimum(z, SP_CLAMP))) * LOG2E, z)
        if mask is not None:
            sp = jnp.where(mask, sp, 0.0)
        cs = _dot(sp.astype(BF16), later)
        a = jnp.exp2(z - sp - cs - carry)
        if mask is not None:
            a = jnp.where(mask, a, 0.0)
        total = cs[:, 0:1] + sp[:, 0:1]
        return carry + total, _dot(a.astype(BF16), v_ref[0, pl.ds(off, SLAB), :])

    carry = jnp.zeros((tq, 1), F32)
    acc = jnp.zeros((tq, SB_HEAD_DIM), F32)
    for d in reversed(range(nslab)):
        r0 = d * SLAB
        rr = lax.broadcasted_iota(I32, (tq - r0, SLAB), 0)
        cc = lax.broadcasted_iota(I32, (tq - r0, SLAB), 1)
        c_new, contrib = slab(q[r0:], i * nslab + d, carry[r0:], cc < rr)
        a_new = acc[r0:] + contrib
        carry = jnp.concatenate([carry[:r0], c_new], axis=0) if r0 else c_new
        acc = jnp.concatenate([acc[:r0], a_new], axis=0) if r0 else a_new

    def body(n, ca):
        carry, acc = ca
        for u in range(nslab):
            carry, contrib = slab(q, (i - n) * nslab - 1 - u, carry, None)
            acc = acc + contrib
        return carry, acc

    carry, acc = lax.fori_loop(0, i, body, (carry, acc))
    o_ref[0] = acc.astype(o_ref.dtype)


def _sb_attention(q, k, v):
    b, s, w = q.shape
    h = w // SB_HEAD_DIM
    jj = lax.broadcasted_iota(I32, (SLAB, SLAB), 0)
    ss = lax.broadcasted_iota(I32, (SLAB, SLAB), 1)
    mcat = jnp.where(jj > ss, 1.0, 0.0).astype(BF16)
    q_spec = pl.BlockSpec((1, TQ, SB_HEAD_DIM), lambda bi, hi, i: (bi, i, hi))
    kv_spec = pl.BlockSpec((1, s, SB_HEAD_DIM), lambda bi, hi, i: (bi, 0, hi))
    return pl.pallas_call(
        _sb_kernel,
        grid=(b, h, s // TQ),
        in_specs=[q_spec, kv_spec, kv_spec, _full(mcat.shape)],
        out_specs=q_spec,
        out_shape=jax.ShapeDtypeStruct((b, s, w), BF16),
        compiler_params=pltpu.CompilerParams(
            dimension_semantics=("parallel", "parallel", "arbitrary"),
            vmem_limit_bytes=VMEM_LIMIT),
        name="sb_attn",
    )(q, k, v, mcat)


def _mem_kv_kernel(m_ref, w_ref, k_ref, v_ref):
    kv = _dot(m_ref[0].astype(BF16), w_ref[...])
    half = kv.shape[1] // 2
    k_ref[0] = kv[:, :half].astype(BF16)
    v_ref[0] = kv[:, half:].astype(BF16)


def _mem_kv(mem, w_kv):
    b, m, d = mem.shape
    half = w_kv.shape[1] // 2
    out = jax.ShapeDtypeStruct((b, m, half), BF16)
    o_spec = pl.BlockSpec((1, m, half), lambda i: (i, 0, 0))
    return pl.pallas_call(
        _mem_kv_kernel,
        grid=(b,),
        in_specs=[pl.BlockSpec((1, m, d), lambda i: (i, 0, 0)), _full(w_kv.shape)],
        out_specs=[o_spec, o_spec],
        out_shape=[out, out],
        compiler_params=pltpu.CompilerParams(dimension_semantics=("parallel",),
                                             vmem_limit_bytes=VMEM_LIMIT),
        name="mem_kv",
    )(mem, w_kv)


def _mid_kernel(x_ref, lg_ref, lb_ref, oa_ref, yb_ref, gb_ref, wo_ref, l1g_ref, l1b_ref,
                wq_ref, km_ref, vm_ref, wmo_ref, l2g_ref, l2b_ref, wrh_ref, wrl_ref,
                wsg_ref, wsu_ref, wsd_ref,
                base_ref, x2_ref, lgt_ref):
    xln = _ln(x_ref[...], lg_ref[...], lb_ref[...])
    merged = oa_ref[...].astype(F32) + gb_ref[...].astype(F32) * yb_ref[...].astype(F32)
    x1 = _ln(ALPHA * xln + _dot(merged.astype(BF16), wo_ref[...]), l1g_ref[...], l1b_ref[...])

    q = (_dot(x1.astype(BF16), wq_ref[...]) * (MEM_HEAD_DIM ** -0.5)).astype(BF16)
    heads = []
    for h in range(MEM_HEADS):
        hs = slice(h * MEM_HEAD_DIM, (h + 1) * MEM_HEAD_DIM)
        logits = _dot_nt(q[:, hs], km_ref[0, :, hs])
        p = jnp.exp(logits - jnp.max(logits, axis=-1, keepdims=True))
        p = p / jnp.sum(p, axis=-1, keepdims=True)
        heads.append(_dot(p.astype(BF16), vm_ref[0, :, hs]))
    o = jnp.concatenate(heads, axis=1).astype(BF16)
    x2 = _ln(ALPHA * x1 + _dot(o, wmo_ref[...]), l2g_ref[...], l2b_ref[...])

    x2h = x2.astype(BF16)
    x2l = (x2 - x2h.astype(F32)).astype(BF16)
    lgt_ref[...] = (_dot_nt(wrh_ref[...], x2h) + _dot_nt(wrh_ref[...], x2l)
                    + _dot_nt(wrl_ref[...], x2h))
    x2_ref[...] = x2h

    hid = jax.nn.silu(_dot(x2h, wsg_ref[...])) * _dot(x2h, wsu_ref[...])
    base_ref[...] = ALPHA * x2 + _dot(hid.astype(BF16), wsd_ref[...])


def _mid(x2d, ln_g, ln_b, oa, yb, gb, w_out, l1g, l1b, w_q, k_mem, v_mem, w_mo, l2g, l2b,
         wr_hi, wr_lo, w_sg, w_su, w_sd, seq):
    t, d = x2d.shape
    tm = TM_PROJ
    per_batch = seq // tm
    row_spec = pl.BlockSpec((tm, d), lambda i: (i, 0))
    mem_spec = pl.BlockSpec((1,) + k_mem.shape[1:], lambda i: (i // per_batch, 0, 0))
    vec = _full((1, d))
    return pl.pallas_call(
        _mid_kernel,
        grid=(t // tm,),
        in_specs=[row_spec, vec, vec, row_spec, row_spec, row_spec, _full(w_out.shape), vec, vec,
                  _full(w_q.shape), mem_spec, mem_spec, _full(w_mo.shape), vec, vec,
                  _full(wr_hi.shape), _full(wr_lo.shape),
                  _full(w_sg.shape), _full(w_su.shape), _full(w_sd.shape)],
        out_specs=[row_spec, row_spec, pl.BlockSpec((N_EXPERTS, tm), lambda i: (0, i))],
        out_shape=[jax.ShapeDtypeStruct((t, d), F32), jax.ShapeDtypeStruct((t, d), BF16),
                   jax.ShapeDtypeStruct((N_EXPERTS, t), F32)],
        compiler_params=pltpu.CompilerParams(dimension_semantics=("parallel",),
                                             vmem_limit_bytes=VMEM_LIMIT),
        name="mid",
    )(x2d, ln_g, ln_b, oa, yb, gb, w_out, l1g, l1b, w_q, k_mem, v_mem, w_mo, l2g, l2b,
      wr_hi, wr_lo, w_sg, w_su, w_sd)


def _route_kernel(lgt_ref, bias_ref, sel_ref, w_ref, cnt_ref):
    e, tm = lgt_ref.shape
    per_group = e // N_GROUPS
    scores = jax.nn.sigmoid(lgt_ref[...])
    sel = scores + bias_ref[...]

    g3 = sel.reshape(N_GROUPS, per_group, tm)
    j3 = lax.broadcasted_iota(I32, g3.shape, 1)
    m1 = jnp.max(g3, axis=1, keepdims=True)
    first = jnp.min(jnp.where(g3 == m1, j3, per_group), axis=1, keepdims=True)
    m2 = jnp.max(jnp.where(j3 == first, -jnp.inf, g3), axis=1, keepdims=True)
    gs = (m1 + m2).reshape(N_GROUPS, tm)

    gi = lax.broadcasted_iota(I32, (N_GROUPS, tm), 0)
    grank = jnp.zeros((N_GROUPS, tm), I32)
    for o in range(N_GROUPS):
        other = gs[o:o + 1, :]
        grank += ((other > gs) | ((other == gs) & (o < gi))).astype(I32)
    gmask = (grank < TOPK_GROUPS).astype(F32)
    emask = jnp.broadcast_to(gmask.reshape(N_GROUPS, 1, tm), (N_GROUPS, per_group, tm)).reshape(e, tm)
    cand = jnp.where(emask > 0.5, sel, -jnp.inf)

    ei = lax.broadcasted_iota(I32, (e, tm), 0)
    erank = jnp.zeros((e, tm), I32)
    for o in range(e):
        other = cand[o:o + 1, :]
        erank += ((other > cand) | ((other == cand) & (o < ei))).astype(I32)
    chosen = erank < TOP_K

    w = jnp.where(chosen, scores, 0.0)
    w = w / jnp.sum(w, axis=0, keepdims=True) * ROUTED_SCALE
    chosen_f = chosen.astype(F32)
    sel_ref[...] = chosen_f
    w_ref[...] = w
    cnt_ref[0] = jnp.broadcast_to(jnp.sum(chosen_f, axis=1, keepdims=True), (e, LANES))


def _route(logits_t, bias_col):
    e, t = logits_t.shape
    tm = TM_MOE
    nb = t // tm
    blk = pl.BlockSpec((e, tm), lambda i: (0, i))
    return pl.pallas_call(
        _route_kernel,
        grid=(nb,),
        in_specs=[blk, _full((e, 1))],
        out_specs=[blk, blk, pl.BlockSpec((1, e, LANES), lambda i: (i, 0, 0))],
        out_shape=[jax.ShapeDtypeStruct((e, t), F32), jax.ShapeDtypeStruct((e, t), F32),
                   jax.ShapeDtypeStruct((nb, e, LANES), F32)],
        compiler_params=pltpu.CompilerParams(dimension_semantics=("parallel",),
                                             vmem_limit_bytes=VMEM_LIMIT),
        name="route",
    )(logits_t, bias_col)


def _pack_rows(x):
    h = x.shape[1] // 2
    hi = lax.bitcast_convert_type(x[:, :h], U32) & jnp.uint32(0xFFFF0000)
    lo = lax.shift_right_logical(lax.bitcast_convert_type(x[:, h:], U32), jnp.uint32(16))
    return hi | lo


def _unpack_rows(u):
    hi = lax.bitcast_convert_type(u & jnp.uint32(0xFFFF0000), F32)
    lo = lax.bitcast_convert_type(lax.shift_left(u, jnp.uint32(16)), F32)
    return jnp.concatenate([hi, lo], axis=1)


def _row_tables(sel, loc_col):
    e, tm = sel.shape
    tj = lax.broadcasted_iota(I32, (tm, tm), 0)
    tt = lax.broadcasted_iota(I32, (tm, tm), 1)
    before_t = jnp.where(tj < tt, 1.0, 0.0).astype(BF16)
    rank = _dot(sel.astype(BF16), before_t)
    pos = jnp.where(sel > 0.5, loc_col + rank, -1.0)
    hi = jnp.floor(pos * (1.0 / ROW_SPLIT)) * ROW_SPLIT
    return jnp.concatenate([hi, pos - hi], axis=0).astype(BF16)


def _row_owner(c, rchunk, loc_row, cnt_row):
    e = loc_row.shape[1]
    ri = (lax.broadcasted_iota(I32, (rchunk, e), 0) + c * rchunk).astype(F32)
    return jnp.where((ri >= loc_row) & (ri < loc_row + cnt_row), 1.0, 0.0).astype(BF16)


def _segment_copy(src, src_row, dst, dst_row, rows, sem):
    rows = pl.multiple_of(rows, GRAN)
    return pltpu.make_async_copy(src.at[pl.ds(pl.multiple_of(src_row, GRAN), rows)],
                                 dst.at[pl.ds(pl.multiple_of(dst_row, GRAN), rows)], sem)


def _dispatch_kernel(loc_ref, glob_ref, rows_ref, tail_lo_ref, tail_hi_ref,
                     x_ref, sel_ref, locc_ref, locr_ref, cntr_ref, xs_hbm, xs_loc, zeros, sem):
    b = pl.program_id(0)
    nb = pl.num_programs(0)
    e, tm = sel_ref.shape
    r_loc = xs_loc.shape[0]
    tables = _row_tables(sel_ref[...], locc_ref[0])
    x = x_ref[...]
    rchunk = tm
    for c in range(r_loc // rchunk):
        own = _row_owner(c, rchunk, locr_ref[0], cntr_ref[0])
        prow = _dot(jnp.concatenate([own, own], axis=1), tables)
        ri = (lax.broadcasted_iota(I32, (rchunk, tm), 0) + c * rchunk).astype(F32)
        onehot = jnp.where(prow == ri, 1.0, 0.0).astype(BF16)
        xs_loc[c * rchunk:(c + 1) * rchunk, :] = _pack_rows(_dot(onehot, x))

    def per_expert(ex, total):
        n = rows_ref[b * e + ex]

        @pl.when(n > 0)
        def _():
            _segment_copy(xs_loc, loc_ref[b * e + ex], xs_hbm, glob_ref[b * e + ex], n, sem).start()

        return total + n

    total = lax.fori_loop(0, e, per_expert, 0)

    @pl.when(total > 0)
    def _():
        _segment_copy(xs_loc, 0, xs_hbm, 0, total, sem).wait()

    @pl.when(b == nb - 1)
    def _():
        zeros[...] = jnp.zeros_like(zeros)
        zrows = zeros.shape[0]

        def each_piece(act):
            def per_tail(ex, _):
                lo = tail_lo_ref[ex]
                n = tail_hi_ref[ex] - lo

                def per_piece(j, _):
                    rows = jnp.minimum(n - j * zrows, zrows)
                    act(_segment_copy(zeros, 0, xs_hbm, lo + j * zrows, rows, sem))
                    return 0

                lax.fori_loop(0, (n + zrows - 1) // zrows, per_piece, 0)
                return 0

            lax.fori_loop(0, tail_lo_ref.shape[0], per_tail, 0)

        each_piece(lambda cp: cp.start())
        each_piece(lambda cp: cp.wait())


def _dispatch(x2b, sel_t, lay):
    t, d = x2b.shape
    e = sel_t.shape[0]
    tm = TM_MOE
    nb = t // tm
    r_loc = tm * TOP_K + e * GRAN
    col = pl.BlockSpec((1, e, 1), lambda i, *_: (i, 0, 0))
    rowv = pl.BlockSpec((1, 1, e), lambda i, *_: (i, 0, 0))
    grid_spec = pltpu.PrefetchScalarGridSpec(
        num_scalar_prefetch=5,
        grid=(nb,),
        in_specs=[pl.BlockSpec((tm, d), lambda i, *_: (i, 0)),
                  pl.BlockSpec((e, tm), lambda i, *_: (0, i)), col, rowv, rowv],
        out_specs=pl.BlockSpec(memory_space=pl.ANY),
        scratch_shapes=[pltpu.VMEM((r_loc, d // 2), U32), pltpu.VMEM((FFN_TILE, d // 2), U32),
                        pltpu.SemaphoreType.DMA],
    )
    return pl.pallas_call(
        _dispatch_kernel,
        grid_spec=grid_spec,
        out_shape=jax.ShapeDtypeStruct((lay["r_glob"], d // 2), U32),
        compiler_params=pltpu.CompilerParams(dimension_semantics=("arbitrary",),
                                             vmem_limit_bytes=VMEM_LIMIT),
        name="dispatch",
    )(lay["loc_flat"], lay["glob_flat"], lay["rows_flat"], lay["tail_lo"], lay["tail_hi"],
      x2b, sel_t, lay["loc_col"], lay["loc_row"], lay["cnt_row"])


def _experts_kernel(tile_e_ref, n_used_ref, xs_ref, wg_ref, wu_ref, wd_ref, ys_ref,
                    wg_b, wu_b, wd_b):
    i = pl.program_id(0)
    prev = tile_e_ref[jnp.maximum(i - 1, 0)]

    @pl.when((i == 0) | (tile_e_ref[i] != prev))
    def _():
        wg_b[...] = wg_ref[0].astype(BF16)
        wu_b[...] = wu_ref[0].astype(BF16)
        wd_b[...] = wd_ref[0].astype(BF16)

    @pl.when(i < n_used_ref[0])
    def _():
        x = _unpack_rows(xs_ref[...]).astype(BF16)
        hid = jax.nn.silu(_dot(x, wg_b[...])) * _dot(x, wu_b[...])
        y = _dot(hid.astype(BF16), wd_b[...])
        ys_ref[...] = _pack_rows(y.astype(BF16).astype(F32))

    @pl.when(i >= n_used_ref[0])
    def _():
        ys_ref[...] = jnp.zeros_like(ys_ref)


def _experts(xs, tile_e, n_used, w_gate, w_up, w_down):
    r_glob, dh = xs.shape
    d, hdim = w_gate.shape[1:]
    n_tiles = r_glob // FFN_TILE

    def row_map(i, tile_e_ref, n_used_ref):
        return (jnp.minimum(i, n_used_ref[0] - 1), 0)

    def w_map(i, tile_e_ref, n_used_ref):
        return (tile_e_ref[i], 0, 0)

    grid_spec = pltpu.PrefetchScalarGridSpec(
        num_scalar_prefetch=2,
        grid=(n_tiles,),
        in_specs=[pl.BlockSpec((FFN_TILE, dh), row_map),
                  pl.BlockSpec((1, d, hdim), w_map),
                  pl.BlockSpec((1, d, hdim), w_map),
                  pl.BlockSpec((1, hdim, d), w_map)],
        out_specs=pl.BlockSpec((FFN_TILE, dh), lambda i, *_: (i, 0)),
        scratch_shapes=[pltpu.VMEM((d, hdim), BF16), pltpu.VMEM((d, hdim), BF16),
                        pltpu.VMEM((hdim, d), BF16)],
    )
    return pl.pallas_call(
        _experts_kernel,
        grid_spec=grid_spec,
        out_shape=jax.ShapeDtypeStruct((r_glob, dh), U32),
        compiler_params=pltpu.CompilerParams(dimension_semantics=("arbitrary",),
                                             vmem_limit_bytes=VMEM_LIMIT),
        name="experts",
    )(tile_e, n_used, xs, w_gate, w_up, w_down)


def _combine_kernel(loc_ref, glob_ref, rows_ref, used_ref,
                    base_ref, sel_ref, w_ref, locc_ref, locr_ref, cntr_ref, g_ref, bta_ref, ys_hbm,
                    o_ref, ys_loc, sem):
    b = pl.program_id(0)
    e, tm = sel_ref.shape
    r_loc = ys_loc.shape[0]

    def per_expert(ex, total):
        n = rows_ref[b * e + ex]

        @pl.when(n > 0)
        def _():
            _segment_copy(ys_hbm, glob_ref[b * e + ex], ys_loc, loc_ref[b * e + ex], n, sem).start()

        return total + n

    total = lax.fori_loop(0, e, per_expert, 0)
    tables = _row_tables(sel_ref[...], locc_ref[0])
    wb = w_ref[...].astype(BF16)

    @pl.when(total > 0)
    def _():
        _segment_copy(ys_hbm, 0, ys_loc, 0, total, sem).wait()

    used = used_ref[b]
    acc = base_ref[...]
    rchunk = tm
    for c in range(r_loc // rchunk):
        own = _row_owner(c, rchunk, locr_ref[0], cntr_ref[0])
        prow = _dot(jnp.concatenate([own, own], axis=1), tables)
        wrow = _dot(own, wb)
        ri = (lax.broadcasted_iota(I32, (rchunk, tm), 0) + c * rchunk).astype(F32)
        wmat = jnp.where(prow == ri, wrow, 0.0).astype(BF16)
        rvalid = (lax.broadcasted_iota(I32, (rchunk, 1), 0) + c * rchunk) < used
        y = jnp.where(rvalid, _unpack_rows(ys_loc[c * rchunk:(c + 1) * rchunk, :]), 0.0)
        acc = acc + _dot_tn(wmat, y.astype(BF16))
    o_ref[...] = _ln(acc, g_ref[...], bta_ref[...])


def _combine(base, sel_t, w_t, lay, ys, ln_g, ln_b):
    t, d = base.shape
    e = sel_t.shape[0]
    tm = TM_MOE
    nb = t // tm
    r_loc = tm * TOP_K + e * GRAN
    blk = pl.BlockSpec((e, tm), lambda i, *_: (0, i))
    row_spec = pl.BlockSpec((tm, d), lambda i, *_: (i, 0))
    vec = pl.BlockSpec((1, d), lambda i, *_: (0, 0))
    col = pl.BlockSpec((1, e, 1), lambda i, *_: (i, 0, 0))
    rowv = pl.BlockSpec((1, 1, e), lambda i, *_: (i, 0, 0))
    grid_spec = pltpu.PrefetchScalarGridSpec(
        num_scalar_prefetch=4,
        grid=(nb,),
        in_specs=[row_spec, blk, blk, col, rowv, rowv, vec, vec, pl.BlockSpec(memory_space=pl.ANY)],
        out_specs=row_spec,
        scratch_shapes=[pltpu.VMEM((r_loc, d // 2), U32), pltpu.SemaphoreType.DMA],
    )
    return pl.pallas_call(
        _combine_kernel,
        grid_spec=grid_spec,
        out_shape=jax.ShapeDtypeStruct((t, d), F32),
        compiler_params=pltpu.CompilerParams(dimension_semantics=("arbitrary",),
                                             vmem_limit_bytes=VMEM_LIMIT),
        name="combine",
    )(lay["loc_flat"], lay["glob_flat"], lay["rows_flat"], lay["used"],
      base, sel_t, w_t, lay["loc_col"], lay["loc_row"], lay["cnt_row"], ln_g, ln_b, ys)


def _round_up(x, m):
    return (x + m - 1) // m * m


def _moe_layout(cnt, t):
    nb, e = cnt.shape
    cnt_g = _round_up(cnt, GRAN)
    loc_off = jnp.cumsum(cnt_g, axis=1) - cnt_g
    used = jnp.sum(cnt_g, axis=1)
    gcnt = jnp.sum(cnt_g, axis=0)
    gpad = _round_up(gcnt, FFN_TILE)
    gend = jnp.cumsum(gpad)
    gstart = gend - gpad
    glob_off = gstart[None, :] + jnp.cumsum(cnt_g, axis=0) - cnt_g
    r_glob = _round_up(t * TOP_K + nb * e * (GRAN - 1) + e * (FFN_TILE - 1), FFN_TILE)
    n_tiles = r_glob // FFN_TILE
    n_used = (gend[-1] // FFN_TILE).astype(I32)
    tile_start = jnp.minimum(jnp.arange(n_tiles, dtype=I32), n_used - 1) * FFN_TILE
    tile_e = jnp.minimum(jnp.sum(gend[None, :] <= tile_start[:, None], axis=1), e - 1).astype(I32)
    return dict(loc_flat=loc_off.astype(I32).reshape(-1), glob_flat=glob_off.astype(I32).reshape(-1),
                rows_flat=cnt_g.astype(I32).reshape(-1), used=used.astype(I32),
                loc_col=loc_off.astype(F32)[:, :, None], loc_row=loc_off.astype(F32)[:, None, :],
                cnt_row=cnt_g.astype(F32)[:, None, :],
                tail_lo=jnp.append(gstart + gcnt, gend[-1]).astype(I32),
                tail_hi=jnp.append(gend, r_glob).astype(I32),
                tile_e=tile_e, n_used=n_used.reshape(1), r_glob=r_glob)


def kernel(x, mem, ln_in_g, ln_in_b, w_in, b_in, ln_v_g, ln_v_b, w_spatial, b_spatial, w_out,
           ln1_g, ln1_b, w_mem_q, w_mem_kv, w_mem_o, ln2_g, ln2_b, w_router, router_bias,
           w_exp_gate, w_exp_up, w_exp_down, w_sh_gate, w_sh_up, w_sh_down, ln3_g, ln3_b):
    bsz, seq, d = x.shape
    t = bsz * seq
    assert w_in.shape[0] == DEPTH
    x2d = x.reshape(t, d)
    row = lambda a: a.reshape(1, -1)

    oa, q, k, v, gb = _in_proj(x2d, row(ln_in_g), row(ln_in_b), w_in[0].astype(BF16), row(b_in[0]),
                               row(ln_v_g[0]), row(ln_v_b[0]), w_spatial[0], b_spatial[0].T)
    yb = _sb_attention(q.reshape(bsz, seq, d), k.reshape(bsz, seq, d), v.reshape(bsz, seq, d))
    k_mem, v_mem = _mem_kv(mem, w_mem_kv[0].astype(BF16))

    wr_t = w_router[0].T
    wr_hi = wr_t.astype(BF16)
    wr_lo = (wr_t - wr_hi.astype(F32)).astype(BF16)
    base, x2b, logits_t = _mid(
        x2d, row(ln_in_g), row(ln_in_b), oa, yb.reshape(t, d), gb, w_out[0].astype(BF16),
        row(ln1_g[0]), row(ln1_b[0]), w_mem_q[0].astype(BF16), k_mem, v_mem,
        w_mem_o[0].astype(BF16), row(ln2_g[0]), row(ln2_b[0]), wr_hi, wr_lo,
        w_sh_gate[0].astype(BF16), w_sh_up[0].astype(BF16), w_sh_down[0].astype(BF16), seq)

    sel_t, w_t, cnt = _route(logits_t, router_bias[0].reshape(-1, 1))
    lay = _moe_layout(cnt[:, :, 0].astype(I32), t)
    xs = _dispatch(x2b, sel_t, lay)
    ys = _experts(xs, lay["tile_e"], lay["n_used"], w_exp_gate[0], w_exp_up[0], w_exp_down[0])
    out = _combine(base, sel_t, w_t, lay, ys, row(ln3_g[0]), row(ln3_b[0]))
    return out.reshape(bsz, seq, d)
```

```python
import functools

import jax
import jax.numpy as jnp
from jax import lax
from jax.experimental import pallas as pl
from jax.experimental.pallas import tpu as pltpu

F32 = jnp.float32
BF16 = jnp.bfloat16
I32 = jnp.int32
U32 = jnp.uint32

LANES = 128
SUBLANES = 8
GRAN = SUBLANES

CHUNK = 128
A_GROUPS = 8
SB_HEADS = 8
SB_HEAD_DIM = 128
MEM_HEADS = 4
MEM_HEAD_DIM = 128
N_EXPERTS = 64
TOP_K = 8
N_GROUPS = 8
TOPK_GROUPS = 4
ROUTED_SCALE = 2.5
LN_EPS = 1e-5
DEPTH = 1
ALPHA = (2 * DEPTH) ** 0.25
LOG2E = 1.4426950408889634
SP_CLAMP = 64.0

TM_PROJ = 256
TQ = 1024
SLAB = 256
TM_MOE = 256
FFN_TILE = 512
ROW_SPLIT = 256.0
VMEM_LIMIT = 56 * 1024 * 1024


def _ln(x, g, b):
    mu = jnp.mean(x, axis=-1, keepdims=True)
    xc = x - mu
    var = jnp.mean(xc * xc, axis=-1, keepdims=True)
    return xc * lax.rsqrt(var + LN_EPS) * g + b


def _gelu(x):
    return 0.5 * x * (1.0 + lax.erf(x * (2.0 ** -0.5)))


def _dot(a, b):
    return jnp.dot(a, b, preferred_element_type=F32)


def _dot_nt(a, b):
    return lax.dot_general(a, b, (((1,), (1,)), ((), ())), preferred_element_type=F32)


def _dot_tn(a, b):
    return lax.dot_general(a, b, (((0,), (0,)), ((), ())), preferred_element_type=F32)


def _full(shape):
    n = len(shape)
    return pl.BlockSpec(shape, lambda *_: (0,) * n)


def _in_proj_kernel(x_ref, lg_ref, lb_ref, w_ref, b_ref, vg_ref, vb_ref, ws_ref, bst_ref,
                    oa_ref, q_ref, k_ref, v_ref, gb_ref):
    tm, d = x_ref.shape
    xb = _ln(x_ref[...], lg_ref[...], lb_ref[...]).astype(BF16)

    def seg(i):
        return _dot(xb, w_ref[:, i * d:(i + 1) * d]) + b_ref[:, i * d:(i + 1) * d]

    q_ref[...] = (seg(2) * (SB_HEAD_DIM ** -0.5 * LOG2E)).astype(BF16)
    k_ref[...] = seg(3).astype(BF16)
    v_ref[...] = seg(4).astype(BF16)
    gb_ref[...] = jax.nn.sigmoid(seg(6)).astype(BF16)

    gu = jax.nn.sigmoid(seg(5)) * _gelu(seg(0))
    vln = _ln(_gelu(seg(1)), vg_ref[...], vb_ref[...]).astype(BF16)
    gd = d // A_GROUPS
    row = lax.broadcasted_iota(I32, (CHUNK, CHUNK), 0)
    col = lax.broadcasted_iota(I32, (CHUNK, CHUNK), 1)
    for g in range(A_GROUPS):
        w = jnp.where(col <= row, ws_ref[g], 0.0).astype(BF16)
        bias = bst_ref[:, g:g + 1]
        for c in range(tm // CHUNK):
            rs = slice(c * CHUNK, (c + 1) * CHUNK)
            cs = slice(g * gd, (g + 1) * gd)
            mixed = _dot(w, vln[rs, cs]) + bias
            oa_ref[rs, cs] = (gu[rs, cs] * mixed).astype(BF16)


def _in_proj(x2d, ln_g, ln_b, w_in, b_in, vg, vb, w_s, b_st):
    t, d = x2d.shape
    n_in = w_in.shape[1]
    tm = TM_PROJ
    row_spec = pl.BlockSpec((tm, d), lambda i: (i, 0))
    out = jax.ShapeDtypeStruct((t, d), BF16)
    return pl.pallas_call(
        _in_proj_kernel,
        grid=(t // tm,),
        in_specs=[row_spec, _full((1, d)), _full((1, d)), _full((d, n_in)), _full((1, n_in)),
                  _full((1, d)), _full((1, d)), _full(w_s.shape), _full(b_st.shape)],
        out_specs=[row_spec] * 5,
        out_shape=[out] * 5,
        compiler_params=pltpu.CompilerParams(dimension_semantics=("parallel",),
                                             vmem_limit_bytes=VMEM_LIMIT),
        name="in_proj",
    )(x2d, ln_g, ln_b, w_in, b_in, vg, vb, w_s, b_st)


def _sb_kernel(q_ref, k_ref, v_ref, m_ref, o_ref):
    i = pl.program_id(2)
    tq = q_ref.shape[1]
    nslab = tq // SLAB
    q = q_ref[0]
    later = m_ref[...]

    def slab(qs, j, carry, mask):
        off = pl.multiple_of(j * SLAB, SLAB)
        z = _dot_nt(qs, k_ref[0, pl.ds(off, SLAB), :])
        sp = jnp.maximum(jnp.log(1.0 + jnp.exp2(jnp.minimum(z, SP_CLAMP))) * LOG2E, z)
        if mask is not None:
            sp = jnp.where(mask, sp, 0.0)
        cs = _dot(sp.astype(BF16), later)
        a = jnp.exp2(z - sp - cs - carry)
        if mask is not None:
            a = jnp.where(mask, a, 0.0)
        total = cs[:, 0:1] + sp[:, 0:1]
        return carry + total, _dot(a.astype(BF16), v_ref[0, pl.ds(off, SLAB), :])

    carry = jnp.zeros((tq, 1), F32)
    acc = jnp.zeros((tq, SB_HEAD_DIM), F32)
    for d in reversed(range(nslab)):
        r0 = d * SLAB
        rr = lax.broadcasted_iota(I32, (tq - r0, SLAB), 0)
        cc = lax.broadcasted_iota(I32, (tq - r0, SLAB), 1)
        c_new, contrib = slab(q[r0:], i * nslab + d, carry[r0:], cc < rr)
        a_new = acc[r0:] + contrib
        carry = jnp.concatenate([carry[:r0], c_new], axis=0) if r0 else c_new
        acc = jnp.concatenate([acc[:r0], a_new], axis=0) if r0 else a_new

    def body(n, ca):
        carry, acc = ca
        for u in range(nslab):
            carry, contrib = slab(q, (i - n) * nslab - 1 - u, carry, None)
            acc = acc + contrib
        return carry, acc

    carry, acc = lax.fori_loop(0, i, body, (carry, acc))
    o_ref[0] = acc.astype(o_ref.dtype)


def _sb_attention(q, k, v):
    b, s, w = q.shape
    h = w // SB_HEAD_DIM
    jj = lax.broadcasted_iota(I32, (SLAB, SLAB), 0)
    ss = lax.broadcasted_iota(I32, (SLAB, SLAB), 1)
    mcat = jnp.where(jj > ss, 1.0, 0.0).astype(BF16)
    q_spec = pl.BlockSpec((1, TQ, SB_HEAD_DIM), lambda bi, hi, i: (bi, i, hi))
    kv_spec = pl.BlockSpec((1, s, SB_HEAD_DIM), lambda bi, hi, i: (bi, 0, hi))
    return pl.pallas_call(
        _sb_kernel,
        grid=(b, h, s // TQ),
        in_specs=[q_spec, kv_spec, kv_spec, _full(mcat.shape)],
        out_specs=q_spec,
        out_shape=jax.ShapeDtypeStruct((b, s, w), BF16),
        compiler_params=pltpu.CompilerParams(
            dimension_semantics=("parallel", "parallel", "arbitrary"),
            vmem_limit_bytes=VMEM_LIMIT),
        name="sb_attn",
    )(q, k, v, mcat)


def _mem_kv_kernel(m_ref, w_ref, k_ref, v_ref):
    kv = _dot(m_ref[0].astype(BF16), w_ref[...])
    half = kv.shape[1] // 2
    k_ref[0] = kv[:, :half].astype(BF16)
    v_ref[0] = kv[:, half:].astype(BF16)


def _mem_kv(mem, w_kv):
    b, m, d = mem.shape
    half = w_kv.shape[1] // 2
    out = jax.ShapeDtypeStruct((b, m, half), BF16)
    o_spec = pl.BlockSpec((1, m, half), lambda i: (i, 0, 0))
    return pl.pallas_call(
        _mem_kv_kernel,
        grid=(b,),
        in_specs=[pl.BlockSpec((1, m, d), lambda i: (i, 0, 0)), _full(w_kv.shape)],
        out_specs=[o_spec, o_spec],
        out_shape=[out, out],
        compiler_params=pltpu.CompilerParams(dimension_semantics=("parallel",),
                                             vmem_limit_bytes=VMEM_LIMIT),
        name="mem_kv",
    )(mem, w_kv)


def _mid_kernel(x_ref, lg_ref, lb_ref, oa_ref, yb_ref, gb_ref, wo_ref, l1g_ref, l1b_ref,
                wq_ref, km_ref, vm_ref, wmo_ref, l2g_ref, l2b_ref, wrh_ref, wrl_ref,
                wsg_ref, wsu_ref, wsd_ref,
                base_ref, x2_ref, lgt_ref):
    xln = _ln(x_ref[...], lg_ref[...], lb_ref[...])
    merged = oa_ref[...].astype(F32) + gb_ref[...].astype(F32) * yb_ref[...].astype(F32)
    x1 = _ln(ALPHA * xln + _dot(merged.astype(BF16), wo_ref[...]), l1g_ref[...], l1b_ref[...])

    q = (_dot(x1.astype(BF16), wq_ref[...]) * (MEM_HEAD_DIM ** -0.5)).astype(BF16)
    heads = []
    for h in range(MEM_HEADS):
        hs = slice(h * MEM_HEAD_DIM, (h + 1) * MEM_HEAD_DIM)
        logits = _dot_nt(q[:, hs], km_ref[0, :, hs])
        p = jnp.exp(logits - jnp.max(logits, axis=-1, keepdims=True))
        p = p / jnp.sum(p, axis=-1, keepdims=True)
        heads.append(_dot(p.astype(BF16), vm_ref[0, :, hs]))
    o = jnp.concatenate(heads, axis=1).astype(BF16)
    x2 = _ln(ALPHA * x1 + _dot(o, wmo_ref[...]), l2g_ref[...], l2b_ref[...])

    x2h = x2.astype(BF16)
    x2l = (x2 - x2h.astype(F32)).astype(BF16)
    lgt_ref[...] = (_dot_nt(wrh_ref[...], x2h) + _dot_nt(wrh_ref[...], x2l)
                    + _dot_nt(wrl_ref[...], x2h))
    x2_ref[...] = x2h

    hid = jax.nn.silu(_dot(x2h, wsg_ref[...])) * _dot(x2h, wsu_ref[...])
    base_ref[...] = ALPHA * x2 + _dot(hid.astype(BF16), wsd_ref[...])


def _mid(x2d, ln_g, ln_b, oa, yb, gb, w_out, l1g, l1b, w_q, k_mem, v_mem, w_mo, l2g, l2b,
         wr_hi, wr_lo, w_sg, w_su, w_sd, seq):
    t, d = x2d.shape
    tm = TM_PROJ
    per_batch = seq // tm
    row_spec = pl.BlockSpec((tm, d), lambda i: (i, 0))
    mem_spec = pl.BlockSpec((1,) + k_mem.shape[1:], lambda i: (i // per_batch, 0, 0))
    vec = _full((1, d))
    return pl.pallas_call(
        _mid_kernel,
        grid=(t // tm,),
        in_specs=[row_spec, vec, vec, row_spec, row_spec, row_spec, _full(w_out.shape), vec, vec,
                  _full(w_q.shape), mem_spec, mem_spec, _full(w_mo.shape), vec, vec,
                  _full(wr_hi.shape), _full(wr_lo.shape),
                  _full(w_sg.shape), _full(w_su.shape), _full(w_sd.shape)],
        out_specs=[row_spec, row_spec, pl.BlockSpec((N_EXPERTS, tm), lambda i: (0, i))],
        out_shape=[jax.ShapeDtypeStruct((t, d), F32), jax.ShapeDtypeStruct((t, d), BF16),
                   jax.ShapeDtypeStruct((N_EXPERTS, t), F32)],
        compiler_params=pltpu.CompilerParams(dimension_semantics=("parallel",),
                                             vmem_limit_bytes=VMEM_LIMIT),
        name="mid",
    )(x2d, ln_g, ln_b, oa, yb, gb, w_out, l1g, l1b, w_q, k_mem, v_mem, w_mo, l2g, l2b,
      wr_hi, wr_lo, w_sg, w_su, w_sd)


def _route_kernel(lgt_ref, bias_ref, sel_ref, w_ref, cnt_ref):
    e, tm = lgt_ref.shape
    per_group = e // N_GROUPS
    scores = jax.nn.sigmoid(lgt_ref[...])
    sel = scores + bias_ref[...]

    g3 = sel.reshape(N_GROUPS, per_group, tm)
    j3 = lax.broadcasted_iota(I32, g3.shape, 1)
    m1 = jnp.max(g3, axis=1, keepdims=True)
    first = jnp.min(jnp.where(g3 == m1, j3, per_group), axis=1, keepdims=True)
    m2 = jnp.max(jnp.where(j3 == first, -jnp.inf, g3), axis=1, keepdims=True)
    gs = (m1 + m2).reshape(N_GROUPS, tm)

    gi = lax.broadcasted_iota(I32, (N_GROUPS, tm), 0)
    grank = jnp.zeros((N_GROUPS, tm), I32)
    for o in range(N_GROUPS):
        other = gs[o:o + 1, :]
        grank += ((other > gs) | ((other == gs) & (o < gi))).astype(I32)
    gmask = (grank < TOPK_GROUPS).astype(F32)
    emask = jnp.broadcast_to(gmask.reshape(N_GROUPS, 1, tm), (N_GROUPS, per_group, tm)).reshape(e, tm)
    cand = jnp.where(emask > 0.5, sel, -jnp.inf)

    ei = lax.broadcasted_iota(I32, (e, tm), 0)
    erank = jnp.zeros((e, tm), I32)
    for o in range(e):
        other = cand[o:o + 1, :]
        erank += ((other > cand) | ((other == cand) & (o < ei))).astype(I32)
    chosen = erank < TOP_K

    w = jnp.where(chosen, scores, 0.0)
    w = w / jnp.sum(w, axis=0, keepdims=True) * ROUTED_SCALE
    chosen_f = chosen.astype(F32)
    sel_ref[...] = chosen_f
    w_ref[...] = w
    cnt_ref[0] = jnp.broadcast_to(jnp.sum(chosen_f, axis=1, keepdims=True), (e, LANES))


def _route(logits_t, bias_col):
    e, t = logits_t.shape
    tm = TM_MOE
    nb = t // tm
    blk = pl.BlockSpec((e, tm), lambda i: (0, i))
    return pl.pallas_call(
        _route_kernel,
        grid=(nb,),
        in_specs=[blk, _full((e, 1))],
        out_specs=[blk, blk, pl.BlockSpec((1, e, LANES), lambda i: (i, 0, 0))],
        out_shape=[jax.ShapeDtypeStruct((e, t), F32), jax.ShapeDtypeStruct((e, t), F32),
                   jax.ShapeDtypeStruct((nb, e, LANES), F32)],
        compiler_params=pltpu.CompilerParams(dimension_semantics=("parallel",),
                                             vmem_limit_bytes=VMEM_LIMIT),
        name="route",
    )(logits_t, bias_col)


def _pack_rows(x):
    h = x.shape[1] // 2
    hi = lax.bitcast_convert_type(x[:, :h], U32) & jnp.uint32(0xFFFF0000)
    lo = lax.shift_right_logical(lax.bitcast_convert_type(x[:, h:], U32), jnp.uint32(16))
    return hi | lo


def _unpack_rows(u):
    hi = lax.bitcast_convert_type(u & jnp.uint32(0xFFFF0000), F32)
    lo = lax.bitcast_convert_type(lax.shift_left(u, jnp.uint32(16)), F32)
    return jnp.concatenate([hi, lo], axis=1)


def _row_tables(sel, loc_col):
    e, tm = sel.shape
    tj = lax.broadcasted_iota(I32, (tm, tm), 0)
    tt = lax.broadcasted_iota(I32, (tm, tm), 1)
    before_t = jnp.where(tj < tt, 1.0, 0.0).astype(BF16)
    rank = _dot(sel.astype(BF16), before_t)
    pos = jnp.where(sel > 0.5, loc_col + rank, -1.0)
    hi = jnp.floor(pos * (1.0 / ROW_SPLIT)) * ROW_SPLIT
    return jnp.concatenate([hi, pos - hi], axis=0).astype(BF16)


def _row_owner(c, rchunk, loc_row, cnt_row):
    e = loc_row.shape[1]
    ri = (lax.broadcasted_iota(I32, (rchunk, e), 0) + c * rchunk).astype(F32)
    return jnp.where((ri >= loc_row) & (ri < loc_row + cnt_row), 1.0, 0.0).astype(BF16)


def _segment_copy(src, src_row, dst, dst_row, rows, sem):
    rows = pl.multiple_of(rows, GRAN)
    return pltpu.make_async_copy(src.at[pl.ds(pl.multiple_of(src_row, GRAN), rows)],
                                 dst.at[pl.ds(pl.multiple_of(dst_row, GRAN), rows)], sem)


def _dispatch_kernel(loc_ref, glob_ref, rows_ref, used_ref, tail_lo_ref, tail_hi_ref,
                     x_ref, sel_ref, locc_ref, locr_ref, cntr_ref, xs_hbm, xs_loc, zeros, sems):
    b = pl.program_id(0)
    nb = pl.num_programs(0)
    e, tm = sel_ref.shape
    r_loc = xs_loc.shape[1]
    slot = b % 2
    buf = xs_loc.at[slot]
    sem = sems.at[slot]

    def wait_block(blk, s):
        @pl.when(used_ref[blk] > 0)
        def _():
            _segment_copy(xs_loc.at[s], 0, xs_hbm, 0, used_ref[blk], sems.at[s]).wait()

    @pl.when(b >= 2)
    def _():
        wait_block(b - 2, slot)

    tables = _row_tables(sel_ref[...], locc_ref[0])
    x = x_ref[...]
    rchunk = tm
    for c in range(r_loc // rchunk):
        own = _row_owner(c, rchunk, locr_ref[0], cntr_ref[0])
        prow = _dot(jnp.concatenate([own, own], axis=1), tables)
        ri = (lax.broadcasted_iota(I32, (rchunk, tm), 0) + c * rchunk).astype(F32)
        onehot = jnp.where(prow == ri, 1.0, 0.0).astype(BF16)
        buf[c * rchunk:(c + 1) * rchunk, :] = _pack_rows(_dot(onehot, x))

    def per_expert(ex, _):
        n = rows_ref[b * e + ex]

        @pl.when(n > 0)
        def _():
            _segment_copy(buf, loc_ref[b * e + ex], xs_hbm, glob_ref[b * e + ex], n, sem).start()

        return 0

    lax.fori_loop(0, e, per_expert, 0)

    @pl.when(b == nb - 1)
    def _():
        @pl.when(b >= 1)
        def _():
            wait_block(b - 1, 1 - slot)

        wait_block(b, slot)
        zeros[...] = jnp.zeros_like(zeros)
        zrows = zeros.shape[0]

        def each_piece(act):
            def per_tail(ex, _):
                lo = tail_lo_ref[ex]
                n = tail_hi_ref[ex] - lo

                def per_piece(j, _):
                    rows = jnp.minimum(n - j * zrows, zrows)
                    act(_segment_copy(zeros, 0, xs_hbm, lo + j * zrows, rows, sem))
                    return 0

                lax.fori_loop(0, (n + zrows - 1) // zrows, per_piece, 0)
                return 0

            lax.fori_loop(0, tail_lo_ref.shape[0], per_tail, 0)

        each_piece(lambda cp: cp.start())
        each_piece(lambda cp: cp.wait())


def _dispatch(x2b, sel_t, lay):
    t, d = x2b.shape
    e = sel_t.shape[0]
    tm = TM_MOE
    nb = t // tm
    r_loc = tm * TOP_K + e * GRAN
    col = pl.BlockSpec((1, e, 1), lambda i, *_: (i, 0, 0))
    rowv = pl.BlockSpec((1, 1, e), lambda i, *_: (i, 0, 0))
    grid_spec = pltpu.PrefetchScalarGridSpec(
        num_scalar_prefetch=6,
        grid=(nb,),
        in_specs=[pl.BlockSpec((tm, d), lambda i, *_: (i, 0)),
                  pl.BlockSpec((e, tm), lambda i, *_: (0, i)), col, rowv, rowv],
        out_specs=pl.BlockSpec(memory_space=pl.ANY),
        scratch_shapes=[pltpu.VMEM((2, r_loc, d // 2), U32), pltpu.VMEM((FFN_TILE, d // 2), U32),
                        pltpu.SemaphoreType.DMA((2,))],
    )
    return pl.pallas_call(
        _dispatch_kernel,
        grid_spec=grid_spec,
        out_shape=jax.ShapeDtypeStruct((lay["r_glob"], d // 2), U32),
        compiler_params=pltpu.CompilerParams(dimension_semantics=("arbitrary",),
                                             vmem_limit_bytes=VMEM_LIMIT),
        name="dispatch",
    )(lay["loc_flat"], lay["glob_flat"], lay["rows_flat"], lay["used"], lay["tail_lo"],
      lay["tail_hi"], x2b, sel_t, lay["loc_col"], lay["loc_row"], lay["cnt_row"])


def _experts_kernel(tile_e_ref, n_used_ref, xs_ref, wg_ref, wu_ref, wd_ref, ys_ref,
                    wg_b, wu_b, wd_b):
    i = pl.program_id(0)
    prev = tile_e_ref[jnp.maximum(i - 1, 0)]

    @pl.when((i == 0) | (tile_e_ref[i] != prev))
    def _():
        wg_b[...] = wg_ref[0].astype(BF16)
        wu_b[...] = wu_ref[0].astype(BF16)
        wd_b[...] = wd_ref[0].astype(BF16)

    @pl.when(i < n_used_ref[0])
    def _():
        x = _unpack_rows(xs_ref[...]).astype(BF16)
        hid = jax.nn.silu(_dot(x, wg_b[...])) * _dot(x, wu_b[...])
        y = _dot(hid.astype(BF16), wd_b[...])
        ys_ref[...] = _pack_rows(y.astype(BF16).astype(F32))

    @pl.when(i >= n_used_ref[0])
    def _():
        ys_ref[...] = jnp.zeros_like(ys_ref)


def _experts(xs, tile_e, n_used, w_gate, w_up, w_down):
    r_glob, dh = xs.shape
    d, hdim = w_gate.shape[1:]
    n_tiles = r_glob // FFN_TILE

    def row_map(i, tile_e_ref, n_used_ref):
        return (jnp.minimum(i, n_used_ref[0] - 1), 0)

    def w_map(i, tile_e_ref, n_used_ref):
        return (tile_e_ref[i], 0, 0)

    grid_spec = pltpu.PrefetchScalarGridSpec(
        num_scalar_prefetch=2,
        grid=(n_tiles,),
        in_specs=[pl.BlockSpec((FFN_TILE, dh), row_map),
                  pl.BlockSpec((1, d, hdim), w_map),
                  pl.BlockSpec((1, d, hdim), w_map),
                  pl.BlockSpec((1, hdim, d), w_map)],
        out_specs=pl.BlockSpec((FFN_TILE, dh), lambda i, *_: (i, 0)),
        scratch_shapes=[pltpu.VMEM((d, hdim), BF16), pltpu.VMEM((d, hdim), BF16),
                        pltpu.VMEM((hdim, d), BF16)],
    )
    return pl.pallas_call(
        _experts_kernel,
        grid_spec=grid_spec,
        out_shape=jax.ShapeDtypeStruct((r_glob, dh), U32),
        compiler_params=pltpu.CompilerParams(dimension_semantics=("arbitrary",),
                                             vmem_limit_bytes=VMEM_LIMIT),
        name="experts",
    )(tile_e, n_used, xs, w_gate, w_up, w_down)


def _combine_kernel(loc_ref, glob_ref, rows_ref, used_ref,
                    base_ref, sel_ref, w_ref, locc_ref, locr_ref, cntr_ref, g_ref, bta_ref, ys_hbm,
                    o_ref, ys_loc, sems):
    b = pl.program_id(0)
    nb = pl.num_programs(0)
    e, tm = sel_ref.shape
    r_loc = ys_loc.shape[1]
    slot = b % 2
    buf = ys_loc.at[slot]

    def fetch_block(blk, s):
        def per_expert(ex, _):
            n = rows_ref[blk * e + ex]

            @pl.when(n > 0)
            def _():
                _segment_copy(ys_hbm, glob_ref[blk * e + ex], ys_loc.at[s], loc_ref[blk * e + ex], n,
                              sems.at[s]).start()

            return 0

        lax.fori_loop(0, e, per_expert, 0)

    @pl.when(b == 0)
    def _():
        fetch_block(b, slot)

    @pl.when(b + 1 < nb)
    def _():
        fetch_block(b + 1, 1 - slot)

    tables = _row_tables(sel_ref[...], locc_ref[0])
    wb = w_ref[...].astype(BF16)
    used = used_ref[b]

    @pl.when(used > 0)
    def _():
        _segment_copy(ys_hbm, 0, buf, 0, used, sems.at[slot]).wait()

    acc = base_ref[...]
    rchunk = tm
    for c in range(r_loc // rchunk):
        own = _row_owner(c, rchunk, locr_ref[0], cntr_ref[0])
        prow = _dot(jnp.concatenate([own, own], axis=1), tables)
        wrow = _dot(own, wb)
        ri = (lax.broadcasted_iota(I32, (rchunk, tm), 0) + c * rchunk).astype(F32)
        wmat = jnp.where(prow == ri, wrow, 0.0).astype(BF16)
        rvalid = (lax.broadcasted_iota(I32, (rchunk, 1), 0) + c * rchunk) < used
        y = jnp.where(rvalid, _unpack_rows(buf[c * rchunk:(c + 1) * rchunk, :]), 0.0)
        acc = acc + _dot_tn(wmat, y.astype(BF16))
    o_ref[...] = _ln(acc, g_ref[...], bta_ref[...])


def _combine(base, sel_t, w_t, lay, ys, ln_g, ln_b):
    t, d = base.shape
    e = sel_t.shape[0]
    tm = TM_MOE
    nb = t // tm
    r_loc = tm * TOP_K + e * GRAN
    blk = pl.BlockSpec((e, tm), lambda i, *_: (0, i))
    row_spec = pl.BlockSpec((tm, d), lambda i, *_: (i, 0))
    vec = pl.BlockSpec((1, d), lambda i, *_: (0, 0))
    col = pl.BlockSpec((1, e, 1), lambda i, *_: (i, 0, 0))
    rowv = pl.BlockSpec((1, 1, e), lambda i, *_: (i, 0, 0))
    grid_spec = pltpu.PrefetchScalarGridSpec(
        num_scalar_prefetch=4,
        grid=(nb,),
        in_specs=[row_spec, blk, blk, col, rowv, rowv, vec, vec, pl.BlockSpec(memory_space=pl.ANY)],
        out_specs=row_spec,
        scratch_shapes=[pltpu.VMEM((2, r_loc, d // 2), U32), pltpu.SemaphoreType.DMA((2,))],
    )
    return pl.pallas_call(
        _combine_kernel,
        grid_spec=grid_spec,
        out_shape=jax.ShapeDtypeStruct((t, d), F32),
        compiler_params=pltpu.CompilerParams(dimension_semantics=("arbitrary",),
                                             vmem_limit_bytes=VMEM_LIMIT),
        name="combine",
    )(lay["loc_flat"], lay["glob_flat"], lay["rows_flat"], lay["used"],
      base, sel_t, w_t, lay["loc_col"], lay["loc_row"], lay["cnt_row"], ln_g, ln_b, ys)


def _round_up(x, m):
    return (x + m - 1) // m * m


def _moe_layout(cnt, t):
    nb, e = cnt.shape
    cnt_g = _round_up(cnt, GRAN)
    loc_off = jnp.cumsum(cnt_g, axis=1) - cnt_g
    used = jnp.sum(cnt_g, axis=1)
    gcnt = jnp.sum(cnt_g, axis=0)
    gpad = _round_up(gcnt, FFN_TILE)
    gend = jnp.cumsum(gpad)
    gstart = gend - gpad
    glob_off = gstart[None, :] + jnp.cumsum(cnt_g, axis=0) - cnt_g
    r_glob = _round_up(t * TOP_K + nb * e * (GRAN - 1) + e * (FFN_TILE - 1), FFN_TILE)
    n_tiles = r_glob // FFN_TILE
    n_used = (gend[-1] // FFN_TILE).astype(I32)
    tile_start = jnp.minimum(jnp.arange(n_tiles, dtype=I32), n_used - 1) * FFN_TILE
    tile_e = jnp.minimum(jnp.sum(gend[None, :] <= tile_start[:, None], axis=1), e - 1).astype(I32)
    return dict(loc_flat=loc_off.astype(I32).reshape(-1), glob_flat=glob_off.astype(I32).reshape(-1),
                rows_flat=cnt_g.astype(I32).reshape(-1), used=used.astype(I32),
                loc_col=loc_off.astype(F32)[:, :, None], loc_row=loc_off.astype(F32)[:, None, :],
                cnt_row=cnt_g.astype(F32)[:, None, :],
                tail_lo=jnp.append(gstart + gcnt, gend[-1]).astype(I32),
                tail_hi=jnp.append(gend, r_glob).astype(I32),
                tile_e=tile_e, n_used=n_used.reshape(1), r_glob=r_glob)


def kernel(x, mem, ln_in_g, ln_in_b, w_in, b_in, ln_v_g, ln_v_b, w_spatial, b_spatial, w_out,
           ln1_g, ln1_b, w_mem_q, w_mem_kv, w_mem_o, ln2_g, ln2_b, w_router, router_bias,
           w_exp_gate, w_exp_up, w_exp_down, w_sh_gate, w_sh_up, w_sh_down, ln3_g, ln3_b):
    bsz, seq, d = x.shape
    t = bsz * seq
    assert w_in.shape[0] == DEPTH
    x2d = x.reshape(t, d)
    row = lambda a: a.reshape(1, -1)

    oa, q, k, v, gb = _in_proj(x2d, row(ln_in_g), row(ln_in_b), w_in[0].astype(BF16), row(b_in[0]),
                               row(ln_v_g[0]), row(ln_v_b[0]), w_spatial[0], b_spatial[0].T)
    yb = _sb_attention(q.reshape(bsz, seq, d), k.reshape(bsz, seq, d), v.reshape(bsz, seq, d))
    k_mem, v_mem = _mem_kv(mem, w_mem_kv[0].astype(BF16))

    wr_t = w_router[0].T
    wr_hi = wr_t.astype(BF16)
    wr_lo = (wr_t - wr_hi.astype(F32)).astype(BF16)
    base, x2b, logits_t = _mid(
        x2d, row(ln_in_g), row(ln_in_b), oa, yb.reshape(t, d), gb, w_out[0].astype(BF16),
        row(ln1_g[0]), row(ln1_b[0]), w_mem_q[0].astype(BF16), k_mem, v_mem,
        w_mem_o[0].astype(BF16), row(ln2_g[0]), row(ln2_b[0]), wr_hi, wr_lo,
        w_sh_gate[0].astype(BF16), w_sh_up[0].astype(BF16), w_sh_down[0].astype(BF16), seq)

    sel_t, w_t, cnt = _route(logits_t, router_bias[0].reshape(-1, 1))
    lay = _moe_layout(cnt[:, :, 0].astype(I32), t)
    xs = _dispatch(x2b, sel_t, lay)
    ys = _experts(xs, lay["tile_e"], lay["n_used"], w_exp_gate[0], w_exp_up[0], w_exp_down[0])
    out = _combine(base, sel_t, w_t, lay, ys, row(ln3_g[0]), row(ln3_b[0]))
    return out.reshape(bsz, seq, d)
```

```python
import functools

import jax
import jax.numpy as jnp
from jax import lax
from jax.experimental import pallas as pl
from jax.experimental.pallas import tpu as pltpu

F32 = jnp.float32
BF16 = jnp.bfloat16
I32 = jnp.int32
U32 = jnp.uint32

LANES = 128
SUBLANES = 8
GRAN = SUBLANES

CHUNK = 128
A_GROUPS = 8
SB_HEADS = 8
SB_HEAD_DIM = 128
MEM_HEADS = 4
MEM_HEAD_DIM = 128
N_EXPERTS = 64
TOP_K = 8
N_GROUPS = 8
TOPK_GROUPS = 4
ROUTED_SCALE = 2.5
LN_EPS = 1e-5
DEPTH = 1
ALPHA = (2 * DEPTH) ** 0.25
LOG2E = 1.4426950408889634
SP_CLAMP = 64.0

TM_PROJ = 256
TQ = 1024
SLAB = 256
TM_MOE = 256
FFN_TILE = 512
ROW_SPLIT = 256.0
VMEM_LIMIT = 56 * 1024 * 1024


def _ln(x, g, b):
    mu = jnp.mean(x, axis=-1, keepdims=True)
    xc = x - mu
    var = jnp.mean(xc * xc, axis=-1, keepdims=True)
    return xc * lax.rsqrt(var + LN_EPS) * g + b


def _gelu(x):
    return 0.5 * x * (1.0 + lax.erf(x * (2.0 ** -0.5)))


def _dot(a, b):
    return jnp.dot(a, b, preferred_element_type=F32)


def _dot_nt(a, b):
    return lax.dot_general(a, b, (((1,), (1,)), ((), ())), preferred_element_type=F32)


def _dot_tn(a, b):
    return lax.dot_general(a, b, (((0,), (0,)), ((), ())), preferred_element_type=F32)


def _full(shape):
    n = len(shape)
    return pl.BlockSpec(shape, lambda *_: (0,) * n)


def _in_proj_kernel(x_ref, lg_ref, lb_ref, w_ref, b_ref, vg_ref, vb_ref, ws_ref, bst_ref,
                    oa_ref, q_ref, k_ref, v_ref, gb_ref):
    tm, d = x_ref.shape
    xb = _ln(x_ref[...], lg_ref[...], lb_ref[...]).astype(BF16)

    def seg(i):
        return _dot(xb, w_ref[:, i * d:(i + 1) * d]) + b_ref[:, i * d:(i + 1) * d]

    q_ref[...] = (seg(2) * (SB_HEAD_DIM ** -0.5 * LOG2E)).astype(BF16)
    k_ref[...] = seg(3).astype(BF16)
    v_ref[...] = seg(4).astype(BF16)
    gb_ref[...] = jax.nn.sigmoid(seg(6)).astype(BF16)

    gu = jax.nn.sigmoid(seg(5)) * _gelu(seg(0))
    vln = _ln(_gelu(seg(1)), vg_ref[...], vb_ref[...]).astype(BF16)
    gd = d // A_GROUPS
    row = lax.broadcasted_iota(I32, (CHUNK, CHUNK), 0)
    col = lax.broadcasted_iota(I32, (CHUNK, CHUNK), 1)
    for g in range(A_GROUPS):
        w = jnp.where(col <= row, ws_ref[g], 0.0).astype(BF16)
        bias = bst_ref[:, g:g + 1]
        for c in range(tm // CHUNK):
            rs = slice(c * CHUNK, (c + 1) * CHUNK)
            cs = slice(g * gd, (g + 1) * gd)
            mixed = _dot(w, vln[rs, cs]) + bias
            oa_ref[rs, cs] = (gu[rs, cs] * mixed).astype(BF16)


def _in_proj(x2d, ln_g, ln_b, w_in, b_in, vg, vb, w_s, b_st):
    t, d = x2d.shape
    n_in = w_in.shape[1]
    tm = TM_PROJ
    row_spec = pl.BlockSpec((tm, d), lambda i: (i, 0))
    out = jax.ShapeDtypeStruct((t, d), BF16)
    return pl.pallas_call(
        _in_proj_kernel,
        grid=(t // tm,),
        in_specs=[row_spec, _full((1, d)), _full((1, d)), _full((d, n_in)), _full((1, n_in)),
                  _full((1, d)), _full((1, d)), _full(w_s.shape), _full(b_st.shape)],
        out_specs=[row_spec] * 5,
        out_shape=[out] * 5,
        compiler_params=pltpu.CompilerParams(dimension_semantics=("parallel",),
                                             vmem_limit_bytes=VMEM_LIMIT),
        name="in_proj",
    )(x2d, ln_g, ln_b, w_in, b_in, vg, vb, w_s, b_st)


def _sb_kernel(q_ref, k_ref, v_ref, m_ref, o_ref):
    i = pl.program_id(2)
    tq = q_ref.shape[1]
    nslab = tq // SLAB
    q = q_ref[0]
    later = m_ref[...]

    def slab(qs, j, carry, mask):
        off = pl.multiple_of(j * SLAB, SLAB)
        z = _dot_nt(qs, k_ref[0, pl.ds(off, SLAB), :])
        sp = jnp.maximum(jnp.log(1.0 + jnp.exp2(jnp.minimum(z, SP_CLAMP))) * LOG2E, z)
        if mask is not None:
            sp = jnp.where(mask, sp, 0.0)
        cs = _dot(sp.astype(BF16), later)
        a = jnp.exp2(z - sp - cs - carry)
        if mask is not None:
            a = jnp.where(mask, a, 0.0)
        total = cs[:, 0:1] + sp[:, 0:1]
        return carry + total, _dot(a.astype(BF16), v_ref[0, pl.ds(off, SLAB), :])

    carry = jnp.zeros((tq, 1), F32)
    acc = jnp.zeros((tq, SB_HEAD_DIM), F32)
    for d in reversed(range(nslab)):
        r0 = d * SLAB
        rr = lax.broadcasted_iota(I32, (tq - r0, SLAB), 0)
        cc = lax.broadcasted_iota(I32, (tq - r0, SLAB), 1)
        c_new, contrib = slab(q[r0:], i * nslab + d, carry[r0:], cc < rr)
        a_new = acc[r0:] + contrib
        carry = jnp.concatenate([carry[:r0], c_new], axis=0) if r0 else c_new
        acc = jnp.concatenate([acc[:r0], a_new], axis=0) if r0 else a_new

    def body(n, ca):
        carry, acc = ca
        for u in range(nslab):
            carry, contrib = slab(q, (i - n) * nslab - 1 - u, carry, None)
            acc = acc + contrib
        return carry, acc

    carry, acc = lax.fori_loop(0, i, body, (carry, acc))
    o_ref[0] = acc.astype(o_ref.dtype)


def _sb_attention(q, k, v):
    b, s, w = q.shape
    h = w // SB_HEAD_DIM
    jj = lax.broadcasted_iota(I32, (SLAB, SLAB), 0)
    ss = lax.broadcasted_iota(I32, (SLAB, SLAB), 1)
    mcat = jnp.where(jj > ss, 1.0, 0.0).astype(BF16)
    q_spec = pl.BlockSpec((1, TQ, SB_HEAD_DIM), lambda bi, hi, i: (bi, i, hi))
    kv_spec = pl.BlockSpec((1, s, SB_HEAD_DIM), lambda bi, hi, i: (bi, 0, hi))
    return pl.pallas_call(
        _sb_kernel,
        grid=(b, h, s // TQ),
        in_specs=[q_spec, kv_spec, kv_spec, _full(mcat.shape)],
        out_specs=q_spec,
        out_shape=jax.ShapeDtypeStruct((b, s, w), BF16),
        compiler_params=pltpu.CompilerParams(
            dimension_semantics=("parallel", "parallel", "arbitrary"),
            vmem_limit_bytes=VMEM_LIMIT),
        name="sb_attn",
    )(q, k, v, mcat)


def _mem_kv_kernel(m_ref, w_ref, k_ref, v_ref):
    kv = _dot(m_ref[0].astype(BF16), w_ref[...])
    half = kv.shape[1] // 2
    k_ref[0] = kv[:, :half].astype(BF16)
    v_ref[0] = kv[:, half:].astype(BF16)


def _mem_kv(mem, w_kv):
    b, m, d = mem.shape
    half = w_kv.shape[1] // 2
    out = jax.ShapeDtypeStruct((b, m, half), BF16)
    o_spec = pl.BlockSpec((1, m, half), lambda i: (i, 0, 0))
    return pl.pallas_call(
        _mem_kv_kernel,
        grid=(b,),
        in_specs=[pl.BlockSpec((1, m, d), lambda i: (i, 0, 0)), _full(w_kv.shape)],
        out_specs=[o_spec, o_spec],
        out_shape=[out, out],
        compiler_params=pltpu.CompilerParams(dimension_semantics=("parallel",),
                                             vmem_limit_bytes=VMEM_LIMIT),
        name="mem_kv",
    )(mem, w_kv)


def _mid_kernel(x_ref, lg_ref, lb_ref, oa_ref, yb_ref, gb_ref, wo_ref, l1g_ref, l1b_ref,
                wq_ref, km_ref, vm_ref, wmo_ref, l2g_ref, l2b_ref, wrh_ref, wrl_ref,
                wsg_ref, wsu_ref, wsd_ref,
                base_ref, x2_ref, lgt_ref):
    xln = _ln(x_ref[...], lg_ref[...], lb_ref[...])
    merged = oa_ref[...].astype(F32) + gb_ref[...].astype(F32) * yb_ref[...].astype(F32)
    x1 = _ln(ALPHA * xln + _dot(merged.astype(BF16), wo_ref[...]), l1g_ref[...], l1b_ref[...])

    q = (_dot(x1.astype(BF16), wq_ref[...]) * (MEM_HEAD_DIM ** -0.5)).astype(BF16)
    heads = []
    for h in range(MEM_HEADS):
        hs = slice(h * MEM_HEAD_DIM, (h + 1) * MEM_HEAD_DIM)
        logits = _dot_nt(q[:, hs], km_ref[0, :, hs])
        p = jnp.exp(logits - jnp.max(logits, axis=-1, keepdims=True))
        p = p / jnp.sum(p, axis=-1, keepdims=True)
        heads.append(_dot(p.astype(BF16), vm_ref[0, :, hs]))
    o = jnp.concatenate(heads, axis=1).astype(BF16)
    x2 = _ln(ALPHA * x1 + _dot(o, wmo_ref[...]), l2g_ref[...], l2b_ref[...])

    x2h = x2.astype(BF16)
    x2l = (x2 - x2h.astype(F32)).astype(BF16)
    lgt_ref[...] = (_dot_nt(wrh_ref[...], x2h) + _dot_nt(wrh_ref[...], x2l)
                    + _dot_nt(wrl_ref[...], x2h))
    x2_ref[...] = x2h

    hid = jax.nn.silu(_dot(x2h, wsg_ref[...])) * _dot(x2h, wsu_ref[...])
    base_ref[...] = ALPHA * x2 + _dot(hid.astype(BF16), wsd_ref[...])


def _mid(x2d, ln_g, ln_b, oa, yb, gb, w_out, l1g, l1b, w_q, k_mem, v_mem, w_mo, l2g, l2b,
         wr_hi, wr_lo, w_sg, w_su, w_sd, seq):
    t, d = x2d.shape
    tm = TM_PROJ
    per_batch = seq // tm
    row_spec = pl.BlockSpec((tm, d), lambda i: (i, 0))
    mem_spec = pl.BlockSpec((1,) + k_mem.shape[1:], lambda i: (i // per_batch, 0, 0))
    vec = _full((1, d))
    return pl.pallas_call(
        _mid_kernel,
        grid=(t // tm,),
        in_specs=[row_spec, vec, vec, row_spec, row_spec, row_spec, _full(w_out.shape), vec, vec,
                  _full(w_q.shape), mem_spec, mem_spec, _full(w_mo.shape), vec, vec,
                  _full(wr_hi.shape), _full(wr_lo.shape),
                  _full(w_sg.shape), _full(w_su.shape), _full(w_sd.shape)],
        out_specs=[row_spec, row_spec, pl.BlockSpec((N_EXPERTS, tm), lambda i: (0, i))],
        out_shape=[jax.ShapeDtypeStruct((t, d), F32), jax.ShapeDtypeStruct((t, d), BF16),
                   jax.ShapeDtypeStruct((N_EXPERTS, t), F32)],
        compiler_params=pltpu.CompilerParams(dimension_semantics=("parallel",),
                                             vmem_limit_bytes=VMEM_LIMIT),
        name="mid",
    )(x2d, ln_g, ln_b, oa, yb, gb, w_out, l1g, l1b, w_q, k_mem, v_mem, w_mo, l2g, l2b,
      wr_hi, wr_lo, w_sg, w_su, w_sd)


def _route_kernel(lgt_ref, bias_ref, sel_ref, w_ref, cnt_ref):
    e, tm = lgt_ref.shape
    per_group = e // N_GROUPS
    scores = jax.nn.sigmoid(lgt_ref[...])
    sel = scores + bias_ref[...]

    g3 = sel.reshape(N_GROUPS, per_group, tm)
    j3 = lax.broadcasted_iota(I32, g3.shape, 1)
    m1 = jnp.max(g3, axis=1, keepdims=True)
    first = jnp.min(jnp.where(g3 == m1, j3, per_group), axis=1, keepdims=True)
    m2 = jnp.max(jnp.where(j3 == first, -jnp.inf, g3), axis=1, keepdims=True)
    gs = (m1 + m2).reshape(N_GROUPS, tm)

    gi = lax.broadcasted_iota(I32, (N_GROUPS, tm), 0)
    grank = jnp.zeros((N_GROUPS, tm), I32)
    for o in range(N_GROUPS):
        other = gs[o:o + 1, :]
        grank += ((other > gs) | ((other == gs) & (o < gi))).astype(I32)
    gmask = (grank < TOPK_GROUPS).astype(F32)
    emask = jnp.broadcast_to(gmask.reshape(N_GROUPS, 1, tm), (N_GROUPS, per_group, tm)).reshape(e, tm)
    cand = jnp.where(emask > 0.5, sel, -jnp.inf)

    ei = lax.broadcasted_iota(I32, (e, tm), 0)
    erank = jnp.zeros((e, tm), I32)
    for o in range(e):
        other = cand[o:o + 1, :]
        erank += ((other > cand) | ((other == cand) & (o < ei))).astype(I32)
    chosen = erank < TOP_K

    w = jnp.where(chosen, scores, 0.0)
    w = w / jnp.sum(w, axis=0, keepdims=True) * ROUTED_SCALE
    chosen_f = chosen.astype(F32)
    sel_ref[...] = chosen_f
    w_ref[...] = w
    cnt_ref[0] = jnp.broadcast_to(jnp.sum(chosen_f, axis=1, keepdims=True), (e, LANES))


def _route(logits_t, bias_col):
    e, t = logits_t.shape
    tm = TM_MOE
    nb = t // tm
    blk = pl.BlockSpec((e, tm), lambda i: (0, i))
    return pl.pallas_call(
        _route_kernel,
        grid=(nb,),
        in_specs=[blk, _full((e, 1))],
        out_specs=[blk, blk, pl.BlockSpec((1, e, LANES), lambda i: (i, 0, 0))],
        out_shape=[jax.ShapeDtypeStruct((e, t), F32), jax.ShapeDtypeStruct((e, t), F32),
                   jax.ShapeDtypeStruct((nb, e, LANES), F32)],
        compiler_params=pltpu.CompilerParams(dimension_semantics=("parallel",),
                                             vmem_limit_bytes=VMEM_LIMIT),
        name="route",
    )(logits_t, bias_col)


def _pack_rows(x):
    h = x.shape[1] // 2
    hi = lax.bitcast_convert_type(x[:, :h], U32) & jnp.uint32(0xFFFF0000)
    lo = lax.shift_right_logical(lax.bitcast_convert_type(x[:, h:], U32), jnp.uint32(16))
    return hi | lo


def _unpack_rows(u):
    hi = lax.bitcast_convert_type(u & jnp.uint32(0xFFFF0000), F32)
    lo = lax.bitcast_convert_type(lax.shift_left(u, jnp.uint32(16)), F32)
    return jnp.concatenate([hi, lo], axis=1)


def _row_tables(sel, loc_col):
    e, tm = sel.shape
    tj = lax.broadcasted_iota(I32, (tm, tm), 0)
    tt = lax.broadcasted_iota(I32, (tm, tm), 1)
    before_t = jnp.where(tj < tt, 1.0, 0.0).astype(BF16)
    rank = _dot(sel.astype(BF16), before_t)
    pos = jnp.where(sel > 0.5, loc_col + rank, -1.0)
    hi = jnp.floor(pos * (1.0 / ROW_SPLIT)) * ROW_SPLIT
    return jnp.concatenate([hi, pos - hi], axis=0).astype(BF16)


def _row_owner(c, rchunk, loc_row, cnt_row):
    e = loc_row.shape[1]
    ri = (lax.broadcasted_iota(I32, (rchunk, e), 0) + c * rchunk).astype(F32)
    return jnp.where((ri >= loc_row) & (ri < loc_row + cnt_row), 1.0, 0.0).astype(BF16)


def _segment_copy(src, src_row, dst, dst_row, rows, sem):
    rows = pl.multiple_of(rows, GRAN)
    return pltpu.make_async_copy(src.at[pl.ds(pl.multiple_of(src_row, GRAN), rows)],
                                 dst.at[pl.ds(pl.multiple_of(dst_row, GRAN), rows)], sem)


def _dispatch_kernel(loc_ref, glob_ref, rows_ref, used_ref, tail_lo_ref, tail_hi_ref,
                     x_ref, sel_ref, locc_ref, locr_ref, cntr_ref, xs_hbm, xs_loc, zeros, sems):
    b = pl.program_id(0)
    nb = pl.num_programs(0)
    e, tm = sel_ref.shape
    r_loc = xs_loc.shape[1]
    slot = b % 2
    buf = xs_loc.at[slot]
    sem = sems.at[slot]

    def wait_block(blk, s):
        @pl.when(used_ref[blk] > 0)
        def _():
            _segment_copy(xs_loc.at[s], 0, xs_hbm, 0, used_ref[blk], sems.at[s]).wait()

    @pl.when(b >= 2)
    def _():
        wait_block(b - 2, slot)

    tables = _row_tables(sel_ref[...], locc_ref[0])
    x = x_ref[...]
    own = _row_owner(0, r_loc, locr_ref[0], cntr_ref[0])
    prow = _dot(jnp.concatenate([own, own], axis=1), tables)
    ri = lax.broadcasted_iota(I32, (r_loc, tm), 0).astype(F32)
    onehot = jnp.where(prow == ri, 1.0, 0.0).astype(BF16)
    buf[...] = _pack_rows(_dot(onehot, x))

    def per_expert(ex, _):
        n = rows_ref[b * e + ex]

        @pl.when(n > 0)
        def _():
            _segment_copy(buf, loc_ref[b * e + ex], xs_hbm, glob_ref[b * e + ex], n, sem).start()

        return 0

    lax.fori_loop(0, e, per_expert, 0)

    @pl.when(b == nb - 1)
    def _():
        @pl.when(b >= 1)
        def _():
            wait_block(b - 1, 1 - slot)

        wait_block(b, slot)
        zeros[...] = jnp.zeros_like(zeros)
        zrows = zeros.shape[0]

        def each_piece(act):
            def per_tail(ex, _):
                lo = tail_lo_ref[ex]
                n = tail_hi_ref[ex] - lo

                def per_piece(j, _):
                    rows = jnp.minimum(n - j * zrows, zrows)
                    act(_segment_copy(zeros, 0, xs_hbm, lo + j * zrows, rows, sem))
                    return 0

                lax.fori_loop(0, (n + zrows - 1) // zrows, per_piece, 0)
                return 0

            lax.fori_loop(0, tail_lo_ref.shape[0], per_tail, 0)

        each_piece(lambda cp: cp.start())
        each_piece(lambda cp: cp.wait())


def _dispatch(x2b, sel_t, lay):
    t, d = x2b.shape
    e = sel_t.shape[0]
    tm = TM_MOE
    nb = t // tm
    r_loc = tm * TOP_K + e * GRAN
    col = pl.BlockSpec((1, e, 1), lambda i, *_: (i, 0, 0))
    rowv = pl.BlockSpec((1, 1, e), lambda i, *_: (i, 0, 0))
    grid_spec = pltpu.PrefetchScalarGridSpec(
        num_scalar_prefetch=6,
        grid=(nb,),
        in_specs=[pl.BlockSpec((tm, d), lambda i, *_: (i, 0)),
                  pl.BlockSpec((e, tm), lambda i, *_: (0, i)), col, rowv, rowv],
        out_specs=pl.BlockSpec(memory_space=pl.ANY),
        scratch_shapes=[pltpu.VMEM((2, r_loc, d // 2), U32), pltpu.VMEM((FFN_TILE, d // 2), U32),
                        pltpu.SemaphoreType.DMA((2,))],
    )
    return pl.pallas_call(
        _dispatch_kernel,
        grid_spec=grid_spec,
        out_shape=jax.ShapeDtypeStruct((lay["r_glob"], d // 2), U32),
        compiler_params=pltpu.CompilerParams(dimension_semantics=("arbitrary",),
                                             vmem_limit_bytes=VMEM_LIMIT),
        name="dispatch",
    )(lay["loc_flat"], lay["glob_flat"], lay["rows_flat"], lay["used"], lay["tail_lo"],
      lay["tail_hi"], x2b, sel_t, lay["loc_col"], lay["loc_row"], lay["cnt_row"])


def _experts_kernel(tile_e_ref, n_used_ref, next_e_ref, xs_ref, wg_hbm, wu_hbm, wd_hbm, ys_ref,
                    wg_f, wu_f, wd_f, wg_b, wu_b, wd_b, sems):
    i = pl.program_id(0)

    def weight_copies(ex):
        return (pltpu.make_async_copy(wg_hbm.at[ex], wg_f, sems.at[0]),
                pltpu.make_async_copy(wu_hbm.at[ex], wu_f, sems.at[1]),
                pltpu.make_async_copy(wd_hbm.at[ex], wd_f, sems.at[2]))

    @pl.when(i == 0)
    def _():
        for cp in weight_copies(tile_e_ref[0]):
            cp.start()

    prev = tile_e_ref[jnp.maximum(i - 1, 0)]

    @pl.when((i == 0) | (tile_e_ref[i] != prev))
    def _():
        for cp in weight_copies(tile_e_ref[i]):
            cp.wait()
        wg_b[...] = wg_f[...].astype(BF16)
        wu_b[...] = wu_f[...].astype(BF16)
        wd_b[...] = wd_f[...].astype(BF16)
        nxt = next_e_ref[i]

        @pl.when(nxt >= 0)
        def _():
            for cp in weight_copies(nxt):
                cp.start()

    @pl.when(i < n_used_ref[0])
    def _():
        x = _unpack_rows(xs_ref[...]).astype(BF16)
        hid = jax.nn.silu(_dot(x, wg_b[...])) * _dot(x, wu_b[...])
        y = _dot(hid.astype(BF16), wd_b[...])
        ys_ref[...] = _pack_rows(y.astype(BF16).astype(F32))

    @pl.when(i >= n_used_ref[0])
    def _():
        ys_ref[...] = jnp.zeros_like(ys_ref)


def _experts(xs, tile_e, n_used, next_e, w_gate, w_up, w_down):
    r_glob, dh = xs.shape
    d, hdim = w_gate.shape[1:]
    n_tiles = r_glob // FFN_TILE

    def row_map(i, tile_e_ref, n_used_ref, next_e_ref):
        return (jnp.minimum(i, n_used_ref[0] - 1), 0)

    hbm = pl.BlockSpec(memory_space=pl.ANY)
    grid_spec = pltpu.PrefetchScalarGridSpec(
        num_scalar_prefetch=3,
        grid=(n_tiles,),
        in_specs=[pl.BlockSpec((FFN_TILE, dh), row_map), hbm, hbm, hbm],
        out_specs=pl.BlockSpec((FFN_TILE, dh), lambda i, *_: (i, 0)),
        scratch_shapes=[pltpu.VMEM((d, hdim), F32), pltpu.VMEM((d, hdim), F32),
                        pltpu.VMEM((hdim, d), F32),
                        pltpu.VMEM((d, hdim), BF16), pltpu.VMEM((d, hdim), BF16),
                        pltpu.VMEM((hdim, d), BF16), pltpu.SemaphoreType.DMA((3,))],
    )
    return pl.pallas_call(
        _experts_kernel,
        grid_spec=grid_spec,
        out_shape=jax.ShapeDtypeStruct((r_glob, dh), U32),
        compiler_params=pltpu.CompilerParams(dimension_semantics=("arbitrary",),
                                             vmem_limit_bytes=VMEM_LIMIT),
        name="experts",
    )(tile_e, n_used, next_e, xs, w_gate, w_up, w_down)


def _combine_kernel(loc_ref, glob_ref, rows_ref, used_ref,
                    base_ref, sel_ref, w_ref, locc_ref, locr_ref, cntr_ref, g_ref, bta_ref, ys_hbm,
                    o_ref, ys_loc, sems):
    b = pl.program_id(0)
    nb = pl.num_programs(0)
    e, tm = sel_ref.shape
    r_loc = ys_loc.shape[1]
    slot = b % 2
    buf = ys_loc.at[slot]

    def fetch_block(blk, s):
        def per_expert(ex, _):
            n = rows_ref[blk * e + ex]

            @pl.when(n > 0)
            def _():
                _segment_copy(ys_hbm, glob_ref[blk * e + ex], ys_loc.at[s], loc_ref[blk * e + ex], n,
                              sems.at[s]).start()

            return 0

        lax.fori_loop(0, e, per_expert, 0)

    @pl.when(b == 0)
    def _():
        fetch_block(b, slot)

    @pl.when(b + 1 < nb)
    def _():
        fetch_block(b + 1, 1 - slot)

    tables = _row_tables(sel_ref[...], locc_ref[0])
    wb = w_ref[...].astype(BF16)
    used = used_ref[b]

    @pl.when(used > 0)
    def _():
        _segment_copy(ys_hbm, 0, buf, 0, used, sems.at[slot]).wait()

    own = _row_owner(0, r_loc, locr_ref[0], cntr_ref[0])
    prow = _dot(jnp.concatenate([own, own], axis=1), tables)
    wrow = _dot(own, wb)
    ri = lax.broadcasted_iota(I32, (r_loc, tm), 0).astype(F32)
    wmat = jnp.where(prow == ri, wrow, 0.0).astype(BF16)
    always = tm * TOP_K
    rvalid = (lax.broadcasted_iota(I32, (r_loc - always, 1), 0) + always) < used
    u = jnp.concatenate([buf[:always, :], jnp.where(rvalid, buf[always:, :], jnp.uint32(0))], axis=0)
    moe = _dot_tn(wmat, _unpack_rows(u).astype(BF16))
    o_ref[...] = _ln(base_ref[...] + moe, g_ref[...], bta_ref[...])


def _combine(base, sel_t, w_t, lay, ys, ln_g, ln_b):
    t, d = base.shape
    e = sel_t.shape[0]
    tm = TM_MOE
    nb = t // tm
    r_loc = tm * TOP_K + e * GRAN
    blk = pl.BlockSpec((e, tm), lambda i, *_: (0, i))
    row_spec = pl.BlockSpec((tm, d), lambda i, *_: (i, 0))
    vec = pl.BlockSpec((1, d), lambda i, *_: (0, 0))
    col = pl.BlockSpec((1, e, 1), lambda i, *_: (i, 0, 0))
    rowv = pl.BlockSpec((1, 1, e), lambda i, *_: (i, 0, 0))
    grid_spec = pltpu.PrefetchScalarGridSpec(
        num_scalar_prefetch=4,
        grid=(nb,),
        in_specs=[row_spec, blk, blk, col, rowv, rowv, vec, vec, pl.BlockSpec(memory_space=pl.ANY)],
        out_specs=row_spec,
        scratch_shapes=[pltpu.VMEM((2, r_loc, d // 2), U32), pltpu.SemaphoreType.DMA((2,))],
    )
    return pl.pallas_call(
        _combine_kernel,
        grid_spec=grid_spec,
        out_shape=jax.ShapeDtypeStruct((t, d), F32),
        compiler_params=pltpu.CompilerParams(dimension_semantics=("arbitrary",),
                                             vmem_limit_bytes=VMEM_LIMIT),
        name="combine",
    )(lay["loc_flat"], lay["glob_flat"], lay["rows_flat"], lay["used"],
      base, sel_t, w_t, lay["loc_col"], lay["loc_row"], lay["cnt_row"], ln_g, ln_b, ys)


def _round_up(x, m):
    return (x + m - 1) // m * m


def _moe_layout(cnt, t):
    nb, e = cnt.shape
    cnt_g = _round_up(cnt, GRAN)
    loc_off = jnp.cumsum(cnt_g, axis=1) - cnt_g
    used = jnp.sum(cnt_g, axis=1)
    gcnt = jnp.sum(cnt_g, axis=0)
    gpad = _round_up(gcnt, FFN_TILE)
    gend = jnp.cumsum(gpad)
    gstart = gend - gpad
    glob_off = gstart[None, :] + jnp.cumsum(cnt_g, axis=0) - cnt_g
    r_glob = _round_up(t * TOP_K + nb * e * (GRAN - 1) + e * (FFN_TILE - 1), FFN_TILE)
    n_tiles = r_glob // FFN_TILE
    n_used = (gend[-1] // FFN_TILE).astype(I32)
    tile_start = jnp.minimum(jnp.arange(n_tiles, dtype=I32), n_used - 1) * FFN_TILE
    tile_e = jnp.minimum(jnp.sum(gend[None, :] <= tile_start[:, None], axis=1), e - 1).astype(I32)
    owner_or_e = jnp.where(gpad > 0, jnp.arange(e, dtype=I32), e)
    later_owner = jnp.append(lax.cummin(owner_or_e, reverse=True)[1:], e)
    next_e = jnp.where(later_owner < e, later_owner, -1)[tile_e].astype(I32)
    return dict(next_e=next_e, loc_flat=loc_off.astype(I32).reshape(-1), glob_flat=glob_off.astype(I32).reshape(-1),
                rows_flat=cnt_g.astype(I32).reshape(-1), used=used.astype(I32),
                loc_col=loc_off.astype(F32)[:, :, None], loc_row=loc_off.astype(F32)[:, None, :],
                cnt_row=cnt_g.astype(F32)[:, None, :],
                tail_lo=jnp.append(gstart + gcnt, gend[-1]).astype(I32),
                tail_hi=jnp.append(gend, r_glob).astype(I32),
                tile_e=tile_e, n_used=n_used.reshape(1), r_glob=r_glob)


def kernel(x, mem, ln_in_g, ln_in_b, w_in, b_in, ln_v_g, ln_v_b, w_spatial, b_spatial, w_out,
           ln1_g, ln1_b, w_mem_q, w_mem_kv, w_mem_o, ln2_g, ln2_b, w_router, router_bias,
           w_exp_gate, w_exp_up, w_exp_down, w_sh_gate, w_sh_up, w_sh_down, ln3_g, ln3_b):
    bsz, seq, d = x.shape
    t = bsz * seq
    assert w_in.shape[0] == DEPTH
    x2d = x.reshape(t, d)
    row = lambda a: a.reshape(1, -1)

    oa, q, k, v, gb = _in_proj(x2d, row(ln_in_g), row(ln_in_b), w_in[0].astype(BF16), row(b_in[0]),
                               row(ln_v_g[0]), row(ln_v_b[0]), w_spatial[0], b_spatial[0].T)
    yb = _sb_attention(q.reshape(bsz, seq, d), k.reshape(bsz, seq, d), v.reshape(bsz, seq, d))
    k_mem, v_mem = _mem_kv(mem, w_mem_kv[0].astype(BF16))

    wr_t = w_router[0].T
    wr_hi = wr_t.astype(BF16)
    wr_lo = (wr_t - wr_hi.astype(F32)).astype(BF16)
    base, x2b, logits_t = _mid(
        x2d, row(ln_in_g), row(ln_in_b), oa, yb.reshape(t, d), gb, w_out[0].astype(BF16),
        row(ln1_g[0]), row(ln1_b[0]), w_mem_q[0].astype(BF16), k_mem, v_mem,
        w_mem_o[0].astype(BF16), row(ln2_g[0]), row(ln2_b[0]), wr_hi, wr_lo,
        w_sh_gate[0].astype(BF16), w_sh_up[0].astype(BF16), w_sh_down[0].astype(BF16), seq)

    sel_t, w_t, cnt = _route(logits_t, router_bias[0].reshape(-1, 1))
    lay = _moe_layout(cnt[:, :, 0].astype(I32), t)
    xs = _dispatch(x2b, sel_t, lay)
    ys = _experts(xs, lay["tile_e"], lay["n_used"], lay["next_e"], w_exp_gate[0], w_exp_up[0],
                  w_exp_down[0])
    out = _combine(base, sel_t, w_t, lay, ys, row(ln3_g[0]), row(ln3_b[0]))
    return out.reshape(bsz, seq, d)
```

```python
import functools

import jax
import jax.numpy as jnp
from jax import lax
from jax.experimental import pallas as pl
from jax.experimental.pallas import tpu as pltpu

F32 = jnp.float32
BF16 = jnp.bfloat16
I32 = jnp.int32
U32 = jnp.uint32

LANES = 128
SUBLANES = 8
GRAN = SUBLANES

CHUNK = 128
A_GROUPS = 8
SB_HEADS = 8
SB_HEAD_DIM = 128
MEM_HEADS = 4
MEM_HEAD_DIM = 128
N_EXPERTS = 64
TOP_K = 8
N_GROUPS = 8
TOPK_GROUPS = 4
ROUTED_SCALE = 2.5
LN_EPS = 1e-5
DEPTH = 1
ALPHA = (2 * DEPTH) ** 0.25
LOG2E = 1.4426950408889634
SP_CLAMP = 64.0

TM_PROJ = 512
TM_MID = 512
TQ = 1024
SLAB = 256
TM_MOE = 256
FFN_TILE = 512
ROW_SPLIT = 256.0
VMEM_LIMIT = 56 * 1024 * 1024


def _ln(x, g, b):
    mu = jnp.mean(x, axis=-1, keepdims=True)
    xc = x - mu
    var = jnp.mean(xc * xc, axis=-1, keepdims=True)
    return xc * lax.rsqrt(var + LN_EPS) * g + b


def _gelu(x):
    return 0.5 * x * (1.0 + lax.erf(x * (2.0 ** -0.5)))


def _dot(a, b):
    return jnp.dot(a, b, preferred_element_type=F32)


def _dot_nt(a, b):
    return lax.dot_general(a, b, (((1,), (1,)), ((), ())), preferred_element_type=F32)


def _dot_tn(a, b):
    return lax.dot_general(a, b, (((0,), (0,)), ((), ())), preferred_element_type=F32)


def _full(shape):
    n = len(shape)
    return pl.BlockSpec(shape, lambda *_: (0,) * n, pipeline_mode=pl.Buffered(1))


def _in_proj_kernel(x_ref, lg_ref, lb_ref, w_ref, b_ref, vg_ref, vb_ref, ws_ref, bst_ref,
                    oa_ref, q_ref, k_ref, v_ref, gb_ref):
    tm, d = x_ref.shape
    xb = _ln(x_ref[...], lg_ref[...], lb_ref[...]).astype(BF16)

    def seg(i):
        return _dot(xb, w_ref[:, i * d:(i + 1) * d]) + b_ref[:, i * d:(i + 1) * d]

    q_ref[...] = (seg(2) * (SB_HEAD_DIM ** -0.5 * LOG2E)).astype(BF16)
    k_ref[...] = seg(3).astype(BF16)
    v_ref[...] = seg(4).astype(BF16)
    gb_ref[...] = jax.nn.sigmoid(seg(6)).astype(BF16)

    gu = jax.nn.sigmoid(seg(5)) * _gelu(seg(0))
    vln = _ln(_gelu(seg(1)), vg_ref[...], vb_ref[...]).astype(BF16)
    gd = d // A_GROUPS
    row = lax.broadcasted_iota(I32, (CHUNK, CHUNK), 0)
    col = lax.broadcasted_iota(I32, (CHUNK, CHUNK), 1)
    for g in range(A_GROUPS):
        w = jnp.where(col <= row, ws_ref[g], 0.0).astype(BF16)
        bias = bst_ref[:, g:g + 1]
        for c in range(tm // CHUNK):
            rs = slice(c * CHUNK, (c + 1) * CHUNK)
            cs = slice(g * gd, (g + 1) * gd)
            mixed = _dot(w, vln[rs, cs]) + bias
            oa_ref[rs, cs] = (gu[rs, cs] * mixed).astype(BF16)


def _in_proj(x2d, ln_g, ln_b, w_in, b_in, vg, vb, w_s, b_st):
    t, d = x2d.shape
    n_in = w_in.shape[1]
    tm = TM_PROJ
    row_spec = pl.BlockSpec((tm, d), lambda i: (i, 0))
    out = jax.ShapeDtypeStruct((t, d), BF16)
    return pl.pallas_call(
        _in_proj_kernel,
        grid=(t // tm,),
        in_specs=[row_spec, _full((1, d)), _full((1, d)), _full((d, n_in)), _full((1, n_in)),
                  _full((1, d)), _full((1, d)), _full(w_s.shape), _full(b_st.shape)],
        out_specs=[row_spec] * 5,
        out_shape=[out] * 5,
        compiler_params=pltpu.CompilerParams(dimension_semantics=("parallel",),
                                             vmem_limit_bytes=VMEM_LIMIT),
        name="in_proj",
    )(x2d, ln_g, ln_b, w_in, b_in, vg, vb, w_s, b_st)


def _sb_kernel(q_ref, k_ref, v_ref, m_ref, o_ref):
    i = pl.program_id(2)
    tq = q_ref.shape[1]
    nslab = tq // SLAB
    q = q_ref[0]
    later = m_ref[...]

    def slab(qs, j, carry, mask):
        off = pl.multiple_of(j * SLAB, SLAB)
        z = _dot_nt(qs, k_ref[0, pl.ds(off, SLAB), :])
        sp = jnp.maximum(jnp.log(1.0 + jnp.exp2(jnp.minimum(z, SP_CLAMP))) * LOG2E, z)
        if mask is not None:
            sp = jnp.where(mask, sp, 0.0)
        cs = _dot(sp.astype(BF16), later)
        a = jnp.exp2(z - sp - cs - carry)
        if mask is not None:
            a = jnp.where(mask, a, 0.0)
        total = cs[:, 0:1] + sp[:, 0:1]
        return carry + total, _dot(a.astype(BF16), v_ref[0, pl.ds(off, SLAB), :])

    carry = jnp.zeros((tq, 1), F32)
    acc = jnp.zeros((tq, SB_HEAD_DIM), F32)
    for d in reversed(range(nslab)):
        r0 = d * SLAB
        rr = lax.broadcasted_iota(I32, (tq - r0, SLAB), 0)
        cc = lax.broadcasted_iota(I32, (tq - r0, SLAB), 1)
        c_new, contrib = slab(q[r0:], i * nslab + d, carry[r0:], cc < rr)
        a_new = acc[r0:] + contrib
        carry = jnp.concatenate([carry[:r0], c_new], axis=0) if r0 else c_new
        acc = jnp.concatenate([acc[:r0], a_new], axis=0) if r0 else a_new

    def body(n, ca):
        carry, acc = ca
        for u in range(nslab):
            carry, contrib = slab(q, (i - n) * nslab - 1 - u, carry, None)
            acc = acc + contrib
        return carry, acc

    carry, acc = lax.fori_loop(0, i, body, (carry, acc))
    o_ref[0] = acc.astype(o_ref.dtype)


def _sb_attention(q, k, v):
    b, s, w = q.shape
    h = w // SB_HEAD_DIM
    jj = lax.broadcasted_iota(I32, (SLAB, SLAB), 0)
    ss = lax.broadcasted_iota(I32, (SLAB, SLAB), 1)
    mcat = jnp.where(jj > ss, 1.0, 0.0).astype(BF16)
    q_spec = pl.BlockSpec((1, TQ, SB_HEAD_DIM), lambda bi, hi, i: (bi, i, hi))
    kv_spec = pl.BlockSpec((1, s, SB_HEAD_DIM), lambda bi, hi, i: (bi, 0, hi))
    return pl.pallas_call(
        _sb_kernel,
        grid=(b, h, s // TQ),
        in_specs=[q_spec, kv_spec, kv_spec, _full(mcat.shape)],
        out_specs=q_spec,
        out_shape=jax.ShapeDtypeStruct((b, s, w), BF16),
        compiler_params=pltpu.CompilerParams(
            dimension_semantics=("parallel", "parallel", "arbitrary"),
            vmem_limit_bytes=VMEM_LIMIT),
        name="sb_attn",
    )(q, k, v, mcat)


def _mem_kv_kernel(m_ref, w_ref, k_ref, v_ref):
    kv = _dot(m_ref[0].astype(BF16), w_ref[...])
    half = kv.shape[1] // 2
    k_ref[0] = kv[:, :half].astype(BF16)
    v_ref[0] = kv[:, half:].astype(BF16)


def _mem_kv(mem, w_kv):
    b, m, d = mem.shape
    half = w_kv.shape[1] // 2
    out = jax.ShapeDtypeStruct((b, m, half), BF16)
    o_spec = pl.BlockSpec((1, m, half), lambda i: (i, 0, 0))
    return pl.pallas_call(
        _mem_kv_kernel,
        grid=(b,),
        in_specs=[pl.BlockSpec((1, m, d), lambda i: (i, 0, 0)), _full(w_kv.shape)],
        out_specs=[o_spec, o_spec],
        out_shape=[out, out],
        compiler_params=pltpu.CompilerParams(dimension_semantics=("parallel",),
                                             vmem_limit_bytes=VMEM_LIMIT),
        name="mem_kv",
    )(mem, w_kv)


def _mid_kernel(x_ref, lg_ref, lb_ref, oa_ref, yb_ref, gb_ref, wo_ref, l1g_ref, l1b_ref,
                wq_ref, km_ref, vm_ref, wmo_ref, l2g_ref, l2b_ref, wrh_ref, wrl_ref,
                wsg_ref, wsu_ref, wsd_ref,
                base_ref, x2_ref, lgt_ref):
    xln = _ln(x_ref[...], lg_ref[...], lb_ref[...])
    merged = oa_ref[...].astype(F32) + gb_ref[...].astype(F32) * yb_ref[...].astype(F32)
    x1 = _ln(ALPHA * xln + _dot(merged.astype(BF16), wo_ref[...]), l1g_ref[...], l1b_ref[...])

    q = (_dot(x1.astype(BF16), wq_ref[...]) * (MEM_HEAD_DIM ** -0.5)).astype(BF16)
    heads = []
    for h in range(MEM_HEADS):
        hs = slice(h * MEM_HEAD_DIM, (h + 1) * MEM_HEAD_DIM)
        logits = _dot_nt(q[:, hs], km_ref[0, :, hs])
        p = jnp.exp(logits - jnp.max(logits, axis=-1, keepdims=True))
        p = p / jnp.sum(p, axis=-1, keepdims=True)
        heads.append(_dot(p.astype(BF16), vm_ref[0, :, hs]))
    o = jnp.concatenate(heads, axis=1).astype(BF16)
    x2 = _ln(ALPHA * x1 + _dot(o, wmo_ref[...]), l2g_ref[...], l2b_ref[...])

    x2h = x2.astype(BF16)
    x2l = (x2 - x2h.astype(F32)).astype(BF16)
    lgt_ref[...] = (_dot_nt(wrh_ref[...], x2h) + _dot_nt(wrh_ref[...], x2l)
                    + _dot_nt(wrl_ref[...], x2h))
    x2_ref[...] = x2h

    hid = jax.nn.silu(_dot(x2h, wsg_ref[...])) * _dot(x2h, wsu_ref[...])
    base_ref[...] = ALPHA * x2 + _dot(hid.astype(BF16), wsd_ref[...])


def _mid(x2d, ln_g, ln_b, oa, yb, gb, w_out, l1g, l1b, w_q, k_mem, v_mem, w_mo, l2g, l2b,
         wr_hi, wr_lo, w_sg, w_su, w_sd, seq):
    t, d = x2d.shape
    tm = TM_MID
    per_batch = seq // tm
    row_spec = pl.BlockSpec((tm, d), lambda i: (i, 0))
    mem_spec = pl.BlockSpec((1,) + k_mem.shape[1:], lambda i: (i // per_batch, 0, 0))
    vec = _full((1, d))
    return pl.pallas_call(
        _mid_kernel,
        grid=(t // tm,),
        in_specs=[row_spec, vec, vec, row_spec, row_spec, row_spec, _full(w_out.shape), vec, vec,
                  _full(w_q.shape), mem_spec, mem_spec, _full(w_mo.shape), vec, vec,
                  _full(wr_hi.shape), _full(wr_lo.shape),
                  _full(w_sg.shape), _full(w_su.shape), _full(w_sd.shape)],
        out_specs=[row_spec, row_spec, pl.BlockSpec((N_EXPERTS, tm), lambda i: (0, i))],
        out_shape=[jax.ShapeDtypeStruct((t, d), F32), jax.ShapeDtypeStruct((t, d), BF16),
                   jax.ShapeDtypeStruct((N_EXPERTS, t), F32)],
        compiler_params=pltpu.CompilerParams(dimension_semantics=("parallel",),
                                             vmem_limit_bytes=VMEM_LIMIT),
        name="mid",
    )(x2d, ln_g, ln_b, oa, yb, gb, w_out, l1g, l1b, w_q, k_mem, v_mem, w_mo, l2g, l2b,
      wr_hi, wr_lo, w_sg, w_su, w_sd)


def _route_kernel(lgt_ref, bias_ref, sel_ref, w_ref, cnt_ref):
    e, tm = lgt_ref.shape
    per_group = e // N_GROUPS
    scores = jax.nn.sigmoid(lgt_ref[...])
    sel = scores + bias_ref[...]

    g3 = sel.reshape(N_GROUPS, per_group, tm)
    j3 = lax.broadcasted_iota(I32, g3.shape, 1)
    m1 = jnp.max(g3, axis=1, keepdims=True)
    first = jnp.min(jnp.where(g3 == m1, j3, per_group), axis=1, keepdims=True)
    m2 = jnp.max(jnp.where(j3 == first, -jnp.inf, g3), axis=1, keepdims=True)
    gs = (m1 + m2).reshape(N_GROUPS, tm)

    gi = lax.broadcasted_iota(I32, (N_GROUPS, tm), 0)
    grank = jnp.zeros((N_GROUPS, tm), I32)
    for o in range(N_GROUPS):
        other = gs[o:o + 1, :]
        grank += ((other > gs) | ((other == gs) & (o < gi))).astype(I32)
    gmask = (grank < TOPK_GROUPS).astype(F32)
    emask = jnp.broadcast_to(gmask.reshape(N_GROUPS, 1, tm), (N_GROUPS, per_group, tm)).reshape(e, tm)
    cand = jnp.where(emask > 0.5, sel, -jnp.inf)

    ei = lax.broadcasted_iota(I32, (e, tm), 0)
    erank = jnp.zeros((e, tm), I32)
    for o in range(e):
        other = cand[o:o + 1, :]
        erank += ((other > cand) | ((other == cand) & (o < ei))).astype(I32)
    chosen = erank < TOP_K

    w = jnp.where(chosen, scores, 0.0)
    w = w / jnp.sum(w, axis=0, keepdims=True) * ROUTED_SCALE
    chosen_f = chosen.astype(F32)
    sel_ref[...] = chosen_f
    w_ref[...] = w
    cnt_ref[0] = jnp.broadcast_to(jnp.sum(chosen_f, axis=1, keepdims=True), (e, LANES))


def _route(logits_t, bias_col):
    e, t = logits_t.shape
    tm = TM_MOE
    nb = t // tm
    blk = pl.BlockSpec((e, tm), lambda i: (0, i))
    return pl.pallas_call(
        _route_kernel,
        grid=(nb,),
        in_specs=[blk, _full((e, 1))],
        out_specs=[blk, blk, pl.BlockSpec((1, e, LANES), lambda i: (i, 0, 0))],
        out_shape=[jax.ShapeDtypeStruct((e, t), F32), jax.ShapeDtypeStruct((e, t), F32),
                   jax.ShapeDtypeStruct((nb, e, LANES), F32)],
        compiler_params=pltpu.CompilerParams(dimension_semantics=("parallel",),
                                             vmem_limit_bytes=VMEM_LIMIT),
        name="route",
    )(logits_t, bias_col)


def _pack_rows(x):
    h = x.shape[1] // 2
    hi = lax.bitcast_convert_type(x[:, :h], U32) & jnp.uint32(0xFFFF0000)
    lo = lax.shift_right_logical(lax.bitcast_convert_type(x[:, h:], U32), jnp.uint32(16))
    return hi | lo


def _unpack_rows(u):
    hi = lax.bitcast_convert_type(u & jnp.uint32(0xFFFF0000), F32)
    lo = lax.bitcast_convert_type(lax.shift_left(u, jnp.uint32(16)), F32)
    return jnp.concatenate([hi, lo], axis=1)


def _row_tables(sel, loc_col):
    e, tm = sel.shape
    tj = lax.broadcasted_iota(I32, (tm, tm), 0)
    tt = lax.broadcasted_iota(I32, (tm, tm), 1)
    before_t = jnp.where(tj < tt, 1.0, 0.0).astype(BF16)
    rank = _dot(sel.astype(BF16), before_t)
    pos = jnp.where(sel > 0.5, loc_col + rank, -1.0)
    hi = jnp.floor(pos * (1.0 / ROW_SPLIT)) * ROW_SPLIT
    return jnp.concatenate([hi, pos - hi], axis=0).astype(BF16)


def _row_owner(c, rchunk, loc_row, cnt_row):
    e = loc_row.shape[1]
    ri = (lax.broadcasted_iota(I32, (rchunk, e), 0) + c * rchunk).astype(F32)
    return jnp.where((ri >= loc_row) & (ri < loc_row + cnt_row), 1.0, 0.0).astype(BF16)


def _segment_copy(src, src_row, dst, dst_row, rows, sem):
    rows = pl.multiple_of(rows, GRAN)
    return pltpu.make_async_copy(src.at[pl.ds(pl.multiple_of(src_row, GRAN), rows)],
                                 dst.at[pl.ds(pl.multiple_of(dst_row, GRAN), rows)], sem)


def _dispatch_kernel(loc_ref, glob_ref, rows_ref, used_ref, tail_lo_ref, tail_hi_ref,
                     x_ref, sel_ref, locc_ref, locr_ref, cntr_ref, xs_hbm, xs_loc, zeros, sems):
    b = pl.program_id(0)
    nb = pl.num_programs(0)
    e, tm = sel_ref.shape
    r_loc = xs_loc.shape[1]
    slot = b % 2
    buf = xs_loc.at[slot]
    sem = sems.at[slot]

    def wait_block(blk, s):
        @pl.when(used_ref[blk] > 0)
        def _():
            _segment_copy(xs_loc.at[s], 0, xs_hbm, 0, used_ref[blk], sems.at[s]).wait()

    @pl.when(b >= 2)
    def _():
        wait_block(b - 2, slot)

    tables = _row_tables(sel_ref[...], locc_ref[0])
    x = x_ref[...]
    own = _row_owner(0, r_loc, locr_ref[0], cntr_ref[0])
    prow = _dot(jnp.concatenate([own, own], axis=1), tables)
    ri = lax.broadcasted_iota(I32, (r_loc, tm), 0).astype(F32)
    onehot = jnp.where(prow == ri, 1.0, 0.0).astype(BF16)
    buf[...] = _pack_rows(_dot(onehot, x))

    def per_expert(ex, _):
        n = rows_ref[b * e + ex]

        @pl.when(n > 0)
        def _():
            _segment_copy(buf, loc_ref[b * e + ex], xs_hbm, glob_ref[b * e + ex], n, sem).start()

        return 0

    lax.fori_loop(0, e, per_expert, 0)

    @pl.when(b == nb - 1)
    def _():
        @pl.when(b >= 1)
        def _():
            wait_block(b - 1, 1 - slot)

        wait_block(b, slot)
        zeros[...] = jnp.zeros_like(zeros)
        zrows = zeros.shape[0]

        def each_piece(act):
            def per_tail(ex, _):
                lo = tail_lo_ref[ex]
                n = tail_hi_ref[ex] - lo

                def per_piece(j, _):
                    rows = jnp.minimum(n - j * zrows, zrows)
                    act(_segment_copy(zeros, 0, xs_hbm, lo + j * zrows, rows, sem))
                    return 0

                lax.fori_loop(0, (n + zrows - 1) // zrows, per_piece, 0)
                return 0

            lax.fori_loop(0, tail_lo_ref.shape[0], per_tail, 0)

        each_piece(lambda cp: cp.start())
        each_piece(lambda cp: cp.wait())


def _dispatch(x2b, sel_t, lay):
    t, d = x2b.shape
    e = sel_t.shape[0]
    tm = TM_MOE
    nb = t // tm
    r_loc = tm * TOP_K + e * GRAN
    col = pl.BlockSpec((1, e, 1), lambda i, *_: (i, 0, 0))
    rowv = pl.BlockSpec((1, 1, e), lambda i, *_: (i, 0, 0))
    grid_spec = pltpu.PrefetchScalarGridSpec(
        num_scalar_prefetch=6,
        grid=(nb,),
        in_specs=[pl.BlockSpec((tm, d), lambda i, *_: (i, 0)),
                  pl.BlockSpec((e, tm), lambda i, *_: (0, i)), col, rowv, rowv],
        out_specs=pl.BlockSpec(memory_space=pl.ANY),
        scratch_shapes=[pltpu.VMEM((2, r_loc, d // 2), U32), pltpu.VMEM((FFN_TILE, d // 2), U32),
                        pltpu.SemaphoreType.DMA((2,))],
    )
    return pl.pallas_call(
        _dispatch_kernel,
        grid_spec=grid_spec,
        out_shape=jax.ShapeDtypeStruct((lay["r_glob"], d // 2), U32),
        compiler_params=pltpu.CompilerParams(dimension_semantics=("arbitrary",),
                                             vmem_limit_bytes=VMEM_LIMIT),
        name="dispatch",
    )(lay["loc_flat"], lay["glob_flat"], lay["rows_flat"], lay["used"], lay["tail_lo"],
      lay["tail_hi"], x2b, sel_t, lay["loc_col"], lay["loc_row"], lay["cnt_row"])


def _experts_kernel(tile_e_ref, n_used_ref, next_e_ref, xs_ref, wg_hbm, wu_hbm, wd_hbm, ys_ref,
                    wg_f, wu_f, wd_f, wg_b, wu_b, wd_b, sems):
    i = pl.program_id(0)

    def weight_copies(ex):
        return (pltpu.make_async_copy(wg_hbm.at[ex], wg_f, sems.at[0]),
                pltpu.make_async_copy(wu_hbm.at[ex], wu_f, sems.at[1]),
                pltpu.make_async_copy(wd_hbm.at[ex], wd_f, sems.at[2]))

    def start_weights(ex):
        for cp in weight_copies(ex):
            cp.start(priority=1)

    @pl.when(i == 0)
    def _():
        start_weights(tile_e_ref[0])

    prev = tile_e_ref[jnp.maximum(i - 1, 0)]

    @pl.when((i == 0) | (tile_e_ref[i] != prev))
    def _():
        for cp in weight_copies(tile_e_ref[i]):
            cp.wait()
        wg_b[...] = wg_f[...].astype(BF16)
        wu_b[...] = wu_f[...].astype(BF16)
        wd_b[...] = wd_f[...].astype(BF16)
        nxt = next_e_ref[i]

        @pl.when(nxt >= 0)
        def _():
            start_weights(nxt)

    @pl.when(i < n_used_ref[0])
    def _():
        x = _unpack_rows(xs_ref[...]).astype(BF16)
        hid = jax.nn.silu(_dot(x, wg_b[...])) * _dot(x, wu_b[...])
        y = _dot(hid.astype(BF16), wd_b[...])
        ys_ref[...] = _pack_rows(y.astype(BF16).astype(F32))

    @pl.when(i >= n_used_ref[0])
    def _():
        ys_ref[...] = jnp.zeros_like(ys_ref)


def _experts(xs, tile_e, n_used, next_e, w_gate, w_up, w_down):
    r_glob, dh = xs.shape
    d, hdim = w_gate.shape[1:]
    n_tiles = r_glob // FFN_TILE

    def row_map(i, tile_e_ref, n_used_ref, next_e_ref):
        return (jnp.minimum(i, n_used_ref[0] - 1), 0)

    hbm = pl.BlockSpec(memory_space=pl.ANY)
    grid_spec = pltpu.PrefetchScalarGridSpec(
        num_scalar_prefetch=3,
        grid=(n_tiles,),
        in_specs=[pl.BlockSpec((FFN_TILE, dh), row_map), hbm, hbm, hbm],
        out_specs=pl.BlockSpec((FFN_TILE, dh), lambda i, *_: (i, 0)),
        scratch_shapes=[pltpu.VMEM((d, hdim), F32), pltpu.VMEM((d, hdim), F32),
                        pltpu.VMEM((hdim, d), F32),
                        pltpu.VMEM((d, hdim), BF16), pltpu.VMEM((d, hdim), BF16),
                        pltpu.VMEM((hdim, d), BF16), pltpu.SemaphoreType.DMA((3,))],
    )
    return pl.pallas_call(
        _experts_kernel,
        grid_spec=grid_spec,
        out_shape=jax.ShapeDtypeStruct((r_glob, dh), U32),
        compiler_params=pltpu.CompilerParams(dimension_semantics=("arbitrary",),
                                             vmem_limit_bytes=VMEM_LIMIT),
        name="experts",
    )(tile_e, n_used, next_e, xs, w_gate, w_up, w_down)


def _combine_kernel(loc_ref, glob_ref, rows_ref, used_ref,
                    base_ref, sel_ref, w_ref, locc_ref, locr_ref, cntr_ref, g_ref, bta_ref, ys_hbm,
                    o_ref, ys_loc, sems):
    b = pl.program_id(0)
    nb = pl.num_programs(0)
    e, tm = sel_ref.shape
    r_loc = ys_loc.shape[1]
    slot = b % 2
    buf = ys_loc.at[slot]

    def fetch_block(blk, s):
        def per_expert(ex, _):
            n = rows_ref[blk * e + ex]

            @pl.when(n > 0)
            def _():
                _segment_copy(ys_hbm, glob_ref[blk * e + ex], ys_loc.at[s], loc_ref[blk * e + ex], n,
                              sems.at[s]).start()

            return 0

        lax.fori_loop(0, e, per_expert, 0)

    @pl.when(b == 0)
    def _():
        fetch_block(b, slot)

    @pl.when(b + 1 < nb)
    def _():
        fetch_block(b + 1, 1 - slot)

    tables = _row_tables(sel_ref[...], locc_ref[0])
    wb = w_ref[...].astype(BF16)
    used = used_ref[b]

    @pl.when(used > 0)
    def _():
        _segment_copy(ys_hbm, 0, buf, 0, used, sems.at[slot]).wait()

    own = _row_owner(0, r_loc, locr_ref[0], cntr_ref[0])
    prow = _dot(jnp.concatenate([own, own], axis=1), tables)
    wrow = _dot(own, wb)
    ri = lax.broadcasted_iota(I32, (r_loc, tm), 0).astype(F32)
    wmat = jnp.where(prow == ri, wrow, 0.0).astype(BF16)
    always = tm * TOP_K
    rvalid = (lax.broadcasted_iota(I32, (r_loc - always, 1), 0) + always) < used
    u = jnp.concatenate([buf[:always, :], jnp.where(rvalid, buf[always:, :], jnp.uint32(0))], axis=0)
    moe = _dot_tn(wmat, _unpack_rows(u).astype(BF16))
    o_ref[...] = _ln(base_ref[...] + moe, g_ref[...], bta_ref[...])


def _combine(base, sel_t, w_t, lay, ys, ln_g, ln_b):
    t, d = base.shape
    e = sel_t.shape[0]
    tm = TM_MOE
    nb = t // tm
    r_loc = tm * TOP_K + e * GRAN
    blk = pl.BlockSpec((e, tm), lambda i, *_: (0, i))
    row_spec = pl.BlockSpec((tm, d), lambda i, *_: (i, 0))
    vec = pl.BlockSpec((1, d), lambda i, *_: (0, 0))
    col = pl.BlockSpec((1, e, 1), lambda i, *_: (i, 0, 0))
    rowv = pl.BlockSpec((1, 1, e), lambda i, *_: (i, 0, 0))
    grid_spec = pltpu.PrefetchScalarGridSpec(
        num_scalar_prefetch=4,
        grid=(nb,),
        in_specs=[row_spec, blk, blk, col, rowv, rowv, vec, vec, pl.BlockSpec(memory_space=pl.ANY)],
        out_specs=row_spec,
        scratch_shapes=[pltpu.VMEM((2, r_loc, d // 2), U32), pltpu.SemaphoreType.DMA((2,))],
    )
    return pl.pallas_call(
        _combine_kernel,
        grid_spec=grid_spec,
        out_shape=jax.ShapeDtypeStruct((t, d), F32),
        compiler_params=pltpu.CompilerParams(dimension_semantics=("arbitrary",),
                                             vmem_limit_bytes=VMEM_LIMIT),
        name="combine",
    )(lay["loc_flat"], lay["glob_flat"], lay["rows_flat"], lay["used"],
      base, sel_t, w_t, lay["loc_col"], lay["loc_row"], lay["cnt_row"], ln_g, ln_b, ys)


def _round_up(x, m):
    return (x + m - 1) // m * m


def _moe_layout(cnt, t):
    nb, e = cnt.shape
    cnt_g = _round_up(cnt, GRAN)
    loc_off = jnp.cumsum(cnt_g, axis=1) - cnt_g
    used = jnp.sum(cnt_g, axis=1)
    gcnt = jnp.sum(cnt_g, axis=0)
    gpad = _round_up(gcnt, FFN_TILE)
    gend = jnp.cumsum(gpad)
    gstart = gend - gpad
    glob_off = gstart[None, :] + jnp.cumsum(cnt_g, axis=0) - cnt_g
    r_glob = _round_up(t * TOP_K + nb * e * (GRAN - 1) + e * (FFN_TILE - 1), FFN_TILE)
    n_tiles = r_glob // FFN_TILE
    n_used = (gend[-1] // FFN_TILE).astype(I32)
    tile_start = jnp.minimum(jnp.arange(n_tiles, dtype=I32), n_used - 1) * FFN_TILE
    tile_e = jnp.minimum(jnp.sum(gend[None, :] <= tile_start[:, None], axis=1), e - 1).astype(I32)
    ids = jnp.arange(e, dtype=I32)
    later_owner = jnp.min(jnp.where((ids[None, :] > ids[:, None]) & (gpad[None, :] > 0), ids[None, :], e),
                          axis=1)
    next_e = jnp.where(later_owner < e, later_owner, -1)[tile_e].astype(I32)
    return dict(next_e=next_e, loc_flat=loc_off.astype(I32).reshape(-1), glob_flat=glob_off.astype(I32).reshape(-1),
                rows_flat=cnt_g.astype(I32).reshape(-1), used=used.astype(I32),
                loc_col=loc_off.astype(F32)[:, :, None], loc_row=loc_off.astype(F32)[:, None, :],
                cnt_row=cnt_g.astype(F32)[:, None, :],
                tail_lo=jnp.append(gstart + gcnt, gend[-1]).astype(I32),
                tail_hi=jnp.append(gend, r_glob).astype(I32),
                tile_e=tile_e, n_used=n_used.reshape(1), r_glob=r_glob)


def kernel(x, mem, ln_in_g, ln_in_b, w_in, b_in, ln_v_g, ln_v_b, w_spatial, b_spatial, w_out,
           ln1_g, ln1_b, w_mem_q, w_mem_kv, w_mem_o, ln2_g, ln2_b, w_router, router_bias,
           w_exp_gate, w_exp_up, w_exp_down, w_sh_gate, w_sh_up, w_sh_down, ln3_g, ln3_b):
    bsz, seq, d = x.shape
    t = bsz * seq
    assert w_in.shape[0] == DEPTH
    x2d = x.reshape(t, d)
    row = lambda a: a.reshape(1, -1)

    oa, q, k, v, gb = _in_proj(x2d, row(ln_in_g), row(ln_in_b), w_in[0].astype(BF16), row(b_in[0]),
                               row(ln_v_g[0]), row(ln_v_b[0]), w_spatial[0], b_spatial[0].T)
    yb = _sb_attention(q.reshape(bsz, seq, d), k.reshape(bsz, seq, d), v.reshape(bsz, seq, d))
    k_mem, v_mem = _mem_kv(mem, w_mem_kv[0].astype(BF16))

    wr_t = w_router[0].T
    wr_hi = wr_t.astype(BF16)
    wr_lo = (wr_t - wr_hi.astype(F32)).astype(BF16)
    base, x2b, logits_t = _mid(
        x2d, row(ln_in_g), row(ln_in_b), oa, yb.reshape(t, d), gb, w_out[0].astype(BF16),
        row(ln1_g[0]), row(ln1_b[0]), w_mem_q[0].astype(BF16), k_mem, v_mem,
        w_mem_o[0].astype(BF16), row(ln2_g[0]), row(ln2_b[0]), wr_hi, wr_lo,
        w_sh_gate[0].astype(BF16), w_sh_up[0].astype(BF16), w_sh_down[0].astype(BF16), seq)

    sel_t, w_t, cnt = _route(logits_t, router_bias[0].reshape(-1, 1))
    lay = _moe_layout(cnt[:, :, 0].astype(I32), t)
    xs = _dispatch(x2b, sel_t, lay)
    ys = _experts(xs, lay["tile_e"], lay["n_used"], lay["next_e"], w_exp_gate[0], w_exp_up[0],
                  w_exp_down[0])
    out = _combine(base, sel_t, w_t, lay, ys, row(ln3_g[0]), row(ln3_b[0]))
    return out.reshape(bsz, seq, d)
```

```python
import functools

import jax
import jax.numpy as jnp
from jax import lax
from jax.experimental import pallas as pl
from jax.experimental.pallas import tpu as pltpu

F32 = jnp.float32
BF16 = jnp.bfloat16
I32 = jnp.int32
U32 = jnp.uint32

LANES = 128
SUBLANES = 8
GRAN = SUBLANES

CHUNK = 128
A_GROUPS = 8
SB_HEADS = 8
SB_HEAD_DIM = 128
MEM_HEADS = 4
MEM_HEAD_DIM = 128
N_EXPERTS = 64
TOP_K = 8
N_GROUPS = 8
TOPK_GROUPS = 4
ROUTED_SCALE = 2.5
LN_EPS = 1e-5
DEPTH = 1
ALPHA = (2 * DEPTH) ** 0.25
LOG2E = 1.4426950408889634
SP_CLAMP = 64.0

TM_PROJ = 512
TM_MID = 512
TQ = 1024
SLAB = 256
TM_MOE = 256
FFN_TILE = 512
ROW_SLOTS = 3
ROW_SPLIT = 256.0
VMEM_LIMIT = 56 * 1024 * 1024


def _ln(x, g, b):
    mu = jnp.mean(x, axis=-1, keepdims=True)
    xc = x - mu
    var = jnp.mean(xc * xc, axis=-1, keepdims=True)
    return xc * lax.rsqrt(var + LN_EPS) * g + b


def _gelu(x):
    return 0.5 * x * (1.0 + lax.erf(x * (2.0 ** -0.5)))


def _dot(a, b):
    return jnp.dot(a, b, preferred_element_type=F32)


def _dot_nt(a, b):
    return lax.dot_general(a, b, (((1,), (1,)), ((), ())), preferred_element_type=F32)


def _dot_tn(a, b):
    return lax.dot_general(a, b, (((0,), (0,)), ((), ())), preferred_element_type=F32)


def _full(shape):
    n = len(shape)
    return pl.BlockSpec(shape, lambda *_: (0,) * n, pipeline_mode=pl.Buffered(1))


def _in_proj_kernel(x_ref, lg_ref, lb_ref, w_ref, b_ref, vg_ref, vb_ref, ws_ref, bst_ref,
                    oa_ref, q_ref, k_ref, v_ref, gb_ref):
    tm, d = x_ref.shape
    xb = _ln(x_ref[...], lg_ref[...], lb_ref[...]).astype(BF16)

    def seg(i):
        return _dot(xb, w_ref[:, i * d:(i + 1) * d]) + b_ref[:, i * d:(i + 1) * d]

    q_ref[...] = (seg(2) * (SB_HEAD_DIM ** -0.5 * LOG2E)).astype(BF16)
    k_ref[...] = seg(3).astype(BF16)
    v_ref[...] = seg(4).astype(BF16)
    gb_ref[...] = jax.nn.sigmoid(seg(6)).astype(BF16)

    gu = jax.nn.sigmoid(seg(5)) * _gelu(seg(0))
    vln = _ln(_gelu(seg(1)), vg_ref[...], vb_ref[...]).astype(BF16)
    gd = d // A_GROUPS
    row = lax.broadcasted_iota(I32, (CHUNK, CHUNK), 0)
    col = lax.broadcasted_iota(I32, (CHUNK, CHUNK), 1)
    for g in range(A_GROUPS):
        w = jnp.where(col <= row, ws_ref[g], 0.0).astype(BF16)
        bias = bst_ref[:, g:g + 1]
        for c in range(tm // CHUNK):
            rs = slice(c * CHUNK, (c + 1) * CHUNK)
            cs = slice(g * gd, (g + 1) * gd)
            mixed = _dot(w, vln[rs, cs]) + bias
            oa_ref[rs, cs] = (gu[rs, cs] * mixed).astype(BF16)


def _in_proj(x2d, ln_g, ln_b, w_in, b_in, vg, vb, w_s, b_st):
    t, d = x2d.shape
    n_in = w_in.shape[1]
    tm = TM_PROJ
    row_spec = pl.BlockSpec((tm, d), lambda i: (i, 0))
    out = jax.ShapeDtypeStruct((t, d), BF16)
    return pl.pallas_call(
        _in_proj_kernel,
        grid=(t // tm,),
        in_specs=[row_spec, _full((1, d)), _full((1, d)), _full((d, n_in)), _full((1, n_in)),
                  _full((1, d)), _full((1, d)), _full(w_s.shape), _full(b_st.shape)],
        out_specs=[row_spec] * 5,
        out_shape=[out] * 5,
        compiler_params=pltpu.CompilerParams(dimension_semantics=("parallel",),
                                             vmem_limit_bytes=VMEM_LIMIT),
        name="in_proj",
    )(x2d, ln_g, ln_b, w_in, b_in, vg, vb, w_s, b_st)


def _sb_kernel(q_ref, k_ref, v_ref, m_ref, o_ref):
    i = pl.program_id(2)
    tq = q_ref.shape[1]
    nslab = tq // SLAB
    q = q_ref[0]
    later = m_ref[...]

    def slab(qs, j, carry, mask):
        off = pl.multiple_of(j * SLAB, SLAB)
        z = _dot_nt(qs, k_ref[0, pl.ds(off, SLAB), :])
        sp = jnp.maximum(jnp.log(1.0 + jnp.exp2(jnp.minimum(z, SP_CLAMP))) * LOG2E, z)
        if mask is not None:
            sp = jnp.where(mask, sp, 0.0)
        cs = _dot(sp.astype(BF16), later)
        a = jnp.exp2(z - sp - cs - carry)
        if mask is not None:
            a = jnp.where(mask, a, 0.0)
        total = cs[:, 0:1] + sp[:, 0:1]
        return carry + total, _dot(a.astype(BF16), v_ref[0, pl.ds(off, SLAB), :])

    carry = jnp.zeros((tq, 1), F32)
    acc = jnp.zeros((tq, SB_HEAD_DIM), F32)
    for d in reversed(range(nslab)):
        r0 = d * SLAB
        rr = lax.broadcasted_iota(I32, (tq - r0, SLAB), 0)
        cc = lax.broadcasted_iota(I32, (tq - r0, SLAB), 1)
        c_new, contrib = slab(q[r0:], i * nslab + d, carry[r0:], cc < rr)
        a_new = acc[r0:] + contrib
        carry = jnp.concatenate([carry[:r0], c_new], axis=0) if r0 else c_new
        acc = jnp.concatenate([acc[:r0], a_new], axis=0) if r0 else a_new

    def body(n, ca):
        carry, acc = ca
        for u in range(nslab):
            carry, contrib = slab(q, (i - n) * nslab - 1 - u, carry, None)
            acc = acc + contrib
        return carry, acc

    carry, acc = lax.fori_loop(0, i, body, (carry, acc))
    o_ref[0] = acc.astype(o_ref.dtype)


def _sb_attention(q, k, v):
    b, s, w = q.shape
    h = w // SB_HEAD_DIM
    jj = lax.broadcasted_iota(I32, (SLAB, SLAB), 0)
    ss = lax.broadcasted_iota(I32, (SLAB, SLAB), 1)
    mcat = jnp.where(jj > ss, 1.0, 0.0).astype(BF16)
    q_spec = pl.BlockSpec((1, TQ, SB_HEAD_DIM), lambda bi, hi, i: (bi, i, hi))
    kv_spec = pl.BlockSpec((1, s, SB_HEAD_DIM), lambda bi, hi, i: (bi, 0, hi))
    return pl.pallas_call(
        _sb_kernel,
        grid=(b, h, s // TQ),
        in_specs=[q_spec, kv_spec, kv_spec, _full(mcat.shape)],
        out_specs=q_spec,
        out_shape=jax.ShapeDtypeStruct((b, s, w), BF16),
        compiler_params=pltpu.CompilerParams(
            dimension_semantics=("parallel", "parallel", "arbitrary"),
            vmem_limit_bytes=VMEM_LIMIT),
        name="sb_attn",
    )(q, k, v, mcat)


def _mem_kv_kernel(m_ref, w_ref, k_ref, v_ref):
    kv = _dot(m_ref[0].astype(BF16), w_ref[...])
    half = kv.shape[1] // 2
    k_ref[0] = kv[:, :half].astype(BF16)
    v_ref[0] = kv[:, half:].astype(BF16)


def _mem_kv(mem, w_kv):
    b, m, d = mem.shape
    half = w_kv.shape[1] // 2
    out = jax.ShapeDtypeStruct((b, m, half), BF16)
    o_spec = pl.BlockSpec((1, m, half), lambda i: (i, 0, 0))
    return pl.pallas_call(
        _mem_kv_kernel,
        grid=(b,),
        in_specs=[pl.BlockSpec((1, m, d), lambda i: (i, 0, 0)), _full(w_kv.shape)],
        out_specs=[o_spec, o_spec],
        out_shape=[out, out],
        compiler_params=pltpu.CompilerParams(dimension_semantics=("parallel",),
                                             vmem_limit_bytes=VMEM_LIMIT),
        name="mem_kv",
    )(mem, w_kv)


def _mid_kernel(x_ref, lg_ref, lb_ref, oa_ref, yb_ref, gb_ref, wo_ref, l1g_ref, l1b_ref,
                wq_ref, km_ref, vm_ref, wmo_ref, l2g_ref, l2b_ref, wrh_ref, wrl_ref,
                wsg_ref, wsu_ref, wsd_ref,
                base_ref, x2_ref, lgt_ref):
    xln = _ln(x_ref[...], lg_ref[...], lb_ref[...])
    merged = oa_ref[...].astype(F32) + gb_ref[...].astype(F32) * yb_ref[...].astype(F32)
    x1 = _ln(ALPHA * xln + _dot(merged.astype(BF16), wo_ref[...]), l1g_ref[...], l1b_ref[...])

    q = (_dot(x1.astype(BF16), wq_ref[...]) * (MEM_HEAD_DIM ** -0.5)).astype(BF16)
    heads = []
    for h in range(MEM_HEADS):
        hs = slice(h * MEM_HEAD_DIM, (h + 1) * MEM_HEAD_DIM)
        logits = _dot_nt(q[:, hs], km_ref[0, :, hs])
        p = jnp.exp(logits - jnp.max(logits, axis=-1, keepdims=True))
        p = p / jnp.sum(p, axis=-1, keepdims=True)
        heads.append(_dot(p.astype(BF16), vm_ref[0, :, hs]))
    o = jnp.concatenate(heads, axis=1).astype(BF16)
    x2 = _ln(ALPHA * x1 + _dot(o, wmo_ref[...]), l2g_ref[...], l2b_ref[...])

    x2h = x2.astype(BF16)
    x2l = (x2 - x2h.astype(F32)).astype(BF16)
    lgt_ref[...] = (_dot_nt(wrh_ref[...], x2h) + _dot_nt(wrh_ref[...], x2l)
                    + _dot_nt(wrl_ref[...], x2h))
    x2_ref[...] = x2h

    hid = jax.nn.silu(_dot(x2h, wsg_ref[...])) * _dot(x2h, wsu_ref[...])
    base_ref[...] = ALPHA * x2 + _dot(hid.astype(BF16), wsd_ref[...])


def _mid(x2d, ln_g, ln_b, oa, yb, gb, w_out, l1g, l1b, w_q, k_mem, v_mem, w_mo, l2g, l2b,
         wr_hi, wr_lo, w_sg, w_su, w_sd, seq):
    t, d = x2d.shape
    tm = TM_MID
    per_batch = seq // tm
    row_spec = pl.BlockSpec((tm, d), lambda i: (i, 0))
    mem_spec = pl.BlockSpec((1,) + k_mem.shape[1:], lambda i: (i // per_batch, 0, 0))
    vec = _full((1, d))
    return pl.pallas_call(
        _mid_kernel,
        grid=(t // tm,),
        in_specs=[row_spec, vec, vec, row_spec, row_spec, row_spec, _full(w_out.shape), vec, vec,
                  _full(w_q.shape), mem_spec, mem_spec, _full(w_mo.shape), vec, vec,
                  _full(wr_hi.shape), _full(wr_lo.shape),
                  _full(w_sg.shape), _full(w_su.shape), _full(w_sd.shape)],
        out_specs=[row_spec, row_spec, pl.BlockSpec((N_EXPERTS, tm), lambda i: (0, i))],
        out_shape=[jax.ShapeDtypeStruct((t, d), F32), jax.ShapeDtypeStruct((t, d), BF16),
                   jax.ShapeDtypeStruct((N_EXPERTS, t), F32)],
        compiler_params=pltpu.CompilerParams(dimension_semantics=("parallel",),
                                             vmem_limit_bytes=VMEM_LIMIT),
        name="mid",
    )(x2d, ln_g, ln_b, oa, yb, gb, w_out, l1g, l1b, w_q, k_mem, v_mem, w_mo, l2g, l2b,
      wr_hi, wr_lo, w_sg, w_su, w_sd)


def _route_kernel(lgt_ref, bias_ref, sel_ref, w_ref, cnt_ref):
    e, tm = lgt_ref.shape
    per_group = e // N_GROUPS
    scores = jax.nn.sigmoid(lgt_ref[...])
    sel = scores + bias_ref[...]

    g3 = sel.reshape(N_GROUPS, per_group, tm)
    j3 = lax.broadcasted_iota(I32, g3.shape, 1)
    m1 = jnp.max(g3, axis=1, keepdims=True)
    first = jnp.min(jnp.where(g3 == m1, j3, per_group), axis=1, keepdims=True)
    m2 = jnp.max(jnp.where(j3 == first, -jnp.inf, g3), axis=1, keepdims=True)
    gs = (m1 + m2).reshape(N_GROUPS, tm)

    gi = lax.broadcasted_iota(I32, (N_GROUPS, tm), 0)
    grank = jnp.zeros((N_GROUPS, tm), I32)
    for o in range(N_GROUPS):
        other = gs[o:o + 1, :]
        grank += ((other > gs) | ((other == gs) & (o < gi))).astype(I32)
    gmask = (grank < TOPK_GROUPS).astype(F32)
    emask = jnp.broadcast_to(gmask.reshape(N_GROUPS, 1, tm), (N_GROUPS, per_group, tm)).reshape(e, tm)
    cand = jnp.where(emask > 0.5, sel, -jnp.inf)

    ei = lax.broadcasted_iota(I32, (e, tm), 0)
    erank = jnp.zeros((e, tm), I32)
    for o in range(e):
        other = cand[o:o + 1, :]
        erank += ((other > cand) | ((other == cand) & (o < ei))).astype(I32)
    chosen = erank < TOP_K

    w = jnp.where(chosen, scores, 0.0)
    w = w / jnp.sum(w, axis=0, keepdims=True) * ROUTED_SCALE
    chosen_f = chosen.astype(F32)
    sel_ref[...] = chosen_f
    w_ref[...] = w
    cnt_ref[0] = jnp.broadcast_to(jnp.sum(chosen_f, axis=1, keepdims=True), (e, LANES))


def _route(logits_t, bias_col):
    e, t = logits_t.shape
    tm = TM_MOE
    nb = t // tm
    blk = pl.BlockSpec((e, tm), lambda i: (0, i))
    return pl.pallas_call(
        _route_kernel,
        grid=(nb,),
        in_specs=[blk, _full((e, 1))],
        out_specs=[blk, blk, pl.BlockSpec((1, e, LANES), lambda i: (i, 0, 0))],
        out_shape=[jax.ShapeDtypeStruct((e, t), F32), jax.ShapeDtypeStruct((e, t), F32),
                   jax.ShapeDtypeStruct((nb, e, LANES), F32)],
        compiler_params=pltpu.CompilerParams(dimension_semantics=("parallel",),
                                             vmem_limit_bytes=VMEM_LIMIT),
        name="route",
    )(logits_t, bias_col)


def _pack_rows(x):
    h = x.shape[1] // 2
    hi = lax.bitcast_convert_type(x[:, :h], U32) & jnp.uint32(0xFFFF0000)
    lo = lax.shift_right_logical(lax.bitcast_convert_type(x[:, h:], U32), jnp.uint32(16))
    return hi | lo


def _unpack_rows(u):
    hi = lax.bitcast_convert_type(u & jnp.uint32(0xFFFF0000), F32)
    lo = lax.bitcast_convert_type(lax.shift_left(u, jnp.uint32(16)), F32)
    return jnp.concatenate([hi, lo], axis=1)


def _row_tables(sel, loc_col):
    e, tm = sel.shape
    tj = lax.broadcasted_iota(I32, (tm, tm), 0)
    tt = lax.broadcasted_iota(I32, (tm, tm), 1)
    before_t = jnp.where(tj < tt, 1.0, 0.0).astype(BF16)
    rank = _dot(sel.astype(BF16), before_t)
    pos = jnp.where(sel > 0.5, loc_col + rank, -1.0)
    hi = jnp.floor(pos * (1.0 / ROW_SPLIT)) * ROW_SPLIT
    return jnp.concatenate([hi, pos - hi], axis=0).astype(BF16)


def _row_owner(c, rchunk, loc_row, cnt_row):
    e = loc_row.shape[1]
    ri = (lax.broadcasted_iota(I32, (rchunk, e), 0) + c * rchunk).astype(F32)
    return jnp.where((ri >= loc_row) & (ri < loc_row + cnt_row), 1.0, 0.0).astype(BF16)


def _segment_copy(src, src_row, dst, dst_row, rows, sem):
    rows = pl.multiple_of(rows, GRAN)
    return pltpu.make_async_copy(src.at[pl.ds(pl.multiple_of(src_row, GRAN), rows)],
                                 dst.at[pl.ds(pl.multiple_of(dst_row, GRAN), rows)], sem)


def _dispatch_kernel(loc_ref, glob_ref, rows_ref, used_ref, tail_lo_ref, tail_hi_ref,
                     x_ref, sel_ref, locc_ref, locr_ref, cntr_ref, xs_hbm, xs_loc, zeros, sems):
    b = pl.program_id(0)
    nb = pl.num_programs(0)
    e, tm = sel_ref.shape
    r_loc = xs_loc.shape[1]
    slot = b % 2
    buf = xs_loc.at[slot]
    sem = sems.at[slot]

    def wait_block(blk, s):
        @pl.when(used_ref[blk] > 0)
        def _():
            _segment_copy(xs_loc.at[s], 0, xs_hbm, 0, used_ref[blk], sems.at[s]).wait()

    @pl.when(b >= 2)
    def _():
        wait_block(b - 2, slot)

    tables = _row_tables(sel_ref[...], locc_ref[0])
    x = x_ref[...]
    own = _row_owner(0, r_loc, locr_ref[0], cntr_ref[0])
    prow = _dot(jnp.concatenate([own, own], axis=1), tables)
    ri = lax.broadcasted_iota(I32, (r_loc, tm), 0).astype(F32)
    onehot = jnp.where(prow == ri, 1.0, 0.0).astype(BF16)
    buf[...] = _pack_rows(_dot(onehot, x))

    def per_expert(ex, _):
        n = rows_ref[b * e + ex]

        @pl.when(n > 0)
        def _():
            _segment_copy(buf, loc_ref[b * e + ex], xs_hbm, glob_ref[b * e + ex], n, sem).start()

        return 0

    lax.fori_loop(0, e, per_expert, 0)

    @pl.when(b == nb - 1)
    def _():
        @pl.when(b >= 1)
        def _():
            wait_block(b - 1, 1 - slot)

        wait_block(b, slot)
        zeros[...] = jnp.zeros_like(zeros)
        zrows = zeros.shape[0]

        def each_piece(act):
            def per_tail(ex, _):
                lo = tail_lo_ref[ex]
                n = tail_hi_ref[ex] - lo

                def per_piece(j, _):
                    rows = jnp.minimum(n - j * zrows, zrows)
                    act(_segment_copy(zeros, 0, xs_hbm, lo + j * zrows, rows, sem))
                    return 0

                lax.fori_loop(0, (n + zrows - 1) // zrows, per_piece, 0)
                return 0

            lax.fori_loop(0, tail_lo_ref.shape[0], per_tail, 0)

        each_piece(lambda cp: cp.start())
        each_piece(lambda cp: cp.wait())


def _dispatch(x2b, sel_t, lay):
    t, d = x2b.shape
    e = sel_t.shape[0]
    tm = TM_MOE
    nb = t // tm
    r_loc = tm * TOP_K + e * GRAN
    col = pl.BlockSpec((1, e, 1), lambda i, *_: (i, 0, 0))
    rowv = pl.BlockSpec((1, 1, e), lambda i, *_: (i, 0, 0))
    grid_spec = pltpu.PrefetchScalarGridSpec(
        num_scalar_prefetch=6,
        grid=(nb,),
        in_specs=[pl.BlockSpec((tm, d), lambda i, *_: (i, 0)),
                  pl.BlockSpec((e, tm), lambda i, *_: (0, i)), col, rowv, rowv],
        out_specs=pl.BlockSpec(memory_space=pl.ANY),
        scratch_shapes=[pltpu.VMEM((2, r_loc, d // 2), U32), pltpu.VMEM((FFN_TILE, d // 2), U32),
                        pltpu.SemaphoreType.DMA((2,))],
    )
    return pl.pallas_call(
        _dispatch_kernel,
        grid_spec=grid_spec,
        out_shape=jax.ShapeDtypeStruct((lay["r_glob"], d // 2), U32),
        compiler_params=pltpu.CompilerParams(dimension_semantics=("arbitrary",),
                                             vmem_limit_bytes=VMEM_LIMIT),
        name="dispatch",
    )(lay["loc_flat"], lay["glob_flat"], lay["rows_flat"], lay["used"], lay["tail_lo"],
      lay["tail_hi"], x2b, sel_t, lay["loc_col"], lay["loc_row"], lay["cnt_row"])


def _experts_kernel(tile_e_ref, n_used_ref, next_e_ref, xs_hbm, wg_hbm, wu_hbm, wd_hbm, ys_hbm,
                    wg_f, wu_f, wd_f, wg_b, wu_b, wd_b, xbuf, ybuf, sems, in_sems, out_sems):
    i = pl.program_id(0)

    def weight_copies(ex):
        return (pltpu.make_async_copy(wg_hbm.at[ex], wg_f, sems.at[0]),
                pltpu.make_async_copy(wu_hbm.at[ex], wu_f, sems.at[1]),
                pltpu.make_async_copy(wd_hbm.at[ex], wd_f, sems.at[2]))

    def start_weights(ex):
        for cp in weight_copies(ex):
            cp.start(priority=1)

    @pl.when(i == 0)
    def _():
        start_weights(tile_e_ref[0])

    prev = tile_e_ref[jnp.maximum(i - 1, 0)]

    @pl.when((i == 0) | (tile_e_ref[i] != prev))
    def _():
        for cp in weight_copies(tile_e_ref[i]):
            cp.wait()
        wg_b[...] = wg_f[...].astype(BF16)
        wu_b[...] = wu_f[...].astype(BF16)
        wd_b[...] = wd_f[...].astype(BF16)
        nxt = next_e_ref[i]

        @pl.when(nxt >= 0)
        def _():
            start_weights(nxt)

    n_tiles = pl.num_programs(0)
    n_used = n_used_ref[0]
    tile_rows = xbuf.shape[1]

    def tile_in(j):
        return pltpu.make_async_copy(xs_hbm.at[pl.ds(pl.multiple_of(j * tile_rows, tile_rows), tile_rows)],
                                     xbuf.at[j % ROW_SLOTS], in_sems.at[j % ROW_SLOTS])

    def tile_out(j, slot=None):
        slot = j % ROW_SLOTS if slot is None else slot
        return pltpu.make_async_copy(ybuf.at[slot],
                                     ys_hbm.at[pl.ds(pl.multiple_of(j * tile_rows, tile_rows), tile_rows)],
                                     out_sems.at[slot])

    @pl.when(i == 0)
    def _():
        for j in range(ROW_SLOTS - 1):
            @pl.when(j < n_used)
            def _():
                tile_in(j).start()

    @pl.when(i + ROW_SLOTS - 1 < n_used)
    def _():
        tile_in(i + ROW_SLOTS - 1).start()

    @pl.when((i >= ROW_SLOTS) & (i - ROW_SLOTS < n_used))
    def _():
        tile_out(i - ROW_SLOTS).wait()

    @pl.when(i < n_used)
    def _():
        tile_in(i).wait()
        x = _unpack_rows(xbuf[i % ROW_SLOTS]).astype(BF16)
        hid = jax.nn.silu(_dot(x, wg_b[...])) * _dot(x, wu_b[...])
        y = _dot(hid.astype(BF16), wd_b[...])
        ybuf[i % ROW_SLOTS] = _pack_rows(y.astype(BF16).astype(F32))
        tile_out(i).start()

    @pl.when(i == n_tiles - 1)
    def _():
        for back in reversed(range(ROW_SLOTS)):
            @pl.when((i - back >= 0) & (i - back < n_used))
            def _():
                tile_out(i - back).wait()

        ybuf[0] = jnp.zeros(ybuf.shape[1:], ybuf.dtype)

        def each_unused(act):
            def body(j, _):
                act(tile_out(j, slot=0))
                return 0

            lax.fori_loop(n_used, n_tiles, body, 0)

        each_unused(lambda cp: cp.start())
        each_unused(lambda cp: cp.wait())


def _experts(xs, tile_e, n_used, next_e, w_gate, w_up, w_down):
    r_glob, dh = xs.shape
    d, hdim = w_gate.shape[1:]
    n_tiles = r_glob // FFN_TILE

    hbm = pl.BlockSpec(memory_space=pl.ANY)
    grid_spec = pltpu.PrefetchScalarGridSpec(
        num_scalar_prefetch=3,
        grid=(n_tiles,),
        in_specs=[hbm, hbm, hbm, hbm],
        out_specs=hbm,
        scratch_shapes=[pltpu.VMEM((d, hdim), F32), pltpu.VMEM((d, hdim), F32),
                        pltpu.VMEM((hdim, d), F32),
                        pltpu.VMEM((d, hdim), BF16), pltpu.VMEM((d, hdim), BF16),
                        pltpu.VMEM((hdim, d), BF16),
                        pltpu.VMEM((ROW_SLOTS, FFN_TILE, dh), U32),
                        pltpu.VMEM((ROW_SLOTS, FFN_TILE, dh), U32),
                        pltpu.SemaphoreType.DMA((3,)), pltpu.SemaphoreType.DMA((ROW_SLOTS,)),
                        pltpu.SemaphoreType.DMA((ROW_SLOTS,))],
    )
    return pl.pallas_call(
        _experts_kernel,
        grid_spec=grid_spec,
        out_shape=jax.ShapeDtypeStruct((r_glob, dh), U32),
        compiler_params=pltpu.CompilerParams(dimension_semantics=("arbitrary",),
                                             vmem_limit_bytes=VMEM_LIMIT),
        name="experts",
    )(tile_e, n_used, next_e, xs, w_gate, w_up, w_down)


def _combine_kernel(loc_ref, glob_ref, rows_ref, used_ref,
                    base_ref, sel_ref, w_ref, locc_ref, locr_ref, cntr_ref, g_ref, bta_ref, ys_hbm,
                    o_ref, ys_loc, sems):
    b = pl.program_id(0)
    nb = pl.num_programs(0)
    e, tm = sel_ref.shape
    r_loc = ys_loc.shape[1]
    slot = b % 2
    buf = ys_loc.at[slot]

    def fetch_block(blk, s):
        def per_expert(ex, _):
            n = rows_ref[blk * e + ex]

            @pl.when(n > 0)
            def _():
                _segment_copy(ys_hbm, glob_ref[blk * e + ex], ys_loc.at[s], loc_ref[blk * e + ex], n,
                              sems.at[s]).start()

            return 0

        lax.fori_loop(0, e, per_expert, 0)

    @pl.when(b == 0)
    def _():
        fetch_block(b, slot)

    @pl.when(b + 1 < nb)
    def _():
        fetch_block(b + 1, 1 - slot)

    tables = _row_tables(sel_ref[...], locc_ref[0])
    wb = w_ref[...].astype(BF16)
    used = used_ref[b]

    @pl.when(used > 0)
    def _():
        _segment_copy(ys_hbm, 0, buf, 0, used, sems.at[slot]).wait()

    own = _row_owner(0, r_loc, locr_ref[0], cntr_ref[0])
    prow = _dot(jnp.concatenate([own, own], axis=1), tables)
    wrow = _dot(own, wb)
    ri = lax.broadcasted_iota(I32, (r_loc, tm), 0).astype(F32)
    wmat = jnp.where(prow == ri, wrow, 0.0).astype(BF16)
    always = tm * TOP_K
    rvalid = (lax.broadcasted_iota(I32, (r_loc - always, 1), 0) + always) < used
    u = jnp.concatenate([buf[:always, :], jnp.where(rvalid, buf[always:, :], jnp.uint32(0))], axis=0)
    moe = _dot_tn(wmat, _unpack_rows(u).astype(BF16))
    o_ref[...] = _ln(base_ref[...] + moe, g_ref[...], bta_ref[...])


def _combine(base, sel_t, w_t, lay, ys, ln_g, ln_b):
    t, d = base.shape
    e = sel_t.shape[0]
    tm = TM_MOE
    nb = t // tm
    r_loc = tm * TOP_K + e * GRAN
    blk = pl.BlockSpec((e, tm), lambda i, *_: (0, i))
    row_spec = pl.BlockSpec((tm, d), lambda i, *_: (i, 0))
    vec = pl.BlockSpec((1, d), lambda i, *_: (0, 0))
    col = pl.BlockSpec((1, e, 1), lambda i, *_: (i, 0, 0))
    rowv = pl.BlockSpec((1, 1, e), lambda i, *_: (i, 0, 0))
    grid_spec = pltpu.PrefetchScalarGridSpec(
        num_scalar_prefetch=4,
        grid=(nb,),
        in_specs=[row_spec, blk, blk, col, rowv, rowv, vec, vec, pl.BlockSpec(memory_space=pl.ANY)],
        out_specs=row_spec,
        scratch_shapes=[pltpu.VMEM((2, r_loc, d // 2), U32), pltpu.SemaphoreType.DMA((2,))],
    )
    return pl.pallas_call(
        _combine_kernel,
        grid_spec=grid_spec,
        out_shape=jax.ShapeDtypeStruct((t, d), F32),
        compiler_params=pltpu.CompilerParams(dimension_semantics=("arbitrary",),
                                             vmem_limit_bytes=VMEM_LIMIT),
        name="combine",
    )(lay["loc_flat"], lay["glob_flat"], lay["rows_flat"], lay["used"],
      base, sel_t, w_t, lay["loc_col"], lay["loc_row"], lay["cnt_row"], ln_g, ln_b, ys)


def _round_up(x, m):
    return (x + m - 1) // m * m


def _moe_layout(cnt, t):
    nb, e = cnt.shape
    cnt_g = _round_up(cnt, GRAN)
    loc_off = jnp.cumsum(cnt_g, axis=1) - cnt_g
    used = jnp.sum(cnt_g, axis=1)
    gcnt = jnp.sum(cnt_g, axis=0)
    gpad = _round_up(gcnt, FFN_TILE)
    gend = jnp.cumsum(gpad)
    gstart = gend - gpad
    glob_off = gstart[None, :] + jnp.cumsum(cnt_g, axis=0) - cnt_g
    r_glob = _round_up(t * TOP_K + nb * e * (GRAN - 1) + e * (FFN_TILE - 1), FFN_TILE)
    n_tiles = r_glob // FFN_TILE
    n_used = (gend[-1] // FFN_TILE).astype(I32)
    tile_start = jnp.minimum(jnp.arange(n_tiles, dtype=I32), n_used - 1) * FFN_TILE
    tile_e = jnp.minimum(jnp.sum(gend[None, :] <= tile_start[:, None], axis=1), e - 1).astype(I32)
    ids = jnp.arange(e, dtype=I32)
    later_owner = jnp.min(jnp.where((ids[None, :] > ids[:, None]) & (gpad[None, :] > 0), ids[None, :], e),
                          axis=1)
    later_owner = jnp.where(later_owner < e, later_owner, -1)
    next_e = jnp.sum(jnp.where(tile_e[:, None] == ids[None, :], later_owner[None, :], 0), axis=1).astype(I32)
    return dict(next_e=next_e, loc_flat=loc_off.astype(I32).reshape(-1), glob_flat=glob_off.astype(I32).reshape(-1),
                rows_flat=cnt_g.astype(I32).reshape(-1), used=used.astype(I32),
                loc_col=loc_off.astype(F32)[:, :, None], loc_row=loc_off.astype(F32)[:, None, :],
                cnt_row=cnt_g.astype(F32)[:, None, :],
                tail_lo=jnp.append(gstart + gcnt, gend[-1]).astype(I32),
                tail_hi=jnp.append(gend, r_glob).astype(I32),
                tile_e=tile_e, n_used=n_used.reshape(1), r_glob=r_glob)


def kernel(x, mem, ln_in_g, ln_in_b, w_in, b_in, ln_v_g, ln_v_b, w_spatial, b_spatial, w_out,
           ln1_g, ln1_b, w_mem_q, w_mem_kv, w_mem_o, ln2_g, ln2_b, w_router, router_bias,
           w_exp_gate, w_exp_up, w_exp_down, w_sh_gate, w_sh_up, w_sh_down, ln3_g, ln3_b):
    bsz, seq, d = x.shape
    t = bsz * seq
    assert w_in.shape[0] == DEPTH
    x2d = x.reshape(t, d)
    row = lambda a: a.reshape(1, -1)

    oa, q, k, v, gb = _in_proj(x2d, row(ln_in_g), row(ln_in_b), w_in[0].astype(BF16), row(b_in[0]),
                               row(ln_v_g[0]), row(ln_v_b[0]), w_spatial[0], b_spatial[0].T)
    yb = _sb_attention(q.reshape(bsz, seq, d), k.reshape(bsz, seq, d), v.reshape(bsz, seq, d))
    k_mem, v_mem = _mem_kv(mem, w_mem_kv[0].astype(BF16))

    wr_t = w_router[0].T
    wr_hi = wr_t.astype(BF16)
    wr_lo = (wr_t - wr_hi.astype(F32)).astype(BF16)
    base, x2b, logits_t = _mid(
        x2d, row(ln_in_g), row(ln_in_b), oa, yb.reshape(t, d), gb, w_out[0].astype(BF16),
        row(ln1_g[0]), row(ln1_b[0]), w_mem_q[0].astype(BF16), k_mem, v_mem,
        w_mem_o[0].astype(BF16), row(ln2_g[0]), row(ln2_b[0]), wr_hi, wr_lo,
        w_sh_gate[0].astype(BF16), w_sh_up[0].astype(BF16), w_sh_down[0].astype(BF16), seq)

    sel_t, w_t, cnt = _route(logits_t, router_bias[0].reshape(-1, 1))
    lay = _moe_layout(cnt[:, :, 0].astype(I32), t)
    xs = _dispatch(x2b, sel_t, lay)
    ys = _experts(xs, lay["tile_e"], lay["n_used"], lay["next_e"], w_exp_gate[0], w_exp_up[0],
                  w_exp_down[0])
    out = _combine(base, sel_t, w_t, lay, ys, row(ln3_g[0]), row(ln3_b[0]))
    return out.reshape(bsz, seq, d)
```

```python
import functools

import jax
import jax.numpy as jnp
from jax import lax
from jax.experimental import pallas as pl
from jax.experimental.pallas import tpu as pltpu

F32 = jnp.float32
BF16 = jnp.bfloat16
I32 = jnp.int32
U32 = jnp.uint32

LANES = 128
SUBLANES = 8
GRAN = SUBLANES

CHUNK = 128
A_GROUPS = 8
SB_HEADS = 8
SB_HEAD_DIM = 128
MEM_HEADS = 4
MEM_HEAD_DIM = 128
N_EXPERTS = 64
TOP_K = 8
N_GROUPS = 8
TOPK_GROUPS = 4
ROUTED_SCALE = 2.5
LN_EPS = 1e-5
DEPTH = 1
ALPHA = (2 * DEPTH) ** 0.25
LOG2E = 1.4426950408889634
SP_CLAMP = 64.0

TM_PROJ = 512
TM_MID = 512
MID_ROW_GROUPS = 1
TQ = 1024
SLAB = 256
TM_MOE = 256
TM_ROUTE = 1024
FFN_TILE = 512
ROW_SLOTS = 3
ROW_SPLIT = 256.0
VMEM_LIMIT = 56 * 1024 * 1024


def _ln(x, g, b):
    mu = jnp.mean(x, axis=-1, keepdims=True)
    xc = x - mu
    var = jnp.mean(xc * xc, axis=-1, keepdims=True)
    return xc * lax.rsqrt(var + LN_EPS) * g + b


def _gelu(x):
    return 0.5 * x * (1.0 + lax.erf(x * (2.0 ** -0.5)))


def _dot(a, b):
    return jnp.dot(a, b, preferred_element_type=F32)


def _dot_nt(a, b):
    return lax.dot_general(a, b, (((1,), (1,)), ((), ())), preferred_element_type=F32)


def _dot_tn(a, b):
    return lax.dot_general(a, b, (((0,), (0,)), ((), ())), preferred_element_type=F32)


def _full(shape):
    n = len(shape)
    return pl.BlockSpec(shape, lambda *_: (0,) * n, pipeline_mode=pl.Buffered(1))


def _in_proj_kernel(x_ref, lg_ref, lb_ref, w_ref, b_ref, vg_ref, vb_ref, ws_ref, bst_ref,
                    oa_ref, q_ref, k_ref, v_ref, gb_ref):
    tm, d = x_ref.shape
    xb = _ln(x_ref[...], lg_ref[...], lb_ref[...]).astype(BF16)

    def seg(i):
        return _dot(xb, w_ref[:, i * d:(i + 1) * d]) + b_ref[:, i * d:(i + 1) * d]

    q_ref[...] = (seg(2) * (SB_HEAD_DIM ** -0.5 * LOG2E)).astype(BF16)
    k_ref[...] = seg(3).astype(BF16)
    v_ref[...] = seg(4).astype(BF16)
    gb_ref[...] = jax.nn.sigmoid(seg(6)).astype(BF16)

    gu = jax.nn.sigmoid(seg(5)) * _gelu(seg(0))
    vln = _ln(_gelu(seg(1)), vg_ref[...], vb_ref[...]).astype(BF16)
    gd = d // A_GROUPS
    row = lax.broadcasted_iota(I32, (CHUNK, CHUNK), 0)
    col = lax.broadcasted_iota(I32, (CHUNK, CHUNK), 1)
    for g in range(A_GROUPS):
        w = jnp.where(col <= row, ws_ref[g], 0.0).astype(BF16)
        bias = bst_ref[:, g:g + 1]
        for c in range(tm // CHUNK):
            rs = slice(c * CHUNK, (c + 1) * CHUNK)
            cs = slice(g * gd, (g + 1) * gd)
            mixed = _dot(w, vln[rs, cs]) + bias
            oa_ref[rs, cs] = (gu[rs, cs] * mixed).astype(BF16)


def _in_proj(x2d, ln_g, ln_b, w_in, b_in, vg, vb, w_s, b_st):
    t, d = x2d.shape
    n_in = w_in.shape[1]
    tm = TM_PROJ
    row_spec = pl.BlockSpec((tm, d), lambda i: (i, 0))
    out = jax.ShapeDtypeStruct((t, d), BF16)
    return pl.pallas_call(
        _in_proj_kernel,
        grid=(t // tm,),
        in_specs=[row_spec, _full((1, d)), _full((1, d)), _full((d, n_in)), _full((1, n_in)),
                  _full((1, d)), _full((1, d)), _full(w_s.shape), _full(b_st.shape)],
        out_specs=[row_spec] * 5,
        out_shape=[out] * 5,
        compiler_params=pltpu.CompilerParams(dimension_semantics=("parallel",),
                                             vmem_limit_bytes=VMEM_LIMIT),
        name="in_proj",
    )(x2d, ln_g, ln_b, w_in, b_in, vg, vb, w_s, b_st)


def _sb_kernel(q_ref, k_ref, v_ref, m_ref, o_ref):
    i = pl.program_id(2)
    tq = q_ref.shape[1]
    nslab = tq // SLAB
    q = q_ref[0]
    later = m_ref[...]

    def slab(qs, j, carry, mask):
        off = pl.multiple_of(j * SLAB, SLAB)
        z = _dot_nt(qs, k_ref[0, pl.ds(off, SLAB), :])
        sp = jnp.maximum(jnp.log(1.0 + jnp.exp2(jnp.minimum(z, SP_CLAMP))) * LOG2E, z)
        if mask is not None:
            sp = jnp.where(mask, sp, 0.0)
        log2_beta = z - sp
        first = sp[:, 0:1]
        cs = _dot(sp.astype(BF16), later)
        a = jnp.exp2(log2_beta - cs - carry)
        if mask is not None:
            a = jnp.where(mask, a, 0.0)
        total = cs[:, 0:1] + first
        return carry + total, _dot(a.astype(BF16), v_ref[0, pl.ds(off, SLAB), :])

    carry = jnp.zeros((tq, 1), F32)
    acc = jnp.zeros((tq, SB_HEAD_DIM), F32)
    for d in reversed(range(nslab)):
        r0 = d * SLAB
        rr = lax.broadcasted_iota(I32, (tq - r0, SLAB), 0)
        cc = lax.broadcasted_iota(I32, (tq - r0, SLAB), 1)
        c_new, contrib = slab(q[r0:], i * nslab + d, carry[r0:], cc < rr)
        a_new = acc[r0:] + contrib
        carry = jnp.concatenate([carry[:r0], c_new], axis=0) if r0 else c_new
        acc = jnp.concatenate([acc[:r0], a_new], axis=0) if r0 else a_new

    def body(n, ca):
        carry, acc = ca
        for u in range(nslab):
            carry, contrib = slab(q, (i - n) * nslab - 1 - u, carry, None)
            acc = acc + contrib
        return carry, acc

    carry, acc = lax.fori_loop(0, i, body, (carry, acc))
    o_ref[0] = acc.astype(o_ref.dtype)


def _sb_attention(q, k, v):
    b, s, w = q.shape
    h = w // SB_HEAD_DIM
    jj = lax.broadcasted_iota(I32, (SLAB, SLAB), 0)
    ss = lax.broadcasted_iota(I32, (SLAB, SLAB), 1)
    mcat = jnp.where(jj > ss, 1.0, 0.0).astype(BF16)
    q_spec = pl.BlockSpec((1, TQ, SB_HEAD_DIM), lambda bi, hi, i: (bi, i, hi))
    kv_spec = pl.BlockSpec((1, s, SB_HEAD_DIM), lambda bi, hi, i: (bi, 0, hi))
    return pl.pallas_call(
        _sb_kernel,
        grid=(b, h, s // TQ),
        in_specs=[q_spec, kv_spec, kv_spec, _full(mcat.shape)],
        out_specs=q_spec,
        out_shape=jax.ShapeDtypeStruct((b, s, w), BF16),
        compiler_params=pltpu.CompilerParams(
            dimension_semantics=("parallel", "parallel", "arbitrary"),
            vmem_limit_bytes=VMEM_LIMIT),
        name="sb_attn",
    )(q, k, v, mcat)


def _mem_kv_kernel(m_ref, w_ref, k_ref, v_ref):
    kv = _dot(m_ref[0].astype(BF16), w_ref[...])
    half = kv.shape[1] // 2
    k_ref[0] = kv[:, :half].astype(BF16)
    v_ref[0] = kv[:, half:].astype(BF16)


def _mem_kv(mem, w_kv):
    b, m, d = mem.shape
    half = w_kv.shape[1] // 2
    out = jax.ShapeDtypeStruct((b, m, half), BF16)
    o_spec = pl.BlockSpec((1, m, half), lambda i: (i, 0, 0))
    return pl.pallas_call(
        _mem_kv_kernel,
        grid=(b,),
        in_specs=[pl.BlockSpec((1, m, d), lambda i: (i, 0, 0)), _full(w_kv.shape)],
        out_specs=[o_spec, o_spec],
        out_shape=[out, out],
        compiler_params=pltpu.CompilerParams(dimension_semantics=("parallel",),
                                             vmem_limit_bytes=VMEM_LIMIT),
        name="mem_kv",
    )(mem, w_kv)


def _mid_kernel(x_ref, lg_ref, lb_ref, oa_ref, yb_ref, gb_ref, wo_ref, l1g_ref, l1b_ref,
                wq_ref, km_ref, vm_ref, wmo_ref, l2g_ref, l2b_ref, wrh_ref, wrl_ref,
                wsg_ref, wsu_ref, wsd_ref,
                base_ref, x2_ref, lgt_ref):
    tm = x_ref.shape[0]
    rows_per = tm // MID_ROW_GROUPS
    for grp in range(MID_ROW_GROUPS):
        rs = slice(grp * rows_per, (grp + 1) * rows_per)
        xln = _ln(x_ref[rs, :], lg_ref[...], lb_ref[...])
        merged = oa_ref[rs, :].astype(F32) + gb_ref[rs, :].astype(F32) * yb_ref[rs, :].astype(F32)
        x1 = _ln(ALPHA * xln + _dot(merged.astype(BF16), wo_ref[...]), l1g_ref[...], l1b_ref[...])

        q = (_dot(x1.astype(BF16), wq_ref[...]) * (MEM_HEAD_DIM ** -0.5)).astype(BF16)
        heads = []
        for h in range(MEM_HEADS):
            hs = slice(h * MEM_HEAD_DIM, (h + 1) * MEM_HEAD_DIM)
            logits = _dot_nt(q[:, hs], km_ref[0, :, hs])
            p = jnp.exp(logits - jnp.max(logits, axis=-1, keepdims=True))
            p = p / jnp.sum(p, axis=-1, keepdims=True)
            heads.append(_dot(p.astype(BF16), vm_ref[0, :, hs]))
        o = jnp.concatenate(heads, axis=1).astype(BF16)
        x2 = _ln(ALPHA * x1 + _dot(o, wmo_ref[...]), l2g_ref[...], l2b_ref[...])

        x2h = x2.astype(BF16)
        x2l = (x2 - x2h.astype(F32)).astype(BF16)
        lgt_ref[:, rs] = (_dot_nt(wrh_ref[...], x2h) + _dot_nt(wrh_ref[...], x2l)
                          + _dot_nt(wrl_ref[...], x2h))
        x2_ref[rs, :] = x2h

        hid = jax.nn.silu(_dot(x2h, wsg_ref[...])) * _dot(x2h, wsu_ref[...])
        base_ref[rs, :] = ALPHA * x2 + _dot(hid.astype(BF16), wsd_ref[...])


def _mid(x2d, ln_g, ln_b, oa, yb, gb, w_out, l1g, l1b, w_q, k_mem, v_mem, w_mo, l2g, l2b,
         wr_hi, wr_lo, w_sg, w_su, w_sd, seq):
    t, d = x2d.shape
    tm = TM_MID
    per_batch = seq // tm
    row_spec = pl.BlockSpec((tm, d), lambda i: (i, 0))
    mem_spec = pl.BlockSpec((1,) + k_mem.shape[1:], lambda i: (i // per_batch, 0, 0))
    vec = _full((1, d))
    return pl.pallas_call(
        _mid_kernel,
        grid=(t // tm,),
        in_specs=[row_spec, vec, vec, row_spec, row_spec, row_spec, _full(w_out.shape), vec, vec,
                  _full(w_q.shape), mem_spec, mem_spec, _full(w_mo.shape), vec, vec,
                  _full(wr_hi.shape), _full(wr_lo.shape),
                  _full(w_sg.shape), _full(w_su.shape), _full(w_sd.shape)],
        out_specs=[row_spec, row_spec, pl.BlockSpec((N_EXPERTS, tm), lambda i: (0, i))],
        out_shape=[jax.ShapeDtypeStruct((t, d), F32), jax.ShapeDtypeStruct((t, d), BF16),
                   jax.ShapeDtypeStruct((N_EXPERTS, t), F32)],
        compiler_params=pltpu.CompilerParams(dimension_semantics=("parallel",),
                                             vmem_limit_bytes=VMEM_LIMIT),
        name="mid",
    )(x2d, ln_g, ln_b, oa, yb, gb, w_out, l1g, l1b, w_q, k_mem, v_mem, w_mo, l2g, l2b,
      wr_hi, wr_lo, w_sg, w_su, w_sd)


def _route_kernel(lgt_ref, bias_ref, sel_ref, w_ref, cnt_ref):
    e, tm = lgt_ref.shape
    per_group = e // N_GROUPS
    scores = jax.nn.sigmoid(lgt_ref[...])
    sel = scores + bias_ref[...]

    g3 = sel.reshape(N_GROUPS, per_group, tm)
    j3 = lax.broadcasted_iota(I32, g3.shape, 1)
    m1 = jnp.max(g3, axis=1, keepdims=True)
    first = jnp.min(jnp.where(g3 == m1, j3, per_group), axis=1, keepdims=True)
    m2 = jnp.max(jnp.where(j3 == first, -jnp.inf, g3), axis=1, keepdims=True)
    gs = (m1 + m2).reshape(N_GROUPS, tm)

    gi = lax.broadcasted_iota(I32, (N_GROUPS, tm), 0)
    grank = jnp.zeros((N_GROUPS, tm), I32)
    for o in range(N_GROUPS):
        other = gs[o:o + 1, :]
        grank += ((other > gs) | ((other == gs) & (o < gi))).astype(I32)
    gmask = (grank < TOPK_GROUPS).astype(F32)
    emask = jnp.broadcast_to(gmask.reshape(N_GROUPS, 1, tm), (N_GROUPS, per_group, tm)).reshape(e, tm)
    cand = jnp.where(emask > 0.5, sel, -jnp.inf)

    ei = lax.broadcasted_iota(I32, (e, tm), 0)
    chosen = jnp.zeros((e, tm), jnp.bool_)
    left = cand
    for _ in range(TOP_K):
        best = jnp.max(left, axis=0, keepdims=True)
        first = jnp.min(jnp.where(left == best, ei, e), axis=0, keepdims=True)
        pick = ei == first
        chosen = chosen | pick
        left = jnp.where(pick, -jnp.inf, left)

    w = jnp.where(chosen, scores, 0.0)
    w = w / jnp.sum(w, axis=0, keepdims=True) * ROUTED_SCALE
    chosen_f = chosen.astype(F32)
    sel_ref[...] = chosen_f
    w_ref[...] = w
    for s in range(tm // TM_MOE):
        part = chosen_f[:, s * TM_MOE:(s + 1) * TM_MOE]
        cnt_ref[s] = jnp.broadcast_to(jnp.sum(part, axis=1, keepdims=True), (e, LANES))


def _route(logits_t, bias_col):
    e, t = logits_t.shape
    tm = TM_ROUTE
    sub = tm // TM_MOE
    blk = pl.BlockSpec((e, tm), lambda i: (0, i))
    return pl.pallas_call(
        _route_kernel,
        grid=(t // tm,),
        in_specs=[blk, _full((e, 1))],
        out_specs=[blk, blk, pl.BlockSpec((sub, e, LANES), lambda i: (i, 0, 0))],
        out_shape=[jax.ShapeDtypeStruct((e, t), F32), jax.ShapeDtypeStruct((e, t), F32),
                   jax.ShapeDtypeStruct((t // TM_MOE, e, LANES), F32)],
        compiler_params=pltpu.CompilerParams(dimension_semantics=("parallel",),
                                             vmem_limit_bytes=VMEM_LIMIT),
        name="route",
    )(logits_t, bias_col)


def _pack_rows(x):
    h = x.shape[1] // 2
    hi = lax.bitcast_convert_type(x[:, :h], U32) & jnp.uint32(0xFFFF0000)
    lo = lax.shift_right_logical(lax.bitcast_convert_type(x[:, h:], U32), jnp.uint32(16))
    return hi | lo


def _unpack_rows(u):
    hi = lax.bitcast_convert_type(u & jnp.uint32(0xFFFF0000), F32)
    lo = lax.bitcast_convert_type(lax.shift_left(u, jnp.uint32(16)), F32)
    return jnp.concatenate([hi, lo], axis=1)


def _row_tables(sel, loc_col):
    e, tm = sel.shape
    tj = lax.broadcasted_iota(I32, (tm, tm), 0)
    tt = lax.broadcasted_iota(I32, (tm, tm), 1)
    before_t = jnp.where(tj < tt, 1.0, 0.0).astype(BF16)
    rank = _dot(sel.astype(BF16), before_t)
    pos = jnp.where(sel > 0.5, loc_col + rank, -1.0)
    hi = jnp.floor(pos * (1.0 / ROW_SPLIT)) * ROW_SPLIT
    return jnp.concatenate([hi, pos - hi], axis=0).astype(BF16)


def _row_owner(c, rchunk, loc_row, cnt_row):
    e = loc_row.shape[1]
    ri = (lax.broadcasted_iota(I32, (rchunk, e), 0) + c * rchunk).astype(F32)
    return jnp.where((ri >= loc_row) & (ri < loc_row + cnt_row), 1.0, 0.0).astype(BF16)


def _segment_copy(src, src_row, dst, dst_row, rows, sem):
    rows = pl.multiple_of(rows, GRAN)
    return pltpu.make_async_copy(src.at[pl.ds(pl.multiple_of(src_row, GRAN), rows)],
                                 dst.at[pl.ds(pl.multiple_of(dst_row, GRAN), rows)], sem)


def _dispatch_kernel(loc_ref, glob_ref, rows_ref, used_ref, tail_lo_ref, tail_hi_ref,
                     x_ref, sel_ref, locc_ref, locr_ref, cntr_ref, xs_hbm, xs_loc, zeros, sems):
    b = pl.program_id(0)
    nb = pl.num_programs(0)
    e, tm = sel_ref.shape
    r_loc = xs_loc.shape[1]
    slot = b % 2
    buf = xs_loc.at[slot]
    sem = sems.at[slot]

    def wait_block(blk, s):
        @pl.when(used_ref[blk] > 0)
        def _():
            _segment_copy(xs_loc.at[s], 0, xs_hbm, 0, used_ref[blk], sems.at[s]).wait()

    @pl.when(b >= 2)
    def _():
        wait_block(b - 2, slot)

    tables = _row_tables(sel_ref[...], locc_ref[0])
    x = x_ref[...]
    own = _row_owner(0, r_loc, locr_ref[0], cntr_ref[0])
    prow = _dot(jnp.concatenate([own, own], axis=1), tables)
    ri = lax.broadcasted_iota(I32, (r_loc, tm), 0).astype(F32)
    onehot = jnp.where(prow == ri, 1.0, 0.0).astype(BF16)
    buf[...] = _pack_rows(_dot(onehot, x))

    def per_expert(ex, _):
        n = rows_ref[b * e + ex]

        @pl.when(n > 0)
        def _():
            _segment_copy(buf, loc_ref[b * e + ex], xs_hbm, glob_ref[b * e + ex], n, sem).start()

        return 0

    lax.fori_loop(0, e, per_expert, 0)

    @pl.when(b == nb - 1)
    def _():
        @pl.when(b >= 1)
        def _():
            wait_block(b - 1, 1 - slot)

        wait_block(b, slot)
        zeros[...] = jnp.zeros_like(zeros)
        zrows = zeros.shape[0]

        def each_piece(act):
            def per_tail(ex, _):
                lo = tail_lo_ref[ex]
                n = tail_hi_ref[ex] - lo

                def per_piece(j, _):
                    rows = jnp.minimum(n - j * zrows, zrows)
                    act(_segment_copy(zeros, 0, xs_hbm, lo + j * zrows, rows, sem))
                    return 0

                lax.fori_loop(0, (n + zrows - 1) // zrows, per_piece, 0)
                return 0

            lax.fori_loop(0, tail_lo_ref.shape[0], per_tail, 0)

        each_piece(lambda cp: cp.start())
        each_piece(lambda cp: cp.wait())


def _dispatch(x2b, sel_t, lay):
    t, d = x2b.shape
    e = sel_t.shape[0]
    tm = TM_MOE
    nb = t // tm
    r_loc = tm * TOP_K + e * GRAN
    col = pl.BlockSpec((1, e, 1), lambda i, *_: (i, 0, 0))
    rowv = pl.BlockSpec((1, 1, e), lambda i, *_: (i, 0, 0))
    grid_spec = pltpu.PrefetchScalarGridSpec(
        num_scalar_prefetch=6,
        grid=(nb,),
        in_specs=[pl.BlockSpec((tm, d), lambda i, *_: (i, 0)),
                  pl.BlockSpec((e, tm), lambda i, *_: (0, i)), col, rowv, rowv],
        out_specs=pl.BlockSpec(memory_space=pl.ANY),
        scratch_shapes=[pltpu.VMEM((2, r_loc, d // 2), U32), pltpu.VMEM((FFN_TILE, d // 2), U32),
                        pltpu.SemaphoreType.DMA((2,))],
    )
    return pl.pallas_call(
        _dispatch_kernel,
        grid_spec=grid_spec,
        out_shape=jax.ShapeDtypeStruct((lay["r_glob"], d // 2), U32),
        compiler_params=pltpu.CompilerParams(dimension_semantics=("arbitrary",),
                                             vmem_limit_bytes=VMEM_LIMIT),
        name="dispatch",
    )(lay["loc_flat"], lay["glob_flat"], lay["rows_flat"], lay["used"], lay["tail_lo"],
      lay["tail_hi"], x2b, sel_t, lay["loc_col"], lay["loc_row"], lay["cnt_row"])


def _experts_kernel(tile_e_ref, n_used_ref, next_e_ref, xs_hbm, wg_hbm, wu_hbm, wd_hbm, ys_hbm,
                    wg_f, wu_f, wd_f, wg_b, wu_b, wd_b, xbuf, ybuf, sems, in_sems, out_sems):
    i = pl.program_id(0)

    def weight_copies(ex):
        return (pltpu.make_async_copy(wg_hbm.at[ex], wg_f, sems.at[0]),
                pltpu.make_async_copy(wu_hbm.at[ex], wu_f, sems.at[1]),
                pltpu.make_async_copy(wd_hbm.at[ex], wd_f, sems.at[2]))

    def start_weights(ex):
        for cp in weight_copies(ex):
            cp.start(priority=1)

    @pl.when(i == 0)
    def _():
        start_weights(tile_e_ref[0])

    prev = tile_e_ref[jnp.maximum(i - 1, 0)]

    @pl.when((i == 0) | (tile_e_ref[i] != prev))
    def _():
        for cp in weight_copies(tile_e_ref[i]):
            cp.wait()
        wg_b[...] = wg_f[...].astype(BF16)
        wu_b[...] = wu_f[...].astype(BF16)
        wd_b[...] = wd_f[...].astype(BF16)
        nxt = next_e_ref[i]

        @pl.when(nxt >= 0)
        def _():
            start_weights(nxt)

    n_tiles = pl.num_programs(0)
    n_used = n_used_ref[0]
    tile_rows = xbuf.shape[1]

    def tile_in(j):
        return pltpu.make_async_copy(xs_hbm.at[pl.ds(pl.multiple_of(j * tile_rows, tile_rows), tile_rows)],
                                     xbuf.at[j % ROW_SLOTS], in_sems.at[j % ROW_SLOTS])

    def tile_out(j, slot=None):
        slot = j % ROW_SLOTS if slot is None else slot
        return pltpu.make_async_copy(ybuf.at[slot],
                                     ys_hbm.at[pl.ds(pl.multiple_of(j * tile_rows, tile_rows), tile_rows)],
                                     out_sems.at[slot])

    @pl.when(i == 0)
    def _():
        for j in range(ROW_SLOTS - 1):
            @pl.when(j < n_used)
            def _():
                tile_in(j).start()

    @pl.when(i + ROW_SLOTS - 1 < n_used)
    def _():
        tile_in(i + ROW_SLOTS - 1).start()

    @pl.when((i >= ROW_SLOTS) & (i - ROW_SLOTS < n_used))
    def _():
        tile_out(i - ROW_SLOTS).wait()

    @pl.when(i < n_used)
    def _():
        tile_in(i).wait()
        x = _unpack_rows(xbuf[i % ROW_SLOTS]).astype(BF16)
        hid = jax.nn.silu(_dot(x, wg_b[...])) * _dot(x, wu_b[...])
        y = _dot(hid.astype(BF16), wd_b[...])
        ybuf[i % ROW_SLOTS] = _pack_rows(y.astype(BF16).astype(F32))
        tile_out(i).start()

    @pl.when(i == n_tiles - 1)
    def _():
        for back in reversed(range(ROW_SLOTS)):
            @pl.when((i - back >= 0) & (i - back < n_used))
            def _():
                tile_out(i - back).wait()

        ybuf[0] = jnp.zeros(ybuf.shape[1:], ybuf.dtype)

        def each_unused(act):
            def body(j, _):
                act(tile_out(j, slot=0))
                return 0

            lax.fori_loop(n_used, n_tiles, body, 0)

        each_unused(lambda cp: cp.start())
        each_unused(lambda cp: cp.wait())


def _experts(xs, tile_e, n_used, next_e, w_gate, w_up, w_down):
    r_glob, dh = xs.shape
    d, hdim = w_gate.shape[1:]
    n_tiles = r_glob // FFN_TILE

    hbm = pl.BlockSpec(memory_space=pl.ANY)
    grid_spec = pltpu.PrefetchScalarGridSpec(
        num_scalar_prefetch=3,
        grid=(n_tiles,),
        in_specs=[hbm, hbm, hbm, hbm],
        out_specs=hbm,
        scratch_shapes=[pltpu.VMEM((d, hdim), F32), pltpu.VMEM((d, hdim), F32),
                        pltpu.VMEM((hdim, d), F32),
                        pltpu.VMEM((d, hdim), BF16), pltpu.VMEM((d, hdim), BF16),
                        pltpu.VMEM((hdim, d), BF16),
                        pltpu.VMEM((ROW_SLOTS, FFN_TILE, dh), U32),
                        pltpu.VMEM((ROW_SLOTS, FFN_TILE, dh), U32),
                        pltpu.SemaphoreType.DMA((3,)), pltpu.SemaphoreType.DMA((ROW_SLOTS,)),
                        pltpu.SemaphoreType.DMA((ROW_SLOTS,))],
    )
    return pl.pallas_call(
        _experts_kernel,
        grid_spec=grid_spec,
        out_shape=jax.ShapeDtypeStruct((r_glob, dh), U32),
        compiler_params=pltpu.CompilerParams(dimension_semantics=("arbitrary",),
                                             vmem_limit_bytes=VMEM_LIMIT),
        name="experts",
    )(tile_e, n_used, next_e, xs, w_gate, w_up, w_down)


def _combine_kernel(loc_ref, glob_ref, rows_ref, used_ref,
                    base_ref, sel_ref, w_ref, locc_ref, locr_ref, cntr_ref, g_ref, bta_ref, ys_hbm,
                    o_ref, ys_loc, sems):
    b = pl.program_id(0)
    nb = pl.num_programs(0)
    e, tm = sel_ref.shape
    r_loc = ys_loc.shape[1]
    slot = b % 2
    buf = ys_loc.at[slot]

    def fetch_block(blk, s):
        def per_expert(ex, _):
            n = rows_ref[blk * e + ex]

            @pl.when(n > 0)
            def _():
                _segment_copy(ys_hbm, glob_ref[blk * e + ex], ys_loc.at[s], loc_ref[blk * e + ex], n,
                              sems.at[s]).start()

            return 0

        lax.fori_loop(0, e, per_expert, 0)

    @pl.when(b == 0)
    def _():
        fetch_block(b, slot)

    @pl.when(b + 1 < nb)
    def _():
        fetch_block(b + 1, 1 - slot)

    tables = _row_tables(sel_ref[...], locc_ref[0])
    wb = w_ref[...].astype(BF16)
    used = used_ref[b]

    @pl.when(used > 0)
    def _():
        _segment_copy(ys_hbm, 0, buf, 0, used, sems.at[slot]).wait()

    own = _row_owner(0, r_loc, locr_ref[0], cntr_ref[0])
    prow = _dot(jnp.concatenate([own, own], axis=1), tables)
    wrow = _dot(own, wb)
    ri = lax.broadcasted_iota(I32, (r_loc, tm), 0).astype(F32)
    wmat = jnp.where(prow == ri, wrow, 0.0).astype(BF16)
    always = tm * TOP_K
    rvalid = (lax.broadcasted_iota(I32, (r_loc - always, 1), 0) + always) < used
    u = jnp.concatenate([buf[:always, :], jnp.where(rvalid, buf[always:, :], jnp.uint32(0))], axis=0)
    moe = _dot_tn(wmat, _unpack_rows(u).astype(BF16))
    o_ref[...] = _ln(base_ref[...] + moe, g_ref[...], bta_ref[...])


def _combine(base, sel_t, w_t, lay, ys, ln_g, ln_b):
    t, d = base.shape
    e = sel_t.shape[0]
    tm = TM_MOE
    nb = t // tm
    r_loc = tm * TOP_K + e * GRAN
    blk = pl.BlockSpec((e, tm), lambda i, *_: (0, i))
    row_spec = pl.BlockSpec((tm, d), lambda i, *_: (i, 0))
    vec = pl.BlockSpec((1, d), lambda i, *_: (0, 0))
    col = pl.BlockSpec((1, e, 1), lambda i, *_: (i, 0, 0))
    rowv = pl.BlockSpec((1, 1, e), lambda i, *_: (i, 0, 0))
    grid_spec = pltpu.PrefetchScalarGridSpec(
        num_scalar_prefetch=4,
        grid=(nb,),
        in_specs=[row_spec, blk, blk, col, rowv, rowv, vec, vec, pl.BlockSpec(memory_space=pl.ANY)],
        out_specs=row_spec,
        scratch_shapes=[pltpu.VMEM((2, r_loc, d // 2), U32), pltpu.SemaphoreType.DMA((2,))],
    )
    return pl.pallas_call(
        _combine_kernel,
        grid_spec=grid_spec,
        out_shape=jax.ShapeDtypeStruct((t, d), F32),
        compiler_params=pltpu.CompilerParams(dimension_semantics=("arbitrary",),
                                             vmem_limit_bytes=VMEM_LIMIT),
        name="combine",
    )(lay["loc_flat"], lay["glob_flat"], lay["rows_flat"], lay["used"],
      base, sel_t, w_t, lay["loc_col"], lay["loc_row"], lay["cnt_row"], ln_g, ln_b, ys)


def _round_up(x, m):
    return (x + m - 1) // m * m


def _moe_layout(cnt, t):
    nb, e = cnt.shape
    cnt_g = _round_up(cnt, GRAN)
    loc_off = jnp.cumsum(cnt_g, axis=1) - cnt_g
    used = jnp.sum(cnt_g, axis=1)
    gcnt = jnp.sum(cnt_g, axis=0)
    gpad = _round_up(gcnt, FFN_TILE)
    gend = jnp.cumsum(gpad)
    gstart = gend - gpad
    glob_off = gstart[None, :] + jnp.cumsum(cnt_g, axis=0) - cnt_g
    r_glob = _round_up(t * TOP_K + nb * e * (GRAN - 1) + e * (FFN_TILE - 1), FFN_TILE)
    n_tiles = r_glob // FFN_TILE
    n_used = (gend[-1] // FFN_TILE).astype(I32)
    tile_start = jnp.minimum(jnp.arange(n_tiles, dtype=I32), n_used - 1) * FFN_TILE
    tile_e = jnp.minimum(jnp.sum(gend[None, :] <= tile_start[:, None], axis=1), e - 1).astype(I32)
    ids = jnp.arange(e, dtype=I32)
    later_owner = jnp.min(jnp.where((ids[None, :] > ids[:, None]) & (gpad[None, :] > 0), ids[None, :], e),
                          axis=1)
    later_owner = jnp.where(later_owner < e, later_owner, -1)
    next_e = jnp.sum(jnp.where(tile_e[:, None] == ids[None, :], later_owner[None, :], 0), axis=1).astype(I32)
    return dict(next_e=next_e, loc_flat=loc_off.astype(I32).reshape(-1), glob_flat=glob_off.astype(I32).reshape(-1),
                rows_flat=cnt_g.astype(I32).reshape(-1), used=used.astype(I32),
                loc_col=loc_off.astype(F32)[:, :, None], loc_row=loc_off.astype(F32)[:, None, :],
                cnt_row=cnt_g.astype(F32)[:, None, :],
                tail_lo=jnp.append(gstart + gcnt, gend[-1]).astype(I32),
                tail_hi=jnp.append(gend, r_glob).astype(I32),
                tile_e=tile_e, n_used=n_used.reshape(1), r_glob=r_glob)


def kernel(x, mem, ln_in_g, ln_in_b, w_in, b_in, ln_v_g, ln_v_b, w_spatial, b_spatial, w_out,
           ln1_g, ln1_b, w_mem_q, w_mem_kv, w_mem_o, ln2_g, ln2_b, w_router, router_bias,
           w_exp_gate, w_exp_up, w_exp_down, w_sh_gate, w_sh_up, w_sh_down, ln3_g, ln3_b):
    bsz, seq, d = x.shape
    t = bsz * seq
    assert w_in.shape[0] == DEPTH
    x2d = x.reshape(t, d)
    row = lambda a: a.reshape(1, -1)

    oa, q, k, v, gb = _in_proj(x2d, row(ln_in_g), row(ln_in_b), w_in[0].astype(BF16), row(b_in[0]),
                               row(ln_v_g[0]), row(ln_v_b[0]), w_spatial[0], b_spatial[0].T)
    yb = _sb_attention(q.reshape(bsz, seq, d), k.reshape(bsz, seq, d), v.reshape(bsz, seq, d))
    k_mem, v_mem = _mem_kv(mem, w_mem_kv[0].astype(BF16))

    wr_t = w_router[0].T
    wr_hi = wr_t.astype(BF16)
    wr_lo = (wr_t - wr_hi.astype(F32)).astype(BF16)
    base, x2b, logits_t = _mid(
        x2d, row(ln_in_g), row(ln_in_b), oa, yb.reshape(t, d), gb, w_out[0].astype(BF16),
        row(ln1_g[0]), row(ln1_b[0]), w_mem_q[0].astype(BF16), k_mem, v_mem,
        w_mem_o[0].astype(BF16), row(ln2_g[0]), row(ln2_b[0]), wr_hi, wr_lo,
        w_sh_gate[0].astype(BF16), w_sh_up[0].astype(BF16), w_sh_down[0].astype(BF16), seq)

    sel_t, w_t, cnt = _route(logits_t, router_bias[0].reshape(-1, 1))
    lay = _moe_layout(cnt[:, :, 0].astype(I32), t)
    xs = _dispatch(x2b, sel_t, lay)
    ys = _experts(xs, lay["tile_e"], lay["n_used"], lay["next_e"], w_exp_gate[0], w_exp_up[0],
                  w_exp_down[0])
    out = _combine(base, sel_t, w_t, lay, ys, row(ln3_g[0]), row(ln3_b[0]))
    return out.reshape(bsz, seq, d)
```

```python
import functools

import jax
import jax.numpy as jnp
from jax import lax
from jax.experimental import pallas as pl
from jax.experimental.pallas import tpu as pltpu

F32 = jnp.float32
BF16 = jnp.bfloat16
I32 = jnp.int32
U32 = jnp.uint32

LANES = 128
SUBLANES = 8
GRAN = SUBLANES

CHUNK = 128
A_GROUPS = 8
SB_HEADS = 8
SB_HEAD_DIM = 128
MEM_HEADS = 4
MEM_HEAD_DIM = 128
N_EXPERTS = 64
TOP_K = 8
N_GROUPS = 8
TOPK_GROUPS = 4
ROUTED_SCALE = 2.5
LN_EPS = 1e-5
DEPTH = 1
ALPHA = (2 * DEPTH) ** 0.25
LOG2E = 1.4426950408889634
SP_CLAMP = 64.0

TM_PROJ = 512
TM_MID = 512
TQ = 1024
SLAB = 256
TM_MOE = 256
TM_ROUTE = 1024
FFN_TILE = 512
EXPERT_UNROLL = 8
ROW_SLOTS = 3
ROW_SPLIT = 256.0
VMEM_LIMIT = 56 * 1024 * 1024


def _ln(x, g, b):
    mu = jnp.mean(x, axis=-1, keepdims=True)
    xc = x - mu
    var = jnp.mean(xc * xc, axis=-1, keepdims=True)
    return xc * lax.rsqrt(var + LN_EPS) * g + b


def _gelu(x):
    return 0.5 * x * (1.0 + lax.erf(x * (2.0 ** -0.5)))


def _dot(a, b):
    return jnp.dot(a, b, preferred_element_type=F32)


def _dot_nt(a, b):
    return lax.dot_general(a, b, (((1,), (1,)), ((), ())), preferred_element_type=F32)


def _dot_tn(a, b):
    return lax.dot_general(a, b, (((0,), (0,)), ((), ())), preferred_element_type=F32)


def _full(shape):
    n = len(shape)
    return pl.BlockSpec(shape, lambda *_: (0,) * n, pipeline_mode=pl.Buffered(1))


def _in_proj_kernel(x_ref, lg_ref, lb_ref, w_ref, b_ref, vg_ref, vb_ref, ws_ref, bst_ref,
                    oa_ref, q_ref, k_ref, v_ref, gb_ref):
    tm, d = x_ref.shape
    xb = _ln(x_ref[...], lg_ref[...], lb_ref[...]).astype(BF16)

    def seg(i):
        return _dot(xb, w_ref[:, i * d:(i + 1) * d]) + b_ref[:, i * d:(i + 1) * d]

    q_ref[...] = (seg(2) * (SB_HEAD_DIM ** -0.5 * LOG2E)).astype(BF16)
    k_ref[...] = seg(3).astype(BF16)
    v_ref[...] = seg(4).astype(BF16)
    gb_ref[...] = jax.nn.sigmoid(seg(6)).astype(BF16)

    gu = jax.nn.sigmoid(seg(5)) * _gelu(seg(0))
    vln = _ln(_gelu(seg(1)), vg_ref[...], vb_ref[...]).astype(BF16)
    gd = d // A_GROUPS
    row = lax.broadcasted_iota(I32, (CHUNK, CHUNK), 0)
    col = lax.broadcasted_iota(I32, (CHUNK, CHUNK), 1)
    for g in range(A_GROUPS):
        w = jnp.where(col <= row, ws_ref[g], 0.0).astype(BF16)
        bias = bst_ref[:, g:g + 1]
        for c in range(tm // CHUNK):
            rs = slice(c * CHUNK, (c + 1) * CHUNK)
            cs = slice(g * gd, (g + 1) * gd)
            mixed = _dot(w, vln[rs, cs]) + bias
            oa_ref[rs, cs] = (gu[rs, cs] * mixed).astype(BF16)


def _in_proj(x2d, ln_g, ln_b, w_in, b_in, vg, vb, w_s, b_st):
    t, d = x2d.shape
    n_in = w_in.shape[1]
    tm = TM_PROJ
    row_spec = pl.BlockSpec((tm, d), lambda i: (i, 0))
    out = jax.ShapeDtypeStruct((t, d), BF16)
    return pl.pallas_call(
        _in_proj_kernel,
        grid=(t // tm,),
        in_specs=[row_spec, _full((1, d)), _full((1, d)), _full((d, n_in)), _full((1, n_in)),
                  _full((1, d)), _full((1, d)), _full(w_s.shape), _full(b_st.shape)],
        out_specs=[row_spec] * 5,
        out_shape=[out] * 5,
        compiler_params=pltpu.CompilerParams(dimension_semantics=("parallel",),
                                             vmem_limit_bytes=VMEM_LIMIT),
        name="in_proj",
    )(x2d, ln_g, ln_b, w_in, b_in, vg, vb, w_s, b_st)


def _sb_kernel(q_ref, k_ref, v_ref, m_ref, o_ref):
    i = pl.program_id(2)
    tq = q_ref.shape[1]
    nslab = tq // SLAB
    q = q_ref[0]
    later = m_ref[...]

    def slab(qs, j, carry, mask):
        off = pl.multiple_of(j * SLAB, SLAB)
        z = _dot_nt(qs, k_ref[0, pl.ds(off, SLAB), :])
        sp = jnp.maximum(jnp.log(1.0 + jnp.exp2(jnp.minimum(z, SP_CLAMP))) * LOG2E, z)
        if mask is not None:
            sp = jnp.where(mask, sp, 0.0)
        log2_beta = z - sp
        first = sp[:, 0:1]
        cs = _dot(sp.astype(BF16), later)
        a = jnp.exp2(log2_beta - cs - carry)
        if mask is not None:
            a = jnp.where(mask, a, 0.0)
        total = cs[:, 0:1] + first
        return carry + total, _dot(a.astype(BF16), v_ref[0, pl.ds(off, SLAB), :])

    carry = jnp.zeros((tq, 1), F32)
    acc = jnp.zeros((tq, SB_HEAD_DIM), F32)
    for d in reversed(range(nslab)):
        r0 = d * SLAB
        rr = lax.broadcasted_iota(I32, (tq - r0, SLAB), 0)
        cc = lax.broadcasted_iota(I32, (tq - r0, SLAB), 1)
        c_new, contrib = slab(q[r0:], i * nslab + d, carry[r0:], cc < rr)
        a_new = acc[r0:] + contrib
        carry = jnp.concatenate([carry[:r0], c_new], axis=0) if r0 else c_new
        acc = jnp.concatenate([acc[:r0], a_new], axis=0) if r0 else a_new

    def body(n, ca):
        carry, acc = ca
        for u in range(nslab):
            carry, contrib = slab(q, (i - n) * nslab - 1 - u, carry, None)
            acc = acc + contrib
        return carry, acc

    carry, acc = lax.fori_loop(0, i, body, (carry, acc))
    o_ref[0] = acc.astype(o_ref.dtype)


def _sb_attention(q, k, v):
    b, s, w = q.shape
    h = w // SB_HEAD_DIM
    jj = lax.broadcasted_iota(I32, (SLAB, SLAB), 0)
    ss = lax.broadcasted_iota(I32, (SLAB, SLAB), 1)
    mcat = jnp.where(jj > ss, 1.0, 0.0).astype(BF16)
    q_spec = pl.BlockSpec((1, TQ, SB_HEAD_DIM), lambda bi, hi, i: (bi, i, hi))
    kv_spec = pl.BlockSpec((1, s, SB_HEAD_DIM), lambda bi, hi, i: (bi, 0, hi))
    return pl.pallas_call(
        _sb_kernel,
        grid=(b, h, s // TQ),
        in_specs=[q_spec, kv_spec, kv_spec, _full(mcat.shape)],
        out_specs=q_spec,
        out_shape=jax.ShapeDtypeStruct((b, s, w), BF16),
        compiler_params=pltpu.CompilerParams(
            dimension_semantics=("parallel", "parallel", "arbitrary"),
            vmem_limit_bytes=VMEM_LIMIT),
        name="sb_attn",
    )(q, k, v, mcat)


def _mem_kv_kernel(m_ref, w_ref, k_ref, v_ref):
    kv = _dot(m_ref[0].astype(BF16), w_ref[...])
    half = kv.shape[1] // 2
    k_ref[0] = kv[:, :half].astype(BF16)
    v_ref[0] = kv[:, half:].astype(BF16)


def _mem_kv(mem, w_kv):
    b, m, d = mem.shape
    half = w_kv.shape[1] // 2
    out = jax.ShapeDtypeStruct((b, m, half), BF16)
    o_spec = pl.BlockSpec((1, m, half), lambda i: (i, 0, 0))
    return pl.pallas_call(
        _mem_kv_kernel,
        grid=(b,),
        in_specs=[pl.BlockSpec((1, m, d), lambda i: (i, 0, 0)), _full(w_kv.shape)],
        out_specs=[o_spec, o_spec],
        out_shape=[out, out],
        compiler_params=pltpu.CompilerParams(dimension_semantics=("parallel",),
                                             vmem_limit_bytes=VMEM_LIMIT),
        name="mem_kv",
    )(mem, w_kv)


def _mid_kernel(x_ref, lg_ref, lb_ref, oa_ref, yb_ref, gb_ref, wo_ref, l1g_ref, l1b_ref,
                wq_ref, km_ref, vm_ref, wmo_ref, l2g_ref, l2b_ref, wrh_ref, wrl_ref,
                wsg_ref, wsu_ref, wsd_ref,
                base_ref, x2_ref, lgt_ref):
    xln = _ln(x_ref[...], lg_ref[...], lb_ref[...])
    merged = oa_ref[...].astype(F32) + gb_ref[...].astype(F32) * yb_ref[...].astype(F32)
    x1 = _ln(ALPHA * xln + _dot(merged.astype(BF16), wo_ref[...]), l1g_ref[...], l1b_ref[...])

    q = (_dot(x1.astype(BF16), wq_ref[...]) * (MEM_HEAD_DIM ** -0.5)).astype(BF16)
    heads = []
    for h in range(MEM_HEADS):
        hs = slice(h * MEM_HEAD_DIM, (h + 1) * MEM_HEAD_DIM)
        logits = _dot_nt(q[:, hs], km_ref[0, :, hs])
        p = jnp.exp(logits - jnp.max(logits, axis=-1, keepdims=True))
        p = p / jnp.sum(p, axis=-1, keepdims=True)
        heads.append(_dot(p.astype(BF16), vm_ref[0, :, hs]))
    o = jnp.concatenate(heads, axis=1).astype(BF16)
    x2 = _ln(ALPHA * x1 + _dot(o, wmo_ref[...]), l2g_ref[...], l2b_ref[...])

    x2h = x2.astype(BF16)
    x2l = (x2 - x2h.astype(F32)).astype(BF16)
    lgt_ref[...] = (_dot_nt(wrh_ref[...], x2h) + _dot_nt(wrh_ref[...], x2l)
                    + _dot_nt(wrl_ref[...], x2h))
    x2_ref[...] = x2h

    hid = jax.nn.silu(_dot(x2h, wsg_ref[...])) * _dot(x2h, wsu_ref[...])
    base_ref[...] = ALPHA * x2 + _dot(hid.astype(BF16), wsd_ref[...])


def _mid(x2d, ln_g, ln_b, oa, yb, gb, w_out, l1g, l1b, w_q, k_mem, v_mem, w_mo, l2g, l2b,
         wr_hi, wr_lo, w_sg, w_su, w_sd, seq):
    t, d = x2d.shape
    tm = TM_MID
    per_batch = seq // tm
    row_spec = pl.BlockSpec((tm, d), lambda i: (i, 0))
    mem_spec = pl.BlockSpec((1,) + k_mem.shape[1:], lambda i: (i // per_batch, 0, 0))
    vec = _full((1, d))
    return pl.pallas_call(
        _mid_kernel,
        grid=(t // tm,),
        in_specs=[row_spec, vec, vec, row_spec, row_spec, row_spec, _full(w_out.shape), vec, vec,
                  _full(w_q.shape), mem_spec, mem_spec, _full(w_mo.shape), vec, vec,
                  _full(wr_hi.shape), _full(wr_lo.shape),
                  _full(w_sg.shape), _full(w_su.shape), _full(w_sd.shape)],
        out_specs=[row_spec, row_spec, pl.BlockSpec((N_EXPERTS, tm), lambda i: (0, i))],
        out_shape=[jax.ShapeDtypeStruct((t, d), F32), jax.ShapeDtypeStruct((t, d), BF16),
                   jax.ShapeDtypeStruct((N_EXPERTS, t), F32)],
        compiler_params=pltpu.CompilerParams(dimension_semantics=("parallel",),
                                             vmem_limit_bytes=VMEM_LIMIT),
        name="mid",
    )(x2d, ln_g, ln_b, oa, yb, gb, w_out, l1g, l1b, w_q, k_mem, v_mem, w_mo, l2g, l2b,
      wr_hi, wr_lo, w_sg, w_su, w_sd)


def _route_kernel(lgt_ref, bias_ref, sel_ref, w_ref, cnt_ref):
    e, tm = lgt_ref.shape
    per_group = e // N_GROUPS
    scores = jax.nn.sigmoid(lgt_ref[...])
    sel = scores + bias_ref[...]

    g3 = sel.reshape(N_GROUPS, per_group, tm)
    j3 = lax.broadcasted_iota(I32, g3.shape, 1)
    m1 = jnp.max(g3, axis=1, keepdims=True)
    first = jnp.min(jnp.where(g3 == m1, j3, per_group), axis=1, keepdims=True)
    m2 = jnp.max(jnp.where(j3 == first, -jnp.inf, g3), axis=1, keepdims=True)
    gs = (m1 + m2).reshape(N_GROUPS, tm)

    gi = lax.broadcasted_iota(I32, (N_GROUPS, tm), 0)
    grank = jnp.zeros((N_GROUPS, tm), I32)
    for o in range(N_GROUPS):
        other = gs[o:o + 1, :]
        grank += ((other > gs) | ((other == gs) & (o < gi))).astype(I32)
    gmask = (grank < TOPK_GROUPS).astype(F32)
    emask = jnp.broadcast_to(gmask.reshape(N_GROUPS, 1, tm), (N_GROUPS, per_group, tm)).reshape(e, tm)
    cand = jnp.where(emask > 0.5, sel, -jnp.inf)

    ei = lax.broadcasted_iota(I32, (e, tm), 0)
    chosen = jnp.zeros((e, tm), jnp.bool_)
    left = cand
    for _ in range(TOP_K):
        best = jnp.max(left, axis=0, keepdims=True)
        first = jnp.min(jnp.where(left == best, ei, e), axis=0, keepdims=True)
        pick = ei == first
        chosen = chosen | pick
        left = jnp.where(pick, -jnp.inf, left)

    w = jnp.where(chosen, scores, 0.0)
    w = w / jnp.sum(w, axis=0, keepdims=True) * ROUTED_SCALE
    chosen_f = chosen.astype(F32)
    sel_ref[...] = chosen_f
    w_ref[...] = w
    for s in range(tm // TM_MOE):
        part = chosen_f[:, s * TM_MOE:(s + 1) * TM_MOE]
        cnt_ref[s] = jnp.broadcast_to(jnp.sum(part, axis=1, keepdims=True), (e, LANES))


def _route(logits_t, bias_col):
    e, t = logits_t.shape
    tm = TM_ROUTE
    sub = tm // TM_MOE
    blk = pl.BlockSpec((e, tm), lambda i: (0, i))
    return pl.pallas_call(
        _route_kernel,
        grid=(t // tm,),
        in_specs=[blk, _full((e, 1))],
        out_specs=[blk, blk, pl.BlockSpec((sub, e, LANES), lambda i: (i, 0, 0))],
        out_shape=[jax.ShapeDtypeStruct((e, t), F32), jax.ShapeDtypeStruct((e, t), F32),
                   jax.ShapeDtypeStruct((t // TM_MOE, e, LANES), F32)],
        compiler_params=pltpu.CompilerParams(dimension_semantics=("parallel",),
                                             vmem_limit_bytes=VMEM_LIMIT),
        name="route",
    )(logits_t, bias_col)


def _pack_rows(x):
    words = []
    for g in range(x.shape[1] // (2 * LANES)):
        hi = lax.bitcast_convert_type(x[:, 2 * g * LANES:(2 * g + 1) * LANES], U32)
        lo = lax.bitcast_convert_type(x[:, (2 * g + 1) * LANES:(2 * g + 2) * LANES], U32)
        words.append((hi & jnp.uint32(0xFFFF0000)) | lax.shift_right_logical(lo, jnp.uint32(16)))
    return jnp.concatenate(words, axis=1)


def _unpack_rows(u):
    tiles = []
    for g in range(u.shape[1] // LANES):
        w = u[:, g * LANES:(g + 1) * LANES]
        tiles.append(lax.bitcast_convert_type(w & jnp.uint32(0xFFFF0000), F32))
        tiles.append(lax.bitcast_convert_type(lax.shift_left(w, jnp.uint32(16)), F32))
    return jnp.concatenate(tiles, axis=1)


def _row_tables(sel, loc_col):
    e, tm = sel.shape
    tj = lax.broadcasted_iota(I32, (tm, tm), 0)
    tt = lax.broadcasted_iota(I32, (tm, tm), 1)
    before_t = jnp.where(tj < tt, 1.0, 0.0).astype(BF16)
    rank = _dot(sel.astype(BF16), before_t)
    pos = jnp.where(sel > 0.5, loc_col + rank, -1.0)
    hi = jnp.floor(pos * (1.0 / ROW_SPLIT)) * ROW_SPLIT
    return jnp.concatenate([hi, pos - hi], axis=0).astype(BF16)


def _row_owner(c, rchunk, loc_row, cnt_row):
    e = loc_row.shape[1]
    ri = (lax.broadcasted_iota(I32, (rchunk, e), 0) + c * rchunk).astype(F32)
    return jnp.where((ri >= loc_row) & (ri < loc_row + cnt_row), 1.0, 0.0).astype(BF16)


def _for_each_expert(e, fn):
    def trip(i, _):
        for u in range(EXPERT_UNROLL):
            fn(i * EXPERT_UNROLL + u)
        return 0

    lax.fori_loop(0, e // EXPERT_UNROLL, trip, 0)


def _segment_copy(src, src_row, dst, dst_row, rows, sem):
    rows = pl.multiple_of(rows, GRAN)
    return pltpu.make_async_copy(src.at[pl.ds(pl.multiple_of(src_row, GRAN), rows)],
                                 dst.at[pl.ds(pl.multiple_of(dst_row, GRAN), rows)], sem)


def _dispatch_kernel(loc_ref, glob_ref, rows_ref, used_ref, tail_lo_ref, tail_hi_ref,
                     x_ref, sel_ref, locc_ref, locr_ref, cntr_ref, xs_hbm, xs_loc, zeros, sems):
    b = pl.program_id(0)
    nb = pl.num_programs(0)
    e, tm = sel_ref.shape
    r_loc = xs_loc.shape[1]
    slot = b % 2
    buf = xs_loc.at[slot]
    sem = sems.at[slot]

    def wait_block(blk, s):
        @pl.when(used_ref[blk] > 0)
        def _():
            _segment_copy(xs_loc.at[s], 0, xs_hbm, 0, used_ref[blk], sems.at[s]).wait()

    @pl.when(b >= 2)
    def _():
        wait_block(b - 2, slot)

    tables = _row_tables(sel_ref[...], locc_ref[0])
    x = x_ref[...]
    own = _row_owner(0, r_loc, locr_ref[0], cntr_ref[0])
    prow = _dot(jnp.concatenate([own, own], axis=1), tables)
    ri = lax.broadcasted_iota(I32, (r_loc, tm), 0).astype(F32)
    onehot = jnp.where(prow == ri, 1.0, 0.0).astype(BF16)
    buf[...] = _pack_rows(_dot(onehot, x))

    def per_expert(ex):
        n = rows_ref[b * e + ex]

        @pl.when(n > 0)
        def _():
            _segment_copy(buf, loc_ref[b * e + ex], xs_hbm, glob_ref[b * e + ex], n, sem).start()

    _for_each_expert(e, per_expert)

    @pl.when(b == nb - 1)
    def _():
        @pl.when(b >= 1)
        def _():
            wait_block(b - 1, 1 - slot)

        wait_block(b, slot)
        zeros[...] = jnp.zeros_like(zeros)
        zrows = zeros.shape[0]

        def each_piece(act):
            def per_tail(ex, _):
                lo = tail_lo_ref[ex]
                n = tail_hi_ref[ex] - lo

                def per_piece(j, _):
                    rows = jnp.minimum(n - j * zrows, zrows)
                    act(_segment_copy(zeros, 0, xs_hbm, lo + j * zrows, rows, sem))
                    return 0

                lax.fori_loop(0, (n + zrows - 1) // zrows, per_piece, 0)
                return 0

            lax.fori_loop(0, tail_lo_ref.shape[0], per_tail, 0)

        each_piece(lambda cp: cp.start())
        each_piece(lambda cp: cp.wait())


def _dispatch(x2b, sel_t, lay):
    t, d = x2b.shape
    e = sel_t.shape[0]
    tm = TM_MOE
    nb = t // tm
    r_loc = tm * TOP_K + e * (GRAN - 1)
    col = pl.BlockSpec((1, e, 1), lambda i, *_: (i, 0, 0))
    rowv = pl.BlockSpec((1, 1, e), lambda i, *_: (i, 0, 0))
    grid_spec = pltpu.PrefetchScalarGridSpec(
        num_scalar_prefetch=6,
        grid=(nb,),
        in_specs=[pl.BlockSpec((tm, d), lambda i, *_: (i, 0)),
                  pl.BlockSpec((e, tm), lambda i, *_: (0, i)), col, rowv, rowv],
        out_specs=pl.BlockSpec(memory_space=pl.ANY),
        scratch_shapes=[pltpu.VMEM((2, r_loc, d // 2), U32), pltpu.VMEM((FFN_TILE, d // 2), U32),
                        pltpu.SemaphoreType.DMA((2,))],
    )
    return pl.pallas_call(
        _dispatch_kernel,
        grid_spec=grid_spec,
        out_shape=jax.ShapeDtypeStruct((lay["r_glob"], d // 2), U32),
        compiler_params=pltpu.CompilerParams(dimension_semantics=("arbitrary",),
                                             vmem_limit_bytes=VMEM_LIMIT),
        name="dispatch",
    )(lay["loc_flat"], lay["glob_flat"], lay["rows_flat"], lay["used"], lay["tail_lo"],
      lay["tail_hi"], x2b, sel_t, lay["loc_col"], lay["loc_row"], lay["cnt_row"])


def _experts_kernel(tile_e_ref, n_used_ref, next_e_ref, xs_hbm, wg_hbm, wu_hbm, wd_hbm, ys_hbm,
                    wg_f, wu_f, wd_f, wg_b, wu_b, wd_b, xbuf, ybuf, sems, in_sems, out_sems):
    i = pl.program_id(0)

    def weight_copies(ex):
        return (pltpu.make_async_copy(wg_hbm.at[ex], wg_f, sems.at[0]),
                pltpu.make_async_copy(wu_hbm.at[ex], wu_f, sems.at[1]),
                pltpu.make_async_copy(wd_hbm.at[ex], wd_f, sems.at[2]))

    def start_weights(ex):
        for cp in weight_copies(ex):
            cp.start(priority=1)

    @pl.when(i == 0)
    def _():
        start_weights(tile_e_ref[0])

    prev = tile_e_ref[jnp.maximum(i - 1, 0)]

    @pl.when((i == 0) | (tile_e_ref[i] != prev))
    def _():
        for cp in weight_copies(tile_e_ref[i]):
            cp.wait()
        wg_b[...] = wg_f[...].astype(BF16)
        wu_b[...] = wu_f[...].astype(BF16)
        wd_b[...] = wd_f[...].astype(BF16)
        nxt = next_e_ref[i]

        @pl.when(nxt >= 0)
        def _():
            start_weights(nxt)

    n_tiles = pl.num_programs(0)
    n_used = n_used_ref[0]
    tile_rows = xbuf.shape[1]

    def tile_in(j):
        return pltpu.make_async_copy(xs_hbm.at[pl.ds(pl.multiple_of(j * tile_rows, tile_rows), tile_rows)],
                                     xbuf.at[j % ROW_SLOTS], in_sems.at[j % ROW_SLOTS])

    def tile_out(j, slot=None):
        slot = j % ROW_SLOTS if slot is None else slot
        return pltpu.make_async_copy(ybuf.at[slot],
                                     ys_hbm.at[pl.ds(pl.multiple_of(j * tile_rows, tile_rows), tile_rows)],
                                     out_sems.at[slot])

    @pl.when(i == 0)
    def _():
        for j in range(ROW_SLOTS - 1):
            @pl.when(j < n_used)
            def _():
                tile_in(j).start()

    @pl.when(i + ROW_SLOTS - 1 < n_used)
    def _():
        tile_in(i + ROW_SLOTS - 1).start()

    @pl.when((i >= ROW_SLOTS) & (i - ROW_SLOTS < n_used))
    def _():
        tile_out(i - ROW_SLOTS).wait()

    @pl.when(i < n_used)
    def _():
        tile_in(i).wait()
        x = _unpack_rows(xbuf[i % ROW_SLOTS]).astype(BF16)
        hid = jax.nn.silu(_dot(x, wg_b[...])) * _dot(x, wu_b[...])
        y = _dot(hid.astype(BF16), wd_b[...])
        ybuf[i % ROW_SLOTS] = _pack_rows(y.astype(BF16).astype(F32))
        tile_out(i).start()

    @pl.when(i == n_tiles - 1)
    def _():
        for back in reversed(range(ROW_SLOTS)):
            @pl.when((i - back >= 0) & (i - back < n_used))
            def _():
                tile_out(i - back).wait()

        ybuf[0] = jnp.zeros(ybuf.shape[1:], ybuf.dtype)

        def each_unused(act):
            def body(j, _):
                act(tile_out(j, slot=0))
                return 0

            lax.fori_loop(n_used, n_tiles, body, 0)

        each_unused(lambda cp: cp.start())
        each_unused(lambda cp: cp.wait())


def _experts(xs, tile_e, n_used, next_e, w_gate, w_up, w_down):
    r_glob, dh = xs.shape
    d, hdim = w_gate.shape[1:]
    n_tiles = r_glob // FFN_TILE

    hbm = pl.BlockSpec(memory_space=pl.ANY)
    grid_spec = pltpu.PrefetchScalarGridSpec(
        num_scalar_prefetch=3,
        grid=(n_tiles,),
        in_specs=[hbm, hbm, hbm, hbm],
        out_specs=hbm,
        scratch_shapes=[pltpu.VMEM((d, hdim), F32), pltpu.VMEM((d, hdim), F32),
                        pltpu.VMEM((hdim, d), F32),
                        pltpu.VMEM((d, hdim), BF16), pltpu.VMEM((d, hdim), BF16),
                        pltpu.VMEM((hdim, d), BF16),
                        pltpu.VMEM((ROW_SLOTS, FFN_TILE, dh), U32),
                        pltpu.VMEM((ROW_SLOTS, FFN_TILE, dh), U32),
                        pltpu.SemaphoreType.DMA((3,)), pltpu.SemaphoreType.DMA((ROW_SLOTS,)),
                        pltpu.SemaphoreType.DMA((ROW_SLOTS,))],
    )
    return pl.pallas_call(
        _experts_kernel,
        grid_spec=grid_spec,
        out_shape=jax.ShapeDtypeStruct((r_glob, dh), U32),
        compiler_params=pltpu.CompilerParams(dimension_semantics=("arbitrary",),
                                             vmem_limit_bytes=VMEM_LIMIT),
        name="experts",
    )(tile_e, n_used, next_e, xs, w_gate, w_up, w_down)


def _combine_kernel(loc_ref, glob_ref, rows_ref, used_ref,
                    base_ref, sel_ref, w_ref, locc_ref, locr_ref, cntr_ref, g_ref, bta_ref, ys_hbm,
                    o_ref, ys_loc, sems):
    b = pl.program_id(0)
    nb = pl.num_programs(0)
    e, tm = sel_ref.shape
    r_loc = ys_loc.shape[1]
    slot = b % 2
    buf = ys_loc.at[slot]

    def fetch_block(blk, s):
        def per_expert(ex):
            n = rows_ref[blk * e + ex]

            @pl.when(n > 0)
            def _():
                _segment_copy(ys_hbm, glob_ref[blk * e + ex], ys_loc.at[s], loc_ref[blk * e + ex], n,
                              sems.at[s]).start()

        _for_each_expert(e, per_expert)

    @pl.when(b == 0)
    def _():
        fetch_block(b, slot)

    @pl.when(b + 1 < nb)
    def _():
        fetch_block(b + 1, 1 - slot)

    tables = _row_tables(sel_ref[...], locc_ref[0])
    wb = w_ref[...].astype(BF16)
    used = used_ref[b]

    @pl.when(used > 0)
    def _():
        _segment_copy(ys_hbm, 0, buf, 0, used, sems.at[slot]).wait()

    own = _row_owner(0, r_loc, locr_ref[0], cntr_ref[0])
    prow = _dot(jnp.concatenate([own, own], axis=1), tables)
    wrow = _dot(own, wb)
    ri = lax.broadcasted_iota(I32, (r_loc, tm), 0).astype(F32)
    wmat = jnp.where(prow == ri, wrow, 0.0).astype(BF16)
    always = tm * TOP_K
    rvalid = (lax.broadcasted_iota(I32, (r_loc - always, 1), 0) + always) < used
    u = jnp.concatenate([buf[:always, :], jnp.where(rvalid, buf[always:, :], jnp.uint32(0))], axis=0)
    moe = _dot_tn(wmat, _unpack_rows(u).astype(BF16))
    o_ref[...] = _ln(base_ref[...] + moe, g_ref[...], bta_ref[...])


def _combine(base, sel_t, w_t, lay, ys, ln_g, ln_b):
    t, d = base.shape
    e = sel_t.shape[0]
    tm = TM_MOE
    nb = t // tm
    r_loc = tm * TOP_K + e * (GRAN - 1)
    blk = pl.BlockSpec((e, tm), lambda i, *_: (0, i))
    row_spec = pl.BlockSpec((tm, d), lambda i, *_: (i, 0))
    vec = pl.BlockSpec((1, d), lambda i, *_: (0, 0))
    col = pl.BlockSpec((1, e, 1), lambda i, *_: (i, 0, 0))
    rowv = pl.BlockSpec((1, 1, e), lambda i, *_: (i, 0, 0))
    grid_spec = pltpu.PrefetchScalarGridSpec(
        num_scalar_prefetch=4,
        grid=(nb,),
        in_specs=[row_spec, blk, blk, col, rowv, rowv, vec, vec, pl.BlockSpec(memory_space=pl.ANY)],
        out_specs=row_spec,
        scratch_shapes=[pltpu.VMEM((2, r_loc, d // 2), U32), pltpu.SemaphoreType.DMA((2,))],
    )
    return pl.pallas_call(
        _combine_kernel,
        grid_spec=grid_spec,
        out_shape=jax.ShapeDtypeStruct((t, d), F32),
        compiler_params=pltpu.CompilerParams(dimension_semantics=("arbitrary",),
                                             vmem_limit_bytes=VMEM_LIMIT),
        name="combine",
    )(lay["loc_flat"], lay["glob_flat"], lay["rows_flat"], lay["used"],
      base, sel_t, w_t, lay["loc_col"], lay["loc_row"], lay["cnt_row"], ln_g, ln_b, ys)


def _round_up(x, m):
    return (x + m - 1) // m * m


def _moe_layout(cnt, t):
    nb, e = cnt.shape
    cnt_g = _round_up(cnt, GRAN)
    loc_off = jnp.cumsum(cnt_g, axis=1) - cnt_g
    used = jnp.sum(cnt_g, axis=1)
    gcnt = jnp.sum(cnt_g, axis=0)
    gpad = _round_up(gcnt, FFN_TILE)
    gend = jnp.cumsum(gpad)
    gstart = gend - gpad
    glob_off = gstart[None, :] + jnp.cumsum(cnt_g, axis=0) - cnt_g
    r_glob = _round_up(t * TOP_K + nb * e * (GRAN - 1) + e * (FFN_TILE - 1), FFN_TILE)
    n_tiles = r_glob // FFN_TILE
    n_used = (gend[-1] // FFN_TILE).astype(I32)
    tile_start = jnp.minimum(jnp.arange(n_tiles, dtype=I32), n_used - 1) * FFN_TILE
    tile_e = jnp.minimum(jnp.sum(gend[None, :] <= tile_start[:, None], axis=1), e - 1).astype(I32)
    ids = jnp.arange(e, dtype=I32)
    later_owner = jnp.min(jnp.where((ids[None, :] > ids[:, None]) & (gpad[None, :] > 0), ids[None, :], e),
                          axis=1)
    later_owner = jnp.where(later_owner < e, later_owner, -1)
    next_e = jnp.sum(jnp.where(tile_e[:, None] == ids[None, :], later_owner[None, :], 0), axis=1).astype(I32)
    return dict(next_e=next_e, loc_flat=loc_off.astype(I32).reshape(-1), glob_flat=glob_off.astype(I32).reshape(-1),
                rows_flat=cnt_g.astype(I32).reshape(-1), used=used.astype(I32),
                loc_col=loc_off.astype(F32)[:, :, None], loc_row=loc_off.astype(F32)[:, None, :],
                cnt_row=cnt_g.astype(F32)[:, None, :],
                tail_lo=jnp.append(gstart + gcnt, gend[-1]).astype(I32),
                tail_hi=jnp.append(gend, r_glob).astype(I32),
                tile_e=tile_e, n_used=n_used.reshape(1), r_glob=r_glob)


def kernel(x, mem, ln_in_g, ln_in_b, w_in, b_in, ln_v_g, ln_v_b, w_spatial, b_spatial, w_out,
           ln1_g, ln1_b, w_mem_q, w_mem_kv, w_mem_o, ln2_g, ln2_b, w_router, router_bias,
           w_exp_gate, w_exp_up, w_exp_down, w_sh_gate, w_sh_up, w_sh_down, ln3_g, ln3_b):
    bsz, seq, d = x.shape
    t = bsz * seq
    assert w_in.shape[0] == DEPTH
    x2d = x.reshape(t, d)
    row = lambda a: a.reshape(1, -1)

    oa, q, k, v, gb = _in_proj(x2d, row(ln_in_g), row(ln_in_b), w_in[0].astype(BF16), row(b_in[0]),
                               row(ln_v_g[0]), row(ln_v_b[0]), w_spatial[0], b_spatial[0].T)
    yb = _sb_attention(q.reshape(bsz, seq, d), k.reshape(bsz, seq, d), v.reshape(bsz, seq, d))
    k_mem, v_mem = _mem_kv(mem, w_mem_kv[0].astype(BF16))

    wr_t = w_router[0].T
    wr_hi = wr_t.astype(BF16)
    wr_lo = (wr_t - wr_hi.astype(F32)).astype(BF16)
    base, x2b, logits_t = _mid(
        x2d, row(ln_in_g), row(ln_in_b), oa, yb.reshape(t, d), gb, w_out[0].astype(BF16),
        row(ln1_g[0]), row(ln1_b[0]), w_mem_q[0].astype(BF16), k_mem, v_mem,
        w_mem_o[0].astype(BF16), row(ln2_g[0]), row(ln2_b[0]), wr_hi, wr_lo,
        w_sh_gate[0].astype(BF16), w_sh_up[0].astype(BF16), w_sh_down[0].astype(BF16), seq)

    sel_t, w_t, cnt = _route(logits_t, router_bias[0].reshape(-1, 1))
    lay = _moe_layout(cnt[:, :, 0].astype(I32), t)
    xs = _dispatch(x2b, sel_t, lay)
    ys = _experts(xs, lay["tile_e"], lay["n_used"], lay["next_e"], w_exp_gate[0], w_exp_up[0],
                  w_exp_down[0])
    out = _combine(base, sel_t, w_t, lay, ys, row(ln3_g[0]), row(ln3_b[0]))
    return out.reshape(bsz, seq, d)
```

```python
import functools

import jax
import jax.numpy as jnp
from jax import lax
from jax.experimental import pallas as pl
from jax.experimental.pallas import tpu as pltpu

F32 = jnp.float32
BF16 = jnp.bfloat16
I32 = jnp.int32
U32 = jnp.uint32

LANES = 128
SUBLANES = 8
GRAN = SUBLANES

CHUNK = 128
A_GROUPS = 8
SB_HEADS = 8
SB_HEAD_DIM = 128
MEM_HEADS = 4
MEM_HEAD_DIM = 128
N_EXPERTS = 64
TOP_K = 8
N_GROUPS = 8
TOPK_GROUPS = 4
ROUTED_SCALE = 2.5
LN_EPS = 1e-5
DEPTH = 1
ALPHA = (2 * DEPTH) ** 0.25
LOG2E = 1.4426950408889634
SP_CLAMP = 64.0

TM_PROJ = 512
TM_MID = 512
TQ = 1024
SLAB = 256
TM_MOE = 256
TM_ROUTE = 1024
FFN_TILE = 512
EXPERT_UNROLL = 8
ROW_SLOTS = 3
ROW_SPLIT = 256.0
VMEM_LIMIT = 56 * 1024 * 1024


def _ln(x, g, b):
    mu = jnp.mean(x, axis=-1, keepdims=True)
    xc = x - mu
    var = jnp.mean(xc * xc, axis=-1, keepdims=True)
    return xc * lax.rsqrt(var + LN_EPS) * g + b


def _gelu(x):
    return 0.5 * x * (1.0 + lax.erf(x * (2.0 ** -0.5)))


def _dot(a, b):
    return jnp.dot(a, b, preferred_element_type=F32)


def _dot_nt(a, b):
    return lax.dot_general(a, b, (((1,), (1,)), ((), ())), preferred_element_type=F32)


def _dot_tn(a, b):
    return lax.dot_general(a, b, (((0,), (0,)), ((), ())), preferred_element_type=F32)


def _full(shape):
    n = len(shape)
    return pl.BlockSpec(shape, lambda *_: (0,) * n, pipeline_mode=pl.Buffered(1))


def _in_proj_kernel(x_ref, lg_ref, lb_ref, w_ref, b_ref, vg_ref, vb_ref, ws_ref, bst_ref,
                    oa_ref, q_ref, k_ref, v_ref, gb_ref):
    tm, d = x_ref.shape
    xb = _ln(x_ref[...], lg_ref[...], lb_ref[...]).astype(BF16)

    def seg(i):
        return _dot(xb, w_ref[:, i * d:(i + 1) * d]) + b_ref[:, i * d:(i + 1) * d]

    q_ref[...] = (seg(2) * (SB_HEAD_DIM ** -0.5 * LOG2E)).astype(BF16)
    k_ref[...] = seg(3).astype(BF16)
    v_ref[...] = seg(4).astype(BF16)
    gb_ref[...] = jax.nn.sigmoid(seg(6)).astype(BF16)

    gu = jax.nn.sigmoid(seg(5)) * _gelu(seg(0))
    vln = _ln(_gelu(seg(1)), vg_ref[...], vb_ref[...]).astype(BF16)
    gd = d // A_GROUPS
    row = lax.broadcasted_iota(I32, (CHUNK, CHUNK), 0)
    col = lax.broadcasted_iota(I32, (CHUNK, CHUNK), 1)
    for g in range(A_GROUPS):
        w = jnp.where(col <= row, ws_ref[g], 0.0).astype(BF16)
        bias = bst_ref[:, g:g + 1]
        for c in range(tm // CHUNK):
            rs = slice(c * CHUNK, (c + 1) * CHUNK)
            cs = slice(g * gd, (g + 1) * gd)
            mixed = _dot(w, vln[rs, cs]) + bias
            oa_ref[rs, cs] = (gu[rs, cs] * mixed).astype(BF16)


def _in_proj(x2d, ln_g, ln_b, w_in, b_in, vg, vb, w_s, b_st):
    t, d = x2d.shape
    n_in = w_in.shape[1]
    tm = TM_PROJ
    row_spec = pl.BlockSpec((tm, d), lambda i: (i, 0))
    out = jax.ShapeDtypeStruct((t, d), BF16)
    return pl.pallas_call(
        _in_proj_kernel,
        grid=(t // tm,),
        in_specs=[row_spec, _full((1, d)), _full((1, d)), _full((d, n_in)), _full((1, n_in)),
                  _full((1, d)), _full((1, d)), _full(w_s.shape), _full(b_st.shape)],
        out_specs=[row_spec] * 5,
        out_shape=[out] * 5,
        compiler_params=pltpu.CompilerParams(dimension_semantics=("parallel",),
                                             vmem_limit_bytes=VMEM_LIMIT),
        name="in_proj",
    )(x2d, ln_g, ln_b, w_in, b_in, vg, vb, w_s, b_st)


def _sb_kernel(q_ref, k_ref, v_ref, m_ref, o_ref):
    i = pl.program_id(2)
    tq = q_ref.shape[1]
    nslab = tq // SLAB
    q = q_ref[0]
    later = m_ref[...]

    def slab(qs, j, carry, mask):
        off = pl.multiple_of(j * SLAB, SLAB)
        z = _dot_nt(qs, k_ref[0, pl.ds(off, SLAB), :])
        sp = jnp.maximum(jnp.log(1.0 + jnp.exp2(jnp.minimum(z, SP_CLAMP))) * LOG2E, z)
        if mask is not None:
            sp = jnp.where(mask, sp, 0.0)
        log2_beta = z - sp
        first = sp[:, 0:1]
        cs = _dot(sp.astype(BF16), later)
        a = jnp.exp2(log2_beta - cs - carry)
        if mask is not None:
            a = jnp.where(mask, a, 0.0)
        total = cs[:, 0:1] + first
        return carry + total, _dot(a.astype(BF16), v_ref[0, pl.ds(off, SLAB), :])

    carry = jnp.zeros((tq, 1), F32)
    acc = jnp.zeros((tq, SB_HEAD_DIM), F32)
    for d in reversed(range(nslab)):
        r0 = d * SLAB
        rr = lax.broadcasted_iota(I32, (tq - r0, SLAB), 0)
        cc = lax.broadcasted_iota(I32, (tq - r0, SLAB), 1)
        c_new, contrib = slab(q[r0:], i * nslab + d, carry[r0:], cc < rr)
        a_new = acc[r0:] + contrib
        carry = jnp.concatenate([carry[:r0], c_new], axis=0) if r0 else c_new
        acc = jnp.concatenate([acc[:r0], a_new], axis=0) if r0 else a_new

    def body(n, ca):
        carry, acc = ca
        for u in range(nslab):
            carry, contrib = slab(q, (i - n) * nslab - 1 - u, carry, None)
            acc = acc + contrib
        return carry, acc

    carry, acc = lax.fori_loop(0, i, body, (carry, acc))
    o_ref[0] = acc.astype(o_ref.dtype)


def _sb_attention(q, k, v):
    b, s, w = q.shape
    h = w // SB_HEAD_DIM
    jj = lax.broadcasted_iota(I32, (SLAB, SLAB), 0)
    ss = lax.broadcasted_iota(I32, (SLAB, SLAB), 1)
    mcat = jnp.where(jj > ss, 1.0, 0.0).astype(BF16)
    q_spec = pl.BlockSpec((1, TQ, SB_HEAD_DIM), lambda bi, hi, i: (bi, i, hi))
    kv_spec = pl.BlockSpec((1, s, SB_HEAD_DIM), lambda bi, hi, i: (bi, 0, hi))
    return pl.pallas_call(
        _sb_kernel,
        grid=(b, h, s // TQ),
        in_specs=[q_spec, kv_spec, kv_spec, _full(mcat.shape)],
        out_specs=q_spec,
        out_shape=jax.ShapeDtypeStruct((b, s, w), BF16),
        compiler_params=pltpu.CompilerParams(
            dimension_semantics=("parallel", "parallel", "arbitrary"),
            vmem_limit_bytes=VMEM_LIMIT),
        name="sb_attn",
    )(q, k, v, mcat)


def _mem_kv_kernel(m_ref, w_ref, k_ref, v_ref):
    kv = _dot(m_ref[0].astype(BF16), w_ref[...])
    half = kv.shape[1] // 2
    k_ref[0] = kv[:, :half].astype(BF16)
    v_ref[0] = kv[:, half:].astype(BF16)


def _mem_kv(mem, w_kv):
    b, m, d = mem.shape
    half = w_kv.shape[1] // 2
    out = jax.ShapeDtypeStruct((b, m, half), BF16)
    o_spec = pl.BlockSpec((1, m, half), lambda i: (i, 0, 0))
    return pl.pallas_call(
        _mem_kv_kernel,
        grid=(b,),
        in_specs=[pl.BlockSpec((1, m, d), lambda i: (i, 0, 0)), _full(w_kv.shape)],
        out_specs=[o_spec, o_spec],
        out_shape=[out, out],
        compiler_params=pltpu.CompilerParams(dimension_semantics=("parallel",),
                                             vmem_limit_bytes=VMEM_LIMIT),
        name="mem_kv",
    )(mem, w_kv)


def _mid_kernel(x_ref, lg_ref, lb_ref, oa_ref, yb_ref, gb_ref, wo_ref, l1g_ref, l1b_ref,
                wq_ref, km_ref, vm_ref, wmo_ref, l2g_ref, l2b_ref, wrh_ref, wrl_ref,
                wsg_ref, wsu_ref, wsd_ref,
                base_ref, x2_ref, lgt_ref):
    xln = _ln(x_ref[...], lg_ref[...], lb_ref[...])
    merged = oa_ref[...].astype(F32) + gb_ref[...].astype(F32) * yb_ref[...].astype(F32)
    x1 = _ln(ALPHA * xln + _dot(merged.astype(BF16), wo_ref[...]), l1g_ref[...], l1b_ref[...])

    q = (_dot(x1.astype(BF16), wq_ref[...]) * (MEM_HEAD_DIM ** -0.5)).astype(BF16)
    heads = []
    for h in range(MEM_HEADS):
        hs = slice(h * MEM_HEAD_DIM, (h + 1) * MEM_HEAD_DIM)
        logits = _dot_nt(q[:, hs], km_ref[0, :, hs])
        p = jnp.exp(logits - jnp.max(logits, axis=-1, keepdims=True))
        p = p / jnp.sum(p, axis=-1, keepdims=True)
        heads.append(_dot(p.astype(BF16), vm_ref[0, :, hs]))
    o = jnp.concatenate(heads, axis=1).astype(BF16)
    x2 = _ln(ALPHA * x1 + _dot(o, wmo_ref[...]), l2g_ref[...], l2b_ref[...])

    x2h = x2.astype(BF16)
    x2l = (x2 - x2h.astype(F32)).astype(BF16)
    lgt_ref[...] = (_dot_nt(wrh_ref[...], x2h) + _dot_nt(wrh_ref[...], x2l)
                    + _dot_nt(wrl_ref[...], x2h))
    x2_ref[...] = x2h

    hid = jax.nn.silu(_dot(x2h, wsg_ref[...])) * _dot(x2h, wsu_ref[...])
    base_ref[...] = ALPHA * x2 + _dot(hid.astype(BF16), wsd_ref[...])


def _mid(x2d, ln_g, ln_b, oa, yb, gb, w_out, l1g, l1b, w_q, k_mem, v_mem, w_mo, l2g, l2b,
         wr_hi, wr_lo, w_sg, w_su, w_sd, seq):
    t, d = x2d.shape
    tm = TM_MID
    per_batch = seq // tm
    row_spec = pl.BlockSpec((tm, d), lambda i: (i, 0))
    mem_spec = pl.BlockSpec((1,) + k_mem.shape[1:], lambda i: (i // per_batch, 0, 0))
    vec = _full((1, d))
    return pl.pallas_call(
        _mid_kernel,
        grid=(t // tm,),
        in_specs=[row_spec, vec, vec, row_spec, row_spec, row_spec, _full(w_out.shape), vec, vec,
                  _full(w_q.shape), mem_spec, mem_spec, _full(w_mo.shape), vec, vec,
                  _full(wr_hi.shape), _full(wr_lo.shape),
                  _full(w_sg.shape), _full(w_su.shape), _full(w_sd.shape)],
        out_specs=[row_spec, row_spec, pl.BlockSpec((N_EXPERTS, tm), lambda i: (0, i))],
        out_shape=[jax.ShapeDtypeStruct((t, d), F32), jax.ShapeDtypeStruct((t, d), BF16),
                   jax.ShapeDtypeStruct((N_EXPERTS, t), F32)],
        compiler_params=pltpu.CompilerParams(dimension_semantics=("parallel",),
                                             vmem_limit_bytes=VMEM_LIMIT),
        name="mid",
    )(x2d, ln_g, ln_b, oa, yb, gb, w_out, l1g, l1b, w_q, k_mem, v_mem, w_mo, l2g, l2b,
      wr_hi, wr_lo, w_sg, w_su, w_sd)


def _route_kernel(lgt_ref, bias_ref, sel_ref, w_ref, cnt_ref):
    e, tm = lgt_ref.shape
    per_group = e // N_GROUPS
    scores = jax.nn.sigmoid(lgt_ref[...])
    sel = scores + bias_ref[...]

    g3 = sel.reshape(N_GROUPS, per_group, tm)
    j3 = lax.broadcasted_iota(I32, g3.shape, 1)
    m1 = jnp.max(g3, axis=1, keepdims=True)
    first = jnp.min(jnp.where(g3 == m1, j3, per_group), axis=1, keepdims=True)
    m2 = jnp.max(jnp.where(j3 == first, -jnp.inf, g3), axis=1, keepdims=True)
    gs = (m1 + m2).reshape(N_GROUPS, tm)

    gi = lax.broadcasted_iota(I32, (N_GROUPS, tm), 0)
    grank = jnp.zeros((N_GROUPS, tm), I32)
    for o in range(N_GROUPS):
        other = gs[o:o + 1, :]
        grank += ((other > gs) | ((other == gs) & (o < gi))).astype(I32)
    gmask = (grank < TOPK_GROUPS).astype(F32)
    emask = jnp.broadcast_to(gmask.reshape(N_GROUPS, 1, tm), (N_GROUPS, per_group, tm)).reshape(e, tm)
    cand = jnp.where(emask > 0.5, sel, -jnp.inf)

    ei = lax.broadcasted_iota(I32, (e, tm), 0)
    chosen = jnp.zeros((e, tm), jnp.bool_)
    left = cand
    for _ in range(TOP_K):
        best = jnp.max(left, axis=0, keepdims=True)
        first = jnp.min(jnp.where(left == best, ei, e), axis=0, keepdims=True)
        pick = ei == first
        chosen = chosen | pick
        left = jnp.where(pick, -jnp.inf, left)

    w = jnp.where(chosen, scores, 0.0)
    w = w / jnp.sum(w, axis=0, keepdims=True) * ROUTED_SCALE
    chosen_f = chosen.astype(F32)
    sel_ref[...] = chosen_f
    w_ref[...] = w
    for s in range(tm // TM_MOE):
        part = chosen_f[:, s * TM_MOE:(s + 1) * TM_MOE]
        cnt_ref[s] = jnp.broadcast_to(jnp.sum(part, axis=1, keepdims=True), (e, LANES))


def _route(logits_t, bias_col):
    e, t = logits_t.shape
    tm = TM_ROUTE
    sub = tm // TM_MOE
    blk = pl.BlockSpec((e, tm), lambda i: (0, i))
    return pl.pallas_call(
        _route_kernel,
        grid=(t // tm,),
        in_specs=[blk, _full((e, 1))],
        out_specs=[blk, blk, pl.BlockSpec((sub, e, LANES), lambda i: (i, 0, 0))],
        out_shape=[jax.ShapeDtypeStruct((e, t), F32), jax.ShapeDtypeStruct((e, t), F32),
                   jax.ShapeDtypeStruct((t // TM_MOE, e, LANES), F32)],
        compiler_params=pltpu.CompilerParams(dimension_semantics=("parallel",),
                                             vmem_limit_bytes=VMEM_LIMIT),
        name="route",
    )(logits_t, bias_col)


def _pack_rows(x):
    words = []
    for g in range(x.shape[1] // (2 * LANES)):
        hi = lax.bitcast_convert_type(x[:, 2 * g * LANES:(2 * g + 1) * LANES], U32)
        lo = lax.bitcast_convert_type(x[:, (2 * g + 1) * LANES:(2 * g + 2) * LANES], U32)
        words.append((hi & jnp.uint32(0xFFFF0000)) | lax.shift_right_logical(lo, jnp.uint32(16)))
    return jnp.concatenate(words, axis=1)


def _unpack_rows(u):
    tiles = []
    for g in range(u.shape[1] // LANES):
        w = u[:, g * LANES:(g + 1) * LANES]
        tiles.append(lax.bitcast_convert_type(w & jnp.uint32(0xFFFF0000), F32))
        tiles.append(lax.bitcast_convert_type(lax.shift_left(w, jnp.uint32(16)), F32))
    return jnp.concatenate(tiles, axis=1)


def _row_tables(sel, loc_col):
    e, tm = sel.shape
    tj = lax.broadcasted_iota(I32, (tm, tm), 0)
    tt = lax.broadcasted_iota(I32, (tm, tm), 1)
    before_t = jnp.where(tj < tt, 1.0, 0.0).astype(BF16)
    rank = _dot(sel.astype(BF16), before_t)
    pos = jnp.where(sel > 0.5, loc_col + rank, -1.0)
    hi = jnp.floor(pos * (1.0 / ROW_SPLIT)) * ROW_SPLIT
    return jnp.concatenate([hi, pos - hi], axis=0).astype(BF16)


def _row_owner(c, rchunk, loc_row, cnt_row):
    e = loc_row.shape[1]
    ri = (lax.broadcasted_iota(I32, (rchunk, e), 0) + c * rchunk).astype(F32)
    return jnp.where((ri >= loc_row) & (ri < loc_row + cnt_row), 1.0, 0.0).astype(BF16)


def _for_each_expert(e, fn):
    def trip(i, _):
        for u in range(EXPERT_UNROLL):
            fn(i * EXPERT_UNROLL + u)
        return 0

    lax.fori_loop(0, e // EXPERT_UNROLL, trip, 0)


def _segment_copy(src, src_row, dst, dst_row, rows, sem):
    rows = pl.multiple_of(rows, GRAN)
    return pltpu.make_async_copy(src.at[pl.ds(pl.multiple_of(src_row, GRAN), rows)],
                                 dst.at[pl.ds(pl.multiple_of(dst_row, GRAN), rows)], sem)


def _dispatch_kernel(loc_ref, glob_ref, rows_ref, used_ref, tail_lo_ref, tail_hi_ref,
                     x_ref, sel_ref, locc_ref, locr_ref, cntr_ref, xs_hbm, xs_loc, zeros, sems):
    b = pl.program_id(0)
    nb = pl.num_programs(0)
    e, tm = sel_ref.shape
    r_loc = xs_loc.shape[1]
    slot = b % 2
    buf = xs_loc.at[slot]
    sem = sems.at[slot]

    def wait_block(blk, s):
        @pl.when(used_ref[blk] > 0)
        def _():
            _segment_copy(xs_loc.at[s], 0, xs_hbm, 0, used_ref[blk], sems.at[s]).wait()

    zrows = zeros.shape[0]

    def each_zero_piece(act):
        def per_tail(ex, _):
            lo = tail_lo_ref[ex]
            n = tail_hi_ref[ex] - lo

            def per_piece(j, _):
                rows = jnp.minimum(n - j * zrows, zrows)
                act(_segment_copy(zeros, 0, xs_hbm, lo + j * zrows, rows, sems.at[2]))
                return 0

            lax.fori_loop(0, (n + zrows - 1) // zrows, per_piece, 0)
            return 0

        lax.fori_loop(0, tail_lo_ref.shape[0], per_tail, 0)

    @pl.when(b == 0)
    def _():
        zeros[...] = jnp.zeros_like(zeros)
        each_zero_piece(lambda cp: cp.start(priority=1))

    @pl.when(b >= 2)
    def _():
        wait_block(b - 2, slot)

    tables = _row_tables(sel_ref[...], locc_ref[0])
    x = x_ref[...]
    own = _row_owner(0, r_loc, locr_ref[0], cntr_ref[0])
    prow = _dot(jnp.concatenate([own, own], axis=1), tables)
    ri = lax.broadcasted_iota(I32, (r_loc, tm), 0).astype(F32)
    onehot = jnp.where(prow == ri, 1.0, 0.0).astype(BF16)
    buf[...] = _pack_rows(_dot(onehot, x))

    def per_expert(ex):
        n = rows_ref[b * e + ex]

        @pl.when(n > 0)
        def _():
            _segment_copy(buf, loc_ref[b * e + ex], xs_hbm, glob_ref[b * e + ex], n, sem).start()

    _for_each_expert(e, per_expert)

    @pl.when(b == nb - 1)
    def _():
        @pl.when(b >= 1)
        def _():
            wait_block(b - 1, 1 - slot)

        wait_block(b, slot)
        each_zero_piece(lambda cp: cp.wait())


def _dispatch(x2b, sel_t, lay):
    t, d = x2b.shape
    e = sel_t.shape[0]
    tm = TM_MOE
    nb = t // tm
    r_loc = tm * TOP_K + e * (GRAN - 1)
    col = pl.BlockSpec((1, e, 1), lambda i, *_: (i, 0, 0))
    rowv = pl.BlockSpec((1, 1, e), lambda i, *_: (i, 0, 0))
    grid_spec = pltpu.PrefetchScalarGridSpec(
        num_scalar_prefetch=6,
        grid=(nb,),
        in_specs=[pl.BlockSpec((tm, d), lambda i, *_: (i, 0)),
                  pl.BlockSpec((e, tm), lambda i, *_: (0, i)), col, rowv, rowv],
        out_specs=pl.BlockSpec(memory_space=pl.ANY),
        scratch_shapes=[pltpu.VMEM((2, r_loc, d // 2), U32), pltpu.VMEM((FFN_TILE, d // 2), U32),
                        pltpu.SemaphoreType.DMA((3,))],
    )
    return pl.pallas_call(
        _dispatch_kernel,
        grid_spec=grid_spec,
        out_shape=jax.ShapeDtypeStruct((lay["r_glob"], d // 2), U32),
        compiler_params=pltpu.CompilerParams(dimension_semantics=("arbitrary",),
                                             vmem_limit_bytes=VMEM_LIMIT),
        name="dispatch",
    )(lay["loc_flat"], lay["glob_flat"], lay["rows_flat"], lay["used"], lay["tail_lo"],
      lay["tail_hi"], x2b, sel_t, lay["loc_col"], lay["loc_row"], lay["cnt_row"])


def _experts_kernel(tile_e_ref, n_used_ref, next_e_ref, xs_hbm, wg_hbm, wu_hbm, wd_hbm, ys_hbm,
                    wg_f, wu_f, wd_f, wg_b, wu_b, wd_b, xbuf, ybuf, zeros, sems, in_sems, out_sems,
                    zero_sem):
    i = pl.program_id(0)

    def weight_copies(ex):
        return (pltpu.make_async_copy(wg_hbm.at[ex], wg_f, sems.at[0]),
                pltpu.make_async_copy(wu_hbm.at[ex], wu_f, sems.at[1]),
                pltpu.make_async_copy(wd_hbm.at[ex], wd_f, sems.at[2]))

    def start_weights(ex):
        for cp in weight_copies(ex):
            cp.start(priority=1)

    @pl.when(i == 0)
    def _():
        start_weights(tile_e_ref[0])

    prev = tile_e_ref[jnp.maximum(i - 1, 0)]

    @pl.when((i == 0) | (tile_e_ref[i] != prev))
    def _():
        for cp in weight_copies(tile_e_ref[i]):
            cp.wait()
        wg_b[...] = wg_f[...].astype(BF16)
        wu_b[...] = wu_f[...].astype(BF16)
        wd_b[...] = wd_f[...].astype(BF16)
        nxt = next_e_ref[i]

        @pl.when(nxt >= 0)
        def _():
            start_weights(nxt)

    n_tiles = pl.num_programs(0)
    n_used = n_used_ref[0]
    tile_rows = xbuf.shape[1]

    def tile_in(j):
        return pltpu.make_async_copy(xs_hbm.at[pl.ds(pl.multiple_of(j * tile_rows, tile_rows), tile_rows)],
                                     xbuf.at[j % ROW_SLOTS], in_sems.at[j % ROW_SLOTS])

    def tile_out(j, slot=None):
        slot = j % ROW_SLOTS if slot is None else slot
        return pltpu.make_async_copy(ybuf.at[slot],
                                     ys_hbm.at[pl.ds(pl.multiple_of(j * tile_rows, tile_rows), tile_rows)],
                                     out_sems.at[slot])

    def each_unused_tile(act):
        def body(j, _):
            act(pltpu.make_async_copy(
                zeros, ys_hbm.at[pl.ds(pl.multiple_of(j * tile_rows, tile_rows), tile_rows)], zero_sem))
            return 0

        lax.fori_loop(n_used, n_tiles, body, 0)

    @pl.when(i == 0)
    def _():
        zeros[...] = jnp.zeros_like(zeros)
        each_unused_tile(lambda cp: cp.start(priority=1))
        for j in range(ROW_SLOTS - 1):
            @pl.when(j < n_used)
            def _():
                tile_in(j).start()

    @pl.when(i + ROW_SLOTS - 1 < n_used)
    def _():
        tile_in(i + ROW_SLOTS - 1).start()

    @pl.when((i >= ROW_SLOTS) & (i - ROW_SLOTS < n_used))
    def _():
        tile_out(i - ROW_SLOTS).wait()

    @pl.when(i < n_used)
    def _():
        tile_in(i).wait()
        x = _unpack_rows(xbuf[i % ROW_SLOTS]).astype(BF16)
        hid = jax.nn.silu(_dot(x, wg_b[...])) * _dot(x, wu_b[...])
        y = _dot(hid.astype(BF16), wd_b[...])
        ybuf[i % ROW_SLOTS] = _pack_rows(y.astype(BF16).astype(F32))
        tile_out(i).start()

    @pl.when(i == n_tiles - 1)
    def _():
        for back in reversed(range(ROW_SLOTS)):
            @pl.when((i - back >= 0) & (i - back < n_used))
            def _():
                tile_out(i - back).wait()

        each_unused_tile(lambda cp: cp.wait())


def _experts(xs, tile_e, n_used, next_e, w_gate, w_up, w_down):
    r_glob, dh = xs.shape
    d, hdim = w_gate.shape[1:]
    n_tiles = r_glob // FFN_TILE

    hbm = pl.BlockSpec(memory_space=pl.ANY)
    grid_spec = pltpu.PrefetchScalarGridSpec(
        num_scalar_prefetch=3,
        grid=(n_tiles,),
        in_specs=[hbm, hbm, hbm, hbm],
        out_specs=hbm,
        scratch_shapes=[pltpu.VMEM((d, hdim), F32), pltpu.VMEM((d, hdim), F32),
                        pltpu.VMEM((hdim, d), F32),
                        pltpu.VMEM((d, hdim), BF16), pltpu.VMEM((d, hdim), BF16),
                        pltpu.VMEM((hdim, d), BF16),
                        pltpu.VMEM((ROW_SLOTS, FFN_TILE, dh), U32),
                        pltpu.VMEM((ROW_SLOTS, FFN_TILE, dh), U32),
                        pltpu.VMEM((FFN_TILE, dh), U32),
                        pltpu.SemaphoreType.DMA((3,)), pltpu.SemaphoreType.DMA((ROW_SLOTS,)),
                        pltpu.SemaphoreType.DMA((ROW_SLOTS,)), pltpu.SemaphoreType.DMA],
    )
    return pl.pallas_call(
        _experts_kernel,
        grid_spec=grid_spec,
        out_shape=jax.ShapeDtypeStruct((r_glob, dh), U32),
        compiler_params=pltpu.CompilerParams(dimension_semantics=("arbitrary",),
                                             vmem_limit_bytes=VMEM_LIMIT),
        name="experts",
    )(tile_e, n_used, next_e, xs, w_gate, w_up, w_down)


def _combine_kernel(loc_ref, glob_ref, rows_ref, used_ref,
                    base_ref, sel_ref, w_ref, locc_ref, locr_ref, cntr_ref, g_ref, bta_ref, ys_hbm,
                    o_ref, ys_loc, sems):
    b = pl.program_id(0)
    nb = pl.num_programs(0)
    e, tm = sel_ref.shape
    r_loc = ys_loc.shape[1]
    slot = b % 2
    buf = ys_loc.at[slot]

    def fetch_block(blk, s):
        def per_expert(ex):
            n = rows_ref[blk * e + ex]

            @pl.when(n > 0)
            def _():
                _segment_copy(ys_hbm, glob_ref[blk * e + ex], ys_loc.at[s], loc_ref[blk * e + ex], n,
                              sems.at[s]).start()

        _for_each_expert(e, per_expert)

    @pl.when(b == 0)
    def _():
        fetch_block(b, slot)

    @pl.when(b + 1 < nb)
    def _():
        fetch_block(b + 1, 1 - slot)

    tables = _row_tables(sel_ref[...], locc_ref[0])
    wb = w_ref[...].astype(BF16)
    used = used_ref[b]

    @pl.when(used > 0)
    def _():
        _segment_copy(ys_hbm, 0, buf, 0, used, sems.at[slot]).wait()

    own = _row_owner(0, r_loc, locr_ref[0], cntr_ref[0])
    prow = _dot(jnp.concatenate([own, own], axis=1), tables)
    wrow = _dot(own, wb)
    ri = lax.broadcasted_iota(I32, (r_loc, tm), 0).astype(F32)
    wmat = jnp.where(prow == ri, wrow, 0.0).astype(BF16)
    always = tm * TOP_K
    rvalid = (lax.broadcasted_iota(I32, (r_loc - always, 1), 0) + always) < used
    u = jnp.concatenate([buf[:always, :], jnp.where(rvalid, buf[always:, :], jnp.uint32(0))], axis=0)
    moe = _dot_tn(wmat, _unpack_rows(u).astype(BF16))
    o_ref[...] = _ln(base_ref[...] + moe, g_ref[...], bta_ref[...])


def _combine(base, sel_t, w_t, lay, ys, ln_g, ln_b):
    t, d = base.shape
    e = sel_t.shape[0]
    tm = TM_MOE
    nb = t // tm
    r_loc = tm * TOP_K + e * (GRAN - 1)
    blk = pl.BlockSpec((e, tm), lambda i, *_: (0, i))
    row_spec = pl.BlockSpec((tm, d), lambda i, *_: (i, 0))
    vec = pl.BlockSpec((1, d), lambda i, *_: (0, 0))
    col = pl.BlockSpec((1, e, 1), lambda i, *_: (i, 0, 0))
    rowv = pl.BlockSpec((1, 1, e), lambda i, *_: (i, 0, 0))
    grid_spec = pltpu.PrefetchScalarGridSpec(
        num_scalar_prefetch=4,
        grid=(nb,),
        in_specs=[row_spec, blk, blk, col, rowv, rowv, vec, vec, pl.BlockSpec(memory_space=pl.ANY)],
        out_specs=row_spec,
        scratch_shapes=[pltpu.VMEM((2, r_loc, d // 2), U32), pltpu.SemaphoreType.DMA((2,))],
    )
    return pl.pallas_call(
        _combine_kernel,
        grid_spec=grid_spec,
        out_shape=jax.ShapeDtypeStruct((t, d), F32),
        compiler_params=pltpu.CompilerParams(dimension_semantics=("arbitrary",),
                                             vmem_limit_bytes=VMEM_LIMIT),
        name="combine",
    )(lay["loc_flat"], lay["glob_flat"], lay["rows_flat"], lay["used"],
      base, sel_t, w_t, lay["loc_col"], lay["loc_row"], lay["cnt_row"], ln_g, ln_b, ys)


def _round_up(x, m):
    return (x + m - 1) // m * m


def _moe_layout(cnt, t):
    nb, e = cnt.shape
    cnt_g = _round_up(cnt, GRAN)
    loc_off = jnp.cumsum(cnt_g, axis=1) - cnt_g
    used = jnp.sum(cnt_g, axis=1)
    gcnt = jnp.sum(cnt_g, axis=0)
    gpad = _round_up(gcnt, FFN_TILE)
    gend = jnp.cumsum(gpad)
    gstart = gend - gpad
    glob_off = gstart[None, :] + jnp.cumsum(cnt_g, axis=0) - cnt_g
    r_glob = _round_up(t * TOP_K + nb * e * (GRAN - 1) + e * (FFN_TILE - 1), FFN_TILE)
    n_tiles = r_glob // FFN_TILE
    n_used = (gend[-1] // FFN_TILE).astype(I32)
    tile_start = jnp.minimum(jnp.arange(n_tiles, dtype=I32), n_used - 1) * FFN_TILE
    tile_e = jnp.minimum(jnp.sum(gend[None, :] <= tile_start[:, None], axis=1), e - 1).astype(I32)
    ids = jnp.arange(e, dtype=I32)
    later_owner = jnp.min(jnp.where((ids[None, :] > ids[:, None]) & (gpad[None, :] > 0), ids[None, :], e),
                          axis=1)
    later_owner = jnp.where(later_owner < e, later_owner, -1)
    next_e = jnp.sum(jnp.where(tile_e[:, None] == ids[None, :], later_owner[None, :], 0), axis=1).astype(I32)
    return dict(next_e=next_e, loc_flat=loc_off.astype(I32).reshape(-1), glob_flat=glob_off.astype(I32).reshape(-1),
                rows_flat=cnt_g.astype(I32).reshape(-1), used=used.astype(I32),
                loc_col=loc_off.astype(F32)[:, :, None], loc_row=loc_off.astype(F32)[:, None, :],
                cnt_row=cnt_g.astype(F32)[:, None, :],
                tail_lo=jnp.append(gstart + gcnt, gend[-1]).astype(I32),
                tail_hi=jnp.append(gend, r_glob).astype(I32),
                tile_e=tile_e, n_used=n_used.reshape(1), r_glob=r_glob)


def kernel(x, mem, ln_in_g, ln_in_b, w_in, b_in, ln_v_g, ln_v_b, w_spatial, b_spatial, w_out,
           ln1_g, ln1_b, w_mem_q, w_mem_kv, w_mem_o, ln2_g, ln2_b, w_router, router_bias,
           w_exp_gate, w_exp_up, w_exp_down, w_sh_gate, w_sh_up, w_sh_down, ln3_g, ln3_b):
    bsz, seq, d = x.shape
    t = bsz * seq
    assert w_in.shape[0] == DEPTH
    x2d = x.reshape(t, d)
    row = lambda a: a.reshape(1, -1)

    oa, q, k, v, gb = _in_proj(x2d, row(ln_in_g), row(ln_in_b), w_in[0].astype(BF16), row(b_in[0]),
                               row(ln_v_g[0]), row(ln_v_b[0]), w_spatial[0], b_spatial[0].T)
    yb = _sb_attention(q.reshape(bsz, seq, d), k.reshape(bsz, seq, d), v.reshape(bsz, seq, d))
    k_mem, v_mem = _mem_kv(mem, w_mem_kv[0].astype(BF16))

    wr_t = w_router[0].T
    wr_hi = wr_t.astype(BF16)
    wr_lo = (wr_t - wr_hi.astype(F32)).astype(BF16)
    base, x2b, logits_t = _mid(
        x2d, row(ln_in_g), row(ln_in_b), oa, yb.reshape(t, d), gb, w_out[0].astype(BF16),
        row(ln1_g[0]), row(ln1_b[0]), w_mem_q[0].astype(BF16), k_mem, v_mem,
        w_mem_o[0].astype(BF16), row(ln2_g[0]), row(ln2_b[0]), wr_hi, wr_lo,
        w_sh_gate[0].astype(BF16), w_sh_up[0].astype(BF16), w_sh_down[0].astype(BF16), seq)

    sel_t, w_t, cnt = _route(logits_t, router_bias[0].reshape(-1, 1))
    lay = _moe_layout(cnt[:, :, 0].astype(I32), t)
    xs = _dispatch(x2b, sel_t, lay)
    ys = _experts(xs, lay["tile_e"], lay["n_used"], lay["next_e"], w_exp_gate[0], w_exp_up[0],
                  w_exp_down[0])
    out = _combine(base, sel_t, w_t, lay, ys, row(ln3_g[0]), row(ln3_b[0]))
    return out.reshape(bsz, seq, d)
```

```python
import functools

import jax
import jax.numpy as jnp
from jax import lax
from jax.experimental import pallas as pl
from jax.experimental.pallas import tpu as pltpu

F32 = jnp.float32
BF16 = jnp.bfloat16
I32 = jnp.int32
U32 = jnp.uint32

LANES = 128
SUBLANES = 8
GRAN = SUBLANES

CHUNK = 128
A_GROUPS = 8
SB_HEADS = 8
SB_HEAD_DIM = 128
MEM_HEADS = 4
MEM_HEAD_DIM = 128
N_EXPERTS = 64
TOP_K = 8
N_GROUPS = 8
TOPK_GROUPS = 4
ROUTED_SCALE = 2.5
LN_EPS = 1e-5
DEPTH = 1
ALPHA = (2 * DEPTH) ** 0.25
LOG2E = 1.4426950408889634
SP_CLAMP = 64.0

TM_PROJ = 512
TM_MID = 512
TQ = 1024
SLAB = 256
TM_MOE = 256
TM_ROUTE = 1024
FFN_TILE = 512
EXPERT_UNROLL = 64
ROW_SLOTS = 3
ROW_SPLIT = 256.0
VMEM_LIMIT = 56 * 1024 * 1024


def _ln(x, g, b):
    mu = jnp.mean(x, axis=-1, keepdims=True)
    xc = x - mu
    var = jnp.mean(xc * xc, axis=-1, keepdims=True)
    return xc * lax.rsqrt(var + LN_EPS) * g + b


def _gelu(x):
    return 0.5 * x * (1.0 + lax.erf(x * (2.0 ** -0.5)))


def _dot(a, b):
    return jnp.dot(a, b, preferred_element_type=F32)


def _dot_nt(a, b):
    return lax.dot_general(a, b, (((1,), (1,)), ((), ())), preferred_element_type=F32)


def _dot_tn(a, b):
    return lax.dot_general(a, b, (((0,), (0,)), ((), ())), preferred_element_type=F32)


def _full(shape):
    n = len(shape)
    return pl.BlockSpec(shape, lambda *_: (0,) * n, pipeline_mode=pl.Buffered(1))


def _in_proj_kernel(x_ref, lg_ref, lb_ref, w_ref, b_ref, vg_ref, vb_ref, ws_ref, bst_ref,
                    oa_ref, q_ref, k_ref, v_ref, gb_ref):
    tm, d = x_ref.shape
    xb = _ln(x_ref[...], lg_ref[...], lb_ref[...]).astype(BF16)

    def seg(i):
        return _dot(xb, w_ref[:, i * d:(i + 1) * d]) + b_ref[:, i * d:(i + 1) * d]

    vln = _ln(_gelu(seg(1)), vg_ref[...], vb_ref[...]).astype(BF16)
    gu = jax.nn.sigmoid(seg(5)) * _gelu(seg(0))

    q_ref[...] = (seg(2) * (SB_HEAD_DIM ** -0.5 * LOG2E)).astype(BF16)
    k_ref[...] = seg(3).astype(BF16)
    v_ref[...] = seg(4).astype(BF16)
    gb_ref[...] = jax.nn.sigmoid(seg(6)).astype(BF16)

    gd = d // A_GROUPS
    row = lax.broadcasted_iota(I32, (CHUNK, CHUNK), 0)
    col = lax.broadcasted_iota(I32, (CHUNK, CHUNK), 1)
    for g in range(A_GROUPS):
        w = jnp.where(col <= row, ws_ref[g], 0.0).astype(BF16)
        bias = bst_ref[:, g:g + 1]
        for c in range(tm // CHUNK):
            rs = slice(c * CHUNK, (c + 1) * CHUNK)
            cs = slice(g * gd, (g + 1) * gd)
            mixed = _dot(w, vln[rs, cs]) + bias
            oa_ref[rs, cs] = (gu[rs, cs] * mixed).astype(BF16)


def _in_proj(x2d, ln_g, ln_b, w_in, b_in, vg, vb, w_s, b_st):
    t, d = x2d.shape
    n_in = w_in.shape[1]
    tm = TM_PROJ
    row_spec = pl.BlockSpec((tm, d), lambda i: (i, 0))
    out = jax.ShapeDtypeStruct((t, d), BF16)
    return pl.pallas_call(
        _in_proj_kernel,
        grid=(t // tm,),
        in_specs=[row_spec, _full((1, d)), _full((1, d)), _full((d, n_in)), _full((1, n_in)),
                  _full((1, d)), _full((1, d)), _full(w_s.shape), _full(b_st.shape)],
        out_specs=[row_spec] * 5,
        out_shape=[out] * 5,
        compiler_params=pltpu.CompilerParams(dimension_semantics=("parallel",),
                                             vmem_limit_bytes=VMEM_LIMIT),
        name="in_proj",
    )(x2d, ln_g, ln_b, w_in, b_in, vg, vb, w_s, b_st)


def _sb_kernel(q_ref, k_ref, v_ref, m_ref, o_ref):
    i = pl.program_id(2)
    tq = q_ref.shape[1]
    nslab = tq // SLAB
    q = q_ref[0]
    later = m_ref[...]

    def logits_stage(qs, j, mask):
        off = pl.multiple_of(j * SLAB, SLAB)
        z = _dot_nt(qs, k_ref[0, pl.ds(off, SLAB), :])
        sp = jnp.maximum(jnp.log(1.0 + jnp.exp2(jnp.minimum(z, SP_CLAMP))) * LOG2E, z)
        if mask is not None:
            sp = jnp.where(mask, sp, 0.0)
        return z - sp, sp.astype(BF16), sp[:, 0:1]

    def weights_stage(staged, j, carry, mask):
        log2_beta, sp_b, first = staged
        off = pl.multiple_of(j * SLAB, SLAB)
        cs = _dot(sp_b, later)
        a = jnp.exp2(log2_beta - cs - carry)
        if mask is not None:
            a = jnp.where(mask, a, 0.0)
        total = cs[:, 0:1] + first
        return carry + total, _dot(a.astype(BF16), v_ref[0, pl.ds(off, SLAB), :])

    def slab(qs, j, carry, mask):
        return weights_stage(logits_stage(qs, j, mask), j, carry, mask)

    carry = jnp.zeros((tq, 1), F32)
    acc = jnp.zeros((tq, SB_HEAD_DIM), F32)
    for d in reversed(range(nslab)):
        r0 = d * SLAB
        rr = lax.broadcasted_iota(I32, (tq - r0, SLAB), 0)
        cc = lax.broadcasted_iota(I32, (tq - r0, SLAB), 1)
        c_new, contrib = slab(q[r0:], i * nslab + d, carry[r0:], cc < rr)
        a_new = acc[r0:] + contrib
        carry = jnp.concatenate([carry[:r0], c_new], axis=0) if r0 else c_new
        acc = jnp.concatenate([acc[:r0], a_new], axis=0) if r0 else a_new

    def body(n, ca):
        carry, acc = ca
        js = [(i - n) * nslab - 1 - u for u in range(nslab)]
        staged = logits_stage(q, js[0], None)
        for u in range(nslab):
            ahead = logits_stage(q, js[u + 1], None) if u + 1 < nslab else None
            carry, contrib = weights_stage(staged, js[u], carry, None)
            acc = acc + contrib
            staged = ahead
        return carry, acc

    carry, acc = lax.fori_loop(0, i, body, (carry, acc))
    o_ref[0] = acc.astype(o_ref.dtype)


def _sb_attention(q, k, v):
    b, s, w = q.shape
    h = w // SB_HEAD_DIM
    jj = lax.broadcasted_iota(I32, (SLAB, SLAB), 0)
    ss = lax.broadcasted_iota(I32, (SLAB, SLAB), 1)
    mcat = jnp.where(jj > ss, 1.0, 0.0).astype(BF16)
    q_spec = pl.BlockSpec((1, TQ, SB_HEAD_DIM), lambda bi, hi, i: (bi, i, hi))
    kv_spec = pl.BlockSpec((1, s, SB_HEAD_DIM), lambda bi, hi, i: (bi, 0, hi))
    return pl.pallas_call(
        _sb_kernel,
        grid=(b, h, s // TQ),
        in_specs=[q_spec, kv_spec, kv_spec, _full(mcat.shape)],
        out_specs=q_spec,
        out_shape=jax.ShapeDtypeStruct((b, s, w), BF16),
        compiler_params=pltpu.CompilerParams(
            dimension_semantics=("parallel", "parallel", "arbitrary"),
            vmem_limit_bytes=VMEM_LIMIT),
        name="sb_attn",
    )(q, k, v, mcat)


def _mem_kv_kernel(m_ref, w_ref, k_ref, v_ref):
    kv = _dot(m_ref[0].astype(BF16), w_ref[...])
    half = kv.shape[1] // 2
    k_ref[0] = kv[:, :half].astype(BF16)
    v_ref[0] = kv[:, half:].astype(BF16)


def _mem_kv(mem, w_kv):
    b, m, d = mem.shape
    half = w_kv.shape[1] // 2
    out = jax.ShapeDtypeStruct((b, m, half), BF16)
    o_spec = pl.BlockSpec((1, m, half), lambda i: (i, 0, 0))
    return pl.pallas_call(
        _mem_kv_kernel,
        grid=(b,),
        in_specs=[pl.BlockSpec((1, m, d), lambda i: (i, 0, 0)), _full(w_kv.shape)],
        out_specs=[o_spec, o_spec],
        out_shape=[out, out],
        compiler_params=pltpu.CompilerParams(dimension_semantics=("parallel",),
                                             vmem_limit_bytes=VMEM_LIMIT),
        name="mem_kv",
    )(mem, w_kv)


def _mid_kernel(x_ref, lg_ref, lb_ref, oa_ref, yb_ref, gb_ref, wo_ref, l1g_ref, l1b_ref,
                wq_ref, km_ref, vm_ref, wmo_ref, l2g_ref, l2b_ref, wrh_ref, wrl_ref,
                wsg_ref, wsu_ref, wsd_ref,
                base_ref, x2_ref, lgt_ref):
    merged = oa_ref[...].astype(F32) + gb_ref[...].astype(F32) * yb_ref[...].astype(F32)
    mixed = _dot(merged.astype(BF16), wo_ref[...])
    xln = _ln(x_ref[...], lg_ref[...], lb_ref[...])
    x1 = _ln(ALPHA * xln + mixed, l1g_ref[...], l1b_ref[...])

    q = (_dot(x1.astype(BF16), wq_ref[...]) * (MEM_HEAD_DIM ** -0.5)).astype(BF16)
    head_cols = [slice(h * MEM_HEAD_DIM, (h + 1) * MEM_HEAD_DIM) for h in range(MEM_HEADS)]
    logits = [_dot_nt(q[:, hs], km_ref[0, :, hs]) for hs in head_cols]
    expd = [jnp.exp(lg - jnp.max(lg, axis=-1, keepdims=True)) for lg in logits]
    inv = [1.0 / jnp.sum(p, axis=-1, keepdims=True) for p in expd]
    heads = [_dot(p.astype(BF16), vm_ref[0, :, hs]) * r for p, r, hs in zip(expd, inv, head_cols)]
    o = jnp.concatenate(heads, axis=1).astype(BF16)
    x2 = _ln(ALPHA * x1 + _dot(o, wmo_ref[...]), l2g_ref[...], l2b_ref[...])

    x2h = x2.astype(BF16)
    x2l = (x2 - x2h.astype(F32)).astype(BF16)
    lgt_ref[...] = (_dot_nt(wrh_ref[...], x2h) + _dot_nt(wrh_ref[...], x2l)
                    + _dot_nt(wrl_ref[...], x2h))
    x2_ref[...] = x2h

    hid = jax.nn.silu(_dot(x2h, wsg_ref[...])) * _dot(x2h, wsu_ref[...])
    base_ref[...] = ALPHA * x2 + _dot(hid.astype(BF16), wsd_ref[...])


def _mid(x2d, ln_g, ln_b, oa, yb, gb, w_out, l1g, l1b, w_q, k_mem, v_mem, w_mo, l2g, l2b,
         wr_hi, wr_lo, w_sg, w_su, w_sd, seq):
    t, d = x2d.shape
    tm = TM_MID
    per_batch = seq // tm
    row_spec = pl.BlockSpec((tm, d), lambda i: (i, 0))
    mem_spec = pl.BlockSpec((1,) + k_mem.shape[1:], lambda i: (i // per_batch, 0, 0))
    vec = _full((1, d))
    return pl.pallas_call(
        _mid_kernel,
        grid=(t // tm,),
        in_specs=[row_spec, vec, vec, row_spec, row_spec, row_spec, _full(w_out.shape), vec, vec,
                  _full(w_q.shape), mem_spec, mem_spec, _full(w_mo.shape), vec, vec,
                  _full(wr_hi.shape), _full(wr_lo.shape),
                  _full(w_sg.shape), _full(w_su.shape), _full(w_sd.shape)],
        out_specs=[row_spec, row_spec, pl.BlockSpec((N_EXPERTS, tm), lambda i: (0, i))],
        out_shape=[jax.ShapeDtypeStruct((t, d), F32), jax.ShapeDtypeStruct((t, d), BF16),
                   jax.ShapeDtypeStruct((N_EXPERTS, t), F32)],
        compiler_params=pltpu.CompilerParams(dimension_semantics=("parallel",),
                                             vmem_limit_bytes=VMEM_LIMIT),
        name="mid",
    )(x2d, ln_g, ln_b, oa, yb, gb, w_out, l1g, l1b, w_q, k_mem, v_mem, w_mo, l2g, l2b,
      wr_hi, wr_lo, w_sg, w_su, w_sd)


def _route_kernel(lgt_ref, bias_ref, sel_ref, w_ref, cnt_ref):
    e, tm = lgt_ref.shape
    per_group = e // N_GROUPS
    scores = jax.nn.sigmoid(lgt_ref[...])
    sel = scores + bias_ref[...]

    g3 = sel.reshape(N_GROUPS, per_group, tm)
    j3 = lax.broadcasted_iota(I32, g3.shape, 1)
    m1 = jnp.max(g3, axis=1, keepdims=True)
    first = jnp.min(jnp.where(g3 == m1, j3, per_group), axis=1, keepdims=True)
    m2 = jnp.max(jnp.where(j3 == first, -jnp.inf, g3), axis=1, keepdims=True)
    gs = (m1 + m2).reshape(N_GROUPS, tm)

    gi = lax.broadcasted_iota(I32, (N_GROUPS, tm), 0)
    grank = jnp.zeros((N_GROUPS, tm), I32)
    for o in range(N_GROUPS):
        other = gs[o:o + 1, :]
        grank += ((other > gs) | ((other == gs) & (o < gi))).astype(I32)
    gmask = (grank < TOPK_GROUPS).astype(F32)
    emask = jnp.broadcast_to(gmask.reshape(N_GROUPS, 1, tm), (N_GROUPS, per_group, tm)).reshape(e, tm)
    cand = jnp.where(emask > 0.5, sel, -jnp.inf)

    ei = lax.broadcasted_iota(I32, (e, tm), 0)
    chosen = jnp.zeros((e, tm), jnp.bool_)
    left = cand
    for _ in range(TOP_K):
        best = jnp.max(left, axis=0, keepdims=True)
        first = jnp.min(jnp.where(left == best, ei, e), axis=0, keepdims=True)
        pick = ei == first
        chosen = chosen | pick
        left = jnp.where(pick, -jnp.inf, left)

    w = jnp.where(chosen, scores, 0.0)
    w = w / jnp.sum(w, axis=0, keepdims=True) * ROUTED_SCALE
    chosen_f = chosen.astype(F32)
    sel_ref[...] = chosen_f
    w_ref[...] = w
    for s in range(tm // TM_MOE):
        part = chosen_f[:, s * TM_MOE:(s + 1) * TM_MOE]
        cnt_ref[s] = jnp.broadcast_to(jnp.sum(part, axis=1, keepdims=True), (e, LANES))


def _route(logits_t, bias_col):
    e, t = logits_t.shape
    tm = TM_ROUTE
    sub = tm // TM_MOE
    blk = pl.BlockSpec((e, tm), lambda i: (0, i))
    return pl.pallas_call(
        _route_kernel,
        grid=(t // tm,),
        in_specs=[blk, _full((e, 1))],
        out_specs=[blk, blk, pl.BlockSpec((sub, e, LANES), lambda i: (i, 0, 0))],
        out_shape=[jax.ShapeDtypeStruct((e, t), F32), jax.ShapeDtypeStruct((e, t), F32),
                   jax.ShapeDtypeStruct((t // TM_MOE, e, LANES), F32)],
        compiler_params=pltpu.CompilerParams(dimension_semantics=("parallel",),
                                             vmem_limit_bytes=VMEM_LIMIT),
        name="route",
    )(logits_t, bias_col)


def _pack_rows(x):
    words = []
    for g in range(x.shape[1] // (2 * LANES)):
        hi = lax.bitcast_convert_type(x[:, 2 * g * LANES:(2 * g + 1) * LANES], U32)
        lo = lax.bitcast_convert_type(x[:, (2 * g + 1) * LANES:(2 * g + 2) * LANES], U32)
        words.append((hi & jnp.uint32(0xFFFF0000)) | lax.shift_right_logical(lo, jnp.uint32(16)))
    return jnp.concatenate(words, axis=1)


def _unpack_rows(u):
    tiles = []
    for g in range(u.shape[1] // LANES):
        w = u[:, g * LANES:(g + 1) * LANES]
        tiles.append(lax.bitcast_convert_type(w & jnp.uint32(0xFFFF0000), F32))
        tiles.append(lax.bitcast_convert_type(lax.shift_left(w, jnp.uint32(16)), F32))
    return jnp.concatenate(tiles, axis=1)


def _row_tables(sel, loc_col):
    e, tm = sel.shape
    tj = lax.broadcasted_iota(I32, (tm, tm), 0)
    tt = lax.broadcasted_iota(I32, (tm, tm), 1)
    before_t = jnp.where(tj < tt, 1.0, 0.0).astype(BF16)
    rank = _dot(sel.astype(BF16), before_t)
    pos = jnp.where(sel > 0.5, loc_col + rank, -1.0)
    hi = jnp.floor(pos * (1.0 / ROW_SPLIT)) * ROW_SPLIT
    return jnp.concatenate([hi, pos - hi], axis=0).astype(BF16)


def _row_owner(c, rchunk, loc_row, cnt_row):
    e = loc_row.shape[1]
    ri = (lax.broadcasted_iota(I32, (rchunk, e), 0) + c * rchunk).astype(F32)
    return jnp.where((ri >= loc_row) & (ri < loc_row + cnt_row), 1.0, 0.0).astype(BF16)


def _for_each_expert(e, fn):
    def trip(i, _):
        for u in range(EXPERT_UNROLL):
            fn(i * EXPERT_UNROLL + u)
        return 0

    lax.fori_loop(0, e // EXPERT_UNROLL, trip, 0)


def _segment_copy(src, src_row, dst, dst_row, rows, sem):
    rows = pl.multiple_of(rows, GRAN)
    return pltpu.make_async_copy(src.at[pl.ds(pl.multiple_of(src_row, GRAN), rows)],
                                 dst.at[pl.ds(pl.multiple_of(dst_row, GRAN), rows)], sem)


def _dispatch_kernel(loc_ref, glob_ref, rows_ref, used_ref, tail_lo_ref, tail_hi_ref,
                     x_ref, sel_ref, locc_ref, locr_ref, cntr_ref, xs_hbm, xs_loc, zeros, sems):
    b = pl.program_id(0)
    nb = pl.num_programs(0)
    e, tm = sel_ref.shape
    r_loc = xs_loc.shape[1]
    slot = b % 2
    buf = xs_loc.at[slot]
    sem = sems.at[slot]

    def wait_block(blk, s):
        @pl.when(used_ref[blk] > 0)
        def _():
            _segment_copy(xs_loc.at[s], 0, xs_hbm, 0, used_ref[blk], sems.at[s]).wait()

    zrows = zeros.shape[0]

    def each_zero_piece(act):
        def per_tail(ex, _):
            lo = tail_lo_ref[ex]
            n = tail_hi_ref[ex] - lo

            def per_piece(j, _):
                rows = jnp.minimum(n - j * zrows, zrows)
                act(_segment_copy(zeros, 0, xs_hbm, lo + j * zrows, rows, sems.at[2]))
                return 0

            lax.fori_loop(0, (n + zrows - 1) // zrows, per_piece, 0)
            return 0

        lax.fori_loop(0, tail_lo_ref.shape[0], per_tail, 0)

    @pl.when(b == 0)
    def _():
        zeros[...] = jnp.zeros_like(zeros)
        each_zero_piece(lambda cp: cp.start(priority=1))

    @pl.when(b >= 2)
    def _():
        wait_block(b - 2, slot)

    tables = _row_tables(sel_ref[...], locc_ref[0])
    x = x_ref[...]
    own = _row_owner(0, r_loc, locr_ref[0], cntr_ref[0])
    prow = _dot(jnp.concatenate([own, own], axis=1), tables)
    ri = lax.broadcasted_iota(I32, (r_loc, tm), 0).astype(F32)
    onehot = jnp.where(prow == ri, 1.0, 0.0).astype(BF16)
    buf[...] = _pack_rows(_dot(onehot, x))

    def per_expert(ex):
        n = rows_ref[b * e + ex]

        @pl.when(n > 0)
        def _():
            _segment_copy(buf, loc_ref[b * e + ex], xs_hbm, glob_ref[b * e + ex], n, sem).start()

    _for_each_expert(e, per_expert)

    @pl.when(b == nb - 1)
    def _():
        @pl.when(b >= 1)
        def _():
            wait_block(b - 1, 1 - slot)

        wait_block(b, slot)
        each_zero_piece(lambda cp: cp.wait())


def _dispatch(x2b, sel_t, lay):
    t, d = x2b.shape
    e = sel_t.shape[0]
    tm = TM_MOE
    nb = t // tm
    r_loc = tm * TOP_K + e * (GRAN - 1)
    col = pl.BlockSpec((1, e, 1), lambda i, *_: (i, 0, 0))
    rowv = pl.BlockSpec((1, 1, e), lambda i, *_: (i, 0, 0))
    grid_spec = pltpu.PrefetchScalarGridSpec(
        num_scalar_prefetch=6,
        grid=(nb,),
        in_specs=[pl.BlockSpec((tm, d), lambda i, *_: (i, 0)),
                  pl.BlockSpec((e, tm), lambda i, *_: (0, i)), col, rowv, rowv],
        out_specs=pl.BlockSpec(memory_space=pl.ANY),
        scratch_shapes=[pltpu.VMEM((2, r_loc, d // 2), U32), pltpu.VMEM((FFN_TILE, d // 2), U32),
                        pltpu.SemaphoreType.DMA((3,))],
    )
    return pl.pallas_call(
        _dispatch_kernel,
        grid_spec=grid_spec,
        out_shape=jax.ShapeDtypeStruct((lay["r_glob"], d // 2), U32),
        compiler_params=pltpu.CompilerParams(dimension_semantics=("arbitrary",),
                                             vmem_limit_bytes=VMEM_LIMIT),
        name="dispatch",
    )(lay["loc_flat"], lay["glob_flat"], lay["rows_flat"], lay["used"], lay["tail_lo"],
      lay["tail_hi"], x2b, sel_t, lay["loc_col"], lay["loc_row"], lay["cnt_row"])


def _experts_kernel(tile_e_ref, n_used_ref, next_e_ref, xs_hbm, wg_hbm, wu_hbm, wd_hbm, ys_hbm,
                    wg_f, wu_f, wd_f, wg_b, wu_b, wd_b, xbuf, ybuf, zeros, sems, in_sems, out_sems,
                    zero_sem):
    i = pl.program_id(0)

    def weight_copies(ex):
        return (pltpu.make_async_copy(wg_hbm.at[ex], wg_f, sems.at[0]),
                pltpu.make_async_copy(wu_hbm.at[ex], wu_f, sems.at[1]),
                pltpu.make_async_copy(wd_hbm.at[ex], wd_f, sems.at[2]))

    def start_weights(ex):
        for cp in weight_copies(ex):
            cp.start(priority=1)

    @pl.when(i == 0)
    def _():
        start_weights(tile_e_ref[0])

    prev = tile_e_ref[jnp.maximum(i - 1, 0)]

    @pl.when((i == 0) | (tile_e_ref[i] != prev))
    def _():
        for cp in weight_copies(tile_e_ref[i]):
            cp.wait()
        wg_b[...] = wg_f[...].astype(BF16)
        wu_b[...] = wu_f[...].astype(BF16)
        wd_b[...] = wd_f[...].astype(BF16)
        nxt = next_e_ref[i]

        @pl.when(nxt >= 0)
        def _():
            start_weights(nxt)

    n_tiles = pl.num_programs(0)
    n_used = n_used_ref[0]
    tile_rows = xbuf.shape[1]

    def tile_in(j):
        return pltpu.make_async_copy(xs_hbm.at[pl.ds(pl.multiple_of(j * tile_rows, tile_rows), tile_rows)],
                                     xbuf.at[j % ROW_SLOTS], in_sems.at[j % ROW_SLOTS])

    def tile_out(j):
        return pltpu.make_async_copy(ybuf.at[j % ROW_SLOTS],
                                     ys_hbm.at[pl.ds(pl.multiple_of(j * tile_rows, tile_rows), tile_rows)],
                                     out_sems.at[j % ROW_SLOTS])

    def each_unused_tile(act):
        def body(j, _):
            act(pltpu.make_async_copy(
                zeros, ys_hbm.at[pl.ds(pl.multiple_of(j * tile_rows, tile_rows), tile_rows)], zero_sem))
            return 0

        lax.fori_loop(n_used, n_tiles, body, 0)

    @pl.when(i == 0)
    def _():
        zeros[...] = jnp.zeros_like(zeros)
        each_unused_tile(lambda cp: cp.start(priority=1))
        for j in range(ROW_SLOTS - 1):
            @pl.when(j < n_used)
            def _():
                tile_in(j).start()

    @pl.when(i + ROW_SLOTS - 1 < n_used)
    def _():
        tile_in(i + ROW_SLOTS - 1).start()

    @pl.when((i >= ROW_SLOTS) & (i - ROW_SLOTS < n_used))
    def _():
        tile_out(i - ROW_SLOTS).wait()

    @pl.when(i < n_used)
    def _():
        tile_in(i).wait()
        x = _unpack_rows(xbuf[i % ROW_SLOTS]).astype(BF16)
        hid = jax.nn.silu(_dot(x, wg_b[...])) * _dot(x, wu_b[...])
        y = _dot(hid.astype(BF16), wd_b[...])
        ybuf[i % ROW_SLOTS] = _pack_rows(y.astype(BF16).astype(F32))
        tile_out(i).start()

    @pl.when(i == n_tiles - 1)
    def _():
        for back in reversed(range(ROW_SLOTS)):
            @pl.when((i - back >= 0) & (i - back < n_used))
            def _():
                tile_out(i - back).wait()

        each_unused_tile(lambda cp: cp.wait())


def _experts(xs, tile_e, n_used, next_e, w_gate, w_up, w_down):
    r_glob, dh = xs.shape
    d, hdim = w_gate.shape[1:]
    n_tiles = r_glob // FFN_TILE

    hbm = pl.BlockSpec(memory_space=pl.ANY)
    grid_spec = pltpu.PrefetchScalarGridSpec(
        num_scalar_prefetch=3,
        grid=(n_tiles,),
        in_specs=[hbm, hbm, hbm, hbm],
        out_specs=hbm,
        scratch_shapes=[pltpu.VMEM((d, hdim), F32), pltpu.VMEM((d, hdim), F32),
                        pltpu.VMEM((hdim, d), F32),
                        pltpu.VMEM((d, hdim), BF16), pltpu.VMEM((d, hdim), BF16),
                        pltpu.VMEM((hdim, d), BF16),
                        pltpu.VMEM((ROW_SLOTS, FFN_TILE, dh), U32),
                        pltpu.VMEM((ROW_SLOTS, FFN_TILE, dh), U32),
                        pltpu.VMEM((FFN_TILE, dh), U32),
                        pltpu.SemaphoreType.DMA((3,)), pltpu.SemaphoreType.DMA((ROW_SLOTS,)),
                        pltpu.SemaphoreType.DMA((ROW_SLOTS,)), pltpu.SemaphoreType.DMA],
    )
    return pl.pallas_call(
        _experts_kernel,
        grid_spec=grid_spec,
        out_shape=jax.ShapeDtypeStruct((r_glob, dh), U32),
        compiler_params=pltpu.CompilerParams(dimension_semantics=("arbitrary",),
                                             vmem_limit_bytes=VMEM_LIMIT),
        name="experts",
    )(tile_e, n_used, next_e, xs, w_gate, w_up, w_down)


def _combine_kernel(loc_ref, glob_ref, rows_ref, used_ref,
                    base_ref, sel_ref, w_ref, locc_ref, locr_ref, cntr_ref, g_ref, bta_ref, ys_hbm,
                    o_ref, ys_loc, sems):
    b = pl.program_id(0)
    nb = pl.num_programs(0)
    e, tm = sel_ref.shape
    r_loc = ys_loc.shape[1]
    slot = b % 2
    buf = ys_loc.at[slot]

    always = tm * TOP_K

    def fetch_block(blk, s):
        ys_loc[s, always:, :] = jnp.zeros((r_loc - always, ys_loc.shape[2]), ys_loc.dtype)

        def per_expert(ex):
            n = rows_ref[blk * e + ex]

            @pl.when(n > 0)
            def _():
                _segment_copy(ys_hbm, glob_ref[blk * e + ex], ys_loc.at[s], loc_ref[blk * e + ex], n,
                              sems.at[s]).start()

        _for_each_expert(e, per_expert)

    @pl.when(b == 0)
    def _():
        fetch_block(b, slot)

    @pl.when(b + 1 < nb)
    def _():
        fetch_block(b + 1, 1 - slot)

    tables = _row_tables(sel_ref[...], locc_ref[0])
    wb = w_ref[...].astype(BF16)
    used = used_ref[b]

    @pl.when(used > 0)
    def _():
        _segment_copy(ys_hbm, 0, buf, 0, used, sems.at[slot]).wait()

    own = _row_owner(0, r_loc, locr_ref[0], cntr_ref[0])
    prow = _dot(jnp.concatenate([own, own], axis=1), tables)
    wrow = _dot(own, wb)
    ri = lax.broadcasted_iota(I32, (r_loc, tm), 0).astype(F32)
    wmat = jnp.where(prow == ri, wrow, 0.0).astype(BF16)
    moe = _dot_tn(wmat, _unpack_rows(buf[...]).astype(BF16))
    o_ref[...] = _ln(base_ref[...] + moe, g_ref[...], bta_ref[...])


def _combine(base, sel_t, w_t, lay, ys, ln_g, ln_b):
    t, d = base.shape
    e = sel_t.shape[0]
    tm = TM_MOE
    nb = t // tm
    r_loc = tm * TOP_K + e * (GRAN - 1)
    blk = pl.BlockSpec((e, tm), lambda i, *_: (0, i))
    row_spec = pl.BlockSpec((tm, d), lambda i, *_: (i, 0))
    vec = pl.BlockSpec((1, d), lambda i, *_: (0, 0))
    col = pl.BlockSpec((1, e, 1), lambda i, *_: (i, 0, 0))
    rowv = pl.BlockSpec((1, 1, e), lambda i, *_: (i, 0, 0))
    grid_spec = pltpu.PrefetchScalarGridSpec(
        num_scalar_prefetch=4,
        grid=(nb,),
        in_specs=[row_spec, blk, blk, col, rowv, rowv, vec, vec, pl.BlockSpec(memory_space=pl.ANY)],
        out_specs=row_spec,
        scratch_shapes=[pltpu.VMEM((2, r_loc, d // 2), U32), pltpu.SemaphoreType.DMA((2,))],
    )
    return pl.pallas_call(
        _combine_kernel,
        grid_spec=grid_spec,
        out_shape=jax.ShapeDtypeStruct((t, d), F32),
        compiler_params=pltpu.CompilerParams(dimension_semantics=("arbitrary",),
                                             vmem_limit_bytes=VMEM_LIMIT),
        name="combine",
    )(lay["loc_flat"], lay["glob_flat"], lay["rows_flat"], lay["used"],
      base, sel_t, w_t, lay["loc_col"], lay["loc_row"], lay["cnt_row"], ln_g, ln_b, ys)


def _round_up(x, m):
    return (x + m - 1) // m * m


def _moe_layout(cnt, t):
    nb, e = cnt.shape
    cnt_g = _round_up(cnt, GRAN)
    loc_off = jnp.cumsum(cnt_g, axis=1) - cnt_g
    used = jnp.sum(cnt_g, axis=1)
    gcnt = jnp.sum(cnt_g, axis=0)
    gpad = _round_up(gcnt, FFN_TILE)
    gend = jnp.cumsum(gpad)
    gstart = gend - gpad
    glob_off = gstart[None, :] + jnp.cumsum(cnt_g, axis=0) - cnt_g
    r_glob = _round_up(t * TOP_K + nb * e * (GRAN - 1) + e * (FFN_TILE - 1), FFN_TILE)
    n_tiles = r_glob // FFN_TILE
    n_used = (gend[-1] // FFN_TILE).astype(I32)
    tile_start = jnp.minimum(jnp.arange(n_tiles, dtype=I32), n_used - 1) * FFN_TILE
    tile_e = jnp.minimum(jnp.sum(gend[None, :] <= tile_start[:, None], axis=1), e - 1).astype(I32)
    ids = jnp.arange(e, dtype=I32)
    later_owner = jnp.min(jnp.where((ids[None, :] > ids[:, None]) & (gpad[None, :] > 0), ids[None, :], e),
                          axis=1)
    later_owner = jnp.where(later_owner < e, later_owner, -1)
    next_e = jnp.sum(jnp.where(tile_e[:, None] == ids[None, :], later_owner[None, :], 0), axis=1).astype(I32)
    return dict(next_e=next_e, loc_flat=loc_off.astype(I32).reshape(-1), glob_flat=glob_off.astype(I32).reshape(-1),
                rows_flat=cnt_g.astype(I32).reshape(-1), used=used.astype(I32),
                loc_col=loc_off.astype(F32)[:, :, None], loc_row=loc_off.astype(F32)[:, None, :],
                cnt_row=cnt_g.astype(F32)[:, None, :],
                tail_lo=jnp.append(gstart + gcnt, gend[-1]).astype(I32),
                tail_hi=jnp.append(gend, r_glob).astype(I32),
                tile_e=tile_e, n_used=n_used.reshape(1), r_glob=r_glob)


def kernel(x, mem, ln_in_g, ln_in_b, w_in, b_in, ln_v_g, ln_v_b, w_spatial, b_spatial, w_out,
           ln1_g, ln1_b, w_mem_q, w_mem_kv, w_mem_o, ln2_g, ln2_b, w_router, router_bias,
           w_exp_gate, w_exp_up, w_exp_down, w_sh_gate, w_sh_up, w_sh_down, ln3_g, ln3_b):
    bsz, seq, d = x.shape
    t = bsz * seq
    assert w_in.shape[0] == DEPTH
    x2d = x.reshape(t, d)
    row = lambda a: a.reshape(1, -1)

    oa, q, k, v, gb = _in_proj(x2d, row(ln_in_g), row(ln_in_b), w_in[0].astype(BF16), row(b_in[0]),
                               row(ln_v_g[0]), row(ln_v_b[0]), w_spatial[0], b_spatial[0].T)
    yb = _sb_attention(q.reshape(bsz, seq, d), k.reshape(bsz, seq, d), v.reshape(bsz, seq, d))
    k_mem, v_mem = _mem_kv(mem, w_mem_kv[0].astype(BF16))

    wr_t = w_router[0].T
    wr_hi = wr_t.astype(BF16)
    wr_lo = (wr_t - wr_hi.astype(F32)).astype(BF16)
    base, x2b, logits_t = _mid(
        x2d, row(ln_in_g), row(ln_in_b), oa, yb.reshape(t, d), gb, w_out[0].astype(BF16),
        row(ln1_g[0]), row(ln1_b[0]), w_mem_q[0].astype(BF16), k_mem, v_mem,
        w_mem_o[0].astype(BF16), row(ln2_g[0]), row(ln2_b[0]), wr_hi, wr_lo,
        w_sh_gate[0].astype(BF16), w_sh_up[0].astype(BF16), w_sh_down[0].astype(BF16), seq)

    sel_t, w_t, cnt = _route(logits_t, router_bias[0].reshape(-1, 1))
    lay = _moe_layout(cnt[:, :, 0].astype(I32), t)
    xs = _dispatch(x2b, sel_t, lay)
    ys = _experts(xs, lay["tile_e"], lay["n_used"], lay["next_e"], w_exp_gate[0], w_exp_up[0],
                  w_exp_down[0])
    out = _combine(base, sel_t, w_t, lay, ys, row(ln3_g[0]), row(ln3_b[0]))
    return out.reshape(bsz, seq, d)
```

```python
import functools

import jax
import jax.numpy as jnp
from jax import lax
from jax.experimental import pallas as pl
from jax.experimental.pallas import tpu as pltpu

F32 = jnp.float32
BF16 = jnp.bfloat16
I32 = jnp.int32
U32 = jnp.uint32

LANES = 128
SUBLANES = 8
GRAN = SUBLANES

CHUNK = 128
A_GROUPS = 8
SB_HEADS = 8
SB_HEAD_DIM = 128
MEM_HEADS = 4
MEM_HEAD_DIM = 128
N_EXPERTS = 64
TOP_K = 8
N_GROUPS = 8
TOPK_GROUPS = 4
ROUTED_SCALE = 2.5
LN_EPS = 1e-5
DEPTH = 1
ALPHA = (2 * DEPTH) ** 0.25
LOG2E = 1.4426950408889634
SP_CLAMP = 64.0

TM_PROJ = 512
TM_MID = 512
TQ = 1024
SLAB = 256
TM_MOE = 256
TM_ROUTE = 1024
FFN_TILE = 512
EXPERT_UNROLL = 64
ROW_SLOTS = 3
ROW_SPLIT = 256.0
VMEM_LIMIT = 56 * 1024 * 1024


def _ln(x, g, b):
    mu = jnp.mean(x, axis=-1, keepdims=True)
    xc = x - mu
    var = jnp.mean(xc * xc, axis=-1, keepdims=True)
    return xc * lax.rsqrt(var + LN_EPS) * g + b


def _gelu(x):
    return 0.5 * x * (1.0 + lax.erf(x * (2.0 ** -0.5)))


def _dot(a, b):
    return jnp.dot(a, b, preferred_element_type=F32)


def _dot_nt(a, b):
    return lax.dot_general(a, b, (((1,), (1,)), ((), ())), preferred_element_type=F32)


def _dot_tn(a, b):
    return lax.dot_general(a, b, (((0,), (0,)), ((), ())), preferred_element_type=F32)


def _full(shape):
    n = len(shape)
    return pl.BlockSpec(shape, lambda *_: (0,) * n, pipeline_mode=pl.Buffered(1))


def _in_proj_kernel(x_ref, lg_ref, lb_ref, w_ref, b_ref, vg_ref, vb_ref, ws_ref, bst_ref,
                    oa_ref, q_ref, k_ref, v_ref, gb_ref):
    tm, d = x_ref.shape
    xb = _ln(x_ref[...], lg_ref[...], lb_ref[...]).astype(BF16)

    def seg(i):
        return _dot(xb, w_ref[:, i * d:(i + 1) * d]) + b_ref[:, i * d:(i + 1) * d]

    pre_v, pre_gate, pre_u = seg(1), seg(5), seg(0)
    q_ref[...] = (seg(2) * (SB_HEAD_DIM ** -0.5 * LOG2E)).astype(BF16)
    vln = _ln(_gelu(pre_v), vg_ref[...], vb_ref[...]).astype(BF16)
    k_ref[...] = seg(3).astype(BF16)
    gu = jax.nn.sigmoid(pre_gate) * _gelu(pre_u)
    v_ref[...] = seg(4).astype(BF16)
    gb_ref[...] = jax.nn.sigmoid(seg(6)).astype(BF16)

    gd = d // A_GROUPS
    row = lax.broadcasted_iota(I32, (CHUNK, CHUNK), 0)
    col = lax.broadcasted_iota(I32, (CHUNK, CHUNK), 1)
    for g in range(A_GROUPS):
        w = jnp.where(col <= row, ws_ref[g], 0.0).astype(BF16)
        bias = bst_ref[:, g:g + 1]
        for c in range(tm // CHUNK):
            rs = slice(c * CHUNK, (c + 1) * CHUNK)
            cs = slice(g * gd, (g + 1) * gd)
            mixed = _dot(w, vln[rs, cs]) + bias
            oa_ref[rs, cs] = (gu[rs, cs] * mixed).astype(BF16)


def _in_proj(x2d, ln_g, ln_b, w_in, b_in, vg, vb, w_s, b_st):
    t, d = x2d.shape
    n_in = w_in.shape[1]
    tm = TM_PROJ
    row_spec = pl.BlockSpec((tm, d), lambda i: (i, 0))
    out = jax.ShapeDtypeStruct((t, d), BF16)
    return pl.pallas_call(
        _in_proj_kernel,
        grid=(t // tm,),
        in_specs=[row_spec, _full((1, d)), _full((1, d)), _full((d, n_in)), _full((1, n_in)),
                  _full((1, d)), _full((1, d)), _full(w_s.shape), _full(b_st.shape)],
        out_specs=[row_spec] * 5,
        out_shape=[out] * 5,
        compiler_params=pltpu.CompilerParams(dimension_semantics=("parallel",),
                                             vmem_limit_bytes=VMEM_LIMIT),
        name="in_proj",
    )(x2d, ln_g, ln_b, w_in, b_in, vg, vb, w_s, b_st)


def _sb_kernel(q_ref, k_ref, v_ref, m_ref, o_ref):
    i = pl.program_id(2)
    tq = q_ref.shape[1]
    nslab = tq // SLAB
    q = q_ref[0]
    later = m_ref[...]

    tri_r = lax.broadcasted_iota(I32, (SLAB, SLAB), 0)
    tri_c = lax.broadcasted_iota(I32, (SLAB, SLAB), 1)

    def causal(x, diagonal):
        if not diagonal:
            return x
        top = jnp.where(tri_c < tri_r, x[:SLAB], 0.0)
        return top if x.shape[0] == SLAB else jnp.concatenate([top, x[SLAB:]], axis=0)

    def logits_stage(qs, j, diagonal):
        off = pl.multiple_of(j * SLAB, SLAB)
        z = _dot_nt(qs, k_ref[0, pl.ds(off, SLAB), :])
        sp = causal(jnp.maximum(jnp.log(1.0 + jnp.exp2(jnp.minimum(z, SP_CLAMP))) * LOG2E, z), diagonal)
        return z - sp, sp.astype(BF16), sp[:, 0:1]

    def weights_stage(staged, j, carry, diagonal):
        log2_beta, sp_b, first = staged
        off = pl.multiple_of(j * SLAB, SLAB)
        cs = _dot(sp_b, later)
        a = causal(jnp.exp2(log2_beta - cs - carry), diagonal)
        total = cs[:, 0:1] + first
        return carry + total, _dot(a.astype(BF16), v_ref[0, pl.ds(off, SLAB), :])

    def slab(qs, j, carry, diagonal):
        return weights_stage(logits_stage(qs, j, diagonal), j, carry, diagonal)

    carry = jnp.zeros((tq, 1), F32)
    acc = jnp.zeros((tq, SB_HEAD_DIM), F32)
    for d in reversed(range(nslab)):
        r0 = d * SLAB
        c_new, contrib = slab(q[r0:], i * nslab + d, carry[r0:], True)
        a_new = acc[r0:] + contrib
        carry = jnp.concatenate([carry[:r0], c_new], axis=0) if r0 else c_new
        acc = jnp.concatenate([acc[:r0], a_new], axis=0) if r0 else a_new

    def body(n, ca):
        carry, acc = ca
        js = [(i - n) * nslab - 1 - u for u in range(nslab)]
        staged = logits_stage(q, js[0], False)
        for u in range(nslab):
            ahead = logits_stage(q, js[u + 1], False) if u + 1 < nslab else None
            carry, contrib = weights_stage(staged, js[u], carry, False)
            acc = acc + contrib
            staged = ahead
        return carry, acc

    carry, acc = lax.fori_loop(0, i, body, (carry, acc))
    o_ref[0] = acc.astype(o_ref.dtype)


def _sb_attention(q, k, v):
    b, s, w = q.shape
    h = w // SB_HEAD_DIM
    jj = lax.broadcasted_iota(I32, (SLAB, SLAB), 0)
    ss = lax.broadcasted_iota(I32, (SLAB, SLAB), 1)
    mcat = jnp.where(jj > ss, 1.0, 0.0).astype(BF16)
    q_spec = pl.BlockSpec((1, TQ, SB_HEAD_DIM), lambda bi, hi, i: (bi, i, hi))
    kv_spec = pl.BlockSpec((1, s, SB_HEAD_DIM), lambda bi, hi, i: (bi, 0, hi))
    return pl.pallas_call(
        _sb_kernel,
        grid=(b, h, s // TQ),
        in_specs=[q_spec, kv_spec, kv_spec, _full(mcat.shape)],
        out_specs=q_spec,
        out_shape=jax.ShapeDtypeStruct((b, s, w), BF16),
        compiler_params=pltpu.CompilerParams(
            dimension_semantics=("parallel", "parallel", "arbitrary"),
            vmem_limit_bytes=VMEM_LIMIT),
        name="sb_attn",
    )(q, k, v, mcat)


def _mem_kv_kernel(m_ref, w_ref, k_ref, v_ref):
    kv = _dot(m_ref[0].astype(BF16), w_ref[...])
    half = kv.shape[1] // 2
    k_ref[0] = kv[:, :half].astype(BF16)
    v_ref[0] = kv[:, half:].astype(BF16)


def _mem_kv(mem, w_kv):
    b, m, d = mem.shape
    half = w_kv.shape[1] // 2
    out = jax.ShapeDtypeStruct((b, m, half), BF16)
    o_spec = pl.BlockSpec((1, m, half), lambda i: (i, 0, 0))
    return pl.pallas_call(
        _mem_kv_kernel,
        grid=(b,),
        in_specs=[pl.BlockSpec((1, m, d), lambda i: (i, 0, 0)), _full(w_kv.shape)],
        out_specs=[o_spec, o_spec],
        out_shape=[out, out],
        compiler_params=pltpu.CompilerParams(dimension_semantics=("parallel",),
                                             vmem_limit_bytes=VMEM_LIMIT),
        name="mem_kv",
    )(mem, w_kv)


def _mid_kernel(x_ref, lg_ref, lb_ref, oa_ref, yb_ref, gb_ref, wo_ref, l1g_ref, l1b_ref,
                wq_ref, km_ref, vm_ref, wmo_ref, l2g_ref, l2b_ref, wrh_ref, wrl_ref,
                wsg_ref, wsu_ref, wsd_ref,
                base_ref, x2_ref, lgt_ref):
    merged = oa_ref[...].astype(F32) + gb_ref[...].astype(F32) * yb_ref[...].astype(F32)
    mixed = _dot(merged.astype(BF16), wo_ref[...])
    xln = _ln(x_ref[...], lg_ref[...], lb_ref[...])
    x1 = _ln(ALPHA * xln + mixed, l1g_ref[...], l1b_ref[...])

    q = (_dot(x1.astype(BF16), wq_ref[...]) * (MEM_HEAD_DIM ** -0.5)).astype(BF16)
    head_cols = [slice(h * MEM_HEAD_DIM, (h + 1) * MEM_HEAD_DIM) for h in range(MEM_HEADS)]
    logits = [_dot_nt(q[:, hs], km_ref[0, :, hs]) for hs in head_cols]
    expd = [jnp.exp(lg - jnp.max(lg, axis=-1, keepdims=True)) for lg in logits]
    inv = [1.0 / jnp.sum(p, axis=-1, keepdims=True) for p in expd]
    heads = [_dot(p.astype(BF16), vm_ref[0, :, hs]) * r for p, r, hs in zip(expd, inv, head_cols)]
    o = jnp.concatenate(heads, axis=1).astype(BF16)
    x2 = _ln(ALPHA * x1 + _dot(o, wmo_ref[...]), l2g_ref[...], l2b_ref[...])

    x2h = x2.astype(BF16)
    x2_ref[...] = x2h
    gate = _dot(x2h, wsg_ref[...])
    up = _dot(x2h, wsu_ref[...])
    x2l = (x2 - x2h.astype(F32)).astype(BF16)
    lgt_ref[...] = (_dot_nt(wrh_ref[...], x2h) + _dot_nt(wrh_ref[...], x2l)
                    + _dot_nt(wrl_ref[...], x2h))
    hid = jax.nn.silu(gate) * up
    base_ref[...] = ALPHA * x2 + _dot(hid.astype(BF16), wsd_ref[...])


def _mid(x2d, ln_g, ln_b, oa, yb, gb, w_out, l1g, l1b, w_q, k_mem, v_mem, w_mo, l2g, l2b,
         wr_hi, wr_lo, w_sg, w_su, w_sd, seq):
    t, d = x2d.shape
    tm = TM_MID
    per_batch = seq // tm
    row_spec = pl.BlockSpec((tm, d), lambda i: (i, 0))
    mem_spec = pl.BlockSpec((1,) + k_mem.shape[1:], lambda i: (i // per_batch, 0, 0))
    vec = _full((1, d))
    return pl.pallas_call(
        _mid_kernel,
        grid=(t // tm,),
        in_specs=[row_spec, vec, vec, row_spec, row_spec, row_spec, _full(w_out.shape), vec, vec,
                  _full(w_q.shape), mem_spec, mem_spec, _full(w_mo.shape), vec, vec,
                  _full(wr_hi.shape), _full(wr_lo.shape),
                  _full(w_sg.shape), _full(w_su.shape), _full(w_sd.shape)],
        out_specs=[row_spec, row_spec, pl.BlockSpec((N_EXPERTS, tm), lambda i: (0, i))],
        out_shape=[jax.ShapeDtypeStruct((t, d), F32), jax.ShapeDtypeStruct((t, d), BF16),
                   jax.ShapeDtypeStruct((N_EXPERTS, t), F32)],
        compiler_params=pltpu.CompilerParams(dimension_semantics=("parallel",),
                                             vmem_limit_bytes=VMEM_LIMIT),
        name="mid",
    )(x2d, ln_g, ln_b, oa, yb, gb, w_out, l1g, l1b, w_q, k_mem, v_mem, w_mo, l2g, l2b,
      wr_hi, wr_lo, w_sg, w_su, w_sd)


def _route_kernel(lgt_ref, bias_ref, sel_ref, w_ref, cnt_ref):
    e, tm = lgt_ref.shape
    per_group = e // N_GROUPS
    scores = jax.nn.sigmoid(lgt_ref[...])
    sel = scores + bias_ref[...]

    g3 = sel.reshape(N_GROUPS, per_group, tm)
    j3 = lax.broadcasted_iota(I32, g3.shape, 1)
    m1 = jnp.max(g3, axis=1, keepdims=True)
    first = jnp.min(jnp.where(g3 == m1, j3, per_group), axis=1, keepdims=True)
    m2 = jnp.max(jnp.where(j3 == first, -jnp.inf, g3), axis=1, keepdims=True)
    gs = (m1 + m2).reshape(N_GROUPS, tm)

    gi = lax.broadcasted_iota(I32, (N_GROUPS, tm), 0)
    grank = jnp.zeros((N_GROUPS, tm), I32)
    for o in range(N_GROUPS):
        other = gs[o:o + 1, :]
        grank += ((other > gs) | ((other == gs) & (o < gi))).astype(I32)
    gmask = (grank < TOPK_GROUPS).astype(F32)
    emask = jnp.broadcast_to(gmask.reshape(N_GROUPS, 1, tm), (N_GROUPS, per_group, tm)).reshape(e, tm)
    cand = jnp.where(emask > 0.5, sel, -jnp.inf)

    ei = lax.broadcasted_iota(I32, (e, tm), 0)
    chosen = jnp.zeros((e, tm), jnp.bool_)
    left = cand
    for _ in range(TOP_K):
        best = jnp.max(left, axis=0, keepdims=True)
        first = jnp.min(jnp.where(left == best, ei, e), axis=0, keepdims=True)
        pick = ei == first
        chosen = chosen | pick
        left = jnp.where(pick, -jnp.inf, left)

    w = jnp.where(chosen, scores, 0.0)
    w = w / jnp.sum(w, axis=0, keepdims=True) * ROUTED_SCALE
    chosen_f = chosen.astype(F32)
    sel_ref[...] = chosen_f
    w_ref[...] = w
    for s in range(tm // TM_MOE):
        part = chosen_f[:, s * TM_MOE:(s + 1) * TM_MOE]
        cnt_ref[s] = jnp.broadcast_to(jnp.sum(part, axis=1, keepdims=True), (e, LANES))


def _route(logits_t, bias_col):
    e, t = logits_t.shape
    tm = TM_ROUTE
    sub = tm // TM_MOE
    blk = pl.BlockSpec((e, tm), lambda i: (0, i))
    return pl.pallas_call(
        _route_kernel,
        grid=(t // tm,),
        in_specs=[blk, _full((e, 1))],
        out_specs=[blk, blk, pl.BlockSpec((sub, e, LANES), lambda i: (i, 0, 0))],
        out_shape=[jax.ShapeDtypeStruct((e, t), F32), jax.ShapeDtypeStruct((e, t), F32),
                   jax.ShapeDtypeStruct((t // TM_MOE, e, LANES), F32)],
        compiler_params=pltpu.CompilerParams(dimension_semantics=("parallel",),
                                             vmem_limit_bytes=VMEM_LIMIT),
        name="route",
    )(logits_t, bias_col)


def _pack_rows(x):
    words = []
    for g in range(x.shape[1] // (2 * LANES)):
        hi = lax.bitcast_convert_type(x[:, 2 * g * LANES:(2 * g + 1) * LANES], U32)
        lo = lax.bitcast_convert_type(x[:, (2 * g + 1) * LANES:(2 * g + 2) * LANES], U32)
        words.append((hi & jnp.uint32(0xFFFF0000)) | lax.shift_right_logical(lo, jnp.uint32(16)))
    return jnp.concatenate(words, axis=1)


def _unpack_rows(u):
    tiles = []
    for g in range(u.shape[1] // LANES):
        w = u[:, g * LANES:(g + 1) * LANES]
        tiles.append(lax.bitcast_convert_type(w & jnp.uint32(0xFFFF0000), F32))
        tiles.append(lax.bitcast_convert_type(lax.shift_left(w, jnp.uint32(16)), F32))
    return jnp.concatenate(tiles, axis=1)


def _row_tables(sel, loc_col):
    e, tm = sel.shape
    tj = lax.broadcasted_iota(I32, (tm, tm), 0)
    tt = lax.broadcasted_iota(I32, (tm, tm), 1)
    before_t = jnp.where(tj < tt, 1.0, 0.0).astype(BF16)
    rank = _dot(sel.astype(BF16), before_t)
    pos = jnp.where(sel > 0.5, loc_col + rank, -1.0)
    hi = jnp.floor(pos * (1.0 / ROW_SPLIT)) * ROW_SPLIT
    return jnp.concatenate([hi, pos - hi], axis=0).astype(BF16)


def _row_owner(c, rchunk, loc_row, cnt_row):
    e = loc_row.shape[1]
    ri = (lax.broadcasted_iota(I32, (rchunk, e), 0) + c * rchunk).astype(F32)
    return jnp.where((ri >= loc_row) & (ri < loc_row + cnt_row), 1.0, 0.0).astype(BF16)


def _for_each_expert(e, fn):
    def trip(i, _):
        for u in range(EXPERT_UNROLL):
            fn(i * EXPERT_UNROLL + u)
        return 0

    lax.fori_loop(0, e // EXPERT_UNROLL, trip, 0)


def _segment_copy(src, src_row, dst, dst_row, rows, sem):
    rows = pl.multiple_of(rows, GRAN)
    return pltpu.make_async_copy(src.at[pl.ds(pl.multiple_of(src_row, GRAN), rows)],
                                 dst.at[pl.ds(pl.multiple_of(dst_row, GRAN), rows)], sem)


def _dispatch_kernel(loc_ref, glob_ref, rows_ref, used_ref, tail_lo_ref, tail_hi_ref,
                     x_ref, sel_ref, locc_ref, locr_ref, cntr_ref, xs_hbm, xs_loc, zeros, sems):
    b = pl.program_id(0)
    nb = pl.num_programs(0)
    e, tm = sel_ref.shape
    r_loc = xs_loc.shape[1]
    slot = b % 2
    buf = xs_loc.at[slot]
    sem = sems.at[slot]

    def wait_block(blk, s):
        @pl.when(used_ref[blk] > 0)
        def _():
            _segment_copy(xs_loc.at[s], 0, xs_hbm, 0, used_ref[blk], sems.at[s]).wait()

    zrows = zeros.shape[0]

    def each_zero_piece(act):
        def per_tail(ex, _):
            lo = tail_lo_ref[ex]
            n = tail_hi_ref[ex] - lo

            def per_piece(j, _):
                rows = jnp.minimum(n - j * zrows, zrows)
                act(_segment_copy(zeros, 0, xs_hbm, lo + j * zrows, rows, sems.at[2]))
                return 0

            lax.fori_loop(0, (n + zrows - 1) // zrows, per_piece, 0)
            return 0

        lax.fori_loop(0, tail_lo_ref.shape[0], per_tail, 0)

    @pl.when(b == 0)
    def _():
        zeros[...] = jnp.zeros_like(zeros)
        each_zero_piece(lambda cp: cp.start(priority=1))

    @pl.when(b >= 2)
    def _():
        wait_block(b - 2, slot)

    tables = _row_tables(sel_ref[...], locc_ref[0])
    x = x_ref[...]
    own = _row_owner(0, r_loc, locr_ref[0], cntr_ref[0])
    prow = _dot(jnp.concatenate([own, own], axis=1), tables)
    ri = lax.broadcasted_iota(I32, (r_loc, tm), 0).astype(F32)
    onehot = jnp.where(prow == ri, 1.0, 0.0).astype(BF16)
    buf[...] = _pack_rows(_dot(onehot, x))

    def per_expert(ex):
        n = rows_ref[b * e + ex]

        @pl.when(n > 0)
        def _():
            _segment_copy(buf, loc_ref[b * e + ex], xs_hbm, glob_ref[b * e + ex], n, sem).start()

    _for_each_expert(e, per_expert)

    @pl.when(b == nb - 1)
    def _():
        @pl.when(b >= 1)
        def _():
            wait_block(b - 1, 1 - slot)

        wait_block(b, slot)
        each_zero_piece(lambda cp: cp.wait())


def _dispatch(x2b, sel_t, lay):
    t, d = x2b.shape
    e = sel_t.shape[0]
    tm = TM_MOE
    nb = t // tm
    r_loc = tm * TOP_K + e * (GRAN - 1)
    col = pl.BlockSpec((1, e, 1), lambda i, *_: (i, 0, 0))
    rowv = pl.BlockSpec((1, 1, e), lambda i, *_: (i, 0, 0))
    grid_spec = pltpu.PrefetchScalarGridSpec(
        num_scalar_prefetch=6,
        grid=(nb,),
        in_specs=[pl.BlockSpec((tm, d), lambda i, *_: (i, 0)),
                  pl.BlockSpec((e, tm), lambda i, *_: (0, i)), col, rowv, rowv],
        out_specs=pl.BlockSpec(memory_space=pl.ANY),
        scratch_shapes=[pltpu.VMEM((2, r_loc, d // 2), U32), pltpu.VMEM((FFN_TILE, d // 2), U32),
                        pltpu.SemaphoreType.DMA((3,))],
    )
    return pl.pallas_call(
        _dispatch_kernel,
        grid_spec=grid_spec,
        out_shape=jax.ShapeDtypeStruct((lay["r_glob"], d // 2), U32),
        compiler_params=pltpu.CompilerParams(dimension_semantics=("arbitrary",),
                                             vmem_limit_bytes=VMEM_LIMIT),
        name="dispatch",
    )(lay["loc_flat"], lay["glob_flat"], lay["rows_flat"], lay["used"], lay["tail_lo"],
      lay["tail_hi"], x2b, sel_t, lay["loc_col"], lay["loc_row"], lay["cnt_row"])


def _experts_kernel(tile_e_ref, n_used_ref, next_e_ref, xs_hbm, wg_hbm, wu_hbm, wd_hbm, ys_hbm,
                    wg_f, wu_f, wd_f, wg_b, wu_b, wd_b, xbuf, ybuf, zeros, sems, in_sems, out_sems,
                    zero_sem):
    i = pl.program_id(0)

    def weight_copies(ex):
        return (pltpu.make_async_copy(wg_hbm.at[ex], wg_f, sems.at[0]),
                pltpu.make_async_copy(wu_hbm.at[ex], wu_f, sems.at[1]),
                pltpu.make_async_copy(wd_hbm.at[ex], wd_f, sems.at[2]))

    def start_weights(ex):
        for cp in weight_copies(ex):
            cp.start(priority=1)

    @pl.when(i == 0)
    def _():
        start_weights(tile_e_ref[0])

    prev = tile_e_ref[jnp.maximum(i - 1, 0)]

    @pl.when((i == 0) | (tile_e_ref[i] != prev))
    def _():
        for cp in weight_copies(tile_e_ref[i]):
            cp.wait()
        wg_b[...] = wg_f[...].astype(BF16)
        wu_b[...] = wu_f[...].astype(BF16)
        wd_b[...] = wd_f[...].astype(BF16)
        nxt = next_e_ref[i]

        @pl.when(nxt >= 0)
        def _():
            start_weights(nxt)

    n_tiles = pl.num_programs(0)
    n_used = n_used_ref[0]
    tile_rows = xbuf.shape[1]

    def tile_in(j):
        return pltpu.make_async_copy(xs_hbm.at[pl.ds(pl.multiple_of(j * tile_rows, tile_rows), tile_rows)],
                                     xbuf.at[j % ROW_SLOTS], in_sems.at[j % ROW_SLOTS])

    def tile_out(j):
        return pltpu.make_async_copy(ybuf.at[j % ROW_SLOTS],
                                     ys_hbm.at[pl.ds(pl.multiple_of(j * tile_rows, tile_rows), tile_rows)],
                                     out_sems.at[j % ROW_SLOTS])

    def each_unused_tile(act):
        def body(j, _):
            act(pltpu.make_async_copy(
                zeros, ys_hbm.at[pl.ds(pl.multiple_of(j * tile_rows, tile_rows), tile_rows)], zero_sem))
            return 0

        lax.fori_loop(n_used, n_tiles, body, 0)

    @pl.when(i == 0)
    def _():
        zeros[...] = jnp.zeros_like(zeros)
        each_unused_tile(lambda cp: cp.start(priority=1))
        for j in range(ROW_SLOTS - 1):
            @pl.when(j < n_used)
            def _():
                tile_in(j).start()

    @pl.when(i + ROW_SLOTS - 1 < n_used)
    def _():
        tile_in(i + ROW_SLOTS - 1).start()

    @pl.when((i >= ROW_SLOTS) & (i - ROW_SLOTS < n_used))
    def _():
        tile_out(i - ROW_SLOTS).wait()

    @pl.when(i < n_used)
    def _():
        tile_in(i).wait()
        x = _unpack_rows(xbuf[i % ROW_SLOTS]).astype(BF16)
        hid = jax.nn.silu(_dot(x, wg_b[...])) * _dot(x, wu_b[...])
        y = _dot(hid.astype(BF16), wd_b[...])
        ybuf[i % ROW_SLOTS] = _pack_rows(y.astype(BF16).astype(F32))
        tile_out(i).start()

    @pl.when(i == n_tiles - 1)
    def _():
        for back in reversed(range(ROW_SLOTS)):
            @pl.when((i - back >= 0) & (i - back < n_used))
            def _():
                tile_out(i - back).wait()

        each_unused_tile(lambda cp: cp.wait())


def _experts(xs, tile_e, n_used, next_e, w_gate, w_up, w_down):
    r_glob, dh = xs.shape
    d, hdim = w_gate.shape[1:]
    n_tiles = r_glob // FFN_TILE

    hbm = pl.BlockSpec(memory_space=pl.ANY)
    grid_spec = pltpu.PrefetchScalarGridSpec(
        num_scalar_prefetch=3,
        grid=(n_tiles,),
        in_specs=[hbm, hbm, hbm, hbm],
        out_specs=hbm,
        scratch_shapes=[pltpu.VMEM((d, hdim), F32), pltpu.VMEM((d, hdim), F32),
                        pltpu.VMEM((hdim, d), F32),
                        pltpu.VMEM((d, hdim), BF16), pltpu.VMEM((d, hdim), BF16),
                        pltpu.VMEM((hdim, d), BF16),
                        pltpu.VMEM((ROW_SLOTS, FFN_TILE, dh), U32),
                        pltpu.VMEM((ROW_SLOTS, FFN_TILE, dh), U32),
                        pltpu.VMEM((FFN_TILE, dh), U32),
                        pltpu.SemaphoreType.DMA((3,)), pltpu.SemaphoreType.DMA((ROW_SLOTS,)),
                        pltpu.SemaphoreType.DMA((ROW_SLOTS,)), pltpu.SemaphoreType.DMA],
    )
    return pl.pallas_call(
        _experts_kernel,
        grid_spec=grid_spec,
        out_shape=jax.ShapeDtypeStruct((r_glob, dh), U32),
        compiler_params=pltpu.CompilerParams(dimension_semantics=("arbitrary",),
                                             vmem_limit_bytes=VMEM_LIMIT),
        name="experts",
    )(tile_e, n_used, next_e, xs, w_gate, w_up, w_down)


def _combine_kernel(loc_ref, glob_ref, rows_ref, used_ref,
                    base_ref, sel_ref, w_ref, locc_ref, locr_ref, cntr_ref, g_ref, bta_ref, ys_hbm,
                    o_ref, ys_loc, sems):
    b = pl.program_id(0)
    nb = pl.num_programs(0)
    e, tm = sel_ref.shape
    r_loc = ys_loc.shape[1]
    slot = b % 2
    buf = ys_loc.at[slot]

    always = tm * TOP_K

    def fetch_block(blk, s):
        ys_loc[s, always:, :] = jnp.zeros((r_loc - always, ys_loc.shape[2]), ys_loc.dtype)

        def per_expert(ex):
            n = rows_ref[blk * e + ex]

            @pl.when(n > 0)
            def _():
                _segment_copy(ys_hbm, glob_ref[blk * e + ex], ys_loc.at[s], loc_ref[blk * e + ex], n,
                              sems.at[s]).start()

        _for_each_expert(e, per_expert)

    @pl.when(b == 0)
    def _():
        fetch_block(b, slot)

    @pl.when(b + 1 < nb)
    def _():
        fetch_block(b + 1, 1 - slot)

    tables = _row_tables(sel_ref[...], locc_ref[0])
    wb = w_ref[...].astype(BF16)
    used = used_ref[b]

    _segment_copy(ys_hbm, 0, buf, 0, used, sems.at[slot]).wait()

    own = _row_owner(0, r_loc, locr_ref[0], cntr_ref[0])
    prow = _dot(jnp.concatenate([own, own], axis=1), tables)
    wrow = _dot(own, wb)
    y = _unpack_rows(buf[...]).astype(BF16)
    ri = lax.broadcasted_iota(I32, (r_loc, tm), 0).astype(F32)
    wmat = jnp.where(prow == ri, wrow, 0.0).astype(BF16)
    moe = _dot_tn(wmat, y)
    o_ref[...] = _ln(base_ref[...] + moe, g_ref[...], bta_ref[...])


def _combine(base, sel_t, w_t, lay, ys, ln_g, ln_b):
    t, d = base.shape
    e = sel_t.shape[0]
    tm = TM_MOE
    nb = t // tm
    r_loc = tm * TOP_K + e * (GRAN - 1)
    blk = pl.BlockSpec((e, tm), lambda i, *_: (0, i))
    row_spec = pl.BlockSpec((tm, d), lambda i, *_: (i, 0))
    vec = pl.BlockSpec((1, d), lambda i, *_: (0, 0))
    col = pl.BlockSpec((1, e, 1), lambda i, *_: (i, 0, 0))
    rowv = pl.BlockSpec((1, 1, e), lambda i, *_: (i, 0, 0))
    grid_spec = pltpu.PrefetchScalarGridSpec(
        num_scalar_prefetch=4,
        grid=(nb,),
        in_specs=[row_spec, blk, blk, col, rowv, rowv, vec, vec, pl.BlockSpec(memory_space=pl.ANY)],
        out_specs=row_spec,
        scratch_shapes=[pltpu.VMEM((2, r_loc, d // 2), U32), pltpu.SemaphoreType.DMA((2,))],
    )
    return pl.pallas_call(
        _combine_kernel,
        grid_spec=grid_spec,
        out_shape=jax.ShapeDtypeStruct((t, d), F32),
        compiler_params=pltpu.CompilerParams(dimension_semantics=("arbitrary",),
                                             vmem_limit_bytes=VMEM_LIMIT),
        name="combine",
    )(lay["loc_flat"], lay["glob_flat"], lay["rows_flat"], lay["used"],
      base, sel_t, w_t, lay["loc_col"], lay["loc_row"], lay["cnt_row"], ln_g, ln_b, ys)


def _round_up(x, m):
    return (x + m - 1) // m * m


def _moe_layout(cnt, t):
    nb, e = cnt.shape
    cnt_g = _round_up(cnt, GRAN)
    loc_off = jnp.cumsum(cnt_g, axis=1) - cnt_g
    used = jnp.sum(cnt_g, axis=1)
    gcnt = jnp.sum(cnt_g, axis=0)
    gpad = _round_up(gcnt, FFN_TILE)
    gend = jnp.cumsum(gpad)
    gstart = gend - gpad
    glob_off = gstart[None, :] + jnp.cumsum(cnt_g, axis=0) - cnt_g
    r_glob = _round_up(t * TOP_K + nb * e * (GRAN - 1) + e * (FFN_TILE - 1), FFN_TILE)
    n_tiles = r_glob // FFN_TILE
    n_used = (gend[-1] // FFN_TILE).astype(I32)
    tile_start = jnp.minimum(jnp.arange(n_tiles, dtype=I32), n_used - 1) * FFN_TILE
    tile_e = jnp.minimum(jnp.sum(gend[None, :] <= tile_start[:, None], axis=1), e - 1).astype(I32)
    ids = jnp.arange(e, dtype=I32)
    later_owner = jnp.min(jnp.where((ids[None, :] > ids[:, None]) & (gpad[None, :] > 0), ids[None, :], e),
                          axis=1)
    later_owner = jnp.where(later_owner < e, later_owner, -1)
    next_e = jnp.sum(jnp.where(tile_e[:, None] == ids[None, :], later_owner[None, :], 0), axis=1).astype(I32)
    return dict(next_e=next_e, loc_flat=loc_off.astype(I32).reshape(-1), glob_flat=glob_off.astype(I32).reshape(-1),
                rows_flat=cnt_g.astype(I32).reshape(-1), used=used.astype(I32),
                loc_col=loc_off.astype(F32)[:, :, None], loc_row=loc_off.astype(F32)[:, None, :],
                cnt_row=cnt_g.astype(F32)[:, None, :],
                tail_lo=jnp.append(gstart + gcnt, gend[-1]).astype(I32),
                tail_hi=jnp.append(gend, r_glob).astype(I32),
                tile_e=tile_e, n_used=n_used.reshape(1), r_glob=r_glob)


def kernel(x, mem, ln_in_g, ln_in_b, w_in, b_in, ln_v_g, ln_v_b, w_spatial, b_spatial, w_out,
           ln1_g, ln1_b, w_mem_q, w_mem_kv, w_mem_o, ln2_g, ln2_b, w_router, router_bias,
           w_exp_gate, w_exp_up, w_exp_down, w_sh_gate, w_sh_up, w_sh_down, ln3_g, ln3_b):
    bsz, seq, d = x.shape
    t = bsz * seq
    assert w_in.shape[0] == DEPTH
    x2d = x.reshape(t, d)
    row = lambda a: a.reshape(1, -1)

    oa, q, k, v, gb = _in_proj(x2d, row(ln_in_g), row(ln_in_b), w_in[0].astype(BF16), row(b_in[0]),
                               row(ln_v_g[0]), row(ln_v_b[0]), w_spatial[0], b_spatial[0].T)
    yb = _sb_attention(q.reshape(bsz, seq, d), k.reshape(bsz, seq, d), v.reshape(bsz, seq, d))
    k_mem, v_mem = _mem_kv(mem, w_mem_kv[0].astype(BF16))

    wr_t = w_router[0].T
    wr_hi = wr_t.astype(BF16)
    wr_lo = (wr_t - wr_hi.astype(F32)).astype(BF16)
    base, x2b, logits_t = _mid(
        x2d, row(ln_in_g), row(ln_in_b), oa, yb.reshape(t, d), gb, w_out[0].astype(BF16),
        row(ln1_g[0]), row(ln1_b[0]), w_mem_q[0].astype(BF16), k_mem, v_mem,
        w_mem_o[0].astype(BF16), row(ln2_g[0]), row(ln2_b[0]), wr_hi, wr_lo,
        w_sh_gate[0].astype(BF16), w_sh_up[0].astype(BF16), w_sh_down[0].astype(BF16), seq)

    sel_t, w_t, cnt = _route(logits_t, router_bias[0].reshape(-1, 1))
    lay = _moe_layout(cnt[:, :, 0].astype(I32), t)
    xs = _dispatch(x2b, sel_t, lay)
    ys = _experts(xs, lay["tile_e"], lay["n_used"], lay["next_e"], w_exp_gate[0], w_exp_up[0],
                  w_exp_down[0])
    out = _combine(base, sel_t, w_t, lay, ys, row(ln3_g[0]), row(ln3_b[0]))
    return out.reshape(bsz, seq, d)
```

```python
import jax
import jax.numpy as jnp
from jax import lax
from jax.experimental import pallas as pl
from jax.experimental.pallas import tpu as pltpu

F32 = jnp.float32
BF16 = jnp.bfloat16
I32 = jnp.int32
U32 = jnp.uint32

LANES = 128
SUBLANES = 8
GRAN = SUBLANES

CHUNK = 128
A_GROUPS = 8
SB_HEAD_DIM = 128
MEM_HEADS = 4
MEM_HEAD_DIM = 128
N_EXPERTS = 64
TOP_K = 8
N_GROUPS = 8
TOPK_GROUPS = 4
ROUTED_SCALE = 2.5
LN_EPS = 1e-5
DEPTH = 1
ALPHA = (2 * DEPTH) ** 0.25
LOG2E = 1.4426950408889634
SP_CLAMP = 64.0

TM_PROJ = 512
TM_MID = 512
TQ = 1024
SLAB = 256
TM_MOE = 256
TM_ROUTE = 1024
FFN_TILE = 512
EXPERT_UNROLL = 64
ROW_SLOTS = 3
ROW_SPLIT = 256.0
VMEM_LIMIT = 56 * 1024 * 1024


def _ln(x, g, b):
    mu = jnp.mean(x, axis=-1, keepdims=True)
    xc = x - mu
    var = jnp.mean(xc * xc, axis=-1, keepdims=True)
    return xc * lax.rsqrt(var + LN_EPS) * g + b


def _gelu(x):
    return 0.5 * x * (1.0 + lax.erf(x * (2.0 ** -0.5)))


def _dot(a, b):
    return jnp.dot(a, b, preferred_element_type=F32)


def _dot_nt(a, b):
    return lax.dot_general(a, b, (((1,), (1,)), ((), ())), preferred_element_type=F32)


def _dot_tn(a, b):
    return lax.dot_general(a, b, (((0,), (0,)), ((), ())), preferred_element_type=F32)


def _full(shape):
    n = len(shape)
    return pl.BlockSpec(shape, lambda *_: (0,) * n, pipeline_mode=pl.Buffered(1))


def _in_proj_kernel(x_ref, lg_ref, lb_ref, w_ref, b_ref, vg_ref, vb_ref, ws_ref, bst_ref,
                    oa_ref, q_ref, k_ref, v_ref, gb_ref):
    tm, d = x_ref.shape
    xb = _ln(x_ref[...], lg_ref[...], lb_ref[...]).astype(BF16)

    def seg(i):
        return _dot(xb, w_ref[:, i * d:(i + 1) * d]) + b_ref[:, i * d:(i + 1) * d]

    pre_v, pre_gate, pre_u = seg(1), seg(5), seg(0)
    q_ref[...] = (seg(2) * (SB_HEAD_DIM ** -0.5 * LOG2E)).astype(BF16)
    vln = _ln(_gelu(pre_v), vg_ref[...], vb_ref[...]).astype(BF16)
    k_ref[...] = seg(3).astype(BF16)
    gu = jax.nn.sigmoid(pre_gate) * _gelu(pre_u)
    v_ref[...] = seg(4).astype(BF16)
    gb_ref[...] = jax.nn.sigmoid(seg(6)).astype(BF16)

    gd = d // A_GROUPS
    row = lax.broadcasted_iota(I32, (CHUNK, CHUNK), 0)
    col = lax.broadcasted_iota(I32, (CHUNK, CHUNK), 1)
    for g in range(A_GROUPS):
        w = jnp.where(col <= row, ws_ref[g], 0.0).astype(BF16)
        bias = bst_ref[:, g:g + 1]
        for c in range(tm // CHUNK):
            rs = slice(c * CHUNK, (c + 1) * CHUNK)
            cs = slice(g * gd, (g + 1) * gd)
            mixed = _dot(w, vln[rs, cs]) + bias
            oa_ref[rs, cs] = (gu[rs, cs] * mixed).astype(BF16)


def _in_proj(x2d, ln_g, ln_b, w_in, b_in, vg, vb, w_s, b_st):
    t, d = x2d.shape
    n_in = w_in.shape[1]
    tm = TM_PROJ
    row_spec = pl.BlockSpec((tm, d), lambda i: (i, 0))
    out = jax.ShapeDtypeStruct((t, d), BF16)
    return pl.pallas_call(
        _in_proj_kernel,
        grid=(t // tm,),
        in_specs=[row_spec, _full((1, d)), _full((1, d)), _full((d, n_in)), _full((1, n_in)),
                  _full((1, d)), _full((1, d)), _full(w_s.shape), _full(b_st.shape)],
        out_specs=[row_spec] * 5,
        out_shape=[out] * 5,
        compiler_params=pltpu.CompilerParams(dimension_semantics=("parallel",),
                                             vmem_limit_bytes=VMEM_LIMIT),
        name="in_proj",
    )(x2d, ln_g, ln_b, w_in, b_in, vg, vb, w_s, b_st)


def _sb_kernel(q_ref, k_ref, v_ref, m_ref, o_ref):
    i = pl.program_id(2)
    tq = q_ref.shape[1]
    nslab = tq // SLAB
    q = q_ref[0]
    later = m_ref[...]

    tri_r = lax.broadcasted_iota(I32, (SLAB, SLAB), 0)
    tri_c = lax.broadcasted_iota(I32, (SLAB, SLAB), 1)

    def causal(x, diagonal):
        if not diagonal:
            return x
        top = jnp.where(tri_c < tri_r, x[:SLAB], 0.0)
        return top if x.shape[0] == SLAB else jnp.concatenate([top, x[SLAB:]], axis=0)

    def logits_stage(qs, j, diagonal):
        off = pl.multiple_of(j * SLAB, SLAB)
        z = _dot_nt(qs, k_ref[0, pl.ds(off, SLAB), :])
        sp = causal(jnp.maximum(jnp.log(1.0 + jnp.exp2(jnp.minimum(z, SP_CLAMP))) * LOG2E, z), diagonal)
        return z - sp, sp.astype(BF16), sp[:, 0:1]

    def weights_stage(staged, j, carry, diagonal):
        log2_beta, sp_b, first = staged
        off = pl.multiple_of(j * SLAB, SLAB)
        cs = _dot(sp_b, later)
        a = causal(jnp.exp2(log2_beta - cs - carry), diagonal)
        total = cs[:, 0:1] + first
        return carry + total, _dot(a.astype(BF16), v_ref[0, pl.ds(off, SLAB), :])

    def slab(qs, j, carry, diagonal):
        return weights_stage(logits_stage(qs, j, diagonal), j, carry, diagonal)

    carry = jnp.zeros((tq, 1), F32)
    acc = jnp.zeros((tq, SB_HEAD_DIM), F32)
    for d in reversed(range(nslab)):
        r0 = d * SLAB
        c_new, contrib = slab(q[r0:], i * nslab + d, carry[r0:], True)
        a_new = acc[r0:] + contrib
        carry = jnp.concatenate([carry[:r0], c_new], axis=0) if r0 else c_new
        acc = jnp.concatenate([acc[:r0], a_new], axis=0) if r0 else a_new

    def body(n, ca):
        carry, acc = ca
        js = [(i - n) * nslab - 1 - u for u in range(nslab)]
        staged = logits_stage(q, js[0], False)
        for u in range(nslab):
            ahead = logits_stage(q, js[u + 1], False) if u + 1 < nslab else None
            carry, contrib = weights_stage(staged, js[u], carry, False)
            acc = acc + contrib
            staged = ahead
        return carry, acc

    carry, acc = lax.fori_loop(0, i, body, (carry, acc))
    o_ref[0] = acc.astype(o_ref.dtype)


def _sb_attention(q, k, v):
    b, s, w = q.shape
    h = w // SB_HEAD_DIM
    jj = lax.broadcasted_iota(I32, (SLAB, SLAB), 0)
    ss = lax.broadcasted_iota(I32, (SLAB, SLAB), 1)
    mcat = jnp.where(jj > ss, 1.0, 0.0).astype(BF16)
    q_spec = pl.BlockSpec((1, TQ, SB_HEAD_DIM), lambda bi, hi, i: (bi, i, hi))
    kv_spec = pl.BlockSpec((1, s, SB_HEAD_DIM), lambda bi, hi, i: (bi, 0, hi))
    return pl.pallas_call(
        _sb_kernel,
        grid=(b, h, s // TQ),
        in_specs=[q_spec, kv_spec, kv_spec, _full(mcat.shape)],
        out_specs=q_spec,
        out_shape=jax.ShapeDtypeStruct((b, s, w), BF16),
        compiler_params=pltpu.CompilerParams(
            dimension_semantics=("parallel", "parallel", "arbitrary"),
            vmem_limit_bytes=VMEM_LIMIT),
        name="sb_attn",
    )(q, k, v, mcat)


def _mem_kv_kernel(m_ref, w_ref, k_ref, v_ref):
    kv = _dot(m_ref[0].astype(BF16), w_ref[...])
    half = kv.shape[1] // 2
    k_ref[0] = kv[:, :half].astype(BF16)
    v_ref[0] = kv[:, half:].astype(BF16)


def _mem_kv(mem, w_kv):
    b, m, d = mem.shape
    half = w_kv.shape[1] // 2
    out = jax.ShapeDtypeStruct((b, m, half), BF16)
    o_spec = pl.BlockSpec((1, m, half), lambda i: (i, 0, 0))
    return pl.pallas_call(
        _mem_kv_kernel,
        grid=(b,),
        in_specs=[pl.BlockSpec((1, m, d), lambda i: (i, 0, 0)), _full(w_kv.shape)],
        out_specs=[o_spec, o_spec],
        out_shape=[out, out],
        compiler_params=pltpu.CompilerParams(dimension_semantics=("parallel",),
                                             vmem_limit_bytes=VMEM_LIMIT),
        name="mem_kv",
    )(mem, w_kv)


def _mid_kernel(x_ref, lg_ref, lb_ref, oa_ref, yb_ref, gb_ref, wo_ref, l1g_ref, l1b_ref,
                wq_ref, km_ref, vm_ref, wmo_ref, l2g_ref, l2b_ref, wrh_ref, wrl_ref,
                wsg_ref, wsu_ref, wsd_ref,
                base_ref, x2_ref, lgt_ref):
    merged = oa_ref[...].astype(F32) + gb_ref[...].astype(F32) * yb_ref[...].astype(F32)
    mixed = _dot(merged.astype(BF16), wo_ref[...])
    xln = _ln(x_ref[...], lg_ref[...], lb_ref[...])
    x1 = _ln(ALPHA * xln + mixed, l1g_ref[...], l1b_ref[...])

    q = (_dot(x1.astype(BF16), wq_ref[...]) * (MEM_HEAD_DIM ** -0.5)).astype(BF16)
    head_cols = [slice(h * MEM_HEAD_DIM, (h + 1) * MEM_HEAD_DIM) for h in range(MEM_HEADS)]
    logits = [_dot_nt(q[:, hs], km_ref[0, :, hs]) for hs in head_cols]
    expd = [jnp.exp(lg - jnp.max(lg, axis=-1, keepdims=True)) for lg in logits]
    inv = [1.0 / jnp.sum(p, axis=-1, keepdims=True) for p in expd]
    heads = [_dot(p.astype(BF16), vm_ref[0, :, hs]) * r for p, r, hs in zip(expd, inv, head_cols)]
    o = jnp.concatenate(heads, axis=1).astype(BF16)
    x2 = _ln(ALPHA * x1 + _dot(o, wmo_ref[...]), l2g_ref[...], l2b_ref[...])

    x2h = x2.astype(BF16)
    x2_ref[...] = x2h
    gate = _dot(x2h, wsg_ref[...])
    up = _dot(x2h, wsu_ref[...])
    x2l = (x2 - x2h.astype(F32)).astype(BF16)
    lgt_ref[...] = (_dot_nt(wrh_ref[...], x2h) + _dot_nt(wrh_ref[...], x2l)
                    + _dot_nt(wrl_ref[...], x2h))
    hid = jax.nn.silu(gate) * up
    base_ref[...] = ALPHA * x2 + _dot(hid.astype(BF16), wsd_ref[...])


def _mid(x2d, ln_g, ln_b, oa, yb, gb, w_out, l1g, l1b, w_q, k_mem, v_mem, w_mo, l2g, l2b,
         wr_hi, wr_lo, w_sg, w_su, w_sd, seq):
    t, d = x2d.shape
    tm = TM_MID
    per_batch = seq // tm
    row_spec = pl.BlockSpec((tm, d), lambda i: (i, 0))
    mem_spec = pl.BlockSpec((1,) + k_mem.shape[1:], lambda i: (i // per_batch, 0, 0))
    vec = _full((1, d))
    return pl.pallas_call(
        _mid_kernel,
        grid=(t // tm,),
        in_specs=[row_spec, vec, vec, row_spec, row_spec, row_spec, _full(w_out.shape), vec, vec,
                  _full(w_q.shape), mem_spec, mem_spec, _full(w_mo.shape), vec, vec,
                  _full(wr_hi.shape), _full(wr_lo.shape),
                  _full(w_sg.shape), _full(w_su.shape), _full(w_sd.shape)],
        out_specs=[row_spec, row_spec, pl.BlockSpec((N_EXPERTS, tm), lambda i: (0, i))],
        out_shape=[jax.ShapeDtypeStruct((t, d), F32), jax.ShapeDtypeStruct((t, d), BF16),
                   jax.ShapeDtypeStruct((N_EXPERTS, t), F32)],
        compiler_params=pltpu.CompilerParams(dimension_semantics=("parallel",),
                                             vmem_limit_bytes=VMEM_LIMIT),
        name="mid",
    )(x2d, ln_g, ln_b, oa, yb, gb, w_out, l1g, l1b, w_q, k_mem, v_mem, w_mo, l2g, l2b,
      wr_hi, wr_lo, w_sg, w_su, w_sd)


def _route_kernel(lgt_ref, bias_ref, sel_ref, w_ref, cnt_ref):
    e, tm = lgt_ref.shape
    per_group = e // N_GROUPS
    scores = jax.nn.sigmoid(lgt_ref[...])
    sel = scores + bias_ref[...]

    g3 = sel.reshape(N_GROUPS, per_group, tm)
    j3 = lax.broadcasted_iota(I32, g3.shape, 1)
    m1 = jnp.max(g3, axis=1, keepdims=True)
    first = jnp.min(jnp.where(g3 == m1, j3, per_group), axis=1, keepdims=True)
    m2 = jnp.max(jnp.where(j3 == first, -jnp.inf, g3), axis=1, keepdims=True)
    gs = (m1 + m2).reshape(N_GROUPS, tm)

    gi = lax.broadcasted_iota(I32, (N_GROUPS, tm), 0)
    grank = jnp.zeros((N_GROUPS, tm), I32)
    for o in range(N_GROUPS):
        other = gs[o:o + 1, :]
        grank += ((other > gs) | ((other == gs) & (o < gi))).astype(I32)
    gmask = (grank < TOPK_GROUPS).astype(F32)
    emask = jnp.broadcast_to(gmask.reshape(N_GROUPS, 1, tm), (N_GROUPS, per_group, tm)).reshape(e, tm)
    cand = jnp.where(emask > 0.5, sel, -jnp.inf)

    ei = lax.broadcasted_iota(I32, (e, tm), 0)
    chosen = jnp.zeros((e, tm), jnp.bool_)
    left = cand
    for _ in range(TOP_K):
        best = jnp.max(left, axis=0, keepdims=True)
        first = jnp.min(jnp.where(left == best, ei, e), axis=0, keepdims=True)
        pick = ei == first
        chosen = chosen | pick
        left = jnp.where(pick, -jnp.inf, left)

    w = jnp.where(chosen, scores, 0.0)
    w = w / jnp.sum(w, axis=0, keepdims=True) * ROUTED_SCALE
    chosen_f = chosen.astype(F32)
    sel_ref[...] = chosen_f
    w_ref[...] = w
    for s in range(tm // TM_MOE):
        part = chosen_f[:, s * TM_MOE:(s + 1) * TM_MOE]
        cnt_ref[s] = jnp.broadcast_to(jnp.sum(part, axis=1, keepdims=True), (e, LANES))


def _route(logits_t, bias_col):
    e, t = logits_t.shape
    tm = TM_ROUTE
    sub = tm // TM_MOE
    blk = pl.BlockSpec((e, tm), lambda i: (0, i))
    return pl.pallas_call(
        _route_kernel,
        grid=(t // tm,),
        in_specs=[blk, _full((e, 1))],
        out_specs=[blk, blk, pl.BlockSpec((sub, e, LANES), lambda i: (i, 0, 0))],
        out_shape=[jax.ShapeDtypeStruct((e, t), F32), jax.ShapeDtypeStruct((e, t), F32),
                   jax.ShapeDtypeStruct((t // TM_MOE, e, LANES), F32)],
        compiler_params=pltpu.CompilerParams(dimension_semantics=("parallel",),
                                             vmem_limit_bytes=VMEM_LIMIT),
        name="route",
    )(logits_t, bias_col)


def _pack_rows(x):
    words = []
    for g in range(x.shape[1] // (2 * LANES)):
        hi = lax.bitcast_convert_type(x[:, 2 * g * LANES:(2 * g + 1) * LANES], U32)
        lo = lax.bitcast_convert_type(x[:, (2 * g + 1) * LANES:(2 * g + 2) * LANES], U32)
        words.append((hi & jnp.uint32(0xFFFF0000)) | lax.shift_right_logical(lo, jnp.uint32(16)))
    return jnp.concatenate(words, axis=1)


def _unpack_rows(u):
    tiles = []
    for g in range(u.shape[1] // LANES):
        w = u[:, g * LANES:(g + 1) * LANES]
        tiles.append(lax.bitcast_convert_type(w & jnp.uint32(0xFFFF0000), F32))
        tiles.append(lax.bitcast_convert_type(lax.shift_left(w, jnp.uint32(16)), F32))
    return jnp.concatenate(tiles, axis=1)


def _row_tables(sel, loc_col):
    e, tm = sel.shape
    tj = lax.broadcasted_iota(I32, (tm, tm), 0)
    tt = lax.broadcasted_iota(I32, (tm, tm), 1)
    before_t = jnp.where(tj < tt, 1.0, 0.0).astype(BF16)
    rank = _dot(sel.astype(BF16), before_t)
    pos = jnp.where(sel > 0.5, loc_col + rank, -1.0)
    hi = jnp.floor(pos * (1.0 / ROW_SPLIT)) * ROW_SPLIT
    return jnp.concatenate([hi, pos - hi], axis=0).astype(BF16)


def _row_owner(rows, loc_row, cnt_row):
    e = loc_row.shape[1]
    ri = lax.broadcasted_iota(I32, (rows, e), 0).astype(F32)
    return jnp.where((ri >= loc_row) & (ri < loc_row + cnt_row), 1.0, 0.0).astype(BF16)


def _for_each_expert(e, fn):
    def trip(i, _):
        for u in range(EXPERT_UNROLL):
            fn(i * EXPERT_UNROLL + u)
        return 0

    lax.fori_loop(0, e // EXPERT_UNROLL, trip, 0)


def _segment_copy(src, src_row, dst, dst_row, rows, sem):
    rows = pl.multiple_of(rows, GRAN)
    return pltpu.make_async_copy(src.at[pl.ds(pl.multiple_of(src_row, GRAN), rows)],
                                 dst.at[pl.ds(pl.multiple_of(dst_row, GRAN), rows)], sem)


def _dispatch_kernel(loc_ref, glob_ref, rows_ref, used_ref, tail_lo_ref, tail_hi_ref,
                     x_ref, sel_ref, locc_ref, locr_ref, cntr_ref, xs_hbm, xs_loc, zeros, sems):
    b = pl.program_id(0)
    nb = pl.num_programs(0)
    e, tm = sel_ref.shape
    r_loc = xs_loc.shape[1]
    slot = b % 2
    buf = xs_loc.at[slot]
    sem = sems.at[slot]

    def wait_block(blk, s):
        @pl.when(used_ref[blk] > 0)
        def _():
            _segment_copy(xs_loc.at[s], 0, xs_hbm, 0, used_ref[blk], sems.at[s]).wait()

    zrows = zeros.shape[0]

    def each_zero_piece(act):
        def per_tail(ex, _):
            lo = tail_lo_ref[ex]
            n = tail_hi_ref[ex] - lo

            def per_piece(j, _):
                rows = jnp.minimum(n - j * zrows, zrows)
                act(_segment_copy(zeros, 0, xs_hbm, lo + j * zrows, rows, sems.at[2]))
                return 0

            lax.fori_loop(0, (n + zrows - 1) // zrows, per_piece, 0)
            return 0

        lax.fori_loop(0, tail_lo_ref.shape[0], per_tail, 0)

    @pl.when(b == 0)
    def _():
        zeros[...] = jnp.zeros_like(zeros)
        each_zero_piece(lambda cp: cp.start(priority=1))

    @pl.when(b >= 2)
    def _():
        wait_block(b - 2, slot)

    tables = _row_tables(sel_ref[...], locc_ref[0])
    x = x_ref[...]
    own = _row_owner(r_loc, locr_ref[0], cntr_ref[0])
    prow = _dot(jnp.concatenate([own, own], axis=1), tables)
    ri = lax.broadcasted_iota(I32, (r_loc, tm), 0).astype(F32)
    onehot = jnp.where(prow == ri, 1.0, 0.0).astype(BF16)
    buf[...] = _pack_rows(_dot(onehot, x))

    def per_expert(ex):
        n = rows_ref[b * e + ex]

        @pl.when(n > 0)
        def _():
            _segment_copy(buf, loc_ref[b * e + ex], xs_hbm, glob_ref[b * e + ex], n, sem).start()

    _for_each_expert(e, per_expert)

    @pl.when(b == nb - 1)
    def _():
        @pl.when(b >= 1)
        def _():
            wait_block(b - 1, 1 - slot)

        wait_block(b, slot)
        each_zero_piece(lambda cp: cp.wait())


def _dispatch(x2b, sel_t, lay):
    t, d = x2b.shape
    e = sel_t.shape[0]
    tm = TM_MOE
    nb = t // tm
    r_loc = tm * TOP_K + e * (GRAN - 1)
    col = pl.BlockSpec((1, e, 1), lambda i, *_: (i, 0, 0))
    rowv = pl.BlockSpec((1, 1, e), lambda i, *_: (i, 0, 0))
    grid_spec = pltpu.PrefetchScalarGridSpec(
        num_scalar_prefetch=6,
        grid=(nb,),
        in_specs=[pl.BlockSpec((tm, d), lambda i, *_: (i, 0)),
                  pl.BlockSpec((e, tm), lambda i, *_: (0, i)), col, rowv, rowv],
        out_specs=pl.BlockSpec(memory_space=pl.ANY),
        scratch_shapes=[pltpu.VMEM((2, r_loc, d // 2), U32), pltpu.VMEM((FFN_TILE, d // 2), U32),
                        pltpu.SemaphoreType.DMA((3,))],
    )
    return pl.pallas_call(
        _dispatch_kernel,
        grid_spec=grid_spec,
        out_shape=jax.ShapeDtypeStruct((lay["r_glob"], d // 2), U32),
        compiler_params=pltpu.CompilerParams(dimension_semantics=("arbitrary",),
                                             vmem_limit_bytes=VMEM_LIMIT),
        name="dispatch",
    )(lay["loc_flat"], lay["glob_flat"], lay["rows_flat"], lay["used"], lay["tail_lo"],
      lay["tail_hi"], x2b, sel_t, lay["loc_col"], lay["loc_row"], lay["cnt_row"])


def _experts_kernel(tile_e_ref, n_used_ref, next_e_ref, xs_hbm, wg_hbm, wu_hbm, wd_hbm, ys_hbm,
                    wg_f, wu_f, wd_f, wg_b, wu_b, wd_b, xbuf, ybuf, zeros, sems, in_sems, out_sems,
                    zero_sem):
    i = pl.program_id(0)

    def weight_copies(ex):
        return (pltpu.make_async_copy(wg_hbm.at[ex], wg_f, sems.at[0]),
                pltpu.make_async_copy(wu_hbm.at[ex], wu_f, sems.at[1]),
                pltpu.make_async_copy(wd_hbm.at[ex], wd_f, sems.at[2]))

    def start_weights(ex):
        for cp in weight_copies(ex):
            cp.start(priority=1)

    @pl.when(i == 0)
    def _():
        start_weights(tile_e_ref[0])

    prev = tile_e_ref[jnp.maximum(i - 1, 0)]

    @pl.when((i == 0) | (tile_e_ref[i] != prev))
    def _():
        for cp in weight_copies(tile_e_ref[i]):
            cp.wait()
        wg_b[...] = wg_f[...].astype(BF16)
        wu_b[...] = wu_f[...].astype(BF16)
        wd_b[...] = wd_f[...].astype(BF16)
        nxt = next_e_ref[i]

        @pl.when(nxt >= 0)
        def _():
            start_weights(nxt)

    n_tiles = pl.num_programs(0)
    n_used = n_used_ref[0]
    tile_rows = xbuf.shape[1]

    def tile_in(j):
        return pltpu.make_async_copy(xs_hbm.at[pl.ds(pl.multiple_of(j * tile_rows, tile_rows), tile_rows)],
                                     xbuf.at[j % ROW_SLOTS], in_sems.at[j % ROW_SLOTS])

    def tile_out(j):
        return pltpu.make_async_copy(ybuf.at[j % ROW_SLOTS],
                                     ys_hbm.at[pl.ds(pl.multiple_of(j * tile_rows, tile_rows), tile_rows)],
                                     out_sems.at[j % ROW_SLOTS])

    def each_unused_tile(act):
        def body(j, _):
            act(pltpu.make_async_copy(
                zeros, ys_hbm.at[pl.ds(pl.multiple_of(j * tile_rows, tile_rows), tile_rows)], zero_sem))
            return 0

        lax.fori_loop(n_used, n_tiles, body, 0)

    @pl.when(i == 0)
    def _():
        zeros[...] = jnp.zeros_like(zeros)
        each_unused_tile(lambda cp: cp.start(priority=1))
        for j in range(ROW_SLOTS - 1):
            @pl.when(j < n_used)
            def _():
                tile_in(j).start()

    @pl.when(i + ROW_SLOTS - 1 < n_used)
    def _():
        tile_in(i + ROW_SLOTS - 1).start()

    @pl.when((i >= ROW_SLOTS) & (i - ROW_SLOTS < n_used))
    def _():
        tile_out(i - ROW_SLOTS).wait()

    @pl.when(i < n_used)
    def _():
        tile_in(i).wait()
        x = _unpack_rows(xbuf[i % ROW_SLOTS]).astype(BF16)
        hid = jax.nn.silu(_dot(x, wg_b[...])) * _dot(x, wu_b[...])
        y = _dot(hid.astype(BF16), wd_b[...])
        ybuf[i % ROW_SLOTS] = _pack_rows(y.astype(BF16).astype(F32))
        tile_out(i).start()

    @pl.when(i == n_tiles - 1)
    def _():
        for back in reversed(range(ROW_SLOTS)):
            @pl.when((i - back >= 0) & (i - back < n_used))
            def _():
                tile_out(i - back).wait()

        each_unused_tile(lambda cp: cp.wait())


def _experts(xs, tile_e, n_used, next_e, w_gate, w_up, w_down):
    r_glob, dh = xs.shape
    d, hdim = w_gate.shape[1:]
    n_tiles = r_glob // FFN_TILE

    hbm = pl.BlockSpec(memory_space=pl.ANY)
    grid_spec = pltpu.PrefetchScalarGridSpec(
        num_scalar_prefetch=3,
        grid=(n_tiles,),
        in_specs=[hbm, hbm, hbm, hbm],
        out_specs=hbm,
        scratch_shapes=[pltpu.VMEM((d, hdim), F32), pltpu.VMEM((d, hdim), F32),
                        pltpu.VMEM((hdim, d), F32),
                        pltpu.VMEM((d, hdim), BF16), pltpu.VMEM((d, hdim), BF16),
                        pltpu.VMEM((hdim, d), BF16),
                        pltpu.VMEM((ROW_SLOTS, FFN_TILE, dh), U32),
                        pltpu.VMEM((ROW_SLOTS, FFN_TILE, dh), U32),
                        pltpu.VMEM((FFN_TILE, dh), U32),
                        pltpu.SemaphoreType.DMA((3,)), pltpu.SemaphoreType.DMA((ROW_SLOTS,)),
                        pltpu.SemaphoreType.DMA((ROW_SLOTS,)), pltpu.SemaphoreType.DMA],
    )
    return pl.pallas_call(
        _experts_kernel,
        grid_spec=grid_spec,
        out_shape=jax.ShapeDtypeStruct((r_glob, dh), U32),
        compiler_params=pltpu.CompilerParams(dimension_semantics=("arbitrary",),
                                             vmem_limit_bytes=VMEM_LIMIT),
        name="experts",
    )(tile_e, n_used, next_e, xs, w_gate, w_up, w_down)


def _combine_kernel(loc_ref, glob_ref, rows_ref, used_ref,
                    base_ref, sel_ref, w_ref, locc_ref, locr_ref, cntr_ref, g_ref, bta_ref, ys_hbm,
                    o_ref, ys_loc, sems):
    b = pl.program_id(0)
    nb = pl.num_programs(0)
    e, tm = sel_ref.shape
    r_loc = ys_loc.shape[1]
    slot = b % 2
    buf = ys_loc.at[slot]

    always = tm * TOP_K

    def fetch_block(blk, s):
        ys_loc[s, always:, :] = jnp.zeros((r_loc - always, ys_loc.shape[2]), ys_loc.dtype)

        def per_expert(ex):
            n = rows_ref[blk * e + ex]

            @pl.when(n > 0)
            def _():
                _segment_copy(ys_hbm, glob_ref[blk * e + ex], ys_loc.at[s], loc_ref[blk * e + ex], n,
                              sems.at[s]).start()

        _for_each_expert(e, per_expert)

    @pl.when(b == 0)
    def _():
        fetch_block(b, slot)

    @pl.when(b + 1 < nb)
    def _():
        fetch_block(b + 1, 1 - slot)

    tables = _row_tables(sel_ref[...], locc_ref[0])
    wb = w_ref[...].astype(BF16)
    used = used_ref[b]

    _segment_copy(ys_hbm, 0, buf, 0, used, sems.at[slot]).wait()

    own = _row_owner(r_loc, locr_ref[0], cntr_ref[0])
    prow = _dot(jnp.concatenate([own, own], axis=1), tables)
    wrow = _dot(own, wb)
    y = _unpack_rows(buf[...]).astype(BF16)
    ri = lax.broadcasted_iota(I32, (r_loc, tm), 0).astype(F32)
    wmat = jnp.where(prow == ri, wrow, 0.0).astype(BF16)
    moe = _dot_tn(wmat, y)
    o_ref[...] = _ln(base_ref[...] + moe, g_ref[...], bta_ref[...])


def _combine(base, sel_t, w_t, lay, ys, ln_g, ln_b):
    t, d = base.shape
    e = sel_t.shape[0]
    tm = TM_MOE
    nb = t // tm
    r_loc = tm * TOP_K + e * (GRAN - 1)
    blk = pl.BlockSpec((e, tm), lambda i, *_: (0, i))
    row_spec = pl.BlockSpec((tm, d), lambda i, *_: (i, 0))
    vec = pl.BlockSpec((1, d), lambda i, *_: (0, 0))
    col = pl.BlockSpec((1, e, 1), lambda i, *_: (i, 0, 0))
    rowv = pl.BlockSpec((1, 1, e), lambda i, *_: (i, 0, 0))
    grid_spec = pltpu.PrefetchScalarGridSpec(
        num_scalar_prefetch=4,
        grid=(nb,),
        in_specs=[row_spec, blk, blk, col, rowv, rowv, vec, vec, pl.BlockSpec(memory_space=pl.ANY)],
        out_specs=row_spec,
        scratch_shapes=[pltpu.VMEM((2, r_loc, d // 2), U32), pltpu.SemaphoreType.DMA((2,))],
    )
    return pl.pallas_call(
        _combine_kernel,
        grid_spec=grid_spec,
        out_shape=jax.ShapeDtypeStruct((t, d), F32),
        compiler_params=pltpu.CompilerParams(dimension_semantics=("arbitrary",),
                                             vmem_limit_bytes=VMEM_LIMIT),
        name="combine",
    )(lay["loc_flat"], lay["glob_flat"], lay["rows_flat"], lay["used"],
      base, sel_t, w_t, lay["loc_col"], lay["loc_row"], lay["cnt_row"], ln_g, ln_b, ys)


def _round_up(x, m):
    return (x + m - 1) // m * m


def _moe_layout(cnt, t):
    nb, e = cnt.shape
    cnt_g = _round_up(cnt, GRAN)
    loc_off = jnp.cumsum(cnt_g, axis=1) - cnt_g
    used = jnp.sum(cnt_g, axis=1)
    gcnt = jnp.sum(cnt_g, axis=0)
    gpad = _round_up(gcnt, FFN_TILE)
    gend = jnp.cumsum(gpad)
    gstart = gend - gpad
    glob_off = gstart[None, :] + jnp.cumsum(cnt_g, axis=0) - cnt_g
    r_glob = _round_up(t * TOP_K + nb * e * (GRAN - 1) + e * (FFN_TILE - 1), FFN_TILE)
    n_tiles = r_glob // FFN_TILE
    n_used = (gend[-1] // FFN_TILE).astype(I32)
    tile_start = jnp.minimum(jnp.arange(n_tiles, dtype=I32), n_used - 1) * FFN_TILE
    tile_e = jnp.minimum(jnp.sum(gend[None, :] <= tile_start[:, None], axis=1), e - 1).astype(I32)
    ids = jnp.arange(e, dtype=I32)
    later_owner = jnp.min(jnp.where((ids[None, :] > ids[:, None]) & (gpad[None, :] > 0), ids[None, :], e),
                          axis=1)
    later_owner = jnp.where(later_owner < e, later_owner, -1)
    next_e = jnp.sum(jnp.where(tile_e[:, None] == ids[None, :], later_owner[None, :], 0), axis=1).astype(I32)
    return dict(next_e=next_e, loc_flat=loc_off.astype(I32).reshape(-1), glob_flat=glob_off.astype(I32).reshape(-1),
                rows_flat=cnt_g.astype(I32).reshape(-1), used=used.astype(I32),
                loc_col=loc_off.astype(F32)[:, :, None], loc_row=loc_off.astype(F32)[:, None, :],
                cnt_row=cnt_g.astype(F32)[:, None, :],
                tail_lo=jnp.append(gstart + gcnt, gend[-1]).astype(I32),
                tail_hi=jnp.append(gend, r_glob).astype(I32),
                tile_e=tile_e, n_used=n_used.reshape(1), r_glob=r_glob)


def kernel(x, mem, ln_in_g, ln_in_b, w_in, b_in, ln_v_g, ln_v_b, w_spatial, b_spatial, w_out,
           ln1_g, ln1_b, w_mem_q, w_mem_kv, w_mem_o, ln2_g, ln2_b, w_router, router_bias,
           w_exp_gate, w_exp_up, w_exp_down, w_sh_gate, w_sh_up, w_sh_down, ln3_g, ln3_b):
    bsz, seq, d = x.shape
    t = bsz * seq
    assert w_in.shape[0] == DEPTH
    assert seq % TQ == 0 and seq % TM_MID == 0 and seq % CHUNK == 0, seq
    assert t % TM_PROJ == 0 and t % TM_ROUTE == 0 and TM_ROUTE % TM_MOE == 0, t
    assert w_router.shape[2] == N_EXPERTS and N_EXPERTS % EXPERT_UNROLL == 0
    assert d % (2 * LANES) == 0 and d % A_GROUPS == 0, d
    x2d = x.reshape(t, d)
    row = lambda a: a.reshape(1, -1)

    oa, q, k, v, gb = _in_proj(x2d, row(ln_in_g), row(ln_in_b), w_in[0].astype(BF16), row(b_in[0]),
                               row(ln_v_g[0]), row(ln_v_b[0]), w_spatial[0], b_spatial[0].T)
    yb = _sb_attention(q.reshape(bsz, seq, d), k.reshape(bsz, seq, d), v.reshape(bsz, seq, d))
    k_mem, v_mem = _mem_kv(mem, w_mem_kv[0].astype(BF16))

    wr_t = w_router[0].T
    wr_hi = wr_t.astype(BF16)
    wr_lo = (wr_t - wr_hi.astype(F32)).astype(BF16)
    base, x2b, logits_t = _mid(
        x2d, row(ln_in_g), row(ln_in_b), oa, yb.reshape(t, d), gb, w_out[0].astype(BF16),
        row(ln1_g[0]), row(ln1_b[0]), w_mem_q[0].astype(BF16), k_mem, v_mem,
        w_mem_o[0].astype(BF16), row(ln2_g[0]), row(ln2_b[0]), wr_hi, wr_lo,
        w_sh_gate[0].astype(BF16), w_sh_up[0].astype(BF16), w_sh_down[0].astype(BF16), seq)

    sel_t, w_t, cnt = _route(logits_t, router_bias[0].reshape(-1, 1))
    lay = _moe_layout(cnt[:, :, 0].astype(I32), t)
    xs = _dispatch(x2b, sel_t, lay)
    ys = _experts(xs, lay["tile_e"], lay["n_used"], lay["next_e"], w_exp_gate[0], w_exp_up[0],
                  w_exp_down[0])
    out = _combine(base, sel_t, w_t, lay, ys, row(ln3_g[0]), row(ln3_b[0]))
    return out.reshape(bsz, seq, d)
```

```python
import functools

import jax
import jax.numpy as jnp
from jax import lax
from jax.experimental import pallas as pl
from jax.experimental.pallas import tpu as pltpu

F32 = jnp.float32
BF16 = jnp.bfloat16
I32 = jnp.int32
U32 = jnp.uint32

LANES = 128
SUBLANES = 8
GRAN = SUBLANES

CHUNK = 128
A_GROUPS = 8
SB_HEAD_DIM = 128
MEM_HEADS = 4
MEM_HEAD_DIM = 128
N_EXPERTS = 64
TOP_K = 8
N_GROUPS = 8
TOPK_GROUPS = 4
ROUTED_SCALE = 2.5
LN_EPS = 1e-5
DEPTH = 1
ALPHA = (2 * DEPTH) ** 0.25
LOG2E = 1.4426950408889634
SP_CLAMP = 64.0

TM_PROJ = 512
TM_MID = 1024
TQ = 1024
SLAB = 256
TM_MOE = 256
TM_ROUTE = 1024
FFN_TILE = 512
EXPERT_UNROLL = 64
ROW_SLOTS = 3
ROW_SPLIT = 256.0
VMEM_LIMIT = 56 * 1024 * 1024


def _ln(x, g, b):
    mu = jnp.mean(x, axis=-1, keepdims=True)
    xc = x - mu
    var = jnp.mean(xc * xc, axis=-1, keepdims=True)
    return xc * lax.rsqrt(var + LN_EPS) * g + b


def _gelu(x):
    return 0.5 * x * (1.0 + lax.erf(x * (2.0 ** -0.5)))


def _dot(a, b):
    return jnp.dot(a, b, preferred_element_type=F32)


def _dot_nt(a, b):
    return lax.dot_general(a, b, (((1,), (1,)), ((), ())), preferred_element_type=F32)


def _dot_tn(a, b):
    return lax.dot_general(a, b, (((0,), (0,)), ((), ())), preferred_element_type=F32)


def _full(shape):
    n = len(shape)
    return pl.BlockSpec(shape, lambda *_: (0,) * n, pipeline_mode=pl.Buffered(1))


def _in_proj_kernel(x_ref, lg_ref, lb_ref, w_ref, b_ref, vg_ref, vb_ref, ws_ref, bst_ref,
                    oa_ref, q_ref, k_ref, v_ref, gb_ref, xln_ref):
    tm, d = x_ref.shape
    xln = _ln(x_ref[...], lg_ref[...], lb_ref[...])
    xln_ref[...] = xln
    xb = xln.astype(BF16)

    def seg(i):
        return _dot(xb, w_ref[:, i * d:(i + 1) * d]) + b_ref[:, i * d:(i + 1) * d]

    pre_v, pre_gate, pre_u = seg(1), seg(5), seg(0)
    q_ref[...] = (seg(2) * (SB_HEAD_DIM ** -0.5 * LOG2E)).astype(BF16)
    vln = _ln(_gelu(pre_v), vg_ref[...], vb_ref[...]).astype(BF16)
    k_ref[...] = seg(3).astype(BF16)
    gu = jax.nn.sigmoid(pre_gate) * _gelu(pre_u)
    v_ref[...] = seg(4).astype(BF16)
    gb_ref[...] = jax.nn.sigmoid(seg(6)).astype(BF16)

    gd = d // A_GROUPS
    row = lax.broadcasted_iota(I32, (CHUNK, CHUNK), 0)
    col = lax.broadcasted_iota(I32, (CHUNK, CHUNK), 1)
    for g in range(A_GROUPS):
        w = jnp.where(col <= row, ws_ref[g], 0.0).astype(BF16)
        bias = bst_ref[:, g:g + 1]
        for c in range(tm // CHUNK):
            rs = slice(c * CHUNK, (c + 1) * CHUNK)
            cs = slice(g * gd, (g + 1) * gd)
            mixed = _dot(w, vln[rs, cs]) + bias
            oa_ref[rs, cs] = (gu[rs, cs] * mixed).astype(BF16)


def _in_proj(x2d, ln_g, ln_b, w_in, b_in, vg, vb, w_s, b_st):
    t, d = x2d.shape
    n_in = w_in.shape[1]
    tm = TM_PROJ
    row_spec = pl.BlockSpec((tm, d), lambda i: (i, 0))
    out = jax.ShapeDtypeStruct((t, d), BF16)
    return pl.pallas_call(
        _in_proj_kernel,
        grid=(t // tm,),
        in_specs=[row_spec, _full((1, d)), _full((1, d)), _full((d, n_in)), _full((1, n_in)),
                  _full((1, d)), _full((1, d)), _full(w_s.shape), _full(b_st.shape)],
        out_specs=[row_spec] * 6,
        out_shape=[out] * 5 + [jax.ShapeDtypeStruct((t, d), F32)],
        compiler_params=pltpu.CompilerParams(dimension_semantics=("parallel",),
                                             vmem_limit_bytes=VMEM_LIMIT),
        name="in_proj",
    )(x2d, ln_g, ln_b, w_in, b_in, vg, vb, w_s, b_st)


def _sb_kernel(q_ref, k_ref, v_ref, m_ref, o_ref):
    i = pl.program_id(2)
    tq = q_ref.shape[1]
    nslab = tq // SLAB
    q = q_ref[0]
    later = m_ref[...]

    tri_r = lax.broadcasted_iota(I32, (SLAB, SLAB), 0)
    tri_c = lax.broadcasted_iota(I32, (SLAB, SLAB), 1)

    def causal(x, diagonal):
        if not diagonal:
            return x
        top = jnp.where(tri_c < tri_r, x[:SLAB], 0.0)
        return top if x.shape[0] == SLAB else jnp.concatenate([top, x[SLAB:]], axis=0)

    def logits_stage(qs, j, diagonal):
        off = pl.multiple_of(j * SLAB, SLAB)
        z = _dot_nt(qs, k_ref[0, pl.ds(off, SLAB), :])
        sp = causal(jnp.maximum(jnp.log(1.0 + jnp.exp2(jnp.minimum(z, SP_CLAMP))) * LOG2E, z), diagonal)
        return z - sp, sp.astype(BF16), sp[:, 0:1]

    def weights_stage(staged, j, carry, diagonal):
        log2_beta, sp_b, first = staged
        off = pl.multiple_of(j * SLAB, SLAB)
        cs = _dot(sp_b, later)
        a = causal(jnp.exp2(log2_beta - cs - carry), diagonal)
        total = cs[:, 0:1] + first
        return carry + total, _dot(a.astype(BF16), v_ref[0, pl.ds(off, SLAB), :])

    def slab(qs, j, carry, diagonal):
        return weights_stage(logits_stage(qs, j, diagonal), j, carry, diagonal)

    carry = jnp.zeros((tq, 1), F32)
    acc = jnp.zeros((tq, SB_HEAD_DIM), F32)
    for d in reversed(range(nslab)):
        r0 = d * SLAB
        c_new, contrib = slab(q[r0:], i * nslab + d, carry[r0:], True)
        a_new = acc[r0:] + contrib
        carry = jnp.concatenate([carry[:r0], c_new], axis=0) if r0 else c_new
        acc = jnp.concatenate([acc[:r0], a_new], axis=0) if r0 else a_new

    def body(n, ca):
        carry, acc = ca
        js = [(i - n) * nslab - 1 - u for u in range(nslab)]
        staged = logits_stage(q, js[0], False)
        for u in range(nslab):
            ahead = logits_stage(q, js[u + 1], False) if u + 1 < nslab else None
            carry, contrib = weights_stage(staged, js[u], carry, False)
            acc = acc + contrib
            staged = ahead
        return carry, acc

    carry, acc = lax.fori_loop(0, i, body, (carry, acc))
    o_ref[0] = acc.astype(o_ref.dtype)


def _sb_attention(q, k, v):
    b, s, w = q.shape
    h = w // SB_HEAD_DIM
    jj = lax.broadcasted_iota(I32, (SLAB, SLAB), 0)
    ss = lax.broadcasted_iota(I32, (SLAB, SLAB), 1)
    mcat = jnp.where(jj > ss, 1.0, 0.0).astype(BF16)
    q_spec = pl.BlockSpec((1, TQ, SB_HEAD_DIM), lambda bi, hi, i: (bi, i, hi))
    kv_spec = pl.BlockSpec((1, s, SB_HEAD_DIM), lambda bi, hi, i: (bi, 0, hi))
    return pl.pallas_call(
        _sb_kernel,
        grid=(b, h, s // TQ),
        in_specs=[q_spec, kv_spec, kv_spec, _full(mcat.shape)],
        out_specs=q_spec,
        out_shape=jax.ShapeDtypeStruct((b, s, w), BF16),
        compiler_params=pltpu.CompilerParams(
            dimension_semantics=("parallel", "parallel", "arbitrary"),
            vmem_limit_bytes=VMEM_LIMIT),
        name="sb_attn",
    )(q, k, v, mcat)


def _mid_kernel(xln_ref, oa_ref, yb_ref, gb_ref, wo_ref, l1g_ref, l1b_ref,
                wq_ref, mem_ref, wkv_ref, wmo_ref, l2g_ref, l2b_ref, wrh_ref, wrl_ref,
                wsg_ref, wsu_ref, wsd_ref,
                base_ref, x2_ref, lgt_ref, km_ref, vm_ref, *, steps_per_batch):
    @pl.when(pl.program_id(0) % steps_per_batch == 0)
    def _():
        kv = _dot(mem_ref[0].astype(BF16), wkv_ref[...])
        half = kv.shape[1] // 2
        km_ref[...] = kv[:, :half].astype(BF16)
        vm_ref[...] = kv[:, half:].astype(BF16)

    merged = oa_ref[...].astype(F32) + gb_ref[...].astype(F32) * yb_ref[...].astype(F32)
    mixed = _dot(merged.astype(BF16), wo_ref[...])
    x1 = _ln(ALPHA * xln_ref[...] + mixed, l1g_ref[...], l1b_ref[...])

    q = (_dot(x1.astype(BF16), wq_ref[...]) * (MEM_HEAD_DIM ** -0.5)).astype(BF16)
    head_cols = [slice(h * MEM_HEAD_DIM, (h + 1) * MEM_HEAD_DIM) for h in range(MEM_HEADS)]
    logits = [_dot_nt(q[:, hs], km_ref[:, hs]) for hs in head_cols]
    expd = [jnp.exp(lg - jnp.max(lg, axis=-1, keepdims=True)) for lg in logits]
    inv = [1.0 / jnp.sum(p, axis=-1, keepdims=True) for p in expd]
    heads = [_dot(p.astype(BF16), vm_ref[:, hs]) * r for p, r, hs in zip(expd, inv, head_cols)]
    o = jnp.concatenate(heads, axis=1).astype(BF16)
    x2 = _ln(ALPHA * x1 + _dot(o, wmo_ref[...]), l2g_ref[...], l2b_ref[...])

    x2h = x2.astype(BF16)
    x2_ref[...] = x2h
    gate = _dot(x2h, wsg_ref[...])
    up = _dot(x2h, wsu_ref[...])
    x2l = (x2 - x2h.astype(F32)).astype(BF16)
    lgt_ref[...] = (_dot_nt(wrh_ref[...], x2h) + _dot_nt(wrh_ref[...], x2l)
                    + _dot_nt(wrl_ref[...], x2h))
    hid = jax.nn.silu(gate) * up
    base_ref[...] = ALPHA * x2 + _dot(hid.astype(BF16), wsd_ref[...])


def _mid(xln, oa, yb, gb, w_out, l1g, l1b, w_q, mem, w_kv, w_mo, l2g, l2b,
         wr_hi, wr_lo, w_sg, w_su, w_sd, seq):
    t, d = xln.shape
    tm = TM_MID
    per_batch = seq // tm
    m_len, half = mem.shape[1], w_kv.shape[1] // 2
    row_spec = pl.BlockSpec((tm, d), lambda i: (i, 0))
    mem_spec = pl.BlockSpec((1,) + mem.shape[1:], lambda i: (i // per_batch, 0, 0))
    vec = _full((1, d))
    return pl.pallas_call(
        functools.partial(_mid_kernel, steps_per_batch=per_batch),
        grid=(t // tm,),
        in_specs=[row_spec, row_spec, row_spec, row_spec, _full(w_out.shape), vec, vec,
                  _full(w_q.shape), mem_spec, _full(w_kv.shape), _full(w_mo.shape), vec, vec,
                  _full(wr_hi.shape), _full(wr_lo.shape),
                  _full(w_sg.shape), _full(w_su.shape), _full(w_sd.shape)],
        out_specs=[row_spec, row_spec, pl.BlockSpec((N_EXPERTS, tm), lambda i: (0, i))],
        out_shape=[jax.ShapeDtypeStruct((t, d), F32), jax.ShapeDtypeStruct((t, d), BF16),
                   jax.ShapeDtypeStruct((N_EXPERTS, t), F32)],
        scratch_shapes=[pltpu.VMEM((m_len, half), BF16), pltpu.VMEM((m_len, half), BF16)],
        compiler_params=pltpu.CompilerParams(dimension_semantics=("arbitrary",),
                                             vmem_limit_bytes=VMEM_LIMIT),
        name="mid",
    )(xln, oa, yb, gb, w_out, l1g, l1b, w_q, mem, w_kv, w_mo, l2g, l2b,
      wr_hi, wr_lo, w_sg, w_su, w_sd)


def _route_kernel(lgt_ref, bias_ref, sel_ref, w_ref, cnt_ref):
    e, tm = lgt_ref.shape
    per_group = e // N_GROUPS
    scores = jax.nn.sigmoid(lgt_ref[...])
    sel = scores + bias_ref[...]

    g3 = sel.reshape(N_GROUPS, per_group, tm)
    j3 = lax.broadcasted_iota(I32, g3.shape, 1)
    m1 = jnp.max(g3, axis=1, keepdims=True)
    first = jnp.min(jnp.where(g3 == m1, j3, per_group), axis=1, keepdims=True)
    m2 = jnp.max(jnp.where(j3 == first, -jnp.inf, g3), axis=1, keepdims=True)
    gs = (m1 + m2).reshape(N_GROUPS, tm)

    gi = lax.broadcasted_iota(I32, (N_GROUPS, tm), 0)
    grank = jnp.zeros((N_GROUPS, tm), I32)
    for o in range(N_GROUPS):
        other = gs[o:o + 1, :]
        grank += ((other > gs) | ((other == gs) & (o < gi))).astype(I32)
    gmask = (grank < TOPK_GROUPS).astype(F32)
    emask = jnp.broadcast_to(gmask.reshape(N_GROUPS, 1, tm), (N_GROUPS, per_group, tm)).reshape(e, tm)
    cand = jnp.where(emask > 0.5, sel, -jnp.inf)

    ei = lax.broadcasted_iota(I32, (e, tm), 0)
    chosen = jnp.zeros((e, tm), jnp.bool_)
    left = cand
    for _ in range(TOP_K):
        best = jnp.max(left, axis=0, keepdims=True)
        first = jnp.min(jnp.where(left == best, ei, e), axis=0, keepdims=True)
        pick = ei == first
        chosen = chosen | pick
        left = jnp.where(pick, -jnp.inf, left)

    w = jnp.where(chosen, scores, 0.0)
    w = w / jnp.sum(w, axis=0, keepdims=True) * ROUTED_SCALE
    chosen_f = chosen.astype(F32)
    sel_ref[...] = chosen_f
    w_ref[...] = w
    for s in range(tm // TM_MOE):
        part = chosen_f[:, s * TM_MOE:(s + 1) * TM_MOE]
        cnt_ref[s] = jnp.broadcast_to(jnp.sum(part, axis=1, keepdims=True), (e, LANES))


def _route(logits_t, bias_col):
    e, t = logits_t.shape
    tm = TM_ROUTE
    sub = tm // TM_MOE
    blk = pl.BlockSpec((e, tm), lambda i: (0, i))
    return pl.pallas_call(
        _route_kernel,
        grid=(t // tm,),
        in_specs=[blk, _full((e, 1))],
        out_specs=[blk, blk, pl.BlockSpec((sub, e, LANES), lambda i: (i, 0, 0))],
        out_shape=[jax.ShapeDtypeStruct((e, t), F32), jax.ShapeDtypeStruct((e, t), F32),
                   jax.ShapeDtypeStruct((t // TM_MOE, e, LANES), F32)],
        compiler_params=pltpu.CompilerParams(dimension_semantics=("parallel",),
                                             vmem_limit_bytes=VMEM_LIMIT),
        name="route",
    )(logits_t, bias_col)


def _pack_rows(x):
    words = []
    for g in range(x.shape[1] // (2 * LANES)):
        hi = lax.bitcast_convert_type(x[:, 2 * g * LANES:(2 * g + 1) * LANES], U32)
        lo = lax.bitcast_convert_type(x[:, (2 * g + 1) * LANES:(2 * g + 2) * LANES], U32)
        words.append((hi & jnp.uint32(0xFFFF0000)) | lax.shift_right_logical(lo, jnp.uint32(16)))
    return jnp.concatenate(words, axis=1)


def _unpack_rows(u):
    tiles = []
    for g in range(u.shape[1] // LANES):
        w = u[:, g * LANES:(g + 1) * LANES]
        tiles.append(lax.bitcast_convert_type(w & jnp.uint32(0xFFFF0000), F32))
        tiles.append(lax.bitcast_convert_type(lax.shift_left(w, jnp.uint32(16)), F32))
    return jnp.concatenate(tiles, axis=1)


def _row_tables(sel, loc_col):
    e, tm = sel.shape
    tj = lax.broadcasted_iota(I32, (tm, tm), 0)
    tt = lax.broadcasted_iota(I32, (tm, tm), 1)
    before_t = jnp.where(tj < tt, 1.0, 0.0).astype(BF16)
    rank = _dot(sel.astype(BF16), before_t)
    pos = jnp.where(sel > 0.5, loc_col + rank, -1.0)
    hi = jnp.floor(pos * (1.0 / ROW_SPLIT)) * ROW_SPLIT
    return jnp.concatenate([hi, pos - hi], axis=0).astype(BF16)


def _row_owner(rows, loc_row, cnt_row):
    e = loc_row.shape[1]
    ri = lax.broadcasted_iota(I32, (rows, e), 0).astype(F32)
    return jnp.where((ri >= loc_row) & (ri < loc_row + cnt_row), 1.0, 0.0).astype(BF16)


def _for_each_expert(e, fn):
    def trip(i, _):
        for u in range(EXPERT_UNROLL):
            fn(i * EXPERT_UNROLL + u)
        return 0

    lax.fori_loop(0, e // EXPERT_UNROLL, trip, 0)


def _segment_copy(src, src_row, dst, dst_row, rows, sem):
    rows = pl.multiple_of(rows, GRAN)
    return pltpu.make_async_copy(src.at[pl.ds(pl.multiple_of(src_row, GRAN), rows)],
                                 dst.at[pl.ds(pl.multiple_of(dst_row, GRAN), rows)], sem)


def _dispatch_kernel(loc_ref, glob_ref, rows_ref, used_ref, tail_lo_ref, tail_hi_ref,
                     x_ref, sel_ref, locc_ref, locr_ref, cntr_ref, xs_hbm, xs_loc, zeros, sems):
    b = pl.program_id(0)
    nb = pl.num_programs(0)
    e, tm = sel_ref.shape
    r_loc = xs_loc.shape[1]
    slot = b % 2
    buf = xs_loc.at[slot]
    sem = sems.at[slot]

    def wait_block(blk, s):
        @pl.when(used_ref[blk] > 0)
        def _():
            _segment_copy(xs_loc.at[s], 0, xs_hbm, 0, used_ref[blk], sems.at[s]).wait()

    zrows = zeros.shape[0]

    def each_zero_piece(act):
        def per_tail(ex, _):
            lo = tail_lo_ref[ex]
            n = tail_hi_ref[ex] - lo

            def per_piece(j, _):
                rows = jnp.minimum(n - j * zrows, zrows)
                act(_segment_copy(zeros, 0, xs_hbm, lo + j * zrows, rows, sems.at[2]))
                return 0

            lax.fori_loop(0, (n + zrows - 1) // zrows, per_piece, 0)
            return 0

        lax.fori_loop(0, tail_lo_ref.shape[0], per_tail, 0)

    @pl.when(b == 0)
    def _():
        zeros[...] = jnp.zeros_like(zeros)
        each_zero_piece(lambda cp: cp.start(priority=1))

    @pl.when(b >= 2)
    def _():
        wait_block(b - 2, slot)

    tables = _row_tables(sel_ref[...], locc_ref[0])
    x = x_ref[...]
    own = _row_owner(r_loc, locr_ref[0], cntr_ref[0])
    prow = _dot(jnp.concatenate([own, own], axis=1), tables)
    ri = lax.broadcasted_iota(I32, (r_loc, tm), 0).astype(F32)
    onehot = jnp.where(prow == ri, 1.0, 0.0).astype(BF16)
    buf[...] = _pack_rows(_dot(onehot, x))

    def per_expert(ex):
        n = rows_ref[b * e + ex]

        @pl.when(n > 0)
        def _():
            _segment_copy(buf, loc_ref[b * e + ex], xs_hbm, glob_ref[b * e + ex], n, sem).start()

    _for_each_expert(e, per_expert)

    @pl.when(b == nb - 1)
    def _():
        @pl.when(b >= 1)
        def _():
            wait_block(b - 1, 1 - slot)

        wait_block(b, slot)
        each_zero_piece(lambda cp: cp.wait())


def _dispatch(x2b, sel_t, lay):
    t, d = x2b.shape
    e = sel_t.shape[0]
    tm = TM_MOE
    nb = t // tm
    r_loc = tm * TOP_K + e * (GRAN - 1)
    col = pl.BlockSpec((1, e, 1), lambda i, *_: (i, 0, 0))
    rowv = pl.BlockSpec((1, 1, e), lambda i, *_: (i, 0, 0))
    grid_spec = pltpu.PrefetchScalarGridSpec(
        num_scalar_prefetch=6,
        grid=(nb,),
        in_specs=[pl.BlockSpec((tm, d), lambda i, *_: (i, 0)),
                  pl.BlockSpec((e, tm), lambda i, *_: (0, i)), col, rowv, rowv],
        out_specs=pl.BlockSpec(memory_space=pl.ANY),
        scratch_shapes=[pltpu.VMEM((2, r_loc, d // 2), U32), pltpu.VMEM((FFN_TILE, d // 2), U32),
                        pltpu.SemaphoreType.DMA((3,))],
    )
    return pl.pallas_call(
        _dispatch_kernel,
        grid_spec=grid_spec,
        out_shape=jax.ShapeDtypeStruct((lay["r_glob"], d // 2), U32),
        compiler_params=pltpu.CompilerParams(dimension_semantics=("arbitrary",),
                                             vmem_limit_bytes=VMEM_LIMIT),
        name="dispatch",
    )(lay["loc_flat"], lay["glob_flat"], lay["rows_flat"], lay["used"], lay["tail_lo"],
      lay["tail_hi"], x2b, sel_t, lay["loc_col"], lay["loc_row"], lay["cnt_row"])


def _experts_kernel(tile_e_ref, n_used_ref, next_e_ref, xs_hbm, wg_hbm, wu_hbm, wd_hbm, ys_hbm,
                    wg_f, wu_f, wd_f, wg_b, wu_b, wd_b, xbuf, ybuf, zeros, sems, in_sems, out_sems,
                    zero_sem):
    i = pl.program_id(0)

    def weight_copies(ex):
        return (pltpu.make_async_copy(wg_hbm.at[ex], wg_f, sems.at[0]),
                pltpu.make_async_copy(wu_hbm.at[ex], wu_f, sems.at[1]),
                pltpu.make_async_copy(wd_hbm.at[ex], wd_f, sems.at[2]))

    def start_weights(ex):
        for cp in weight_copies(ex):
            cp.start(priority=1)

    @pl.when(i == 0)
    def _():
        start_weights(tile_e_ref[0])

    prev = tile_e_ref[jnp.maximum(i - 1, 0)]

    @pl.when((i == 0) | (tile_e_ref[i] != prev))
    def _():
        for cp in weight_copies(tile_e_ref[i]):
            cp.wait()
        wg_b[...] = wg_f[...].astype(BF16)
        wu_b[...] = wu_f[...].astype(BF16)
        wd_b[...] = wd_f[...].astype(BF16)
        nxt = next_e_ref[i]

        @pl.when(nxt >= 0)
        def _():
            start_weights(nxt)

    n_tiles = pl.num_programs(0)
    n_used = n_used_ref[0]
    tile_rows = xbuf.shape[1]

    def tile_in(j):
        return pltpu.make_async_copy(xs_hbm.at[pl.ds(pl.multiple_of(j * tile_rows, tile_rows), tile_rows)],
                                     xbuf.at[j % ROW_SLOTS], in_sems.at[j % ROW_SLOTS])

    def tile_out(j):
        return pltpu.make_async_copy(ybuf.at[j % ROW_SLOTS],
                                     ys_hbm.at[pl.ds(pl.multiple_of(j * tile_rows, tile_rows), tile_rows)],
                                     out_sems.at[j % ROW_SLOTS])

    def each_unused_tile(act):
        def body(j, _):
            act(pltpu.make_async_copy(
                zeros, ys_hbm.at[pl.ds(pl.multiple_of(j * tile_rows, tile_rows), tile_rows)], zero_sem))
            return 0

        lax.fori_loop(n_used, n_tiles, body, 0)

    @pl.when(i == 0)
    def _():
        zeros[...] = jnp.zeros_like(zeros)
        each_unused_tile(lambda cp: cp.start(priority=1))
        for j in range(ROW_SLOTS - 1):
            @pl.when(j < n_used)
            def _():
                tile_in(j).start()

    @pl.when(i + ROW_SLOTS - 1 < n_used)
    def _():
        tile_in(i + ROW_SLOTS - 1).start()

    @pl.when((i >= ROW_SLOTS) & (i - ROW_SLOTS < n_used))
    def _():
        tile_out(i - ROW_SLOTS).wait()

    @pl.when(i < n_used)
    def _():
        tile_in(i).wait()
        x = _unpack_rows(xbuf[i % ROW_SLOTS]).astype(BF16)
        hid = jax.nn.silu(_dot(x, wg_b[...])) * _dot(x, wu_b[...])
        y = _dot(hid.astype(BF16), wd_b[...])
        ybuf[i % ROW_SLOTS] = _pack_rows(y.astype(BF16).astype(F32))
        tile_out(i).start()

    @pl.when(i == n_tiles - 1)
    def _():
        for back in reversed(range(ROW_SLOTS)):
            @pl.when((i - back >= 0) & (i - back < n_used))
            def _():
                tile_out(i - back).wait()

        each_unused_tile(lambda cp: cp.wait())


def _experts(xs, tile_e, n_used, next_e, w_gate, w_up, w_down):
    r_glob, dh = xs.shape
    d, hdim = w_gate.shape[1:]
    n_tiles = r_glob // FFN_TILE

    hbm = pl.BlockSpec(memory_space=pl.ANY)
    grid_spec = pltpu.PrefetchScalarGridSpec(
        num_scalar_prefetch=3,
        grid=(n_tiles,),
        in_specs=[hbm, hbm, hbm, hbm],
        out_specs=hbm,
        scratch_shapes=[pltpu.VMEM((d, hdim), F32), pltpu.VMEM((d, hdim), F32),
                        pltpu.VMEM((hdim, d), F32),
                        pltpu.VMEM((d, hdim), BF16), pltpu.VMEM((d, hdim), BF16),
                        pltpu.VMEM((hdim, d), BF16),
                        pltpu.VMEM((ROW_SLOTS, FFN_TILE, dh), U32),
                        pltpu.VMEM((ROW_SLOTS, FFN_TILE, dh), U32),
                        pltpu.VMEM((FFN_TILE, dh), U32),
                        pltpu.SemaphoreType.DMA((3,)), pltpu.SemaphoreType.DMA((ROW_SLOTS,)),
                        pltpu.SemaphoreType.DMA((ROW_SLOTS,)), pltpu.SemaphoreType.DMA],
    )
    return pl.pallas_call(
        _experts_kernel,
        grid_spec=grid_spec,
        out_shape=jax.ShapeDtypeStruct((r_glob, dh), U32),
        compiler_params=pltpu.CompilerParams(dimension_semantics=("arbitrary",),
                                             vmem_limit_bytes=VMEM_LIMIT),
        name="experts",
    )(tile_e, n_used, next_e, xs, w_gate, w_up, w_down)


def _combine_kernel(loc_ref, glob_ref, rows_ref, used_ref,
                    base_ref, sel_ref, w_ref, locc_ref, locr_ref, cntr_ref, g_ref, bta_ref, ys_hbm,
                    o_ref, ys_loc, sems):
    b = pl.program_id(0)
    nb = pl.num_programs(0)
    e, tm = sel_ref.shape
    r_loc = ys_loc.shape[1]
    slot = b % 2
    buf = ys_loc.at[slot]

    always = tm * TOP_K

    def fetch_block(blk, s):
        ys_loc[s, always:, :] = jnp.zeros((r_loc - always, ys_loc.shape[2]), ys_loc.dtype)

        def per_expert(ex):
            n = rows_ref[blk * e + ex]

            @pl.when(n > 0)
            def _():
                _segment_copy(ys_hbm, glob_ref[blk * e + ex], ys_loc.at[s], loc_ref[blk * e + ex], n,
                              sems.at[s]).start()

        _for_each_expert(e, per_expert)

    @pl.when(b == 0)
    def _():
        fetch_block(b, slot)

    @pl.when(b + 1 < nb)
    def _():
        fetch_block(b + 1, 1 - slot)

    tables = _row_tables(sel_ref[...], locc_ref[0])
    wb = w_ref[...].astype(BF16)
    used = used_ref[b]

    _segment_copy(ys_hbm, 0, buf, 0, used, sems.at[slot]).wait()

    own = _row_owner(r_loc, locr_ref[0], cntr_ref[0])
    prow = _dot(jnp.concatenate([own, own], axis=1), tables)
    wrow = _dot(own, wb)
    y = _unpack_rows(buf[...]).astype(BF16)
    ri = lax.broadcasted_iota(I32, (r_loc, tm), 0).astype(F32)
    wmat = jnp.where(prow == ri, wrow, 0.0).astype(BF16)
    moe = _dot_tn(wmat, y)
    o_ref[...] = _ln(base_ref[...] + moe, g_ref[...], bta_ref[...])


def _combine(base, sel_t, w_t, lay, ys, ln_g, ln_b):
    t, d = base.shape
    e = sel_t.shape[0]
    tm = TM_MOE
    nb = t // tm
    r_loc = tm * TOP_K + e * (GRAN - 1)
    blk = pl.BlockSpec((e, tm), lambda i, *_: (0, i))
    row_spec = pl.BlockSpec((tm, d), lambda i, *_: (i, 0))
    vec = pl.BlockSpec((1, d), lambda i, *_: (0, 0))
    col = pl.BlockSpec((1, e, 1), lambda i, *_: (i, 0, 0))
    rowv = pl.BlockSpec((1, 1, e), lambda i, *_: (i, 0, 0))
    grid_spec = pltpu.PrefetchScalarGridSpec(
        num_scalar_prefetch=4,
        grid=(nb,),
        in_specs=[row_spec, blk, blk, col, rowv, rowv, vec, vec, pl.BlockSpec(memory_space=pl.ANY)],
        out_specs=row_spec,
        scratch_shapes=[pltpu.VMEM((2, r_loc, d // 2), U32), pltpu.SemaphoreType.DMA((2,))],
    )
    return pl.pallas_call(
        _combine_kernel,
        grid_spec=grid_spec,
        out_shape=jax.ShapeDtypeStruct((t, d), F32),
        compiler_params=pltpu.CompilerParams(dimension_semantics=("arbitrary",),
                                             vmem_limit_bytes=VMEM_LIMIT),
        name="combine",
    )(lay["loc_flat"], lay["glob_flat"], lay["rows_flat"], lay["used"],
      base, sel_t, w_t, lay["loc_col"], lay["loc_row"], lay["cnt_row"], ln_g, ln_b, ys)


def _round_up(x, m):
    return (x + m - 1) // m * m


def _moe_layout(cnt, t):
    nb, e = cnt.shape
    cnt_g = _round_up(cnt, GRAN)
    loc_off = jnp.cumsum(cnt_g, axis=1) - cnt_g
    used = jnp.sum(cnt_g, axis=1)
    gcnt = jnp.sum(cnt_g, axis=0)
    gpad = _round_up(gcnt, FFN_TILE)
    gend = jnp.cumsum(gpad)
    gstart = gend - gpad
    glob_off = gstart[None, :] + jnp.cumsum(cnt_g, axis=0) - cnt_g
    r_glob = _round_up(t * TOP_K + nb * e * (GRAN - 1) + e * (FFN_TILE - 1), FFN_TILE)
    n_tiles = r_glob // FFN_TILE
    n_used = (gend[-1] // FFN_TILE).astype(I32)
    tile_start = jnp.minimum(jnp.arange(n_tiles, dtype=I32), n_used - 1) * FFN_TILE
    tile_e = jnp.minimum(jnp.sum(gend[None, :] <= tile_start[:, None], axis=1), e - 1).astype(I32)
    ids = jnp.arange(e, dtype=I32)
    later_owner = jnp.min(jnp.where((ids[None, :] > ids[:, None]) & (gpad[None, :] > 0), ids[None, :], e),
                          axis=1)
    later_owner = jnp.where(later_owner < e, later_owner, -1)
    next_e = jnp.sum(jnp.where(tile_e[:, None] == ids[None, :], later_owner[None, :], 0), axis=1).astype(I32)
    return dict(next_e=next_e, loc_flat=loc_off.astype(I32).reshape(-1), glob_flat=glob_off.astype(I32).reshape(-1),
                rows_flat=cnt_g.astype(I32).reshape(-1), used=used.astype(I32),
                loc_col=loc_off.astype(F32)[:, :, None], loc_row=loc_off.astype(F32)[:, None, :],
                cnt_row=cnt_g.astype(F32)[:, None, :],
                tail_lo=jnp.append(gstart + gcnt, gend[-1]).astype(I32),
                tail_hi=jnp.append(gend, r_glob).astype(I32),
                tile_e=tile_e, n_used=n_used.reshape(1), r_glob=r_glob)


def kernel(x, mem, ln_in_g, ln_in_b, w_in, b_in, ln_v_g, ln_v_b, w_spatial, b_spatial, w_out,
           ln1_g, ln1_b, w_mem_q, w_mem_kv, w_mem_o, ln2_g, ln2_b, w_router, router_bias,
           w_exp_gate, w_exp_up, w_exp_down, w_sh_gate, w_sh_up, w_sh_down, ln3_g, ln3_b):
    bsz, seq, d = x.shape
    t = bsz * seq
    assert w_in.shape[0] == DEPTH
    assert seq % TQ == 0 and seq % TM_MID == 0 and seq % CHUNK == 0, seq
    assert t % TM_PROJ == 0 and t % TM_ROUTE == 0 and TM_ROUTE % TM_MOE == 0, t
    assert w_router.shape[2] == N_EXPERTS and N_EXPERTS % EXPERT_UNROLL == 0
    assert d % (2 * LANES) == 0 and d % A_GROUPS == 0, d
    x2d = x.reshape(t, d)
    row = lambda a: a.reshape(1, -1)

    oa, q, k, v, gb, xln = _in_proj(x2d, row(ln_in_g), row(ln_in_b), w_in[0].astype(BF16),
                                    row(b_in[0]), row(ln_v_g[0]), row(ln_v_b[0]), w_spatial[0],
                                    b_spatial[0].T)
    yb = _sb_attention(q.reshape(bsz, seq, d), k.reshape(bsz, seq, d), v.reshape(bsz, seq, d))

    wr_t = w_router[0].T
    wr_hi = wr_t.astype(BF16)
    wr_lo = (wr_t - wr_hi.astype(F32)).astype(BF16)
    base, x2b, logits_t = _mid(
        xln, oa, yb.reshape(t, d), gb, w_out[0].astype(BF16),
        row(ln1_g[0]), row(ln1_b[0]), w_mem_q[0].astype(BF16), mem, w_mem_kv[0].astype(BF16),
        w_mem_o[0].astype(BF16), row(ln2_g[0]), row(ln2_b[0]), wr_hi, wr_lo,
        w_sh_gate[0].astype(BF16), w_sh_up[0].astype(BF16), w_sh_down[0].astype(BF16), seq)

    sel_t, w_t, cnt = _route(logits_t, router_bias[0].reshape(-1, 1))
    lay = _moe_layout(cnt[:, :, 0].astype(I32), t)
    xs = _dispatch(x2b, sel_t, lay)
    ys = _experts(xs, lay["tile_e"], lay["n_used"], lay["next_e"], w_exp_gate[0], w_exp_up[0],
                  w_exp_down[0])
    out = _combine(base, sel_t, w_t, lay, ys, row(ln3_g[0]), row(ln3_b[0]))
    return out.reshape(bsz, seq, d)
```

```python
import functools

import jax
import jax.numpy as jnp
from jax import lax
from jax.experimental import pallas as pl
from jax.experimental.pallas import tpu as pltpu

F32 = jnp.float32
BF16 = jnp.bfloat16
I32 = jnp.int32
U32 = jnp.uint32

LANES = 128
SUBLANES = 8
GRAN = SUBLANES

CHUNK = 128
A_GROUPS = 8
SB_HEAD_DIM = 128
MEM_HEADS = 4
MEM_HEAD_DIM = 128
N_EXPERTS = 64
TOP_K = 8
N_GROUPS = 8
TOPK_GROUPS = 4
ROUTED_SCALE = 2.5
LN_EPS = 1e-5
DEPTH = 1
ALPHA = (2 * DEPTH) ** 0.25
LOG2E = 1.4426950408889634
SP_CLAMP = 64.0

TM_PROJ = 512
TM_MID = 1024
TQ = 1024
SLAB = 256
TM_MOE = 256
TM_ROUTE = 1024
FFN_TILE = 512
EXPERT_UNROLL = 64
ROW_SLOTS = 3
ROW_SPLIT = 256.0
VMEM_LIMIT = 56 * 1024 * 1024


def _ln(x, g, b):
    mu = jnp.mean(x, axis=-1, keepdims=True)
    xc = x - mu
    var = jnp.mean(xc * xc, axis=-1, keepdims=True)
    return xc * lax.rsqrt(var + LN_EPS) * g + b


def _gelu(x):
    return 0.5 * x * (1.0 + lax.erf(x * (2.0 ** -0.5)))


def _dot(a, b):
    return jnp.dot(a, b, preferred_element_type=F32)


def _dot_nt(a, b):
    return lax.dot_general(a, b, (((1,), (1,)), ((), ())), preferred_element_type=F32)


def _dot_tn(a, b):
    return lax.dot_general(a, b, (((0,), (0,)), ((), ())), preferred_element_type=F32)


def _full(shape):
    n = len(shape)
    return pl.BlockSpec(shape, lambda *_: (0,) * n, pipeline_mode=pl.Buffered(1))


def _in_proj_kernel(x_ref, lg_ref, lb_ref, w_hbm, b_ref, vg_ref, vb_ref, ws_ref, bst_ref,
                    oa_ref, q_ref, k_ref, v_ref, gb_ref, xln_ref, w_ref, stage, sems):
    tm, d = x_ref.shape

    @pl.when(pl.program_id(0) == 0)
    def _():
        n_seg = w_ref.shape[1] // d

        def seg_copy(s):
            return pltpu.make_async_copy(w_hbm.at[:, pl.ds(s * d, d)], stage.at[s % 2], sems.at[s % 2])

        seg_copy(0).start()
        for s in range(n_seg):
            if s + 1 < n_seg:
                seg_copy(s + 1).start()
            seg_copy(s).wait()
            w_ref[:, s * d:(s + 1) * d] = stage[s % 2].astype(BF16)

    xln = _ln(x_ref[...], lg_ref[...], lb_ref[...])
    xln_ref[...] = xln
    xb = xln.astype(BF16)

    def seg(i):
        return _dot(xb, w_ref[:, i * d:(i + 1) * d]) + b_ref[:, i * d:(i + 1) * d]

    pre_v, pre_gate, pre_u = seg(1), seg(5), seg(0)
    q_ref[...] = (seg(2) * (SB_HEAD_DIM ** -0.5 * LOG2E)).astype(BF16)
    vln = _ln(_gelu(pre_v), vg_ref[...], vb_ref[...]).astype(BF16)
    k_ref[...] = seg(3).astype(BF16)
    gu = jax.nn.sigmoid(pre_gate) * _gelu(pre_u)
    v_ref[...] = seg(4).astype(BF16)
    gb_ref[...] = jax.nn.sigmoid(seg(6)).astype(BF16)

    gd = d // A_GROUPS
    row = lax.broadcasted_iota(I32, (CHUNK, CHUNK), 0)
    col = lax.broadcasted_iota(I32, (CHUNK, CHUNK), 1)
    for g in range(A_GROUPS):
        w = jnp.where(col <= row, ws_ref[g], 0.0).astype(BF16)
        bias = bst_ref[:, g:g + 1]
        for c in range(tm // CHUNK):
            rs = slice(c * CHUNK, (c + 1) * CHUNK)
            cs = slice(g * gd, (g + 1) * gd)
            mixed = _dot(w, vln[rs, cs]) + bias
            oa_ref[rs, cs] = (gu[rs, cs] * mixed).astype(BF16)


def _in_proj(x2d, ln_g, ln_b, w_in, b_in, vg, vb, w_s, b_st):
    t, d = x2d.shape
    n_in = w_in.shape[1]
    tm = TM_PROJ
    row_spec = pl.BlockSpec((tm, d), lambda i: (i, 0))
    out = jax.ShapeDtypeStruct((t, d), BF16)
    return pl.pallas_call(
        _in_proj_kernel,
        grid=(t // tm,),
        in_specs=[row_spec, _full((1, d)), _full((1, d)), pl.BlockSpec(memory_space=pl.ANY),
                  _full((1, n_in)), _full((1, d)), _full((1, d)), _full(w_s.shape), _full(b_st.shape)],
        out_specs=[row_spec] * 6,
        out_shape=[out] * 5 + [jax.ShapeDtypeStruct((t, d), F32)],
        scratch_shapes=[pltpu.VMEM((d, n_in), BF16), pltpu.VMEM((2, d, d), F32),
                        pltpu.SemaphoreType.DMA((2,))],
        compiler_params=pltpu.CompilerParams(dimension_semantics=("arbitrary",),
                                             vmem_limit_bytes=VMEM_LIMIT),
        name="in_proj",
    )(x2d, ln_g, ln_b, w_in, b_in, vg, vb, w_s, b_st)


def _sb_kernel(q_ref, k_ref, v_ref, m_ref, o_ref):
    i = pl.program_id(2)
    tq = q_ref.shape[1]
    nslab = tq // SLAB
    q = q_ref[0]
    later = m_ref[...]

    tri_r = lax.broadcasted_iota(I32, (SLAB, SLAB), 0)
    tri_c = lax.broadcasted_iota(I32, (SLAB, SLAB), 1)

    def causal(x, diagonal):
        if not diagonal:
            return x
        top = jnp.where(tri_c < tri_r, x[:SLAB], 0.0)
        return top if x.shape[0] == SLAB else jnp.concatenate([top, x[SLAB:]], axis=0)

    def logits_stage(qs, j, diagonal):
        off = pl.multiple_of(j * SLAB, SLAB)
        z = _dot_nt(qs, k_ref[0, pl.ds(off, SLAB), :])
        sp = causal(jnp.maximum(jnp.log(1.0 + jnp.exp2(jnp.minimum(z, SP_CLAMP))) * LOG2E, z), diagonal)
        return z - sp, sp.astype(BF16), sp[:, 0:1]

    def weights_stage(staged, j, carry, diagonal):
        log2_beta, sp_b, first = staged
        off = pl.multiple_of(j * SLAB, SLAB)
        cs = _dot(sp_b, later)
        a = causal(jnp.exp2(log2_beta - cs - carry), diagonal)
        total = cs[:, 0:1] + first
        return carry + total, _dot(a.astype(BF16), v_ref[0, pl.ds(off, SLAB), :])

    def slab(qs, j, carry, diagonal):
        return weights_stage(logits_stage(qs, j, diagonal), j, carry, diagonal)

    carry = jnp.zeros((tq, 1), F32)
    acc = jnp.zeros((tq, SB_HEAD_DIM), F32)
    for d in reversed(range(nslab)):
        r0 = d * SLAB
        c_new, contrib = slab(q[r0:], i * nslab + d, carry[r0:], True)
        a_new = acc[r0:] + contrib
        carry = jnp.concatenate([carry[:r0], c_new], axis=0) if r0 else c_new
        acc = jnp.concatenate([acc[:r0], a_new], axis=0) if r0 else a_new

    def body(n, ca):
        carry, acc = ca
        js = [(i - n) * nslab - 1 - u for u in range(nslab)]
        staged = logits_stage(q, js[0], False)
        for u in range(nslab):
            ahead = logits_stage(q, js[u + 1], False) if u + 1 < nslab else None
            carry, contrib = weights_stage(staged, js[u], carry, False)
            acc = acc + contrib
            staged = ahead
        return carry, acc

    carry, acc = lax.fori_loop(0, i, body, (carry, acc))
    o_ref[0] = acc.astype(o_ref.dtype)


def _sb_attention(q, k, v):
    b, s, w = q.shape
    h = w // SB_HEAD_DIM
    jj = lax.broadcasted_iota(I32, (SLAB, SLAB), 0)
    ss = lax.broadcasted_iota(I32, (SLAB, SLAB), 1)
    mcat = jnp.where(jj > ss, 1.0, 0.0).astype(BF16)
    q_spec = pl.BlockSpec((1, TQ, SB_HEAD_DIM), lambda bi, hi, i: (bi, i, hi))
    kv_spec = pl.BlockSpec((1, s, SB_HEAD_DIM), lambda bi, hi, i: (bi, 0, hi))
    return pl.pallas_call(
        _sb_kernel,
        grid=(b, h, s // TQ),
        in_specs=[q_spec, kv_spec, kv_spec, _full(mcat.shape)],
        out_specs=q_spec,
        out_shape=jax.ShapeDtypeStruct((b, s, w), BF16),
        compiler_params=pltpu.CompilerParams(
            dimension_semantics=("parallel", "parallel", "arbitrary"),
            vmem_limit_bytes=VMEM_LIMIT),
        name="sb_attn",
    )(q, k, v, mcat)


def _mid_kernel(xln_ref, oa_ref, yb_ref, gb_ref, wo_ref, l1g_ref, l1b_ref,
                wq_ref, mem_ref, wkv_ref, wmo_ref, l2g_ref, l2b_ref, wrh_ref, wrl_ref,
                wsg_ref, wsu_ref, wsd_ref,
                base_ref, x2_ref, lgt_ref, km_ref, vm_ref, *, steps_per_batch):
    @pl.when(pl.program_id(0) % steps_per_batch == 0)
    def _():
        kv = _dot(mem_ref[0].astype(BF16), wkv_ref[...])
        half = kv.shape[1] // 2
        km_ref[...] = kv[:, :half].astype(BF16)
        vm_ref[...] = kv[:, half:].astype(BF16)

    merged = oa_ref[...].astype(F32) + gb_ref[...].astype(F32) * yb_ref[...].astype(F32)
    mixed = _dot(merged.astype(BF16), wo_ref[...])
    x1 = _ln(ALPHA * xln_ref[...] + mixed, l1g_ref[...], l1b_ref[...])

    q = (_dot(x1.astype(BF16), wq_ref[...]) * (MEM_HEAD_DIM ** -0.5)).astype(BF16)
    head_cols = [slice(h * MEM_HEAD_DIM, (h + 1) * MEM_HEAD_DIM) for h in range(MEM_HEADS)]
    logits = [_dot_nt(q[:, hs], km_ref[:, hs]) for hs in head_cols]
    expd = [jnp.exp(lg - jnp.max(lg, axis=-1, keepdims=True)) for lg in logits]
    inv = [1.0 / jnp.sum(p, axis=-1, keepdims=True) for p in expd]
    heads = [_dot(p.astype(BF16), vm_ref[:, hs]) * r for p, r, hs in zip(expd, inv, head_cols)]
    o = jnp.concatenate(heads, axis=1).astype(BF16)
    x2 = _ln(ALPHA * x1 + _dot(o, wmo_ref[...]), l2g_ref[...], l2b_ref[...])

    x2h = x2.astype(BF16)
    x2_ref[...] = x2h
    gate = _dot(x2h, wsg_ref[...])
    up = _dot(x2h, wsu_ref[...])
    x2l = (x2 - x2h.astype(F32)).astype(BF16)
    lgt_ref[...] = (_dot_nt(wrh_ref[...], x2h) + _dot_nt(wrh_ref[...], x2l)
                    + _dot_nt(wrl_ref[...], x2h))
    hid = jax.nn.silu(gate) * up
    base_ref[...] = ALPHA * x2 + _dot(hid.astype(BF16), wsd_ref[...])


def _mid(xln, oa, yb, gb, w_out, l1g, l1b, w_q, mem, w_kv, w_mo, l2g, l2b,
         wr_hi, wr_lo, w_sg, w_su, w_sd, seq):
    t, d = xln.shape
    tm = TM_MID
    per_batch = seq // tm
    m_len, half = mem.shape[1], w_kv.shape[1] // 2
    row_spec = pl.BlockSpec((tm, d), lambda i: (i, 0))
    mem_spec = pl.BlockSpec((1,) + mem.shape[1:], lambda i: (i // per_batch, 0, 0))
    vec = _full((1, d))
    return pl.pallas_call(
        functools.partial(_mid_kernel, steps_per_batch=per_batch),
        grid=(t // tm,),
        in_specs=[row_spec, row_spec, row_spec, row_spec, _full(w_out.shape), vec, vec,
                  _full(w_q.shape), mem_spec, _full(w_kv.shape), _full(w_mo.shape), vec, vec,
                  _full(wr_hi.shape), _full(wr_lo.shape),
                  _full(w_sg.shape), _full(w_su.shape), _full(w_sd.shape)],
        out_specs=[row_spec, row_spec, pl.BlockSpec((N_EXPERTS, tm), lambda i: (0, i))],
        out_shape=[jax.ShapeDtypeStruct((t, d), F32), jax.ShapeDtypeStruct((t, d), BF16),
                   jax.ShapeDtypeStruct((N_EXPERTS, t), F32)],
        scratch_shapes=[pltpu.VMEM((m_len, half), BF16), pltpu.VMEM((m_len, half), BF16)],
        compiler_params=pltpu.CompilerParams(dimension_semantics=("arbitrary",),
                                             vmem_limit_bytes=VMEM_LIMIT),
        name="mid",
    )(xln, oa, yb, gb, w_out, l1g, l1b, w_q, mem, w_kv, w_mo, l2g, l2b,
      wr_hi, wr_lo, w_sg, w_su, w_sd)


def _route_kernel(lgt_ref, bias_ref, sel_ref, w_ref, cnt_ref):
    e, tm = lgt_ref.shape
    per_group = e // N_GROUPS
    scores = jax.nn.sigmoid(lgt_ref[...])
    sel = scores + bias_ref[...]

    g3 = sel.reshape(N_GROUPS, per_group, tm)
    j3 = lax.broadcasted_iota(I32, g3.shape, 1)
    m1 = jnp.max(g3, axis=1, keepdims=True)
    first = jnp.min(jnp.where(g3 == m1, j3, per_group), axis=1, keepdims=True)
    m2 = jnp.max(jnp.where(j3 == first, -jnp.inf, g3), axis=1, keepdims=True)
    gs = (m1 + m2).reshape(N_GROUPS, tm)

    gi = lax.broadcasted_iota(I32, (N_GROUPS, tm), 0)
    grank = jnp.zeros((N_GROUPS, tm), I32)
    for o in range(N_GROUPS):
        other = gs[o:o + 1, :]
        grank += ((other > gs) | ((other == gs) & (o < gi))).astype(I32)
    gmask = (grank < TOPK_GROUPS).astype(F32)
    emask = jnp.broadcast_to(gmask.reshape(N_GROUPS, 1, tm), (N_GROUPS, per_group, tm)).reshape(e, tm)
    cand = jnp.where(emask > 0.5, sel, -jnp.inf)

    ei = lax.broadcasted_iota(I32, (e, tm), 0)
    chosen = jnp.zeros((e, tm), jnp.bool_)
    left = cand
    for _ in range(TOP_K):
        best = jnp.max(left, axis=0, keepdims=True)
        first = jnp.min(jnp.where(left == best, ei, e), axis=0, keepdims=True)
        pick = ei == first
        chosen = chosen | pick
        left = jnp.where(pick, -jnp.inf, left)

    w = jnp.where(chosen, scores, 0.0)
    w = w / jnp.sum(w, axis=0, keepdims=True) * ROUTED_SCALE
    chosen_f = chosen.astype(F32)
    sel_ref[...] = chosen_f
    w_ref[...] = w
    for s in range(tm // TM_MOE):
        part = chosen_f[:, s * TM_MOE:(s + 1) * TM_MOE]
        cnt_ref[s] = jnp.broadcast_to(jnp.sum(part, axis=1, keepdims=True), (e, LANES))


def _route(logits_t, bias_col):
    e, t = logits_t.shape
    tm = TM_ROUTE
    sub = tm // TM_MOE
    blk = pl.BlockSpec((e, tm), lambda i: (0, i))
    return pl.pallas_call(
        _route_kernel,
        grid=(t // tm,),
        in_specs=[blk, _full((e, 1))],
        out_specs=[blk, blk, pl.BlockSpec((sub, e, LANES), lambda i: (i, 0, 0))],
        out_shape=[jax.ShapeDtypeStruct((e, t), F32), jax.ShapeDtypeStruct((e, t), F32),
                   jax.ShapeDtypeStruct((t // TM_MOE, e, LANES), F32)],
        compiler_params=pltpu.CompilerParams(dimension_semantics=("parallel",),
                                             vmem_limit_bytes=VMEM_LIMIT),
        name="route",
    )(logits_t, bias_col)


def _pack_rows(x):
    words = []
    for g in range(x.shape[1] // (2 * LANES)):
        hi = lax.bitcast_convert_type(x[:, 2 * g * LANES:(2 * g + 1) * LANES], U32)
        lo = lax.bitcast_convert_type(x[:, (2 * g + 1) * LANES:(2 * g + 2) * LANES], U32)
        words.append((hi & jnp.uint32(0xFFFF0000)) | lax.shift_right_logical(lo, jnp.uint32(16)))
    return jnp.concatenate(words, axis=1)


def _unpack_rows(u):
    tiles = []
    for g in range(u.shape[1] // LANES):
        w = u[:, g * LANES:(g + 1) * LANES]
        tiles.append(lax.bitcast_convert_type(w & jnp.uint32(0xFFFF0000), F32))
        tiles.append(lax.bitcast_convert_type(lax.shift_left(w, jnp.uint32(16)), F32))
    return jnp.concatenate(tiles, axis=1)


def _row_tables(sel, loc_col):
    e, tm = sel.shape
    tj = lax.broadcasted_iota(I32, (tm, tm), 0)
    tt = lax.broadcasted_iota(I32, (tm, tm), 1)
    before_t = jnp.where(tj < tt, 1.0, 0.0).astype(BF16)
    rank = _dot(sel.astype(BF16), before_t)
    pos = jnp.where(sel > 0.5, loc_col + rank, -1.0)
    hi = jnp.floor(pos * (1.0 / ROW_SPLIT)) * ROW_SPLIT
    return jnp.concatenate([hi, pos - hi], axis=0).astype(BF16)


def _row_owner(rows, loc_row, cnt_row):
    e = loc_row.shape[1]
    ri = lax.broadcasted_iota(I32, (rows, e), 0).astype(F32)
    return jnp.where((ri >= loc_row) & (ri < loc_row + cnt_row), 1.0, 0.0).astype(BF16)


def _for_each_expert(e, fn):
    def trip(i, _):
        for u in range(EXPERT_UNROLL):
            fn(i * EXPERT_UNROLL + u)
        return 0

    lax.fori_loop(0, e // EXPERT_UNROLL, trip, 0)


def _segment_copy(src, src_row, dst, dst_row, rows, sem):
    rows = pl.multiple_of(rows, GRAN)
    return pltpu.make_async_copy(src.at[pl.ds(pl.multiple_of(src_row, GRAN), rows)],
                                 dst.at[pl.ds(pl.multiple_of(dst_row, GRAN), rows)], sem)


def _dispatch_kernel(loc_ref, glob_ref, rows_ref, used_ref, tail_lo_ref, tail_hi_ref,
                     x_ref, sel_ref, locc_ref, locr_ref, cntr_ref, xs_hbm, xs_loc, zeros, sems):
    b = pl.program_id(0)
    nb = pl.num_programs(0)
    e, tm = sel_ref.shape
    r_loc = xs_loc.shape[1]
    slot = b % 2
    buf = xs_loc.at[slot]
    sem = sems.at[slot]

    def wait_block(blk, s):
        @pl.when(used_ref[blk] > 0)
        def _():
            _segment_copy(xs_loc.at[s], 0, xs_hbm, 0, used_ref[blk], sems.at[s]).wait()

    zrows = zeros.shape[0]

    def each_zero_piece(act):
        def per_tail(ex, _):
            lo = tail_lo_ref[ex]
            n = tail_hi_ref[ex] - lo

            def per_piece(j, _):
                rows = jnp.minimum(n - j * zrows, zrows)
                act(_segment_copy(zeros, 0, xs_hbm, lo + j * zrows, rows, sems.at[2]))
                return 0

            lax.fori_loop(0, (n + zrows - 1) // zrows, per_piece, 0)
            return 0

        lax.fori_loop(0, tail_lo_ref.shape[0], per_tail, 0)

    @pl.when(b == 0)
    def _():
        zeros[...] = jnp.zeros_like(zeros)
        each_zero_piece(lambda cp: cp.start(priority=1))

    @pl.when(b >= 2)
    def _():
        wait_block(b - 2, slot)

    tables = _row_tables(sel_ref[...], locc_ref[0])
    x = x_ref[...]
    own = _row_owner(r_loc, locr_ref[0], cntr_ref[0])
    prow = _dot(jnp.concatenate([own, own], axis=1), tables)
    ri = lax.broadcasted_iota(I32, (r_loc, tm), 0).astype(F32)
    onehot = jnp.where(prow == ri, 1.0, 0.0).astype(BF16)
    buf[...] = _pack_rows(_dot(onehot, x))

    def per_expert(ex):
        n = rows_ref[b * e + ex]

        @pl.when(n > 0)
        def _():
            _segment_copy(buf, loc_ref[b * e + ex], xs_hbm, glob_ref[b * e + ex], n, sem).start()

    _for_each_expert(e, per_expert)

    @pl.when(b == nb - 1)
    def _():
        @pl.when(b >= 1)
        def _():
            wait_block(b - 1, 1 - slot)

        wait_block(b, slot)
        each_zero_piece(lambda cp: cp.wait())


def _dispatch(x2b, sel_t, lay):
    t, d = x2b.shape
    e = sel_t.shape[0]
    tm = TM_MOE
    nb = t // tm
    r_loc = tm * TOP_K + e * (GRAN - 1)
    col = pl.BlockSpec((1, e, 1), lambda i, *_: (i, 0, 0))
    rowv = pl.BlockSpec((1, 1, e), lambda i, *_: (i, 0, 0))
    grid_spec = pltpu.PrefetchScalarGridSpec(
        num_scalar_prefetch=6,
        grid=(nb,),
        in_specs=[pl.BlockSpec((tm, d), lambda i, *_: (i, 0)),
                  pl.BlockSpec((e, tm), lambda i, *_: (0, i)), col, rowv, rowv],
        out_specs=pl.BlockSpec(memory_space=pl.ANY),
        scratch_shapes=[pltpu.VMEM((2, r_loc, d // 2), U32), pltpu.VMEM((FFN_TILE, d // 2), U32),
                        pltpu.SemaphoreType.DMA((3,))],
    )
    return pl.pallas_call(
        _dispatch_kernel,
        grid_spec=grid_spec,
        out_shape=jax.ShapeDtypeStruct((lay["r_glob"], d // 2), U32),
        compiler_params=pltpu.CompilerParams(dimension_semantics=("arbitrary",),
                                             vmem_limit_bytes=VMEM_LIMIT),
        name="dispatch",
    )(lay["loc_flat"], lay["glob_flat"], lay["rows_flat"], lay["used"], lay["tail_lo"],
      lay["tail_hi"], x2b, sel_t, lay["loc_col"], lay["loc_row"], lay["cnt_row"])


def _experts_kernel(tile_e_ref, n_used_ref, next_e_ref, xs_hbm, wg_hbm, wu_hbm, wd_hbm, ys_hbm,
                    wg_f, wu_f, wd_f, wg_b, wu_b, wd_b, xbuf, ybuf, zeros, sems, in_sems, out_sems,
                    zero_sem):
    i = pl.program_id(0)

    def weight_copies(ex):
        return (pltpu.make_async_copy(wg_hbm.at[ex], wg_f, sems.at[0]),
                pltpu.make_async_copy(wu_hbm.at[ex], wu_f, sems.at[1]),
                pltpu.make_async_copy(wd_hbm.at[ex], wd_f, sems.at[2]))

    def start_weights(ex):
        for cp in weight_copies(ex):
            cp.start(priority=1)

    @pl.when(i == 0)
    def _():
        start_weights(tile_e_ref[0])

    prev = tile_e_ref[jnp.maximum(i - 1, 0)]

    @pl.when((i == 0) | (tile_e_ref[i] != prev))
    def _():
        for cp in weight_copies(tile_e_ref[i]):
            cp.wait()
        wg_b[...] = wg_f[...].astype(BF16)
        wu_b[...] = wu_f[...].astype(BF16)
        wd_b[...] = wd_f[...].astype(BF16)
        nxt = next_e_ref[i]

        @pl.when(nxt >= 0)
        def _():
            start_weights(nxt)

    n_tiles = pl.num_programs(0)
    n_used = n_used_ref[0]
    tile_rows = xbuf.shape[1]

    def tile_in(j):
        return pltpu.make_async_copy(xs_hbm.at[pl.ds(pl.multiple_of(j * tile_rows, tile_rows), tile_rows)],
                                     xbuf.at[j % ROW_SLOTS], in_sems.at[j % ROW_SLOTS])

    def tile_out(j):
        return pltpu.make_async_copy(ybuf.at[j % ROW_SLOTS],
                                     ys_hbm.at[pl.ds(pl.multiple_of(j * tile_rows, tile_rows), tile_rows)],
                                     out_sems.at[j % ROW_SLOTS])

    def each_unused_tile(act):
        def body(j, _):
            act(pltpu.make_async_copy(
                zeros, ys_hbm.at[pl.ds(pl.multiple_of(j * tile_rows, tile_rows), tile_rows)], zero_sem))
            return 0

        lax.fori_loop(n_used, n_tiles, body, 0)

    @pl.when(i == 0)
    def _():
        zeros[...] = jnp.zeros_like(zeros)
        each_unused_tile(lambda cp: cp.start(priority=1))
        for j in range(ROW_SLOTS - 1):
            @pl.when(j < n_used)
            def _():
                tile_in(j).start()

    @pl.when(i + ROW_SLOTS - 1 < n_used)
    def _():
        tile_in(i + ROW_SLOTS - 1).start()

    @pl.when((i >= ROW_SLOTS) & (i - ROW_SLOTS < n_used))
    def _():
        tile_out(i - ROW_SLOTS).wait()

    @pl.when(i < n_used)
    def _():
        tile_in(i).wait()
        x = _unpack_rows(xbuf[i % ROW_SLOTS]).astype(BF16)
        hid = jax.nn.silu(_dot(x, wg_b[...])) * _dot(x, wu_b[...])
        y = _dot(hid.astype(BF16), wd_b[...])
        ybuf[i % ROW_SLOTS] = _pack_rows(y.astype(BF16).astype(F32))
        tile_out(i).start()

    @pl.when(i == n_tiles - 1)
    def _():
        for back in reversed(range(ROW_SLOTS)):
            @pl.when((i - back >= 0) & (i - back < n_used))
            def _():
                tile_out(i - back).wait()

        each_unused_tile(lambda cp: cp.wait())


def _experts(xs, tile_e, n_used, next_e, w_gate, w_up, w_down):
    r_glob, dh = xs.shape
    d, hdim = w_gate.shape[1:]
    n_tiles = r_glob // FFN_TILE

    hbm = pl.BlockSpec(memory_space=pl.ANY)
    grid_spec = pltpu.PrefetchScalarGridSpec(
        num_scalar_prefetch=3,
        grid=(n_tiles,),
        in_specs=[hbm, hbm, hbm, hbm],
        out_specs=hbm,
        scratch_shapes=[pltpu.VMEM((d, hdim), F32), pltpu.VMEM((d, hdim), F32),
                        pltpu.VMEM((hdim, d), F32),
                        pltpu.VMEM((d, hdim), BF16), pltpu.VMEM((d, hdim), BF16),
                        pltpu.VMEM((hdim, d), BF16),
                        pltpu.VMEM((ROW_SLOTS, FFN_TILE, dh), U32),
                        pltpu.VMEM((ROW_SLOTS, FFN_TILE, dh), U32),
                        pltpu.VMEM((FFN_TILE, dh), U32),
                        pltpu.SemaphoreType.DMA((3,)), pltpu.SemaphoreType.DMA((ROW_SLOTS,)),
                        pltpu.SemaphoreType.DMA((ROW_SLOTS,)), pltpu.SemaphoreType.DMA],
    )
    return pl.pallas_call(
        _experts_kernel,
        grid_spec=grid_spec,
        out_shape=jax.ShapeDtypeStruct((r_glob, dh), U32),
        compiler_params=pltpu.CompilerParams(dimension_semantics=("arbitrary",),
                                             vmem_limit_bytes=VMEM_LIMIT),
        name="experts",
    )(tile_e, n_used, next_e, xs, w_gate, w_up, w_down)


def _combine_kernel(loc_ref, glob_ref, rows_ref, used_ref,
                    base_ref, sel_ref, w_ref, locc_ref, locr_ref, cntr_ref, g_ref, bta_ref, ys_hbm,
                    o_ref, ys_loc, sems):
    b = pl.program_id(0)
    nb = pl.num_programs(0)
    e, tm = sel_ref.shape
    r_loc = ys_loc.shape[1]
    slot = b % 2
    buf = ys_loc.at[slot]

    always = tm * TOP_K

    def fetch_block(blk, s):
        ys_loc[s, always:, :] = jnp.zeros((r_loc - always, ys_loc.shape[2]), ys_loc.dtype)

        def per_expert(ex):
            n = rows_ref[blk * e + ex]

            @pl.when(n > 0)
            def _():
                _segment_copy(ys_hbm, glob_ref[blk * e + ex], ys_loc.at[s], loc_ref[blk * e + ex], n,
                              sems.at[s]).start()

        _for_each_expert(e, per_expert)

    @pl.when(b == 0)
    def _():
        fetch_block(b, slot)

    @pl.when(b + 1 < nb)
    def _():
        fetch_block(b + 1, 1 - slot)

    tables = _row_tables(sel_ref[...], locc_ref[0])
    wb = w_ref[...].astype(BF16)
    used = used_ref[b]

    _segment_copy(ys_hbm, 0, buf, 0, used, sems.at[slot]).wait()

    own = _row_owner(r_loc, locr_ref[0], cntr_ref[0])
    prow = _dot(jnp.concatenate([own, own], axis=1), tables)
    wrow = _dot(own, wb)
    y = _unpack_rows(buf[...]).astype(BF16)
    ri = lax.broadcasted_iota(I32, (r_loc, tm), 0).astype(F32)
    wmat = jnp.where(prow == ri, wrow, 0.0).astype(BF16)
    moe = _dot_tn(wmat, y)
    o_ref[...] = _ln(base_ref[...] + moe, g_ref[...], bta_ref[...])


def _combine(base, sel_t, w_t, lay, ys, ln_g, ln_b):
    t, d = base.shape
    e = sel_t.shape[0]
    tm = TM_MOE
    nb = t // tm
    r_loc = tm * TOP_K + e * (GRAN - 1)
    blk = pl.BlockSpec((e, tm), lambda i, *_: (0, i))
    row_spec = pl.BlockSpec((tm, d), lambda i, *_: (i, 0))
    vec = pl.BlockSpec((1, d), lambda i, *_: (0, 0))
    col = pl.BlockSpec((1, e, 1), lambda i, *_: (i, 0, 0))
    rowv = pl.BlockSpec((1, 1, e), lambda i, *_: (i, 0, 0))
    grid_spec = pltpu.PrefetchScalarGridSpec(
        num_scalar_prefetch=4,
        grid=(nb,),
        in_specs=[row_spec, blk, blk, col, rowv, rowv, vec, vec, pl.BlockSpec(memory_space=pl.ANY)],
        out_specs=row_spec,
        scratch_shapes=[pltpu.VMEM((2, r_loc, d // 2), U32), pltpu.SemaphoreType.DMA((2,))],
    )
    return pl.pallas_call(
        _combine_kernel,
        grid_spec=grid_spec,
        out_shape=jax.ShapeDtypeStruct((t, d), F32),
        compiler_params=pltpu.CompilerParams(dimension_semantics=("arbitrary",),
                                             vmem_limit_bytes=VMEM_LIMIT),
        name="combine",
    )(lay["loc_flat"], lay["glob_flat"], lay["rows_flat"], lay["used"],
      base, sel_t, w_t, lay["loc_col"], lay["loc_row"], lay["cnt_row"], ln_g, ln_b, ys)


def _round_up(x, m):
    return (x + m - 1) // m * m


def _moe_layout(cnt, t):
    nb, e = cnt.shape
    cnt_g = _round_up(cnt, GRAN)
    loc_off = jnp.cumsum(cnt_g, axis=1) - cnt_g
    used = jnp.sum(cnt_g, axis=1)
    gcnt = jnp.sum(cnt_g, axis=0)
    gpad = _round_up(gcnt, FFN_TILE)
    gend = jnp.cumsum(gpad)
    gstart = gend - gpad
    glob_off = gstart[None, :] + jnp.cumsum(cnt_g, axis=0) - cnt_g
    r_glob = _round_up(t * TOP_K + nb * e * (GRAN - 1) + e * (FFN_TILE - 1), FFN_TILE)
    n_tiles = r_glob // FFN_TILE
    n_used = (gend[-1] // FFN_TILE).astype(I32)
    tile_start = jnp.minimum(jnp.arange(n_tiles, dtype=I32), n_used - 1) * FFN_TILE
    tile_e = jnp.minimum(jnp.sum(gend[None, :] <= tile_start[:, None], axis=1), e - 1).astype(I32)
    ids = jnp.arange(e, dtype=I32)
    later_owner = jnp.min(jnp.where((ids[None, :] > ids[:, None]) & (gpad[None, :] > 0), ids[None, :], e),
                          axis=1)
    later_owner = jnp.where(later_owner < e, later_owner, -1)
    next_e = jnp.sum(jnp.where(tile_e[:, None] == ids[None, :], later_owner[None, :], 0), axis=1).astype(I32)
    return dict(next_e=next_e, loc_flat=loc_off.astype(I32).reshape(-1), glob_flat=glob_off.astype(I32).reshape(-1),
                rows_flat=cnt_g.astype(I32).reshape(-1), used=used.astype(I32),
                loc_col=loc_off.astype(F32)[:, :, None], loc_row=loc_off.astype(F32)[:, None, :],
                cnt_row=cnt_g.astype(F32)[:, None, :],
                tail_lo=jnp.append(gstart + gcnt, gend[-1]).astype(I32),
                tail_hi=jnp.append(gend, r_glob).astype(I32),
                tile_e=tile_e, n_used=n_used.reshape(1), r_glob=r_glob)


def kernel(x, mem, ln_in_g, ln_in_b, w_in, b_in, ln_v_g, ln_v_b, w_spatial, b_spatial, w_out,
           ln1_g, ln1_b, w_mem_q, w_mem_kv, w_mem_o, ln2_g, ln2_b, w_router, router_bias,
           w_exp_gate, w_exp_up, w_exp_down, w_sh_gate, w_sh_up, w_sh_down, ln3_g, ln3_b):
    bsz, seq, d = x.shape
    t = bsz * seq
    assert w_in.shape[0] == DEPTH
    assert seq % TQ == 0 and seq % TM_MID == 0 and seq % CHUNK == 0, seq
    assert t % TM_PROJ == 0 and t % TM_ROUTE == 0 and TM_ROUTE % TM_MOE == 0, t
    assert w_router.shape[2] == N_EXPERTS and N_EXPERTS % EXPERT_UNROLL == 0
    assert d % (2 * LANES) == 0 and d % A_GROUPS == 0, d
    x2d = x.reshape(t, d)
    row = lambda a: a.reshape(1, -1)

    oa, q, k, v, gb, xln = _in_proj(x2d, row(ln_in_g), row(ln_in_b), w_in[0], row(b_in[0]),
                                    row(ln_v_g[0]), row(ln_v_b[0]), w_spatial[0], b_spatial[0].T)
    yb = _sb_attention(q.reshape(bsz, seq, d), k.reshape(bsz, seq, d), v.reshape(bsz, seq, d))

    wr_t = w_router[0].T
    wr_hi = wr_t.astype(BF16)
    wr_lo = (wr_t - wr_hi.astype(F32)).astype(BF16)
    base, x2b, logits_t = _mid(
        xln, oa, yb.reshape(t, d), gb, w_out[0].astype(BF16),
        row(ln1_g[0]), row(ln1_b[0]), w_mem_q[0].astype(BF16), mem, w_mem_kv[0].astype(BF16),
        w_mem_o[0].astype(BF16), row(ln2_g[0]), row(ln2_b[0]), wr_hi, wr_lo,
        w_sh_gate[0].astype(BF16), w_sh_up[0].astype(BF16), w_sh_down[0].astype(BF16), seq)

    sel_t, w_t, cnt = _route(logits_t, router_bias[0].reshape(-1, 1))
    lay = _moe_layout(cnt[:, :, 0].astype(I32), t)
    xs = _dispatch(x2b, sel_t, lay)
    ys = _experts(xs, lay["tile_e"], lay["n_used"], lay["next_e"], w_exp_gate[0], w_exp_up[0],
                  w_exp_down[0])
    out = _combine(base, sel_t, w_t, lay, ys, row(ln3_g[0]), row(ln3_b[0]))
    return out.reshape(bsz, seq, d)
```

```python
import functools

import jax
import jax.numpy as jnp
from jax import lax
from jax.experimental import pallas as pl
from jax.experimental.pallas import tpu as pltpu

F32 = jnp.float32
BF16 = jnp.bfloat16
I32 = jnp.int32
U32 = jnp.uint32

LANES = 128
SUBLANES = 8
GRAN = SUBLANES

CHUNK = 128
A_GROUPS = 8
SB_HEAD_DIM = 128
MEM_HEADS = 4
MEM_HEAD_DIM = 128
N_EXPERTS = 64
TOP_K = 8
N_GROUPS = 8
TOPK_GROUPS = 4
ROUTED_SCALE = 2.5
LN_EPS = 1e-5
DEPTH = 1
ALPHA = (2 * DEPTH) ** 0.25
LOG2E = 1.4426950408889634
SP_CLAMP = 64.0

TM_PROJ = 512
TM_MID = 1024
TQ = 1024
SLAB = 256
TM_MOE = 256
TM_ROUTE = 1024
FFN_TILE = 512
EXPERT_UNROLL = 64
ROW_SLOTS = 3
ROW_SPLIT = 256.0
VMEM_LIMIT = 56 * 1024 * 1024


def _ln(x, g, b):
    mu = jnp.mean(x, axis=-1, keepdims=True)
    xc = x - mu
    var = jnp.mean(xc * xc, axis=-1, keepdims=True)
    return xc * lax.rsqrt(var + LN_EPS) * g + b


def _gelu(x):
    return 0.5 * x * (1.0 + lax.erf(x * (2.0 ** -0.5)))


def _dot(a, b):
    return jnp.dot(a, b, preferred_element_type=F32)


def _dot_nt(a, b):
    return lax.dot_general(a, b, (((1,), (1,)), ((), ())), preferred_element_type=F32)


def _dot_tn(a, b):
    return lax.dot_general(a, b, (((0,), (0,)), ((), ())), preferred_element_type=F32)


def _full(shape):
    n = len(shape)
    return pl.BlockSpec(shape, lambda *_: (0,) * n, pipeline_mode=pl.Buffered(1))


def _in_proj_kernel(x_ref, lg_ref, lb_ref, w_hbm, b_ref, vg_ref, vb_ref, ws_ref, bst_ref,
                    oa_ref, q_ref, k_ref, v_ref, gb_ref, xln_ref, w_ref, stage, sems):
    tm, d = x_ref.shape

    @pl.when(pl.program_id(0) == 0)
    def _():
        n_seg = w_ref.shape[1] // d

        def seg_copy(s):
            return pltpu.make_async_copy(w_hbm.at[:, pl.ds(s * d, d)], stage.at[s % 2], sems.at[s % 2])

        seg_copy(0).start()
        for s in range(n_seg):
            if s + 1 < n_seg:
                seg_copy(s + 1).start()
            seg_copy(s).wait()
            w_ref[:, s * d:(s + 1) * d] = stage[s % 2].astype(BF16)

    xln = _ln(x_ref[...], lg_ref[...], lb_ref[...])
    xln_ref[...] = xln
    xb = xln.astype(BF16)

    def seg(i):
        return _dot(xb, w_ref[:, i * d:(i + 1) * d]) + b_ref[:, i * d:(i + 1) * d]

    pre_v, pre_gate, pre_u = seg(1), seg(5), seg(0)
    q_ref[...] = (seg(2) * (SB_HEAD_DIM ** -0.5 * LOG2E)).astype(BF16)
    vln = _ln(_gelu(pre_v), vg_ref[...], vb_ref[...]).astype(BF16)
    k_ref[...] = seg(3).astype(BF16)
    gu = jax.nn.sigmoid(pre_gate) * _gelu(pre_u)
    v_ref[...] = seg(4).astype(BF16)
    gb_ref[...] = jax.nn.sigmoid(seg(6)).astype(BF16)

    gd = d // A_GROUPS
    row = lax.broadcasted_iota(I32, (CHUNK, CHUNK), 0)
    col = lax.broadcasted_iota(I32, (CHUNK, CHUNK), 1)
    for g in range(A_GROUPS):
        w = jnp.where(col <= row, ws_ref[g], 0.0).astype(BF16)
        bias = bst_ref[:, g:g + 1]
        for c in range(tm // CHUNK):
            rs = slice(c * CHUNK, (c + 1) * CHUNK)
            cs = slice(g * gd, (g + 1) * gd)
            mixed = _dot(w, vln[rs, cs]) + bias
            oa_ref[rs, cs] = (gu[rs, cs] * mixed).astype(BF16)


def _in_proj(x2d, ln_g, ln_b, w_in, b_in, vg, vb, w_s, b_st):
    t, d = x2d.shape
    n_in = w_in.shape[1]
    tm = TM_PROJ
    row_spec = pl.BlockSpec((tm, d), lambda i: (i, 0))
    out = jax.ShapeDtypeStruct((t, d), BF16)
    return pl.pallas_call(
        _in_proj_kernel,
        grid=(t // tm,),
        in_specs=[row_spec, _full((1, d)), _full((1, d)), pl.BlockSpec(memory_space=pl.ANY),
                  _full((1, n_in)), _full((1, d)), _full((1, d)), _full(w_s.shape), _full(b_st.shape)],
        out_specs=[row_spec] * 6,
        out_shape=[out] * 5 + [jax.ShapeDtypeStruct((t, d), F32)],
        scratch_shapes=[pltpu.VMEM((d, n_in), BF16), pltpu.VMEM((2, d, d), F32),
                        pltpu.SemaphoreType.DMA((2,))],
        compiler_params=pltpu.CompilerParams(dimension_semantics=("arbitrary",),
                                             vmem_limit_bytes=VMEM_LIMIT),
        name="in_proj",
    )(x2d, ln_g, ln_b, w_in, b_in, vg, vb, w_s, b_st)


def _sb_kernel(q_ref, k_ref, v_ref, m_ref, o_ref):
    i = pl.program_id(2)
    tq = q_ref.shape[1]
    nslab = tq // SLAB
    q = q_ref[0]
    later = m_ref[...]

    tri_r = lax.broadcasted_iota(I32, (SLAB, SLAB), 0)
    tri_c = lax.broadcasted_iota(I32, (SLAB, SLAB), 1)

    def causal(x, diagonal):
        if not diagonal:
            return x
        top = jnp.where(tri_c < tri_r, x[:SLAB], 0.0)
        return top if x.shape[0] == SLAB else jnp.concatenate([top, x[SLAB:]], axis=0)

    def logits_stage(qs, j, diagonal):
        off = pl.multiple_of(j * SLAB, SLAB)
        z = _dot_nt(qs, k_ref[0, pl.ds(off, SLAB), :])
        sp = causal(jnp.maximum(jnp.log(1.0 + jnp.exp2(jnp.minimum(z, SP_CLAMP))) * LOG2E, z), diagonal)
        return z - sp, sp.astype(BF16), sp[:, 0:1]

    def weights_stage(staged, j, carry, diagonal):
        log2_beta, sp_b, first = staged
        off = pl.multiple_of(j * SLAB, SLAB)
        cs = _dot(sp_b, later)
        a = causal(jnp.exp2(log2_beta - cs - carry), diagonal)
        total = cs[:, 0:1] + first
        return carry + total, _dot(a.astype(BF16), v_ref[0, pl.ds(off, SLAB), :])

    def slab(qs, j, carry, diagonal):
        return weights_stage(logits_stage(qs, j, diagonal), j, carry, diagonal)

    carry = jnp.zeros((tq, 1), F32)
    acc = jnp.zeros((tq, SB_HEAD_DIM), F32)
    for d in reversed(range(nslab)):
        r0 = d * SLAB
        c_new, contrib = slab(q[r0:], i * nslab + d, carry[r0:], True)
        a_new = acc[r0:] + contrib
        carry = jnp.concatenate([carry[:r0], c_new], axis=0) if r0 else c_new
        acc = jnp.concatenate([acc[:r0], a_new], axis=0) if r0 else a_new

    def body(n, ca):
        carry, acc = ca
        js = [(i - n) * nslab - 1 - u for u in range(nslab)]
        staged = logits_stage(q, js[0], False)
        for u in range(nslab):
            ahead = logits_stage(q, js[u + 1], False) if u + 1 < nslab else None
            carry, contrib = weights_stage(staged, js[u], carry, False)
            acc = acc + contrib
            staged = ahead
        return carry, acc

    carry, acc = lax.fori_loop(0, i, body, (carry, acc))
    o_ref[0] = acc.astype(o_ref.dtype)


def _sb_attention(q, k, v):
    b, s, w = q.shape
    h = w // SB_HEAD_DIM
    jj = lax.broadcasted_iota(I32, (SLAB, SLAB), 0)
    ss = lax.broadcasted_iota(I32, (SLAB, SLAB), 1)
    mcat = jnp.where(jj > ss, 1.0, 0.0).astype(BF16)
    q_spec = pl.BlockSpec((1, TQ, SB_HEAD_DIM), lambda bi, hi, i: (bi, i, hi))
    kv_spec = pl.BlockSpec((1, s, SB_HEAD_DIM), lambda bi, hi, i: (bi, 0, hi))
    return pl.pallas_call(
        _sb_kernel,
        grid=(b, h, s // TQ),
        in_specs=[q_spec, kv_spec, kv_spec, _full(mcat.shape)],
        out_specs=q_spec,
        out_shape=jax.ShapeDtypeStruct((b, s, w), BF16),
        compiler_params=pltpu.CompilerParams(
            dimension_semantics=("parallel", "parallel", "arbitrary"),
            vmem_limit_bytes=VMEM_LIMIT),
        name="sb_attn",
    )(q, k, v, mcat)


def _mid_kernel(xln_ref, oa_ref, yb_ref, gb_ref, wo_ref, l1g_ref, l1b_ref,
                wq_ref, mem_ref, wkv_ref, wmo_ref, l2g_ref, l2b_ref, wrh_ref, wrl_ref,
                wsg_ref, wsu_ref, wsd_ref,
                base_ref, x2_ref, lgt_ref, km_ref, vm_ref, *, steps_per_batch):
    @pl.when(pl.program_id(0) % steps_per_batch == 0)
    def _():
        kv = _dot(mem_ref[0].astype(BF16), wkv_ref[...])
        half = kv.shape[1] // 2
        km_ref[...] = kv[:, :half].astype(BF16)
        vm_ref[...] = kv[:, half:].astype(BF16)

    merged = oa_ref[...].astype(F32) + gb_ref[...].astype(F32) * yb_ref[...].astype(F32)
    mixed = _dot(merged.astype(BF16), wo_ref[...])
    x1 = _ln(ALPHA * xln_ref[...] + mixed, l1g_ref[...], l1b_ref[...])

    q = (_dot(x1.astype(BF16), wq_ref[...]) * (MEM_HEAD_DIM ** -0.5)).astype(BF16)
    head_cols = [slice(h * MEM_HEAD_DIM, (h + 1) * MEM_HEAD_DIM) for h in range(MEM_HEADS)]
    logits = [_dot_nt(q[:, hs], km_ref[:, hs]) for hs in head_cols]
    expd = [jnp.exp(lg - jnp.max(lg, axis=-1, keepdims=True)) for lg in logits]
    inv = [1.0 / jnp.sum(p, axis=-1, keepdims=True) for p in expd]
    heads = [_dot(p.astype(BF16), vm_ref[:, hs]) * r for p, r, hs in zip(expd, inv, head_cols)]
    o = jnp.concatenate(heads, axis=1).astype(BF16)
    x2 = _ln(ALPHA * x1 + _dot(o, wmo_ref[...]), l2g_ref[...], l2b_ref[...])

    x2h = x2.astype(BF16)
    x2_ref[...] = x2h
    gate = _dot(x2h, wsg_ref[...])
    up = _dot(x2h, wsu_ref[...])
    x2l = (x2 - x2h.astype(F32)).astype(BF16)
    lgt_ref[...] = (_dot_nt(wrh_ref[...], x2h) + _dot_nt(wrh_ref[...], x2l)
                    + _dot_nt(wrl_ref[...], x2h))
    hid = jax.nn.silu(gate) * up
    base_ref[...] = ALPHA * x2 + _dot(hid.astype(BF16), wsd_ref[...])


def _mid(xln, oa, yb, gb, w_out, l1g, l1b, w_q, mem, w_kv, w_mo, l2g, l2b,
         wr_hi, wr_lo, w_sg, w_su, w_sd, seq):
    t, d = xln.shape
    tm = TM_MID
    per_batch = seq // tm
    m_len, half = mem.shape[1], w_kv.shape[1] // 2
    row_spec = pl.BlockSpec((tm, d), lambda i: (i, 0))
    mem_spec = pl.BlockSpec((1,) + mem.shape[1:], lambda i: (i // per_batch, 0, 0))
    vec = _full((1, d))
    return pl.pallas_call(
        functools.partial(_mid_kernel, steps_per_batch=per_batch),
        grid=(t // tm,),
        in_specs=[row_spec, row_spec, row_spec, row_spec, _full(w_out.shape), vec, vec,
                  _full(w_q.shape), mem_spec, _full(w_kv.shape), _full(w_mo.shape), vec, vec,
                  _full(wr_hi.shape), _full(wr_lo.shape),
                  _full(w_sg.shape), _full(w_su.shape), _full(w_sd.shape)],
        out_specs=[row_spec, row_spec, pl.BlockSpec((N_EXPERTS, tm), lambda i: (0, i))],
        out_shape=[jax.ShapeDtypeStruct((t, d), F32), jax.ShapeDtypeStruct((t, d), BF16),
                   jax.ShapeDtypeStruct((N_EXPERTS, t), F32)],
        scratch_shapes=[pltpu.VMEM((m_len, half), BF16), pltpu.VMEM((m_len, half), BF16)],
        compiler_params=pltpu.CompilerParams(dimension_semantics=("arbitrary",),
                                             vmem_limit_bytes=VMEM_LIMIT),
        name="mid",
    )(xln, oa, yb, gb, w_out, l1g, l1b, w_q, mem, w_kv, w_mo, l2g, l2b,
      wr_hi, wr_lo, w_sg, w_su, w_sd)


def _route_kernel(lgt_ref, bias_ref, sel_ref, w_ref, cnt_ref):
    e, tm = lgt_ref.shape
    per_group = e // N_GROUPS
    scores = jax.nn.sigmoid(lgt_ref[...])
    sel = scores + bias_ref[...]

    g3 = sel.reshape(N_GROUPS, per_group, tm)
    j3 = lax.broadcasted_iota(I32, g3.shape, 1)
    m1 = jnp.max(g3, axis=1, keepdims=True)
    first = jnp.min(jnp.where(g3 == m1, j3, per_group), axis=1, keepdims=True)
    m2 = jnp.max(jnp.where(j3 == first, -jnp.inf, g3), axis=1, keepdims=True)
    gs = (m1 + m2).reshape(N_GROUPS, tm)

    gi = lax.broadcasted_iota(I32, (N_GROUPS, tm), 0)
    grank = jnp.zeros((N_GROUPS, tm), I32)
    for o in range(N_GROUPS):
        other = gs[o:o + 1, :]
        grank += ((other > gs) | ((other == gs) & (o < gi))).astype(I32)
    gmask = (grank < TOPK_GROUPS).astype(F32)
    emask = jnp.broadcast_to(gmask.reshape(N_GROUPS, 1, tm), (N_GROUPS, per_group, tm)).reshape(e, tm)
    cand = jnp.where(emask > 0.5, sel, -jnp.inf)

    ei = lax.broadcasted_iota(I32, (e, tm), 0)
    chosen = jnp.zeros((e, tm), jnp.bool_)
    left = cand
    for _ in range(TOP_K):
        best = jnp.max(left, axis=0, keepdims=True)
        first = jnp.min(jnp.where(left == best, ei, e), axis=0, keepdims=True)
        pick = ei == first
        chosen = chosen | pick
        left = jnp.where(pick, -jnp.inf, left)

    w = jnp.where(chosen, scores, 0.0)
    w = w / jnp.sum(w, axis=0, keepdims=True) * ROUTED_SCALE
    chosen_f = chosen.astype(F32)
    sel_ref[...] = chosen_f
    w_ref[...] = w
    for s in range(tm // TM_MOE):
        part = chosen_f[:, s * TM_MOE:(s + 1) * TM_MOE]
        cnt_ref[s] = jnp.sum(part, axis=1, keepdims=True)


def _route(logits_t, bias_col):
    e, t = logits_t.shape
    tm = TM_ROUTE
    sub = tm // TM_MOE
    blk = pl.BlockSpec((e, tm), lambda i: (0, i))
    return pl.pallas_call(
        _route_kernel,
        grid=(t // tm,),
        in_specs=[blk, _full((e, 1))],
        out_specs=[blk, blk, pl.BlockSpec((sub, e, 1), lambda i: (i, 0, 0))],
        out_shape=[jax.ShapeDtypeStruct((e, t), F32), jax.ShapeDtypeStruct((e, t), F32),
                   jax.ShapeDtypeStruct((t // TM_MOE, e, 1), F32)],
        compiler_params=pltpu.CompilerParams(dimension_semantics=("parallel",),
                                             vmem_limit_bytes=VMEM_LIMIT),
        name="route",
    )(logits_t, bias_col)


def _pack_rows(x):
    words = []
    for g in range(x.shape[1] // (2 * LANES)):
        hi = lax.bitcast_convert_type(x[:, 2 * g * LANES:(2 * g + 1) * LANES], U32)
        lo = lax.bitcast_convert_type(x[:, (2 * g + 1) * LANES:(2 * g + 2) * LANES], U32)
        words.append((hi & jnp.uint32(0xFFFF0000)) | lax.shift_right_logical(lo, jnp.uint32(16)))
    return jnp.concatenate(words, axis=1)


def _unpack_rows(u):
    tiles = []
    for g in range(u.shape[1] // LANES):
        w = u[:, g * LANES:(g + 1) * LANES]
        tiles.append(lax.bitcast_convert_type(w & jnp.uint32(0xFFFF0000), F32))
        tiles.append(lax.bitcast_convert_type(lax.shift_left(w, jnp.uint32(16)), F32))
    return jnp.concatenate(tiles, axis=1)


def _row_tables(sel, loc_col):
    e, tm = sel.shape
    tj = lax.broadcasted_iota(I32, (tm, tm), 0)
    tt = lax.broadcasted_iota(I32, (tm, tm), 1)
    before_t = jnp.where(tj < tt, 1.0, 0.0).astype(BF16)
    rank = _dot(sel.astype(BF16), before_t)
    pos = jnp.where(sel > 0.5, loc_col + rank, -1.0)
    hi = jnp.floor(pos * (1.0 / ROW_SPLIT)) * ROW_SPLIT
    return jnp.concatenate([hi, pos - hi], axis=0).astype(BF16)


def _row_owner(rows, loc_row, cnt_row):
    e = loc_row.shape[1]
    ri = lax.broadcasted_iota(I32, (rows, e), 0).astype(F32)
    return jnp.where((ri >= loc_row) & (ri < loc_row + cnt_row), 1.0, 0.0).astype(BF16)


def _for_each_expert(e, fn):
    def trip(i, _):
        for u in range(EXPERT_UNROLL):
            fn(i * EXPERT_UNROLL + u)
        return 0

    lax.fori_loop(0, e // EXPERT_UNROLL, trip, 0)


def _segment_copy(src, src_row, dst, dst_row, rows, sem):
    rows = pl.multiple_of(rows, GRAN)
    return pltpu.make_async_copy(src.at[pl.ds(pl.multiple_of(src_row, GRAN), rows)],
                                 dst.at[pl.ds(pl.multiple_of(dst_row, GRAN), rows)], sem)


def _dispatch_kernel(loc_ref, glob_ref, rows_ref, used_ref, tail_lo_ref, tail_hi_ref,
                     x_ref, sel_ref, locc_ref, locr_ref, cntr_ref, xs_hbm, xs_loc, zeros, sems):
    b = pl.program_id(0)
    nb = pl.num_programs(0)
    e, tm = sel_ref.shape
    r_loc = xs_loc.shape[1]
    slot = b % 2
    buf = xs_loc.at[slot]
    sem = sems.at[slot]

    def wait_block(blk, s):
        @pl.when(used_ref[blk] > 0)
        def _():
            _segment_copy(xs_loc.at[s], 0, xs_hbm, 0, used_ref[blk], sems.at[s]).wait()

    zrows = zeros.shape[0]

    def each_zero_piece(act):
        def per_tail(ex, _):
            lo = tail_lo_ref[ex]
            n = tail_hi_ref[ex] - lo

            def per_piece(j, _):
                rows = jnp.minimum(n - j * zrows, zrows)
                act(_segment_copy(zeros, 0, xs_hbm, lo + j * zrows, rows, sems.at[2]))
                return 0

            lax.fori_loop(0, (n + zrows - 1) // zrows, per_piece, 0)
            return 0

        lax.fori_loop(0, tail_lo_ref.shape[0], per_tail, 0)

    @pl.when(b == 0)
    def _():
        zeros[...] = jnp.zeros_like(zeros)
        each_zero_piece(lambda cp: cp.start(priority=1))

    @pl.when(b >= 2)
    def _():
        wait_block(b - 2, slot)

    tables = _row_tables(sel_ref[...], locc_ref[0])
    x = x_ref[...]
    own = _row_owner(r_loc, locr_ref[0], cntr_ref[0])
    prow = _dot(jnp.concatenate([own, own], axis=1), tables)
    ri = lax.broadcasted_iota(I32, (r_loc, tm), 0).astype(F32)
    onehot = jnp.where(prow == ri, 1.0, 0.0).astype(BF16)
    buf[...] = _pack_rows(_dot(onehot, x))

    def per_expert(ex):
        n = rows_ref[b * e + ex]

        @pl.when(n > 0)
        def _():
            _segment_copy(buf, loc_ref[b * e + ex], xs_hbm, glob_ref[b * e + ex], n, sem).start()

    _for_each_expert(e, per_expert)

    @pl.when(b == nb - 1)
    def _():
        @pl.when(b >= 1)
        def _():
            wait_block(b - 1, 1 - slot)

        wait_block(b, slot)
        each_zero_piece(lambda cp: cp.wait())


def _dispatch(x2b, sel_t, lay):
    t, d = x2b.shape
    e = sel_t.shape[0]
    tm = TM_MOE
    nb = t // tm
    r_loc = tm * TOP_K + e * (GRAN - 1)
    col = pl.BlockSpec((1, e, 1), lambda i, *_: (i, 0, 0))
    rowv = pl.BlockSpec((1, 1, e), lambda i, *_: (i, 0, 0))
    grid_spec = pltpu.PrefetchScalarGridSpec(
        num_scalar_prefetch=6,
        grid=(nb,),
        in_specs=[pl.BlockSpec((tm, d), lambda i, *_: (i, 0)),
                  pl.BlockSpec((e, tm), lambda i, *_: (0, i)), col, rowv, rowv],
        out_specs=pl.BlockSpec(memory_space=pl.ANY),
        scratch_shapes=[pltpu.VMEM((2, r_loc, d // 2), U32), pltpu.VMEM((FFN_TILE, d // 2), U32),
                        pltpu.SemaphoreType.DMA((3,))],
    )
    return pl.pallas_call(
        _dispatch_kernel,
        grid_spec=grid_spec,
        out_shape=jax.ShapeDtypeStruct((lay["r_glob"], d // 2), U32),
        compiler_params=pltpu.CompilerParams(dimension_semantics=("arbitrary",),
                                             vmem_limit_bytes=VMEM_LIMIT),
        name="dispatch",
    )(lay["loc_flat"], lay["glob_flat"], lay["rows_flat"], lay["used"], lay["tail_lo"],
      lay["tail_hi"], x2b, sel_t, lay["loc_col"], lay["loc_row"], lay["cnt_row"])


def _experts_kernel(tile_e_ref, n_used_ref, next_e_ref, xs_hbm, wg_hbm, wu_hbm, wd_hbm, ys_hbm,
                    wg_f, wu_f, wd_f, wg_b, wu_b, wd_b, xbuf, ybuf, zeros, sems, in_sems, out_sems,
                    zero_sem):
    i = pl.program_id(0)

    def weight_copies(ex):
        return (pltpu.make_async_copy(wg_hbm.at[ex], wg_f, sems.at[0]),
                pltpu.make_async_copy(wu_hbm.at[ex], wu_f, sems.at[1]),
                pltpu.make_async_copy(wd_hbm.at[ex], wd_f, sems.at[2]))

    def start_weights(ex):
        for cp in weight_copies(ex):
            cp.start(priority=1)

    @pl.when(i == 0)
    def _():
        start_weights(tile_e_ref[0])

    prev = tile_e_ref[jnp.maximum(i - 1, 0)]

    @pl.when((i == 0) | (tile_e_ref[i] != prev))
    def _():
        for cp in weight_copies(tile_e_ref[i]):
            cp.wait()
        wg_b[...] = wg_f[...].astype(BF16)
        wu_b[...] = wu_f[...].astype(BF16)
        wd_b[...] = wd_f[...].astype(BF16)
        nxt = next_e_ref[i]

        @pl.when(nxt >= 0)
        def _():
            start_weights(nxt)

    n_tiles = pl.num_programs(0)
    n_used = n_used_ref[0]
    tile_rows = xbuf.shape[1]

    def tile_in(j):
        return pltpu.make_async_copy(xs_hbm.at[pl.ds(pl.multiple_of(j * tile_rows, tile_rows), tile_rows)],
                                     xbuf.at[j % ROW_SLOTS], in_sems.at[j % ROW_SLOTS])

    def tile_out(j):
        return pltpu.make_async_copy(ybuf.at[j % ROW_SLOTS],
                                     ys_hbm.at[pl.ds(pl.multiple_of(j * tile_rows, tile_rows), tile_rows)],
                                     out_sems.at[j % ROW_SLOTS])

    def each_unused_tile(act):
        def body(j, _):
            act(pltpu.make_async_copy(
                zeros, ys_hbm.at[pl.ds(pl.multiple_of(j * tile_rows, tile_rows), tile_rows)], zero_sem))
            return 0

        lax.fori_loop(n_used, n_tiles, body, 0)

    @pl.when(i == 0)
    def _():
        zeros[...] = jnp.zeros_like(zeros)
        each_unused_tile(lambda cp: cp.start(priority=1))
        for j in range(ROW_SLOTS - 1):
            @pl.when(j < n_used)
            def _():
                tile_in(j).start()

    @pl.when(i + ROW_SLOTS - 1 < n_used)
    def _():
        tile_in(i + ROW_SLOTS - 1).start()

    @pl.when((i >= ROW_SLOTS) & (i - ROW_SLOTS < n_used))
    def _():
        tile_out(i - ROW_SLOTS).wait()

    @pl.when(i < n_used)
    def _():
        tile_in(i).wait()
        x = _unpack_rows(xbuf[i % ROW_SLOTS]).astype(BF16)
        hid = jax.nn.silu(_dot(x, wg_b[...])) * _dot(x, wu_b[...])
        y = _dot(hid.astype(BF16), wd_b[...])
        ybuf[i % ROW_SLOTS] = _pack_rows(y.astype(BF16).astype(F32))
        tile_out(i).start()

    @pl.when(i == n_tiles - 1)
    def _():
        for back in reversed(range(ROW_SLOTS)):
            @pl.when((i - back >= 0) & (i - back < n_used))
            def _():
                tile_out(i - back).wait()

        each_unused_tile(lambda cp: cp.wait())


def _experts(xs, tile_e, n_used, next_e, w_gate, w_up, w_down):
    r_glob, dh = xs.shape
    d, hdim = w_gate.shape[1:]
    n_tiles = r_glob // FFN_TILE

    hbm = pl.BlockSpec(memory_space=pl.ANY)
    grid_spec = pltpu.PrefetchScalarGridSpec(
        num_scalar_prefetch=3,
        grid=(n_tiles,),
        in_specs=[hbm, hbm, hbm, hbm],
        out_specs=hbm,
        scratch_shapes=[pltpu.VMEM((d, hdim), F32), pltpu.VMEM((d, hdim), F32),
                        pltpu.VMEM((hdim, d), F32),
                        pltpu.VMEM((d, hdim), BF16), pltpu.VMEM((d, hdim), BF16),
                        pltpu.VMEM((hdim, d), BF16),
                        pltpu.VMEM((ROW_SLOTS, FFN_TILE, dh), U32),
                        pltpu.VMEM((ROW_SLOTS, FFN_TILE, dh), U32),
                        pltpu.VMEM((FFN_TILE, dh), U32),
                        pltpu.SemaphoreType.DMA((3,)), pltpu.SemaphoreType.DMA((ROW_SLOTS,)),
                        pltpu.SemaphoreType.DMA((ROW_SLOTS,)), pltpu.SemaphoreType.DMA],
    )
    return pl.pallas_call(
        _experts_kernel,
        grid_spec=grid_spec,
        out_shape=jax.ShapeDtypeStruct((r_glob, dh), U32),
        compiler_params=pltpu.CompilerParams(dimension_semantics=("arbitrary",),
                                             vmem_limit_bytes=VMEM_LIMIT),
        name="experts",
    )(tile_e, n_used, next_e, xs, w_gate, w_up, w_down)


def _combine_kernel(loc_ref, glob_ref, rows_ref, used_ref,
                    base_ref, sel_ref, w_ref, locc_ref, locr_ref, cntr_ref, g_ref, bta_ref, ys_hbm,
                    o_ref, ys_loc, sems):
    b = pl.program_id(0)
    nb = pl.num_programs(0)
    e, tm = sel_ref.shape
    r_loc = ys_loc.shape[1]
    slot = b % 2
    buf = ys_loc.at[slot]

    always = tm * TOP_K

    def fetch_block(blk, s):
        ys_loc[s, always:, :] = jnp.zeros((r_loc - always, ys_loc.shape[2]), ys_loc.dtype)

        def per_expert(ex):
            n = rows_ref[blk * e + ex]

            @pl.when(n > 0)
            def _():
                _segment_copy(ys_hbm, glob_ref[blk * e + ex], ys_loc.at[s], loc_ref[blk * e + ex], n,
                              sems.at[s]).start()

        _for_each_expert(e, per_expert)

    @pl.when(b == 0)
    def _():
        fetch_block(b, slot)

    @pl.when(b + 1 < nb)
    def _():
        fetch_block(b + 1, 1 - slot)

    tables = _row_tables(sel_ref[...], locc_ref[0])
    wb = w_ref[...].astype(BF16)
    used = used_ref[b]

    _segment_copy(ys_hbm, 0, buf, 0, used, sems.at[slot]).wait()

    own = _row_owner(r_loc, locr_ref[0], cntr_ref[0])
    prow = _dot(jnp.concatenate([own, own], axis=1), tables)
    wrow = _dot(own, wb)
    y = _unpack_rows(buf[...]).astype(BF16)
    ri = lax.broadcasted_iota(I32, (r_loc, tm), 0).astype(F32)
    wmat = jnp.where(prow == ri, wrow, 0.0).astype(BF16)
    moe = _dot_tn(wmat, y)
    o_ref[...] = _ln(base_ref[...] + moe, g_ref[...], bta_ref[...])


def _combine(base, sel_t, w_t, lay, ys, ln_g, ln_b):
    t, d = base.shape
    e = sel_t.shape[0]
    tm = TM_MOE
    nb = t // tm
    r_loc = tm * TOP_K + e * (GRAN - 1)
    blk = pl.BlockSpec((e, tm), lambda i, *_: (0, i))
    row_spec = pl.BlockSpec((tm, d), lambda i, *_: (i, 0))
    vec = pl.BlockSpec((1, d), lambda i, *_: (0, 0))
    col = pl.BlockSpec((1, e, 1), lambda i, *_: (i, 0, 0))
    rowv = pl.BlockSpec((1, 1, e), lambda i, *_: (i, 0, 0))
    grid_spec = pltpu.PrefetchScalarGridSpec(
        num_scalar_prefetch=4,
        grid=(nb,),
        in_specs=[row_spec, blk, blk, col, rowv, rowv, vec, vec, pl.BlockSpec(memory_space=pl.ANY)],
        out_specs=row_spec,
        scratch_shapes=[pltpu.VMEM((2, r_loc, d // 2), U32), pltpu.SemaphoreType.DMA((2,))],
    )
    return pl.pallas_call(
        _combine_kernel,
        grid_spec=grid_spec,
        out_shape=jax.ShapeDtypeStruct((t, d), F32),
        compiler_params=pltpu.CompilerParams(dimension_semantics=("arbitrary",),
                                             vmem_limit_bytes=VMEM_LIMIT),
        name="combine",
    )(lay["loc_flat"], lay["glob_flat"], lay["rows_flat"], lay["used"],
      base, sel_t, w_t, lay["loc_col"], lay["loc_row"], lay["cnt_row"], ln_g, ln_b, ys)


def _round_up(x, m):
    return (x + m - 1) // m * m


def _moe_layout(cnt, t):
    nb, e = cnt.shape
    cnt_g = _round_up(cnt, GRAN)
    loc_off = jnp.cumsum(cnt_g, axis=1) - cnt_g
    used = jnp.sum(cnt_g, axis=1)
    gcnt = jnp.sum(cnt_g, axis=0)
    gpad = _round_up(gcnt, FFN_TILE)
    gend = jnp.cumsum(gpad)
    gstart = gend - gpad
    glob_off = gstart[None, :] + jnp.cumsum(cnt_g, axis=0) - cnt_g
    r_glob = _round_up(t * TOP_K + nb * e * (GRAN - 1) + e * (FFN_TILE - 1), FFN_TILE)
    n_tiles = r_glob // FFN_TILE
    n_used = (gend[-1] // FFN_TILE).astype(I32)
    tile_start = jnp.minimum(jnp.arange(n_tiles, dtype=I32), n_used - 1) * FFN_TILE
    tile_e = jnp.minimum(jnp.sum(gend[None, :] <= tile_start[:, None], axis=1), e - 1).astype(I32)
    ids = jnp.arange(e, dtype=I32)
    later_owner = jnp.min(jnp.where((ids[None, :] > ids[:, None]) & (gpad[None, :] > 0), ids[None, :], e),
                          axis=1)
    later_owner = jnp.where(later_owner < e, later_owner, -1)
    next_e = jnp.sum(jnp.where(tile_e[:, None] == ids[None, :], later_owner[None, :], 0), axis=1).astype(I32)
    return dict(next_e=next_e, loc_flat=loc_off.astype(I32).reshape(-1), glob_flat=glob_off.astype(I32).reshape(-1),
                rows_flat=cnt_g.astype(I32).reshape(-1), used=used.astype(I32),
                loc_col=loc_off.astype(F32)[:, :, None], loc_row=loc_off.astype(F32)[:, None, :],
                cnt_row=cnt_g.astype(F32)[:, None, :],
                tail_lo=jnp.append(gstart + gcnt, gend[-1]).astype(I32),
                tail_hi=jnp.append(gend, r_glob).astype(I32),
                tile_e=tile_e, n_used=n_used.reshape(1), r_glob=r_glob)


def kernel(x, mem, ln_in_g, ln_in_b, w_in, b_in, ln_v_g, ln_v_b, w_spatial, b_spatial, w_out,
           ln1_g, ln1_b, w_mem_q, w_mem_kv, w_mem_o, ln2_g, ln2_b, w_router, router_bias,
           w_exp_gate, w_exp_up, w_exp_down, w_sh_gate, w_sh_up, w_sh_down, ln3_g, ln3_b):
    bsz, seq, d = x.shape
    t = bsz * seq
    assert w_in.shape[0] == DEPTH
    assert seq % TQ == 0 and seq % TM_MID == 0 and seq % CHUNK == 0, seq
    assert t % TM_PROJ == 0 and t % TM_ROUTE == 0 and TM_ROUTE % TM_MOE == 0, t
    assert w_router.shape[2] == N_EXPERTS and N_EXPERTS % EXPERT_UNROLL == 0
    assert d % (2 * LANES) == 0 and d % A_GROUPS == 0, d
    x2d = x.reshape(t, d)
    row = lambda a: a.reshape(1, -1)

    oa, q, k, v, gb, xln = _in_proj(x2d, row(ln_in_g), row(ln_in_b), w_in[0], row(b_in[0]),
                                    row(ln_v_g[0]), row(ln_v_b[0]), w_spatial[0], b_spatial[0].T)
    yb = _sb_attention(q.reshape(bsz, seq, d), k.reshape(bsz, seq, d), v.reshape(bsz, seq, d))

    wr_t = w_router[0].T
    wr_hi = wr_t.astype(BF16)
    wr_lo = (wr_t - wr_hi.astype(F32)).astype(BF16)
    base, x2b, logits_t = _mid(
        xln, oa, yb.reshape(t, d), gb, w_out[0].astype(BF16),
        row(ln1_g[0]), row(ln1_b[0]), w_mem_q[0].astype(BF16), mem, w_mem_kv[0].astype(BF16),
        w_mem_o[0].astype(BF16), row(ln2_g[0]), row(ln2_b[0]), wr_hi, wr_lo,
        w_sh_gate[0].astype(BF16), w_sh_up[0].astype(BF16), w_sh_down[0].astype(BF16), seq)

    sel_t, w_t, cnt = _route(logits_t, router_bias[0].reshape(-1, 1))
    lay = _moe_layout(cnt[:, :, 0].astype(I32), t)
    xs = _dispatch(x2b, sel_t, lay)
    ys = _experts(xs, lay["tile_e"], lay["n_used"], lay["next_e"], w_exp_gate[0], w_exp_up[0],
                  w_exp_down[0])
    out = _combine(base, sel_t, w_t, lay, ys, row(ln3_g[0]), row(ln3_b[0]))
    return out.reshape(bsz, seq, d)
```

```python
import functools

import jax
import jax.numpy as jnp
from jax import lax
from jax.experimental import pallas as pl
from jax.experimental.pallas import tpu as pltpu

F32 = jnp.float32
BF16 = jnp.bfloat16
I32 = jnp.int32
U32 = jnp.uint32

LANES = 128
SUBLANES = 8
GRAN = SUBLANES

CHUNK = 128
A_GROUPS = 8
SB_HEAD_DIM = 128
MEM_HEADS = 4
MEM_HEAD_DIM = 128
N_EXPERTS = 64
TOP_K = 8
N_GROUPS = 8
TOPK_GROUPS = 4
ROUTED_SCALE = 2.5
LN_EPS = 1e-5
DEPTH = 1
ALPHA = (2 * DEPTH) ** 0.25
LOG2E = 1.4426950408889634
SP_CLAMP = 64.0

TM_PROJ = 512
TM_MID = 1024
TQ = 1024
SLAB = 256
TM_MOE = 256
TM_ROUTE = 1024
FFN_TILE = 512
EXPERT_UNROLL = 64
WEIGHT_STAGE_SLOTS = 3
ROW_SLOTS = 3
ROW_SPLIT = 256.0
VMEM_LIMIT = 56 * 1024 * 1024


def _ln(x, g, b):
    mu = jnp.mean(x, axis=-1, keepdims=True)
    xc = x - mu
    var = jnp.mean(xc * xc, axis=-1, keepdims=True)
    return xc * lax.rsqrt(var + LN_EPS) * g + b


def _gelu(x):
    return 0.5 * x * (1.0 + lax.erf(x * (2.0 ** -0.5)))


def _dot(a, b):
    return jnp.dot(a, b, preferred_element_type=F32)


def _dot_nt(a, b):
    return lax.dot_general(a, b, (((1,), (1,)), ((), ())), preferred_element_type=F32)


def _dot_tn(a, b):
    return lax.dot_general(a, b, (((0,), (0,)), ((), ())), preferred_element_type=F32)


def _full(shape):
    n = len(shape)
    return pl.BlockSpec(shape, lambda *_: (0,) * n, pipeline_mode=pl.Buffered(1))


def _in_proj_kernel(x_ref, lg_ref, lb_ref, w_hbm, b_ref, vg_ref, vb_ref, ws_ref, bst_ref,
                    oa_ref, q_ref, k_ref, v_ref, gb_ref, xln_ref, w_ref, stage, sems):
    tm, d = x_ref.shape

    @pl.when(pl.program_id(0) == 0)
    def _():
        n_seg = w_ref.shape[1] // d

        slots = stage.shape[0]

        def seg_copy(s):
            return pltpu.make_async_copy(w_hbm.at[:, pl.ds(s * d, d)], stage.at[s % slots],
                                         sems.at[s % slots])

        for s in range(min(slots - 1, n_seg)):
            seg_copy(s).start()
        for s in range(n_seg):
            if s + slots - 1 < n_seg:
                seg_copy(s + slots - 1).start()
            seg_copy(s).wait()
            w_ref[:, s * d:(s + 1) * d] = stage[s % slots].astype(BF16)

    xln = _ln(x_ref[...], lg_ref[...], lb_ref[...])
    xln_ref[...] = xln
    xb = xln.astype(BF16)

    def seg(i):
        return _dot(xb, w_ref[:, i * d:(i + 1) * d]) + b_ref[:, i * d:(i + 1) * d]

    pre_v, pre_gate, pre_u = seg(1), seg(5), seg(0)
    q_ref[...] = (seg(2) * (SB_HEAD_DIM ** -0.5 * LOG2E)).astype(BF16)
    vln = _ln(_gelu(pre_v), vg_ref[...], vb_ref[...]).astype(BF16)
    k_ref[...] = seg(3).astype(BF16)
    gu = jax.nn.sigmoid(pre_gate) * _gelu(pre_u)
    v_ref[...] = seg(4).astype(BF16)
    gb_ref[...] = jax.nn.sigmoid(seg(6)).astype(BF16)

    gd = d // A_GROUPS
    row = lax.broadcasted_iota(I32, (CHUNK, CHUNK), 0)
    col = lax.broadcasted_iota(I32, (CHUNK, CHUNK), 1)
    for g in range(A_GROUPS):
        w = jnp.where(col <= row, ws_ref[g], 0.0).astype(BF16)
        bias = bst_ref[:, g:g + 1]
        for c in range(tm // CHUNK):
            rs = slice(c * CHUNK, (c + 1) * CHUNK)
            cs = slice(g * gd, (g + 1) * gd)
            mixed = _dot(w, vln[rs, cs]) + bias
            oa_ref[rs, cs] = (gu[rs, cs] * mixed).astype(BF16)


def _in_proj(x2d, ln_g, ln_b, w_in, b_in, vg, vb, w_s, b_st):
    t, d = x2d.shape
    n_in = w_in.shape[1]
    tm = TM_PROJ
    row_spec = pl.BlockSpec((tm, d), lambda i: (i, 0))
    out = jax.ShapeDtypeStruct((t, d), BF16)
    return pl.pallas_call(
        _in_proj_kernel,
        grid=(t // tm,),
        in_specs=[row_spec, _full((1, d)), _full((1, d)), pl.BlockSpec(memory_space=pl.ANY),
                  _full((1, n_in)), _full((1, d)), _full((1, d)), _full(w_s.shape), _full(b_st.shape)],
        out_specs=[row_spec] * 6,
        out_shape=[out] * 5 + [jax.ShapeDtypeStruct((t, d), F32)],
        scratch_shapes=[pltpu.VMEM((d, n_in), BF16), pltpu.VMEM((WEIGHT_STAGE_SLOTS, d, d), F32),
                        pltpu.SemaphoreType.DMA((WEIGHT_STAGE_SLOTS,))],
        compiler_params=pltpu.CompilerParams(dimension_semantics=("arbitrary",),
                                             vmem_limit_bytes=VMEM_LIMIT),
        name="in_proj",
    )(x2d, ln_g, ln_b, w_in, b_in, vg, vb, w_s, b_st)


def _sb_kernel(q_ref, k_ref, v_ref, m_ref, o_ref):
    i = pl.program_id(2)
    tq = q_ref.shape[1]
    nslab = tq // SLAB
    q = q_ref[0]
    later = m_ref[...]

    tri_r = lax.broadcasted_iota(I32, (SLAB, SLAB), 0)
    tri_c = lax.broadcasted_iota(I32, (SLAB, SLAB), 1)

    def causal(x, diagonal):
        if not diagonal:
            return x
        top = jnp.where(tri_c < tri_r, x[:SLAB], 0.0)
        return top if x.shape[0] == SLAB else jnp.concatenate([top, x[SLAB:]], axis=0)

    def logits_stage(qs, j, diagonal):
        off = pl.multiple_of(j * SLAB, SLAB)
        z = _dot_nt(qs, k_ref[0, pl.ds(off, SLAB), :])
        sp = causal(jnp.maximum(jnp.log(1.0 + jnp.exp2(jnp.minimum(z, SP_CLAMP))) * LOG2E, z), diagonal)
        return z - sp, sp.astype(BF16), sp[:, 0:1]

    def weights_stage(staged, j, carry, diagonal):
        log2_beta, sp_b, first = staged
        off = pl.multiple_of(j * SLAB, SLAB)
        cs = _dot(sp_b, later)
        a = causal(jnp.exp2(log2_beta - cs - carry), diagonal)
        total = cs[:, 0:1] + first
        return carry + total, _dot(a.astype(BF16), v_ref[0, pl.ds(off, SLAB), :])

    def slab(qs, j, carry, diagonal):
        return weights_stage(logits_stage(qs, j, diagonal), j, carry, diagonal)

    carry = jnp.zeros((tq, 1), F32)
    acc = jnp.zeros((tq, SB_HEAD_DIM), F32)
    for d in reversed(range(nslab)):
        r0 = d * SLAB
        c_new, contrib = slab(q[r0:], i * nslab + d, carry[r0:], True)
        a_new = acc[r0:] + contrib
        carry = jnp.concatenate([carry[:r0], c_new], axis=0) if r0 else c_new
        acc = jnp.concatenate([acc[:r0], a_new], axis=0) if r0 else a_new

    def body(n, ca):
        carry, acc = ca
        js = [(i - n) * nslab - 1 - u for u in range(nslab)]
        staged = logits_stage(q, js[0], False)
        for u in range(nslab):
            ahead = logits_stage(q, js[u + 1], False) if u + 1 < nslab else None
            carry, contrib = weights_stage(staged, js[u], carry, False)
            acc = acc + contrib
            staged = ahead
        return carry, acc

    carry, acc = lax.fori_loop(0, i, body, (carry, acc))
    o_ref[0] = acc.astype(o_ref.dtype)


def _sb_attention(q, k, v):
    b, s, w = q.shape
    h = w // SB_HEAD_DIM
    jj = lax.broadcasted_iota(I32, (SLAB, SLAB), 0)
    ss = lax.broadcasted_iota(I32, (SLAB, SLAB), 1)
    mcat = jnp.where(jj > ss, 1.0, 0.0).astype(BF16)
    q_spec = pl.BlockSpec((1, TQ, SB_HEAD_DIM), lambda bi, hi, i: (bi, i, hi))
    kv_spec = pl.BlockSpec((1, s, SB_HEAD_DIM), lambda bi, hi, i: (bi, 0, hi))
    return pl.pallas_call(
        _sb_kernel,
        grid=(b, h, s // TQ),
        in_specs=[q_spec, kv_spec, kv_spec, _full(mcat.shape)],
        out_specs=q_spec,
        out_shape=jax.ShapeDtypeStruct((b, s, w), BF16),
        compiler_params=pltpu.CompilerParams(
            dimension_semantics=("parallel", "parallel", "arbitrary"),
            vmem_limit_bytes=VMEM_LIMIT),
        name="sb_attn",
    )(q, k, v, mcat)


def _mid_kernel(xln_ref, oa_ref, yb_ref, gb_ref, wo_ref, l1g_ref, l1b_ref,
                wq_ref, mem_ref, wkv_ref, wmo_ref, l2g_ref, l2b_ref, wrh_ref, wrl_ref,
                wsg_ref, wsu_ref, wsd_ref,
                base_ref, x2_ref, lgt_ref, km_ref, vm_ref, *, steps_per_batch):
    @pl.when(pl.program_id(0) % steps_per_batch == 0)
    def _():
        kv = _dot(mem_ref[0].astype(BF16), wkv_ref[...])
        half = kv.shape[1] // 2
        km_ref[...] = kv[:, :half].astype(BF16)
        vm_ref[...] = kv[:, half:].astype(BF16)

    merged = oa_ref[...].astype(F32) + gb_ref[...].astype(F32) * yb_ref[...].astype(F32)
    mixed = _dot(merged.astype(BF16), wo_ref[...])
    x1 = _ln(ALPHA * xln_ref[...] + mixed, l1g_ref[...], l1b_ref[...])

    q = (_dot(x1.astype(BF16), wq_ref[...]) * (MEM_HEAD_DIM ** -0.5)).astype(BF16)
    head_cols = [slice(h * MEM_HEAD_DIM, (h + 1) * MEM_HEAD_DIM) for h in range(MEM_HEADS)]
    logits = [_dot_nt(q[:, hs], km_ref[:, hs]) for hs in head_cols]
    expd = [jnp.exp(lg - jnp.max(lg, axis=-1, keepdims=True)) for lg in logits]
    inv = [1.0 / jnp.sum(p, axis=-1, keepdims=True) for p in expd]
    heads = [_dot(p.astype(BF16), vm_ref[:, hs]) * r for p, r, hs in zip(expd, inv, head_cols)]
    o = jnp.concatenate(heads, axis=1).astype(BF16)
    x2 = _ln(ALPHA * x1 + _dot(o, wmo_ref[...]), l2g_ref[...], l2b_ref[...])

    x2h = x2.astype(BF16)
    x2_ref[...] = x2h
    gate = _dot(x2h, wsg_ref[...])
    up = _dot(x2h, wsu_ref[...])
    x2l = (x2 - x2h.astype(F32)).astype(BF16)
    lgt_ref[...] = (_dot_nt(wrh_ref[...], x2h) + _dot_nt(wrh_ref[...], x2l)
                    + _dot_nt(wrl_ref[...], x2h))
    hid = jax.nn.silu(gate) * up
    base_ref[...] = ALPHA * x2 + _dot(hid.astype(BF16), wsd_ref[...])


def _mid(xln, oa, yb, gb, w_out, l1g, l1b, w_q, mem, w_kv, w_mo, l2g, l2b,
         wr_hi, wr_lo, w_sg, w_su, w_sd, seq):
    t, d = xln.shape
    tm = TM_MID
    per_batch = seq // tm
    m_len, half = mem.shape[1], w_kv.shape[1] // 2
    row_spec = pl.BlockSpec((tm, d), lambda i: (i, 0))
    mem_spec = pl.BlockSpec((1,) + mem.shape[1:], lambda i: (i // per_batch, 0, 0))
    vec = _full((1, d))
    return pl.pallas_call(
        functools.partial(_mid_kernel, steps_per_batch=per_batch),
        grid=(t // tm,),
        in_specs=[row_spec, row_spec, row_spec, row_spec, _full(w_out.shape), vec, vec,
                  _full(w_q.shape), mem_spec, _full(w_kv.shape), _full(w_mo.shape), vec, vec,
                  _full(wr_hi.shape), _full(wr_lo.shape),
                  _full(w_sg.shape), _full(w_su.shape), _full(w_sd.shape)],
        out_specs=[row_spec, row_spec, pl.BlockSpec((N_EXPERTS, tm), lambda i: (0, i))],
        out_shape=[jax.ShapeDtypeStruct((t, d), F32), jax.ShapeDtypeStruct((t, d), BF16),
                   jax.ShapeDtypeStruct((N_EXPERTS, t), F32)],
        scratch_shapes=[pltpu.VMEM((m_len, half), BF16), pltpu.VMEM((m_len, half), BF16)],
        compiler_params=pltpu.CompilerParams(dimension_semantics=("arbitrary",),
                                             vmem_limit_bytes=VMEM_LIMIT),
        name="mid",
    )(xln, oa, yb, gb, w_out, l1g, l1b, w_q, mem, w_kv, w_mo, l2g, l2b,
      wr_hi, wr_lo, w_sg, w_su, w_sd)


def _route_kernel(lgt_ref, bias_ref, sel_ref, w_ref, cnt_ref):
    e, tm = lgt_ref.shape
    per_group = e // N_GROUPS
    scores = jax.nn.sigmoid(lgt_ref[...])
    sel = scores + bias_ref[...]

    g3 = sel.reshape(N_GROUPS, per_group, tm)
    j3 = lax.broadcasted_iota(I32, g3.shape, 1)
    m1 = jnp.max(g3, axis=1, keepdims=True)
    first = jnp.min(jnp.where(g3 == m1, j3, per_group), axis=1, keepdims=True)
    m2 = jnp.max(jnp.where(j3 == first, -jnp.inf, g3), axis=1, keepdims=True)
    gs = (m1 + m2).reshape(N_GROUPS, tm)

    gi = lax.broadcasted_iota(I32, (N_GROUPS, tm), 0)
    grank = jnp.zeros((N_GROUPS, tm), I32)
    for o in range(N_GROUPS):
        other = gs[o:o + 1, :]
        grank += ((other > gs) | ((other == gs) & (o < gi))).astype(I32)
    gmask = (grank < TOPK_GROUPS).astype(F32)
    emask = jnp.broadcast_to(gmask.reshape(N_GROUPS, 1, tm), (N_GROUPS, per_group, tm)).reshape(e, tm)
    cand = jnp.where(emask > 0.5, sel, -jnp.inf)

    ei = lax.broadcasted_iota(I32, (e, tm), 0)
    chosen = jnp.zeros((e, tm), jnp.bool_)
    left = cand
    for _ in range(TOP_K):
        best = jnp.max(left, axis=0, keepdims=True)
        first = jnp.min(jnp.where(left == best, ei, e), axis=0, keepdims=True)
        pick = ei == first
        chosen = chosen | pick
        left = jnp.where(pick, -jnp.inf, left)

    w = jnp.where(chosen, scores, 0.0)
    w = w / jnp.sum(w, axis=0, keepdims=True) * ROUTED_SCALE
    chosen_f = chosen.astype(F32)
    sel_ref[...] = chosen_f
    w_ref[...] = w
    for s in range(tm // TM_MOE):
        part = chosen_f[:, s * TM_MOE:(s + 1) * TM_MOE]
        cnt_ref[s] = jnp.sum(part, axis=1, keepdims=True)


def _route(logits_t, bias_col):
    e, t = logits_t.shape
    tm = TM_ROUTE
    sub = tm // TM_MOE
    blk = pl.BlockSpec((e, tm), lambda i: (0, i))
    return pl.pallas_call(
        _route_kernel,
        grid=(t // tm,),
        in_specs=[blk, _full((e, 1))],
        out_specs=[blk, blk, pl.BlockSpec((sub, e, 1), lambda i: (i, 0, 0))],
        out_shape=[jax.ShapeDtypeStruct((e, t), F32), jax.ShapeDtypeStruct((e, t), F32),
                   jax.ShapeDtypeStruct((t // TM_MOE, e, 1), F32)],
        compiler_params=pltpu.CompilerParams(dimension_semantics=("parallel",),
                                             vmem_limit_bytes=VMEM_LIMIT),
        name="route",
    )(logits_t, bias_col)


def _pack_rows(x):
    words = []
    for g in range(x.shape[1] // (2 * LANES)):
        hi = lax.bitcast_convert_type(x[:, 2 * g * LANES:(2 * g + 1) * LANES], U32)
        lo = lax.bitcast_convert_type(x[:, (2 * g + 1) * LANES:(2 * g + 2) * LANES], U32)
        words.append((hi & jnp.uint32(0xFFFF0000)) | lax.shift_right_logical(lo, jnp.uint32(16)))
    return jnp.concatenate(words, axis=1)


def _unpack_rows(u):
    tiles = []
    for g in range(u.shape[1] // LANES):
        w = u[:, g * LANES:(g + 1) * LANES]
        tiles.append(lax.bitcast_convert_type(w & jnp.uint32(0xFFFF0000), F32))
        tiles.append(lax.bitcast_convert_type(lax.shift_left(w, jnp.uint32(16)), F32))
    return jnp.concatenate(tiles, axis=1)


def _row_tables(sel, loc_col):
    e, tm = sel.shape
    tj = lax.broadcasted_iota(I32, (tm, tm), 0)
    tt = lax.broadcasted_iota(I32, (tm, tm), 1)
    before_t = jnp.where(tj < tt, 1.0, 0.0).astype(BF16)
    rank = _dot(sel.astype(BF16), before_t)
    pos = jnp.where(sel > 0.5, loc_col + rank, -1.0)
    hi = jnp.floor(pos * (1.0 / ROW_SPLIT)) * ROW_SPLIT
    return jnp.concatenate([hi, pos - hi], axis=0).astype(BF16)


def _row_owner(rows, loc_row, cnt_row):
    e = loc_row.shape[1]
    ri = lax.broadcasted_iota(I32, (rows, e), 0).astype(F32)
    return jnp.where((ri >= loc_row) & (ri < loc_row + cnt_row), 1.0, 0.0).astype(BF16)


def _for_each_expert(e, fn):
    def trip(i, _):
        for u in range(EXPERT_UNROLL):
            fn(i * EXPERT_UNROLL + u)
        return 0

    lax.fori_loop(0, e // EXPERT_UNROLL, trip, 0)


def _segment_copy(src, src_row, dst, dst_row, rows, sem):
    rows = pl.multiple_of(rows, GRAN)
    return pltpu.make_async_copy(src.at[pl.ds(pl.multiple_of(src_row, GRAN), rows)],
                                 dst.at[pl.ds(pl.multiple_of(dst_row, GRAN), rows)], sem)


def _dispatch_kernel(loc_ref, glob_ref, rows_ref, used_ref, tail_lo_ref, tail_hi_ref,
                     x_ref, sel_ref, locc_ref, locr_ref, cntr_ref, xs_hbm, xs_loc, zeros, sems):
    b = pl.program_id(0)
    nb = pl.num_programs(0)
    e, tm = sel_ref.shape
    r_loc = xs_loc.shape[1]
    slot = b % 2
    buf = xs_loc.at[slot]
    sem = sems.at[slot]

    def wait_block(blk, s):
        @pl.when(used_ref[blk] > 0)
        def _():
            _segment_copy(xs_loc.at[s], 0, xs_hbm, 0, used_ref[blk], sems.at[s]).wait()

    zrows = zeros.shape[0]

    def each_zero_piece(act):
        def per_tail(ex, _):
            lo = tail_lo_ref[ex]
            n = tail_hi_ref[ex] - lo

            def per_piece(j, _):
                rows = jnp.minimum(n - j * zrows, zrows)
                act(_segment_copy(zeros, 0, xs_hbm, lo + j * zrows, rows, sems.at[2]))
                return 0

            lax.fori_loop(0, (n + zrows - 1) // zrows, per_piece, 0)
            return 0

        lax.fori_loop(0, tail_lo_ref.shape[0], per_tail, 0)

    @pl.when(b == 0)
    def _():
        zeros[...] = jnp.zeros_like(zeros)
        each_zero_piece(lambda cp: cp.start(priority=1))

    @pl.when(b >= 2)
    def _():
        wait_block(b - 2, slot)

    tables = _row_tables(sel_ref[...], locc_ref[0])
    x = x_ref[...]
    own = _row_owner(r_loc, locr_ref[0], cntr_ref[0])
    prow = _dot(jnp.concatenate([own, own], axis=1), tables)
    ri = lax.broadcasted_iota(I32, (r_loc, tm), 0).astype(F32)
    onehot = jnp.where(prow == ri, 1.0, 0.0).astype(BF16)
    buf[...] = _pack_rows(_dot(onehot, x))

    def per_expert(ex):
        n = rows_ref[b * e + ex]

        @pl.when(n > 0)
        def _():
            _segment_copy(buf, loc_ref[b * e + ex], xs_hbm, glob_ref[b * e + ex], n, sem).start()

    _for_each_expert(e, per_expert)

    @pl.when(b == nb - 1)
    def _():
        @pl.when(b >= 1)
        def _():
            wait_block(b - 1, 1 - slot)

        wait_block(b, slot)
        each_zero_piece(lambda cp: cp.wait())


def _dispatch(x2b, sel_t, lay):
    t, d = x2b.shape
    e = sel_t.shape[0]
    tm = TM_MOE
    nb = t // tm
    r_loc = tm * TOP_K + e * (GRAN - 1)
    col = pl.BlockSpec((1, e, 1), lambda i, *_: (i, 0, 0))
    rowv = pl.BlockSpec((1, 1, e), lambda i, *_: (i, 0, 0))
    grid_spec = pltpu.PrefetchScalarGridSpec(
        num_scalar_prefetch=6,
        grid=(nb,),
        in_specs=[pl.BlockSpec((tm, d), lambda i, *_: (i, 0)),
                  pl.BlockSpec((e, tm), lambda i, *_: (0, i)), col, rowv, rowv],
        out_specs=pl.BlockSpec(memory_space=pl.ANY),
        scratch_shapes=[pltpu.VMEM((2, r_loc, d // 2), U32), pltpu.VMEM((FFN_TILE, d // 2), U32),
                        pltpu.SemaphoreType.DMA((3,))],
    )
    return pl.pallas_call(
        _dispatch_kernel,
        grid_spec=grid_spec,
        out_shape=jax.ShapeDtypeStruct((lay["r_glob"], d // 2), U32),
        compiler_params=pltpu.CompilerParams(dimension_semantics=("arbitrary",),
                                             vmem_limit_bytes=VMEM_LIMIT),
        name="dispatch",
    )(lay["loc_flat"], lay["glob_flat"], lay["rows_flat"], lay["used"], lay["tail_lo"],
      lay["tail_hi"], x2b, sel_t, lay["loc_col"], lay["loc_row"], lay["cnt_row"])


def _experts_kernel(tile_e_ref, n_used_ref, next_e_ref, xs_hbm, wg_hbm, wu_hbm, wd_hbm, ys_hbm,
                    wg_f, wu_f, wd_f, wg_b, wu_b, wd_b, xbuf, ybuf, zeros, sems, in_sems, out_sems,
                    zero_sem):
    i = pl.program_id(0)

    def weight_copies(ex):
        return (pltpu.make_async_copy(wg_hbm.at[ex], wg_f, sems.at[0]),
                pltpu.make_async_copy(wu_hbm.at[ex], wu_f, sems.at[1]),
                pltpu.make_async_copy(wd_hbm.at[ex], wd_f, sems.at[2]))

    def start_weights(ex):
        for cp in weight_copies(ex):
            cp.start(priority=1)

    @pl.when(i == 0)
    def _():
        start_weights(tile_e_ref[0])

    prev = tile_e_ref[jnp.maximum(i - 1, 0)]

    @pl.when((i == 0) | (tile_e_ref[i] != prev))
    def _():
        for cp in weight_copies(tile_e_ref[i]):
            cp.wait()
        wg_b[...] = wg_f[...].astype(BF16)
        wu_b[...] = wu_f[...].astype(BF16)
        wd_b[...] = wd_f[...].astype(BF16)
        nxt = next_e_ref[i]

        @pl.when(nxt >= 0)
        def _():
            start_weights(nxt)

    n_tiles = pl.num_programs(0)
    n_used = n_used_ref[0]
    tile_rows = xbuf.shape[1]

    def tile_in(j):
        return pltpu.make_async_copy(xs_hbm.at[pl.ds(pl.multiple_of(j * tile_rows, tile_rows), tile_rows)],
                                     xbuf.at[j % ROW_SLOTS], in_sems.at[j % ROW_SLOTS])

    def tile_out(j):
        return pltpu.make_async_copy(ybuf.at[j % ROW_SLOTS],
                                     ys_hbm.at[pl.ds(pl.multiple_of(j * tile_rows, tile_rows), tile_rows)],
                                     out_sems.at[j % ROW_SLOTS])

    def each_unused_tile(act):
        def body(j, _):
            act(pltpu.make_async_copy(
                zeros, ys_hbm.at[pl.ds(pl.multiple_of(j * tile_rows, tile_rows), tile_rows)], zero_sem))
            return 0

        lax.fori_loop(n_used, n_tiles, body, 0)

    @pl.when(i == 0)
    def _():
        zeros[...] = jnp.zeros_like(zeros)
        each_unused_tile(lambda cp: cp.start(priority=1))
        for j in range(ROW_SLOTS - 1):
            @pl.when(j < n_used)
            def _():
                tile_in(j).start()

    @pl.when(i + ROW_SLOTS - 1 < n_used)
    def _():
        tile_in(i + ROW_SLOTS - 1).start()

    @pl.when((i >= ROW_SLOTS) & (i - ROW_SLOTS < n_used))
    def _():
        tile_out(i - ROW_SLOTS).wait()

    @pl.when(i < n_used)
    def _():
        tile_in(i).wait()
        x = _unpack_rows(xbuf[i % ROW_SLOTS]).astype(BF16)
        hid = jax.nn.silu(_dot(x, wg_b[...])) * _dot(x, wu_b[...])
        y = _dot(hid.astype(BF16), wd_b[...])
        ybuf[i % ROW_SLOTS] = _pack_rows(y.astype(BF16).astype(F32))
        tile_out(i).start()

    @pl.when(i == n_tiles - 1)
    def _():
        for back in reversed(range(ROW_SLOTS)):
            @pl.when((i - back >= 0) & (i - back < n_used))
            def _():
                tile_out(i - back).wait()

        each_unused_tile(lambda cp: cp.wait())


def _experts(xs, tile_e, n_used, next_e, w_gate, w_up, w_down):
    r_glob, dh = xs.shape
    d, hdim = w_gate.shape[1:]
    n_tiles = r_glob // FFN_TILE

    hbm = pl.BlockSpec(memory_space=pl.ANY)
    grid_spec = pltpu.PrefetchScalarGridSpec(
        num_scalar_prefetch=3,
        grid=(n_tiles,),
        in_specs=[hbm, hbm, hbm, hbm],
        out_specs=hbm,
        scratch_shapes=[pltpu.VMEM((d, hdim), F32), pltpu.VMEM((d, hdim), F32),
                        pltpu.VMEM((hdim, d), F32),
                        pltpu.VMEM((d, hdim), BF16), pltpu.VMEM((d, hdim), BF16),
                        pltpu.VMEM((hdim, d), BF16),
                        pltpu.VMEM((ROW_SLOTS, FFN_TILE, dh), U32),
                        pltpu.VMEM((ROW_SLOTS, FFN_TILE, dh), U32),
                        pltpu.VMEM((FFN_TILE, dh), U32),
                        pltpu.SemaphoreType.DMA((3,)), pltpu.SemaphoreType.DMA((ROW_SLOTS,)),
                        pltpu.SemaphoreType.DMA((ROW_SLOTS,)), pltpu.SemaphoreType.DMA],
    )
    return pl.pallas_call(
        _experts_kernel,
        grid_spec=grid_spec,
        out_shape=jax.ShapeDtypeStruct((r_glob, dh), U32),
        compiler_params=pltpu.CompilerParams(dimension_semantics=("arbitrary",),
                                             vmem_limit_bytes=VMEM_LIMIT),
        name="experts",
    )(tile_e, n_used, next_e, xs, w_gate, w_up, w_down)


def _combine_kernel(loc_ref, glob_ref, rows_ref, used_ref,
                    base_ref, sel_ref, w_ref, locc_ref, locr_ref, cntr_ref, g_ref, bta_ref, ys_hbm,
                    o_ref, ys_loc, sems):
    b = pl.program_id(0)
    nb = pl.num_programs(0)
    e, tm = sel_ref.shape
    r_loc = ys_loc.shape[1]
    slot = b % 2
    buf = ys_loc.at[slot]

    always = tm * TOP_K

    def fetch_block(blk, s):
        ys_loc[s, always:, :] = jnp.zeros((r_loc - always, ys_loc.shape[2]), ys_loc.dtype)

        def per_expert(ex):
            n = rows_ref[blk * e + ex]

            @pl.when(n > 0)
            def _():
                _segment_copy(ys_hbm, glob_ref[blk * e + ex], ys_loc.at[s], loc_ref[blk * e + ex], n,
                              sems.at[s]).start()

        _for_each_expert(e, per_expert)

    @pl.when(b == 0)
    def _():
        fetch_block(b, slot)

    @pl.when(b + 1 < nb)
    def _():
        fetch_block(b + 1, 1 - slot)

    tables = _row_tables(sel_ref[...], locc_ref[0])
    wb = w_ref[...].astype(BF16)
    used = used_ref[b]

    _segment_copy(ys_hbm, 0, buf, 0, used, sems.at[slot]).wait()

    own = _row_owner(r_loc, locr_ref[0], cntr_ref[0])
    prow = _dot(jnp.concatenate([own, own], axis=1), tables)
    wrow = _dot(own, wb)
    y = _unpack_rows(buf[...]).astype(BF16)
    ri = lax.broadcasted_iota(I32, (r_loc, tm), 0).astype(F32)
    wmat = jnp.where(prow == ri, wrow, 0.0).astype(BF16)
    moe = _dot_tn(wmat, y)
    o_ref[...] = _ln(base_ref[...] + moe, g_ref[...], bta_ref[...])


def _combine(base, sel_t, w_t, lay, ys, ln_g, ln_b):
    t, d = base.shape
    e = sel_t.shape[0]
    tm = TM_MOE
    nb = t // tm
    r_loc = tm * TOP_K + e * (GRAN - 1)
    blk = pl.BlockSpec((e, tm), lambda i, *_: (0, i))
    row_spec = pl.BlockSpec((tm, d), lambda i, *_: (i, 0))
    vec = pl.BlockSpec((1, d), lambda i, *_: (0, 0))
    col = pl.BlockSpec((1, e, 1), lambda i, *_: (i, 0, 0))
    rowv = pl.BlockSpec((1, 1, e), lambda i, *_: (i, 0, 0))
    grid_spec = pltpu.PrefetchScalarGridSpec(
        num_scalar_prefetch=4,
        grid=(nb,),
        in_specs=[row_spec, blk, blk, col, rowv, rowv, vec, vec, pl.BlockSpec(memory_space=pl.ANY)],
        out_specs=row_spec,
        scratch_shapes=[pltpu.VMEM((2, r_loc, d // 2), U32), pltpu.SemaphoreType.DMA((2,))],
    )
    return pl.pallas_call(
        _combine_kernel,
        grid_spec=grid_spec,
        out_shape=jax.ShapeDtypeStruct((t, d), F32),
        compiler_params=pltpu.CompilerParams(dimension_semantics=("arbitrary",),
                                             vmem_limit_bytes=VMEM_LIMIT),
        name="combine",
    )(lay["loc_flat"], lay["glob_flat"], lay["rows_flat"], lay["used"],
      base, sel_t, w_t, lay["loc_col"], lay["loc_row"], lay["cnt_row"], ln_g, ln_b, ys)


def _round_up(x, m):
    return (x + m - 1) // m * m


def _moe_layout(cnt, t):
    nb, e = cnt.shape
    cnt_g = _round_up(cnt, GRAN)
    loc_off = jnp.cumsum(cnt_g, axis=1) - cnt_g
    used = jnp.sum(cnt_g, axis=1)
    gcnt = jnp.sum(cnt_g, axis=0)
    gpad = _round_up(gcnt, FFN_TILE)
    gend = jnp.cumsum(gpad)
    gstart = gend - gpad
    glob_off = gstart[None, :] + jnp.cumsum(cnt_g, axis=0) - cnt_g
    r_glob = _round_up(t * TOP_K + nb * e * (GRAN - 1) + e * (FFN_TILE - 1), FFN_TILE)
    n_tiles = r_glob // FFN_TILE
    n_used = (gend[-1] // FFN_TILE).astype(I32)
    tile_start = jnp.minimum(jnp.arange(n_tiles, dtype=I32), n_used - 1) * FFN_TILE
    tile_e = jnp.minimum(jnp.sum(gend[None, :] <= tile_start[:, None], axis=1), e - 1).astype(I32)
    ids = jnp.arange(e, dtype=I32)
    later_owner = jnp.min(jnp.where((ids[None, :] > ids[:, None]) & (gpad[None, :] > 0), ids[None, :], e),
                          axis=1)
    later_owner = jnp.where(later_owner < e, later_owner, -1)
    next_e = jnp.sum(jnp.where(tile_e[:, None] == ids[None, :], later_owner[None, :], 0), axis=1).astype(I32)
    return dict(next_e=next_e, loc_flat=loc_off.astype(I32).reshape(-1), glob_flat=glob_off.astype(I32).reshape(-1),
                rows_flat=cnt_g.astype(I32).reshape(-1), used=used.astype(I32),
                loc_col=loc_off.astype(F32)[:, :, None], loc_row=loc_off.astype(F32)[:, None, :],
                cnt_row=cnt_g.astype(F32)[:, None, :],
                tail_lo=jnp.append(gstart + gcnt, gend[-1]).astype(I32),
                tail_hi=jnp.append(gend, r_glob).astype(I32),
                tile_e=tile_e, n_used=n_used.reshape(1), r_glob=r_glob)


def kernel(x, mem, ln_in_g, ln_in_b, w_in, b_in, ln_v_g, ln_v_b, w_spatial, b_spatial, w_out,
           ln1_g, ln1_b, w_mem_q, w_mem_kv, w_mem_o, ln2_g, ln2_b, w_router, router_bias,
           w_exp_gate, w_exp_up, w_exp_down, w_sh_gate, w_sh_up, w_sh_down, ln3_g, ln3_b):
    bsz, seq, d = x.shape
    t = bsz * seq
    assert w_in.shape[0] == DEPTH
    assert seq % TQ == 0 and seq % TM_MID == 0 and seq % CHUNK == 0, seq
    assert t % TM_PROJ == 0 and t % TM_ROUTE == 0 and TM_ROUTE % TM_MOE == 0, t
    assert w_router.shape[2] == N_EXPERTS and N_EXPERTS % EXPERT_UNROLL == 0
    assert d % (2 * LANES) == 0 and d % A_GROUPS == 0, d
    x2d = x.reshape(t, d)
    row = lambda a: a.reshape(1, -1)

    oa, q, k, v, gb, xln = _in_proj(x2d, row(ln_in_g), row(ln_in_b), w_in[0], row(b_in[0]),
                                    row(ln_v_g[0]), row(ln_v_b[0]), w_spatial[0], b_spatial[0].T)
    yb = _sb_attention(q.reshape(bsz, seq, d), k.reshape(bsz, seq, d), v.reshape(bsz, seq, d))

    wr_t = w_router[0].T
    wr_hi = wr_t.astype(BF16)
    wr_lo = (wr_t - wr_hi.astype(F32)).astype(BF16)
    base, x2b, logits_t = _mid(
        xln, oa, yb.reshape(t, d), gb, w_out[0].astype(BF16),
        row(ln1_g[0]), row(ln1_b[0]), w_mem_q[0].astype(BF16), mem, w_mem_kv[0].astype(BF16),
        w_mem_o[0].astype(BF16), row(ln2_g[0]), row(ln2_b[0]), wr_hi, wr_lo,
        w_sh_gate[0].astype(BF16), w_sh_up[0].astype(BF16), w_sh_down[0].astype(BF16), seq)

    sel_t, w_t, cnt = _route(logits_t, router_bias[0].reshape(-1, 1))
    lay = _moe_layout(cnt[:, :, 0].astype(I32), t)
    xs = _dispatch(x2b, sel_t, lay)
    ys = _experts(xs, lay["tile_e"], lay["n_used"], lay["next_e"], w_exp_gate[0], w_exp_up[0],
                  w_exp_down[0])
    out = _combine(base, sel_t, w_t, lay, ys, row(ln3_g[0]), row(ln3_b[0]))
    return out.reshape(bsz, seq, d)
```
